```python
import math
import jax, jax.numpy as jnp
from jax import lax
import numpy as np

D_MODEL = 1024
BATCH = 8
SEQ = 2048
DEPTH = 2
DEC_BATCH = 128
DEC_SEQ = 1
PAST_LEN = 16384
PAGE_SIZE = 128

GLA_HEADS = 4
GLA_DK = 64
GLA_DV = 128
GLA_RANK = 16
GLA_TAU = 16.0
S5_WIDTH = D_MODEL // 2
S5_GROUP = 16
S5_GROUPS = S5_WIDTH // S5_GROUP
S5_STATE = 64
HG_HEADS = 4
HG_DK = 64
HG_DV = 128
CHUNK = 16
MOE_GROUPS = 4
MOE_PER_GROUP = 8
MOE_EXPERTS = MOE_GROUPS * MOE_PER_GROUP
MOE_TOPK = 2
MOE_FF = 256
PLE_DIM = 256
N_BRANCH = 3
EPS = 1e-6

GLA_QK = GLA_HEADS * GLA_DK
GLA_V = GLA_HEADS * GLA_DV
HG_QK = HG_HEADS * HG_DK
HG_V = HG_HEADS * HG_DV
IN_WIDTHS = (GLA_QK, GLA_QK, GLA_V, GLA_V, GLA_RANK, S5_WIDTH, HG_QK, HG_QK, HG_V, HG_V, N_BRANCH * D_MODEL)
IN_DIM = sum(IN_WIDTHS)
IN_OFFSETS = tuple(int(v) for v in np.cumsum(IN_WIDTHS)[:-1])

kernel_name = 'hybrid_gla_s5_hgrn2_hmoe_step'


def rmsnorm(x, g):
    x32 = x.astype(jnp.float32)
    y = x32 * lax.rsqrt(jnp.mean(x32 * x32, axis=-1, keepdims=True) + EPS)
    return (y * g.astype(jnp.float32)).astype(x.dtype)


def chunked_gated_linear(q, k, v, log_decay, s0):
    f32 = jnp.float32
    bsz, length, heads, _ = q.shape
    dv = v.shape[-1]
    pad = (-length) % CHUNK

    def prep(t):
        t = jnp.pad(t.astype(f32), ((0, 0), (0, pad), (0, 0), (0, 0)))
        return t.reshape(bsz, -1, CHUNK, heads, t.shape[-1])

    q, k, v, g = prep(q), prep(k), prep(v), prep(log_decay)
    b = jnp.cumsum(g, axis=2)
    causal = jnp.tril(jnp.ones((CHUNK, CHUNK), dtype=bool))
    diff = b[:, :, :, None] - b[:, :, None, :]
    decay_ij = jnp.exp(jnp.where(causal[:, :, None, None], diff, -jnp.inf))
    scores = jnp.einsum('bnihk,bnjhk,bnijhk->bnhij', q, k, decay_ij)
    o_intra = jnp.einsum('bnhij,bnjhv->bnihv', scores, v)
    b_last = b[:, :, -1]
    q_in = q * jnp.exp(b)
    k_out = k * jnp.exp(b_last[:, :, None] - b)

    def step(state, xs):
        qc, kc, vc, dc = xs
        o = jnp.einsum('bihk,bhkv->bihv', qc, state)
        state = dc[..., None] * state + jnp.einsum('bjhk,bjhv->bhkv', kc, vc)
        return state, o

    xs = tuple(jnp.moveaxis(t, 1, 0) for t in (q_in, k_out, v, jnp.exp(b_last)))
    s_final, o_inter = lax.scan(step, s0.astype(f32), xs)
    o = o_intra + jnp.moveaxis(o_inter, 0, 1)
    o = o.reshape(bsz, -1, heads, dv)[:, :length]
    return o, s_final


def s5_branch(u, h0_re, h0_im, a_re, a_im, log_dt, b_re, b_im, c_re, c_im, d_skip, w_glu, b_glu):
    f32 = jnp.float32
    bsz, length, _ = u.shape
    u32 = u.astype(f32)
    ug = u32.reshape(bsz, length, S5_GROUPS, S5_GROUP)
    dt = jnp.exp(log_dt.astype(f32))[:, None]
    lr, li = a_re.astype(f32), a_im.astype(f32)
    mag = jnp.exp(lr * dt)
    ab_re = mag * jnp.cos(li * dt)
    ab_im = mag * jnp.sin(li * dt)
    den = lr * lr + li * li
    num_re = ab_re - 1.0
    coef_re = (num_re * lr + ab_im * li) / den
    coef_im = (ab_im * lr - num_re * li) / den
    br, bi = b_re.astype(f32), b_im.astype(f32)
    bb_re = coef_re[..., None] * br - coef_im[..., None] * bi
    bb_im = coef_re[..., None] * bi + coef_im[..., None] * br
    bu_re = jnp.einsum('blgm,gpm->blgp', ug, bb_re)
    bu_im = jnp.einsum('blgm,gpm->blgp', ug, bb_im)
    h0r, h0i = h0_re.astype(f32), h0_im.astype(f32)
    bu_re = bu_re.at[:, 0].add(ab_re * h0r - ab_im * h0i)
    bu_im = bu_im.at[:, 0].add(ab_re * h0i + ab_im * h0r)
    a_re_t = jnp.broadcast_to(ab_re, bu_re.shape)
    a_im_t = jnp.broadcast_to(ab_im, bu_im.shape)

    def combine(e1, e2):
        a1r, a1i, b1r, b1i = e1
        a2r, a2i, b2r, b2i = e2
        return (a1r * a2r - a1i * a2i, a1r * a2i + a1i * a2r,
                a2r * b1r - a2i * b1i + b2r, a2r * b1i + a2i * b1r + b2i)

    _, _, x_re, x_im = lax.associative_scan(combine, (a_re_t, a_im_t, bu_re, bu_im), axis=1)
    y = (jnp.einsum('blgp,gmp->blgm', x_re, c_re.astype(f32))
         - jnp.einsum('blgp,gmp->blgm', x_im, c_im.astype(f32)))
    y = y.reshape(bsz, length, S5_WIDTH) + d_skip.astype(f32) * u32
    z = jax.nn.gelu(y)
    out = z * jax.nn.sigmoid(z @ w_glu.astype(f32) + b_glu.astype(f32))
    return out, x_re[:, -1], x_im[:, -1]


def hier_moe(h, w_group, b_group, w_expert, b_expert, w_gate, w_up, w_down):
    f32 = jnp.float32
    glog = (h @ w_group + b_group).astype(f32)
    gprob = jax.nn.softmax(glog, axis=-1)
    g_idx = jnp.argmax(glog, axis=-1)
    g_w = jnp.take_along_axis(gprob, g_idx[..., None], axis=-1)
    elog = (h @ w_expert + b_expert).astype(f32).reshape(h.shape[:-1] + (MOE_GROUPS, MOE_PER_GROUP))
    in_group = jnp.take_along_axis(elog, g_idx[..., None, None], axis=-2)[..., 0, :]
    top_v, top_i = lax.top_k(in_group, MOE_TOPK)
    w_top = jax.nn.softmax(top_v, axis=-1) * g_w
    expert_id = g_idx[..., None] * MOE_PER_GROUP + top_i
    comb = jnp.einsum('blk,blke->ble', w_top, jax.nn.one_hot(expert_id, MOE_EXPERTS, dtype=f32))
    hg = jnp.einsum('bld,edf->blef', h, w_gate)
    hu = jnp.einsum('bld,edf->blef', h, w_up)
    act = jax.nn.silu(hg) * hu * comb[..., None].astype(hg.dtype)
    return jnp.einsum('blef,efd->bld', act, w_down)


def hgrn_lower_bounds(logits):
    cs = jnp.cumsum(jax.nn.softmax(logits.astype(jnp.float32), axis=0), axis=0)
    return cs - cs[0]


def decoder_layer(x, p, s_gla, s5_re, s5_im, s_hg, lb, norm_mix, w_in, gla_w_gate2, gla_b_gate, gla_norm,
                  s5_a_re, s5_a_im, s5_log_dt, s5_b_re, s5_b_im, s5_c_re, s5_c_im, s5_d, s5_w_glu, s5_b_glu,
                  hgrn_norm, w_br_gla, w_br_s5, w_br_hgrn, w_out, norm_ffn, moe_w_group, moe_b_group,
                  moe_w_expert, moe_b_expert, moe_w_gate, moe_w_up, moe_w_down, norm_ple, w_ple_gate, w_ple_proj):
    f32 = jnp.float32
    bsz, length, _ = x.shape

    def heads(t, n):
        return t.reshape(bsz, length, n, -1)

    h = rmsnorm(x, norm_mix)
    proj = h @ w_in
    (gq, gk, gv, gr, glr, su, hq, hf, hi, hgate, gm) = jnp.split(proj, IN_OFFSETS, axis=-1)

    log_alpha = jax.nn.log_sigmoid((glr @ gla_w_gate2 + gla_b_gate).astype(f32)) / GLA_TAU
    o_gla, s_gla_new = chunked_gated_linear(heads(gq, GLA_HEADS) * (GLA_DK ** -0.5), heads(gk, GLA_HEADS),
                                            heads(gv, GLA_HEADS), heads(log_alpha, GLA_HEADS), s_gla)
    o_gla = rmsnorm(o_gla, gla_norm).reshape(bsz, length, GLA_V) * jax.nn.silu(gr.astype(f32))

    o_s5, s5_re_new, s5_im_new = s5_branch(su, s5_re, s5_im, s5_a_re, s5_a_im, s5_log_dt, s5_b_re, s5_b_im,
                                           s5_c_re, s5_c_im, s5_d, s5_w_glu, s5_b_glu)

    z = heads(hf, HG_HEADS).astype(f32)
    lbh = lb.reshape(HG_HEADS, HG_DK).astype(f32)
    log_f = jnp.logaddexp(jnp.log(lbh), jnp.log1p(-lbh) + jax.nn.log_sigmoid(z))
    k_hg = (1.0 - lbh) * jax.nn.sigmoid(-z)
    o_hg, s_hg_new = chunked_gated_linear(jax.nn.silu(heads(hq, HG_HEADS).astype(f32)), k_hg,
                                          heads(hi, HG_HEADS), log_f, s_hg)
    o_hg = rmsnorm(o_hg, hgrn_norm).reshape(bsz, length, HG_V) * jax.nn.silu(hgate.astype(f32))

    gates = jax.nn.sigmoid(gm.astype(f32)).reshape(bsz, length, N_BRANCH, D_MODEL)
    mixed = (gates[:, :, 0] * (o_gla @ w_br_gla) + gates[:, :, 1] * (o_s5 @ w_br_s5)
             + gates[:, :, 2] * (o_hg @ w_br_hgrn))
    x = x + (mixed @ w_out).astype(x.dtype)

    x = x + hier_moe(rmsnorm(x, norm_ffn), moe_w_group, moe_b_group, moe_w_expert, moe_b_expert,
                     moe_w_gate, moe_w_up, moe_w_down).astype(x.dtype)

    x = x + (jax.nn.sigmoid(rmsnorm(x, norm_ple) @ w_ple_gate) * (p @ w_ple_proj)).astype(x.dtype)
    return (x, s_gla_new.astype(s_gla.dtype), s5_re_new.astype(s5_re.dtype),
            s5_im_new.astype(s5_im.dtype), s_hg_new.astype(s_hg.dtype))


def setup_inputs(seed: int = 0) -> dict:
    key = jax.random.key(seed)
    ks = iter(jax.random.split(key, 48))
    f32 = jnp.float32

    def nrm(shape, scale):
        return scale * jax.random.normal(next(ks), shape, f32)

    def gain(shape):
        return 1.0 + 0.02 * jax.random.normal(next(ks), shape, f32)

    G, P, M = S5_GROUPS, S5_STATE, S5_GROUP
    return {
        'x_prompt': nrm((BATCH, SEQ, D_MODEL), 1.0),
        'x_sample': nrm((DEC_BATCH, DEC_SEQ, D_MODEL), 1.0),
        'state_gla': nrm((DEPTH, DEC_BATCH, GLA_HEADS, GLA_DK, GLA_DV), 0.5),
        'state_s5_re': nrm((DEPTH, DEC_BATCH, G, P), 0.05),
        'state_s5_im': nrm((DEPTH, DEC_BATCH, G, P), 0.05),
        'state_hgrn': nrm((DEPTH, DEC_BATCH, HG_HEADS, HG_DK, HG_DV), 0.5),
        'p_prompt': nrm((DEPTH, BATCH, SEQ, PLE_DIM), 1.0),
        'p_sample': nrm((DEPTH, DEC_BATCH, DEC_SEQ, PLE_DIM), 1.0),
        'norm_mix': gain((DEPTH, D_MODEL)),
        'w_in': nrm((DEPTH, D_MODEL, IN_DIM), D_MODEL ** -0.5),
        'gla_w_gate2': nrm((DEPTH, GLA_RANK, GLA_QK), GLA_RANK ** -0.5),
        'gla_b_gate': nrm((DEPTH, GLA_QK), 0.1),
        'gla_norm': gain((DEPTH, GLA_DV)),
        's5_a_re': -0.5 + nrm((DEPTH, G, P), 0.01),
        's5_a_im': jnp.pi * jnp.arange(P, dtype=f32) + nrm((DEPTH, G, P), 0.01),
        's5_log_dt': jax.random.uniform(next(ks), (DEPTH, G), f32, math.log(1e-3), math.log(1e-1)),
        's5_b_re': nrm((DEPTH, G, P, M), (2 * M) ** -0.5),
        's5_b_im': nrm((DEPTH, G, P, M), (2 * M) ** -0.5),
        's5_c_re': nrm((DEPTH, G, M, P), (2 * P) ** -0.5),
        's5_c_im': nrm((DEPTH, G, M, P), (2 * P) ** -0.5),
        's5_d': nrm((DEPTH, S5_WIDTH), 1.0),
        's5_w_glu': nrm((DEPTH, S5_WIDTH, S5_WIDTH), S5_WIDTH ** -0.5),
        's5_b_glu': nrm((DEPTH, S5_WIDTH), 0.02),
        'hgrn_lb_logits': nrm((DEPTH, HG_QK), 1.0),
        'hgrn_norm': gain((DEPTH, HG_DV)),
        'w_br_gla': nrm((DEPTH, GLA_V, D_MODEL), GLA_V ** -0.5),
        'w_br_s5': nrm((DEPTH, S5_WIDTH, D_MODEL), S5_WIDTH ** -0.5),
        'w_br_hgrn': nrm((DEPTH, HG_V, D_MODEL), HG_V ** -0.5),
        'w_out': nrm((DEPTH, D_MODEL, D_MODEL), D_MODEL ** -0.5),
        'norm_ffn': gain((DEPTH, D_MODEL)),
        'moe_w_group': nrm((DEPTH, D_MODEL, MOE_GROUPS), D_MODEL ** -0.5),
        'moe_b_group': nrm((DEPTH, MOE_GROUPS), 0.01),
        'moe_w_expert': nrm((DEPTH, D_MODEL, MOE_EXPERTS), D_MODEL ** -0.5),
        'moe_b_expert': nrm((DEPTH, MOE_EXPERTS), 0.01),
        'moe_w_gate': nrm((DEPTH, MOE_EXPERTS, D_MODEL, MOE_FF), D_MODEL ** -0.5),
        'moe_w_up': nrm((DEPTH, MOE_EXPERTS, D_MODEL, MOE_FF), D_MODEL ** -0.5),
        'moe_w_down': nrm((DEPTH, MOE_EXPERTS, MOE_FF, D_MODEL), MOE_FF ** -0.5),
        'norm_ple': gain((DEPTH, D_MODEL)),
        'w_ple_gate': nrm((DEPTH, D_MODEL, D_MODEL), D_MODEL ** -0.5),
        'w_ple_proj': nrm((DEPTH, PLE_DIM, D_MODEL), PLE_DIM ** -0.5),
        'norm_final': gain((D_MODEL,)),
    }


def reference(x_prompt, x_sample, state_gla, state_s5_re, state_s5_im, state_hgrn, p_prompt, p_sample,
              norm_mix, w_in, gla_w_gate2, gla_b_gate, gla_norm, s5_a_re, s5_a_im, s5_log_dt, s5_b_re, s5_b_im,
              s5_c_re, s5_c_im, s5_d, s5_w_glu, s5_b_glu, hgrn_lb_logits, hgrn_norm, w_br_gla, w_br_s5, w_br_hgrn,
              w_out, norm_ffn, moe_w_group, moe_b_group, moe_w_expert, moe_b_expert, moe_w_gate, moe_w_up,
              moe_w_down, norm_ple, w_ple_gate, w_ple_proj, norm_final):
    lb_all = hgrn_lower_bounds(hgrn_lb_logits)
    layer_params = (norm_mix, w_in, gla_w_gate2, gla_b_gate, gla_norm, s5_a_re, s5_a_im, s5_log_dt,
                    s5_b_re, s5_b_im, s5_c_re, s5_c_im, s5_d, s5_w_glu, s5_b_glu, hgrn_norm, w_br_gla,
                    w_br_s5, w_br_hgrn, w_out, norm_ffn, moe_w_group, moe_b_group, moe_w_expert, moe_b_expert,
                    moe_w_gate, moe_w_up, moe_w_down, norm_ple, w_ple_gate, w_ple_proj)
    dt = x_prompt.dtype
    zero_gla = jnp.zeros((BATCH, GLA_HEADS, GLA_DK, GLA_DV), dt)
    zero_s5 = jnp.zeros((BATCH, S5_GROUPS, S5_STATE), dt)
    zero_hg = jnp.zeros((BATCH, HG_HEADS, HG_DK, HG_DV), dt)
    xp, xs = x_prompt, x_sample
    new_p = ([], [], [], [])
    new_s = ([], [], [], [])
    for i in range(DEPTH):
        lp = tuple(w[i] for w in layer_params)
        xp, *st_p = decoder_layer(xp, p_prompt[i], zero_gla, zero_s5, zero_s5, zero_hg, lb_all[i], *lp)
        xs, *st_s = decoder_layer(xs, p_sample[i], state_gla[i], state_s5_re[i], state_s5_im[i],
                                  state_hgrn[i], lb_all[i], *lp)
        for j in range(4):
            new_p[j].append(st_p[j])
            new_s[j].append(st_s[j])
    y_prompt = rmsnorm(xp, norm_final)
    y_sample = rmsnorm(xs, norm_final)
    new_state_gla_prompt = jnp.stack(new_p[0])
    new_state_s5_re_prompt = jnp.stack(new_p[1])
    new_state_s5_im_prompt = jnp.stack(new_p[2])
    new_state_hgrn_prompt = jnp.stack(new_p[3])
    new_state_gla_sample = jnp.stack(new_s[0])
    new_state_s5_re_sample = jnp.stack(new_s[1])
    new_state_s5_im_sample = jnp.stack(new_s[2])
    new_state_hgrn_sample = jnp.stack(new_s[3])
    return (y_prompt, y_sample, new_state_gla_prompt, new_state_s5_re_prompt, new_state_s5_im_prompt,
            new_state_hgrn_prompt, new_state_gla_sample, new_state_s5_re_sample, new_state_s5_im_sample,
            new_state_hgrn_sample)
```

```python
import functools

import jax
import jax.numpy as jnp
import numpy as np
from jax import lax
from jax.experimental import pallas as pl
from jax.experimental.pallas import tpu as pltpu

F32, BF16, I32 = jnp.float32, jnp.bfloat16, jnp.int32

D_MODEL = 1024
DEPTH = 2
HEADS, DK, DV = 4, 64, 128
QK, VW = HEADS * DK, HEADS * DV
GLA_RANK, GLA_TAU = 16, 16.0
S5_WIDTH, S5_GROUP, S5_GROUPS, S5_STATE = 512, 16, 32, 64
S5_LANES = S5_GROUPS * S5_STATE
S5_SLABS = 4
MOE_GROUPS, MOE_PER_GROUP, MOE_EXPERTS, MOE_FF = 4, 8, 32, 256
PLE_DIM = 256
EPS = 1e-6

LANES = 128
CHUNK = 64
S5_CHUNK = 64
MOE_TILE = 256
SAFE_EXP = 80.0
VMEM_LIMIT = 56 * 1024 * 1024

C_GLA, C_S5, C_HG, C_GLR, C_END = 0, 1536, 2048, 3584, 3712


def _cparams(*sem):
    return pltpu.CompilerParams(dimension_semantics=sem, vmem_limit_bytes=VMEM_LIMIT)


def _dot(a, b):
    return jnp.dot(a, b, preferred_element_type=F32)


def _rms(x, g):
    return x * lax.rsqrt(jnp.mean(x * x, axis=-1, keepdims=True) + EPS) * g


def _log_sigmoid(x):
    return jnp.minimum(x, 0.0) - jnp.log1p(jnp.exp(-jnp.abs(x)))


def _sigmoid(x):
    return 1.0 / (1.0 + jnp.exp(-x))


def _silu(x):
    return x * _sigmoid(x)


def _split3(x):
    hi = x.astype(BF16)
    r1 = x - hi.astype(F32)
    mid = r1.astype(BF16)
    lo = (r1 - mid.astype(F32)).astype(BF16)
    return hi, mid, lo


def _dot01(m, parts):
    return _dot(m, parts[0]) + _dot(m, parts[1]) + _dot(m, parts[2])


def _inproj_kernel(x_ref, nm_ref, w_ref, wg2_ref, bg_ref, lbl_ref, qk_ref, v_ref, gate_ref, g_ref, su_ref, *, layer):
    hb = _rms(x_ref[...], nm_ref[...]).astype(BF16)

    def proj(lo, hi):
        return _dot(hb, w_ref[:, lo:hi])

    qk_ref[0, :, 0:QK] = (proj(0, 256) * (DK ** -0.5)).astype(BF16)
    qk_ref[0, :, QK:2 * QK] = proj(256, 512).astype(BF16)
    v_ref[0] = proj(512, 1024).astype(BF16)
    gate_ref[0] = _silu(proj(1024, 1536)).astype(BF16)
    glr = proj(C_GLR, C_END).astype(BF16)
    g_ref[0] = _log_sigmoid(_dot(glr, wg2_ref[...]) + bg_ref[...]) * (1.0 / GLA_TAU)

    su_ref[...] = proj(C_S5, C_HG)

    lg = lbl_ref[...]
    mx = jnp.max(lg, axis=0, keepdims=True)
    ex = jnp.exp(lg - mx)
    sm = ex / jnp.sum(ex, axis=0, keepdims=True)
    cs = sm[0:1]
    for j in range(1, layer + 1):
        cs = cs + sm[j:j + 1]
    lb = cs - sm[0:1]
    log_lb, log_1mlb = jnp.log(lb), jnp.log1p(-lb)
    z = proj(C_HG + 256, C_HG + 512)
    a, c = log_lb, log_1mlb + _log_sigmoid(z)
    g_ref[1] = jnp.maximum(a, c) + jnp.log1p(jnp.exp(-jnp.abs(a - c)))
    qk_ref[1, :, 0:QK] = _silu(proj(C_HG, C_HG + 256)).astype(BF16)
    qk_ref[1, :, QK:2 * QK] = ((1.0 - lb) * _sigmoid(-z)).astype(BF16)
    v_ref[1] = proj(C_HG + 512, C_HG + 1024).astype(BF16)
    gate_ref[1] = _silu(proj(C_HG + 1024, C_HG + 1536)).astype(BF16)


def _inproj(x, nm, w, wg2, bg, lbl, *, layer, tm):
    rows = x.shape[0]
    const = lambda i: (0, 0)
    return pl.pallas_call(
        functools.partial(_inproj_kernel, layer=layer),
        grid=(rows // tm,),
        in_specs=[pl.BlockSpec((tm, D_MODEL), lambda i: (i, 0)),
                  pl.BlockSpec((1, D_MODEL), const),
                  pl.BlockSpec((D_MODEL, C_END), const),
                  pl.BlockSpec((LANES, QK), const),
                  pl.BlockSpec((1, QK), const),
                  pl.BlockSpec((DEPTH, QK), const)],
        out_specs=[pl.BlockSpec((2, tm, 2 * QK), lambda i: (0, i, 0)),
                   pl.BlockSpec((2, tm, VW), lambda i: (0, i, 0)),
                   pl.BlockSpec((2, tm, VW), lambda i: (0, i, 0)),
                   pl.BlockSpec((2, tm, QK), lambda i: (0, i, 0)),
                   pl.BlockSpec((tm, S5_WIDTH), lambda i: (i, 0))],
        out_shape=[jax.ShapeDtypeStruct((2, rows, 2 * QK), BF16),
                   jax.ShapeDtypeStruct((2, rows, VW), BF16),
                   jax.ShapeDtypeStruct((2, rows, VW), BF16),
                   jax.ShapeDtypeStruct((2, rows, QK), F32),
                   jax.ShapeDtypeStruct((rows, S5_WIDTH), F32)],
        compiler_params=_cparams("parallel"),
        name="inproj",
    )(x, nm, w, wg2, bg, lbl)


def _head_stack(x):
    head = lax.broadcasted_iota(I32, x.shape, 1) // DK
    return jnp.concatenate([jnp.where(head == h, x, 0.0) for h in range(HEADS)], axis=0).astype(BF16)


def _stack_scores(qt, kt):
    return lax.dot_general(_head_stack(qt), kt.astype(BF16), (((1,), (1,)), ((), ())), preferred_element_type=F32)


def _chunk_kernel(qk_ref, v_ref, g_ref, o_ref, st_ref, bc_ref, sc_ref, *, nb, c):
    @pl.when(pl.program_id(1) == 0)
    def _():
        st_ref[...] = jnp.zeros_like(st_ref)

    row = lax.broadcasted_iota(I32, (c, c), 0)
    col = lax.broadcasted_iota(I32, (c, c), 1)
    tri = jnp.where(col <= row, 1.0, 0.0).astype(BF16)
    srow = lax.broadcasted_iota(I32, (HEADS * c, c), 0) & (c - 1)
    scol = lax.broadcasted_iota(I32, (HEADS * c, c), 1)
    mid = c // 2 - 1

    spread = None
    for b in range(nb):
        bc = _dot01(tri, _split3(g_ref[0, b]))
        bc_ref[b] = bc
        ref, last = bc[mid:mid + 1, :], bc[c - 1:c, :]
        s = jnp.maximum(jnp.max(-ref), jnp.max(ref - last))
        spread = s if spread is None else jnp.maximum(spread, s)

    def qk_of(b):
        return qk_ref[0, b, :, 0:QK].astype(F32), qk_ref[0, b, :, QK:2 * QK].astype(F32)

    def scores_one_reference():
        for b in range(nb):
            q, k = qk_of(b)
            bc = bc_ref[b]
            ref = bc[mid:mid + 1, :]
            s = _stack_scores(q * jnp.exp(bc - ref), k * jnp.exp(ref - bc))
            sc_ref[b] = jnp.where(scol <= srow, s, 0.0).astype(BF16)

    def scores_by_levels():
        qrow = lax.broadcasted_iota(I32, (c, QK), 0)
        for b in range(nb):
            q, k = qk_of(b)
            bc = bc_ref[b]
            parts = _split3(g_ref[0, b])
            acc = jnp.where(scol == srow, _stack_scores(q, k), 0.0)
            half = c // 2
            while half >= 1:
                blk = 2 * half
                last_low = (row & ~(blk - 1)) + (half - 1)
                ref = _dot01(jnp.where(col <= last_low, 1.0, 0.0).astype(BF16), parts)
                upper = (qrow & (blk - 1)) >= half
                dq = jnp.minimum(jnp.where(upper, bc - ref, 0.0), 0.0)
                dk = jnp.minimum(jnp.where(upper, 0.0, ref - bc), 0.0)
                s = _stack_scores(q * jnp.exp(dq), k * jnp.exp(dk))
                pair = ((srow & ~(blk - 1)) == (scol & ~(blk - 1))) & ((srow & (blk - 1)) >= half) & ((scol & (blk - 1)) < half)
                acc = acc + jnp.where(pair, s, 0.0)
                half //= 2
            sc_ref[b] = acc.astype(BF16)

    lax.cond(spread <= SAFE_EXP, scores_one_reference, scores_by_levels)

    for b in range(nb):
        q, k = qk_of(b)
        bc = bc_ref[b]
        last = bc[c - 1:c, :]
        v = v_ref[0, b]
        state = st_ref[0, b]
        o_intra = _dot(sc_ref[b], v)
        o_inter = _dot(_head_stack(q * jnp.exp(bc)), state.astype(BF16))
        upd = lax.dot_general((k * jnp.exp(last - bc)).astype(BF16), v, (((0,), (0,)), ((), ())),
                              preferred_element_type=F32)
        decay = jnp.transpose(jnp.broadcast_to(jnp.exp(last), (DV, QK)))
        for h in range(HEADS):
            rs, ls = slice(h * c, (h + 1) * c), slice(h * DV, (h + 1) * DV)
            o_ref[0, b, :, ls] = (o_intra[rs, ls] + o_inter[rs, :]).astype(BF16)
            ks = slice(h * DK, (h + 1) * DK)
            st_ref[0, b, ks, :] = decay[ks, :] * state[ks, :] + upd[ks, ls]


def _chunk_scan(qk, v, g, *, nb, length):
    c = CHUNK
    blk = lambda w: pl.BlockSpec((1, nb, c, w), lambda br, i: (br, 0, i, 0))
    return pl.pallas_call(
        functools.partial(_chunk_kernel, nb=nb, c=c),
        grid=(2, length // c),
        in_specs=[blk(2 * QK), blk(VW), blk(QK)],
        out_specs=[blk(VW), pl.BlockSpec((1, nb, QK, DV), lambda br, i: (br, 0, 0, 0))],
        out_shape=[jax.ShapeDtypeStruct((2, nb, length, VW), BF16),
                   jax.ShapeDtypeStruct((2, nb, QK, DV), F32)],
        scratch_shapes=[pltpu.VMEM((nb, c, QK), F32), pltpu.VMEM((nb, HEADS * c, c), BF16)],
        compiler_params=_cparams("arbitrary", "arbitrary"),
        name="chunk_scan",
    )(qk, v, g)


def _decode_kernel(qk_ref, v_ref, g_ref, s0_ref, s1_ref, o_ref, n0_ref, n1_ref, *, nt):
    for br, (s_ref, n_ref) in enumerate(((s0_ref, n0_ref), (s1_ref, n1_ref))):
        for j in range(nt):
            d = jnp.exp(g_ref[br, j:j + 1, :])
            q = qk_ref[br, j:j + 1, 0:QK].astype(F32)
            k = qk_ref[br, j:j + 1, QK:2 * QK].astype(F32)
            cols = jnp.transpose(jnp.concatenate([d, k, q, jnp.zeros((5, QK), F32)], axis=0))
            vrow = v_ref[br, j:j + 1, :].astype(F32)
            vfull = jnp.concatenate([jnp.broadcast_to(vrow[:, h * DV:(h + 1) * DV], (DK, DV)) for h in range(HEADS)], axis=0)
            new = cols[:, 0:1] * s_ref[j] + cols[:, 1:2] * vfull
            n_ref[j] = new
            t = cols[:, 2:3] * new
            for h in range(HEADS):
                o_ref[br, j:j + 1, h * DV:(h + 1) * DV] = jnp.sum(t[h * DK:(h + 1) * DK, :], axis=0, keepdims=True).astype(BF16)


def _decode(qk, v, g, s_gla, s_hg):
    n = qk.shape[1]
    nt = 8
    row = lambda w: pl.BlockSpec((2, nt, w), lambda i: (0, i, 0))
    st = pl.BlockSpec((nt, QK, DV), lambda i: (i, 0, 0))
    return pl.pallas_call(
        functools.partial(_decode_kernel, nt=nt),
        grid=(n // nt,),
        in_specs=[row(2 * QK), row(VW), row(QK), st, st],
        out_specs=[row(VW), st, st],
        out_shape=[jax.ShapeDtypeStruct((2, n, VW), BF16),
                   jax.ShapeDtypeStruct((n, QK, DV), F32),
                   jax.ShapeDtypeStruct((n, QK, DV), F32)],
        compiler_params=_cparams("parallel"),
        name="decode_step",
    )(qk, v, g, s_gla, s_hg)


def _s5_disc_kernel(lr_ref, li_ref, ldt_ref, br_ref, bi_ref, abr_ref, abi_ref, bbr_ref, bbi_ref):
    lr, li, dt = lr_ref[...], li_ref[...], jnp.exp(ldt_ref[...])
    mag = jnp.exp(lr * dt)
    ab_re, ab_im = mag * jnp.cos(li * dt), mag * jnp.sin(li * dt)
    den = lr * lr + li * li
    num_re = ab_re - 1.0
    coef_re = (num_re * lr + ab_im * li) / den
    coef_im = (ab_im * lr - num_re * li) / den
    br, bi = br_ref[...], bi_ref[...]
    abr_ref[...] = ab_re
    abi_ref[...] = ab_im
    bbr_ref[...] = coef_re * br - coef_im * bi
    bbi_ref[...] = coef_re * bi + coef_im * br


def _s5_discretise(a_re, a_im, log_dt, b_re, b_im):
    n = DEPTH * S5_GROUPS
    rep = lambda t: jnp.repeat(t.reshape(n, S5_STATE), S5_GROUP, axis=1)
    ldt = jnp.broadcast_to(log_dt.reshape(n, 1), (n, S5_STATE * S5_GROUP))
    shape = jax.ShapeDtypeStruct((n, S5_STATE * S5_GROUP), F32)
    ab_re, ab_im, bb_re, bb_im = pl.pallas_call(_s5_disc_kernel, out_shape=[shape] * 4, name="s5_discretise")(
        rep(a_re), rep(a_im), ldt, b_re.reshape(n, -1), b_im.reshape(n, -1))
    pole = lambda t: t[:, ::S5_GROUP].reshape(DEPTH, 1, S5_LANES)
    eye = jnp.eye(8, dtype=F32)

    def blockdiag_in(bb):
        t = bb.reshape(DEPTH, S5_SLABS, 8, S5_STATE, S5_GROUP).transpose(0, 1, 2, 4, 3)
        return jnp.einsum("dcgmp,gh->dcgmhp", t, eye).reshape(DEPTH, S5_SLABS, LANES, 8 * S5_STATE)

    w_in = jnp.concatenate([blockdiag_in(bb_re), blockdiag_in(bb_im)], axis=-1).astype(BF16)
    return w_in, pole(ab_re), pole(ab_im)


def _s5_blockdiag_out(c):
    t = c.reshape(DEPTH, S5_SLABS, 8, S5_GROUP, S5_STATE).transpose(0, 1, 2, 4, 3)
    return jnp.einsum("dcgpm,gh->dcgphm", t, jnp.eye(8, dtype=F32)).reshape(DEPTH, S5_SLABS, 8 * S5_STATE, LANES).astype(BF16)


def _gelu_tanh(y):
    return 0.5 * y * (1.0 + jnp.tanh(0.7978845608028654 * (y + 0.044715 * (y * y * y))))


def _s5_input(u, win_ref, xre_ref, xim_ref):
    ub = u.astype(BF16)
    half = 8 * S5_STATE
    for s in range(S5_SLABS):
        r = _dot(ub[:, s * LANES:(s + 1) * LANES], win_ref[s])
        xre_ref[:, s * half:(s + 1) * half] = r[:, :half]
        xim_ref[:, s * half:(s + 1) * half] = r[:, half:]


def _s5_output(u, xre_ref, xim_ref, cre_ref, cim_ref, d_ref, wglu_ref, bglu_ref):
    half = 8 * S5_STATE
    ys = []
    for s in range(S5_SLABS):
        ls = slice(s * half, (s + 1) * half)
        ys.append(_dot(xre_ref[:, ls].astype(BF16), cre_ref[s]) - _dot(xim_ref[:, ls].astype(BF16), cim_ref[s]))
    z = _gelu_tanh(jnp.concatenate(ys, axis=-1) + d_ref[...] * u)
    return z * _sigmoid(_dot(z.astype(BF16), wglu_ref[...]) + bglu_ref[...])


def _s5_scan_kernel(su_ref, win_ref, are_ref, aim_ref, cre_ref, cim_ref, d_ref, wglu_ref, bglu_ref,
                    o_ref, hre_ref, him_ref, utm_ref, xre_ref, xim_ref, ytm_ref, *, nb, ct):
    @pl.when(pl.program_id(0) == 0)
    def _():
        hre_ref[...] = jnp.zeros_like(hre_ref)
        him_ref[...] = jnp.zeros_like(him_ref)

    def to_time_major(t, carry):
        for b in range(nb):
            utm_ref[pl.ds(t * nb + b, 1), :] = su_ref[b, pl.ds(t, 1), :]
        return carry

    lax.fori_loop(0, ct, to_time_major, 0)
    _s5_input(utm_ref[...], win_ref, xre_ref, xim_ref)

    width = 512
    for lc in range(S5_LANES // width):
        ls = slice(lc * width, (lc + 1) * width)
        ar = jnp.broadcast_to(are_ref[:, ls], (nb, width))
        ai = jnp.broadcast_to(aim_ref[:, ls], (nb, width))

        def step(t, carry):
            hr, hi = carry
            rows = pl.ds(pl.multiple_of(t * nb, nb), nb)
            nr = ar * hr - ai * hi + xre_ref[rows, ls]
            ni = ar * hi + ai * hr + xim_ref[rows, ls]
            xre_ref[rows, ls] = nr
            xim_ref[rows, ls] = ni
            return nr, ni

        hr, hi = lax.fori_loop(0, ct, step, (hre_ref[:, ls], him_ref[:, ls]), unroll=4)
        hre_ref[:, ls] = hr
        him_ref[:, ls] = hi

    ytm_ref[...] = _s5_output(utm_ref[...], xre_ref, xim_ref, cre_ref, cim_ref, d_ref, wglu_ref, bglu_ref)

    def from_time_major(t, carry):
        for b in range(nb):
            o_ref[b, pl.ds(t, 1), :] = ytm_ref[pl.ds(t * nb + b, 1), :]
        return carry

    lax.fori_loop(0, ct, from_time_major, 0)


def _s5_weight_specs(nd):
    zeros = lambda k: (lambda *_: (0,) * k)
    return [pl.BlockSpec((S5_SLABS, LANES, 2 * 8 * S5_STATE), zeros(3)),
            pl.BlockSpec((1, S5_LANES), zeros(2)), pl.BlockSpec((1, S5_LANES), zeros(2)),
            pl.BlockSpec((S5_SLABS, 8 * S5_STATE, LANES), zeros(3)), pl.BlockSpec((S5_SLABS, 8 * S5_STATE, LANES), zeros(3)),
            pl.BlockSpec((1, S5_WIDTH), zeros(2)), pl.BlockSpec((S5_WIDTH, S5_WIDTH), zeros(2)),
            pl.BlockSpec((1, S5_WIDTH), zeros(2))]


def _s5_scan(su, weights, *, nb, length):
    ct = S5_CHUNK
    return pl.pallas_call(
        functools.partial(_s5_scan_kernel, nb=nb, ct=ct),
        grid=(length // ct,),
        in_specs=[pl.BlockSpec((nb, ct, S5_WIDTH), lambda i: (0, i, 0))] + _s5_weight_specs(1),
        out_specs=[pl.BlockSpec((nb, ct, S5_WIDTH), lambda i: (0, i, 0)),
                   pl.BlockSpec((nb, S5_LANES), lambda i: (0, 0)), pl.BlockSpec((nb, S5_LANES), lambda i: (0, 0))],
        out_shape=[jax.ShapeDtypeStruct((nb, length, S5_WIDTH), F32),
                   jax.ShapeDtypeStruct((nb, S5_LANES), F32), jax.ShapeDtypeStruct((nb, S5_LANES), F32)],
        scratch_shapes=[pltpu.VMEM((nb * ct, S5_WIDTH), F32), pltpu.VMEM((nb * ct, S5_LANES), F32),
                        pltpu.VMEM((nb * ct, S5_LANES), F32), pltpu.VMEM((nb * ct, S5_WIDTH), F32)],
        compiler_params=_cparams("arbitrary"),
        name="s5_scan",
    )(su, *weights)


def _s5_step_kernel(su_ref, h0r_ref, h0i_ref, win_ref, are_ref, aim_ref, cre_ref, cim_ref, d_ref, wglu_ref, bglu_ref,
                    o_ref, hre_ref, him_ref):
    u = su_ref[...]
    _s5_input(u, win_ref, hre_ref, him_ref)
    ar, ai = are_ref[...], aim_ref[...]
    h0r, h0i = h0r_ref[...], h0i_ref[...]
    nr = ar * h0r - ai * h0i + hre_ref[...]
    ni = ar * h0i + ai * h0r + him_ref[...]
    hre_ref[...] = nr
    him_ref[...] = ni
    o_ref[...] = _s5_output(u, hre_ref, him_ref, cre_ref, cim_ref, d_ref, wglu_ref, bglu_ref)


def _s5_step(su, h0r, h0i, weights):
    n = su.shape[0]
    full = lambda w: pl.BlockSpec((n, w), lambda i: (0, 0))
    return pl.pallas_call(
        _s5_step_kernel,
        grid=(1,),
        in_specs=[full(S5_WIDTH), full(S5_LANES), full(S5_LANES)] + _s5_weight_specs(1),
        out_specs=[full(S5_WIDTH), full(S5_LANES), full(S5_LANES)],
        out_shape=[jax.ShapeDtypeStruct((n, S5_WIDTH), F32),
                   jax.ShapeDtypeStruct((n, S5_LANES), F32), jax.ShapeDtypeStruct((n, S5_LANES), F32)],
        compiler_params=_cparams("arbitrary"),
        name="s5_step",
    )(su, h0r, h0i, *weights)


def _head_norm(o, g):
    parts = []
    for h in range(HEADS):
        seg = o[:, h * DV:(h + 1) * DV]
        parts.append(seg * lax.rsqrt(jnp.mean(seg * seg, axis=-1, keepdims=True) + EPS))
    return jnp.concatenate(parts, axis=-1) * g


def _route(logits):
    lane = lax.broadcasted_iota(I32, logits.shape, 1)
    neg = -jnp.inf
    first = lambda hit: jnp.min(jnp.where(hit, lane, LANES), axis=-1, keepdims=True)
    glog = jnp.where(lane < MOE_GROUPS, logits, neg)
    gmax = jnp.max(glog, axis=-1, keepdims=True)
    gidx = first(glog == gmax)
    gw = 1.0 / jnp.sum(jnp.where(lane < MOE_GROUPS, jnp.exp(logits - gmax), 0.0), axis=-1, keepdims=True)
    inside = (lane >= MOE_GROUPS) & (lane < MOE_GROUPS + MOE_EXPERTS) & (((lane - MOE_GROUPS) >> 3) == gidx)
    el = jnp.where(inside, logits, neg)
    v1 = jnp.max(el, axis=-1, keepdims=True)
    i1 = first(el == v1)
    el2 = jnp.where(lane == i1, neg, el)
    v2 = jnp.max(el2, axis=-1, keepdims=True)
    i2 = first(el2 == v2)
    p2 = jnp.exp(v2 - v1)
    w1 = gw / (1.0 + p2)
    w2 = gw * p2 / (1.0 + p2)
    e1 = (i1 - MOE_GROUPS).astype(F32)
    e2 = (i2 - MOE_GROUPS).astype(F32)
    return jnp.where(lane == 0, e1, jnp.where(lane == 1, e2, jnp.where(lane == 2, w1, jnp.where(lane == 3, w2, 0.0))))


def _merge_kernel(x_ref, o_ref, gate_ref, os5_ref, nm_ref, wgm_ref, gn_ref, wbr_ref, wout_ref, nf_ref,
                  wrh_ref, wrl_ref, br_ref, xm_ref, h2_ref, rt_ref):
    x = x_ref[...]
    hb = _rms(x, nm_ref[...]).astype(BF16)
    mixed = None
    for i, src in enumerate((0, None, 1)):
        if src is None:
            branch = os5_ref[...].astype(BF16)
        else:
            branch = (_head_norm(o_ref[src].astype(F32), gn_ref[src]) * gate_ref[src].astype(F32)).astype(BF16)
        gate = _sigmoid(_dot(hb, wgm_ref[:, i * D_MODEL:(i + 1) * D_MODEL]))
        term = gate * _dot(branch, wbr_ref[i])
        mixed = term if mixed is None else mixed + term
    xm = x + _dot(mixed.astype(BF16), wout_ref[...])
    xm_ref[...] = xm
    h2 = _rms(xm, nf_ref[...])
    h2_hi = h2.astype(BF16)
    h2_lo = (h2 - h2_hi.astype(F32)).astype(BF16)
    h2_ref[...] = h2_hi
    logits = _dot(h2_hi, wrh_ref[...]) + (_dot(h2_hi, wrl_ref[...]) + _dot(h2_lo, wrh_ref[...])) + br_ref[...]
    rt_ref[...] = _route(logits)


def _merge(x, o, gate, os5, nm, wgm, gn, wbr, wout, nf, wrh, wrl, br, *, tm):
    rows = x.shape[0]
    zeros = lambda k: (lambda i: (0,) * k)
    return pl.pallas_call(
        _merge_kernel,
        grid=(rows // tm,),
        in_specs=[pl.BlockSpec((tm, D_MODEL), lambda i: (i, 0)),
                  pl.BlockSpec((2, tm, VW), lambda i: (0, i, 0)),
                  pl.BlockSpec((2, tm, VW), lambda i: (0, i, 0)),
                  pl.BlockSpec((tm, S5_WIDTH), lambda i: (i, 0)),
                  pl.BlockSpec((1, D_MODEL), zeros(2)),
                  pl.BlockSpec((D_MODEL, 3 * D_MODEL), zeros(2)),
                  pl.BlockSpec((2, 1, VW), zeros(3)),
                  pl.BlockSpec((3, VW, D_MODEL), zeros(3)),
                  pl.BlockSpec((D_MODEL, D_MODEL), zeros(2)),
                  pl.BlockSpec((1, D_MODEL), zeros(2)),
                  pl.BlockSpec((D_MODEL, LANES), zeros(2)),
                  pl.BlockSpec((D_MODEL, LANES), zeros(2)),
                  pl.BlockSpec((1, LANES), zeros(2))],
        out_specs=[pl.BlockSpec((tm, D_MODEL), lambda i: (i, 0)),
                   pl.BlockSpec((tm, D_MODEL), lambda i: (i, 0)),
                   pl.BlockSpec((tm, LANES), lambda i: (i, 0))],
        out_shape=[jax.ShapeDtypeStruct((rows, D_MODEL), F32),
                   jax.ShapeDtypeStruct((rows, D_MODEL), BF16),
                   jax.ShapeDtypeStruct((rows, LANES), F32)],
        compiler_params=_cparams("parallel"),
        name="merge_route",
    )(x, o, gate, os5, nm, wgm, gn, wbr, wout, nf, wrh, wrl, br)


def _ffn_kernel(te_ref, nu_ref, x_ref, wg_ref, wu_ref, wd_ref, y_ref):
    @pl.when(pl.program_id(0) < nu_ref[0])
    def _():
        x = x_ref[...]
        act = _silu(_dot(x, wg_ref[0])) * _dot(x, wu_ref[0])
        y_ref[...] = _dot(act.astype(BF16), wd_ref[0]).astype(BF16)


def _grouped_ffn(tile_expert, n_used, xg, wg, wu, wd):
    rows = xg.shape[0]
    grid_spec = pltpu.PrefetchScalarGridSpec(
        num_scalar_prefetch=2,
        grid=(rows // MOE_TILE,),
        in_specs=[pl.BlockSpec((MOE_TILE, D_MODEL), lambda i, te, nu: (i, 0)),
                  pl.BlockSpec((1, D_MODEL, MOE_FF), lambda i, te, nu: (te[i], 0, 0)),
                  pl.BlockSpec((1, D_MODEL, MOE_FF), lambda i, te, nu: (te[i], 0, 0)),
                  pl.BlockSpec((1, MOE_FF, D_MODEL), lambda i, te, nu: (te[i], 0, 0))],
        out_specs=pl.BlockSpec((MOE_TILE, D_MODEL), lambda i, te, nu: (i, 0)),
    )
    return pl.pallas_call(
        _ffn_kernel,
        grid_spec=grid_spec,
        out_shape=jax.ShapeDtypeStruct((rows, D_MODEL), BF16),
        compiler_params=_cparams("arbitrary"),
        name="expert_ffn",
    )(tile_expert, n_used, xg, wg, wu, wd)


def _moe_plan(route):
    tokens = route.shape[0]
    n_assign = 2 * tokens
    n_tiles = (n_assign + MOE_EXPERTS * MOE_TILE) // MOE_TILE
    e = route[:, 0:2].astype(I32).reshape(-1)
    onehot = (e[:, None] == jnp.arange(MOE_EXPERTS, dtype=I32)[None, :]).astype(I32)
    csum = jnp.cumsum(onehot, axis=0)
    counts = csum[-1]
    rank = jnp.take_along_axis(csum, e[:, None], axis=1)[:, 0] - 1
    padded = ((counts + MOE_TILE - 1) // MOE_TILE) * MOE_TILE
    gend = jnp.cumsum(padded)
    pos = (gend - padded)[e] + rank
    src = jnp.zeros((n_tiles * MOE_TILE,), I32).at[pos].set(jnp.arange(n_assign, dtype=I32) // 2)
    tile_start = jnp.arange(n_tiles, dtype=I32) * MOE_TILE
    tile_expert = jnp.minimum(jnp.searchsorted(gend, tile_start, side="right"), MOE_EXPERTS - 1).astype(I32)
    n_used = (gend[-1] // MOE_TILE).astype(I32).reshape(1)
    return src, pos.reshape(tokens, 2), tile_expert, n_used


def _ple_kernel(xm_ref, y1_ref, y2_ref, rt_ref, p_ref, np_ref, wg_ref, wp_ref, nfin_ref, out_ref, *, final):
    rt = rt_ref[...]
    x1 = xm_ref[...] + rt[:, 2:3] * y1_ref[...].astype(F32) + rt[:, 3:4] * y2_ref[...].astype(F32)
    gate = _sigmoid(_dot(_rms(x1, np_ref[...]).astype(BF16), wg_ref[...]))
    x2 = x1 + gate * _dot(p_ref[...].astype(BF16), wp_ref[...])
    out_ref[...] = _rms(x2, nfin_ref[...]) if final else x2


def _ple(xm, y1, y2, route, p, npl, wg, wp, nfin, *, final, tm):
    rows = xm.shape[0]
    zeros = lambda i: (0, 0)
    tile = lambda w: pl.BlockSpec((tm, w), lambda i: (i, 0))
    return pl.pallas_call(
        functools.partial(_ple_kernel, final=final),
        grid=(rows // tm,),
        in_specs=[tile(D_MODEL), tile(D_MODEL), tile(D_MODEL), tile(LANES), tile(PLE_DIM),
                  pl.BlockSpec((1, D_MODEL), zeros), pl.BlockSpec((D_MODEL, D_MODEL), zeros),
                  pl.BlockSpec((PLE_DIM, D_MODEL), zeros), pl.BlockSpec((1, D_MODEL), zeros)],
        out_specs=tile(D_MODEL),
        out_shape=jax.ShapeDtypeStruct((rows, D_MODEL), F32),
        compiler_params=_cparams("parallel"),
        name="combine_ple",
    )(xm, y1, y2, route, p, npl, wg, wp, nfin)


def kernel(x_prompt, x_sample, state_gla, state_s5_re, state_s5_im, state_hgrn, p_prompt, p_sample, norm_mix, w_in, gla_w_gate2, gla_b_gate, gla_norm, s5_a_re, s5_a_im, s5_log_dt, s5_b_re, s5_b_im, s5_c_re, s5_c_im, s5_d, s5_w_glu, s5_b_glu, hgrn_lb_logits, hgrn_norm, w_br_gla, w_br_s5, w_br_hgrn, w_out, norm_ffn, moe_w_group, moe_b_group, moe_w_expert, moe_b_expert, moe_w_gate, moe_w_up, moe_w_down, norm_ple, w_ple_gate, w_ple_proj, norm_final):
    nb, length, _ = x_prompt.shape
    ns = x_sample.shape[0]
    n_p = nb * length
    row = lambda t: t.reshape(DEPTH, 1, -1)

    o1, o2, o3, o4 = 1536, 1552, 2064, 3600
    w_a = jnp.concatenate([w_in[:, :, :o1], w_in[:, :, o2:o4], w_in[:, :, o1:o2],
                           jnp.zeros((DEPTH, D_MODEL, LANES - GLA_RANK), F32)], axis=-1).astype(BF16)
    w_gm = w_in[:, :, o4:].astype(BF16)
    wg2 = jnp.concatenate([gla_w_gate2, jnp.zeros((DEPTH, LANES - GLA_RANK, QK), F32)], axis=1).astype(BF16)
    s5_win, s5_are, s5_aim = _s5_discretise(s5_a_re, s5_a_im, s5_log_dt, s5_b_re, s5_b_im)
    s5_cre, s5_cim = _s5_blockdiag_out(s5_c_re), _s5_blockdiag_out(s5_c_im)
    s5_wglu = s5_w_glu.astype(BF16)
    head_gain = jnp.stack([jnp.tile(gla_norm, (1, HEADS)), jnp.tile(hgrn_norm, (1, HEADS))], axis=1).reshape(DEPTH, 2, 1, VW)
    w_br = jnp.stack([w_br_gla, w_br_s5, w_br_hgrn], axis=1).astype(BF16)
    w_o = w_out.astype(BF16)
    w_router = jnp.concatenate([moe_w_group, moe_w_expert,
                                jnp.zeros((DEPTH, D_MODEL, LANES - MOE_GROUPS - MOE_EXPERTS), F32)], axis=-1)
    wr_hi = w_router.astype(BF16)
    wr_lo = (w_router - wr_hi.astype(F32)).astype(BF16)
    b_router = jnp.concatenate([moe_b_group, moe_b_expert,
                                jnp.zeros((DEPTH, LANES - MOE_GROUPS - MOE_EXPERTS), F32)], axis=-1).reshape(DEPTH, 1, LANES)
    wm_gate, wm_up, wm_down = moe_w_gate.astype(BF16), moe_w_up.astype(BF16), moe_w_down.astype(BF16)
    w_pg, w_pp = w_ple_gate.astype(BF16), w_ple_proj.astype(BF16)
    nm, nf, npl = row(norm_mix), row(norm_ffn), row(norm_ple)
    bg, s5d, s5bg = row(gla_b_gate), row(s5_d), row(s5_b_glu)
    nfin = norm_final.reshape(1, D_MODEL)

    xp = x_prompt.reshape(n_p, D_MODEL)
    xs = x_sample.reshape(ns, D_MODEL)
    new_p, new_s = [], []
    for i in range(DEPTH):
        s5_w = (s5_win[i], s5_are[i], s5_aim[i], s5_cre[i], s5_cim[i], s5d[i], s5_wglu[i], s5bg[i])
        merge_w = (nm[i], w_gm[i], head_gain[i], w_br[i], w_o[i], nf[i], wr_hi[i], wr_lo[i], b_router[i])

        qk, v, gate, g, su = _inproj(xp, nm[i], w_a[i], wg2[i], bg[i], hgrn_lb_logits, layer=i, tm=512)
        o_p, st_p = _chunk_scan(qk.reshape(2, nb, length, -1), v.reshape(2, nb, length, -1),
                                g.reshape(2, nb, length, -1), nb=nb, length=length)
        os5_p, hre_p, him_p = _s5_scan(su.reshape(nb, length, -1), s5_w, nb=nb, length=length)
        xm_p, h2_p, rt_p = _merge(xp, o_p.reshape(2, n_p, VW), gate, os5_p.reshape(n_p, -1), *merge_w, tm=256)
        new_p.append((st_p[0].reshape(nb, HEADS, DK, DV), hre_p.reshape(nb, S5_GROUPS, S5_STATE),
                      him_p.reshape(nb, S5_GROUPS, S5_STATE), st_p[1].reshape(nb, HEADS, DK, DV)))

        qk, v, gate, g, su = _inproj(xs, nm[i], w_a[i], wg2[i], bg[i], hgrn_lb_logits, layer=i, tm=ns)
        o_s, sg_s, sh_s = _decode(qk, v, g, state_gla[i].reshape(ns, QK, DV), state_hgrn[i].reshape(ns, QK, DV))
        os5_s, hre_s, him_s = _s5_step(su, state_s5_re[i].reshape(ns, -1), state_s5_im[i].reshape(ns, -1), s5_w)
        xm_s, h2_s, rt_s = _merge(xs, o_s, gate, os5_s, *merge_w, tm=ns)
        new_s.append((sg_s.reshape(ns, HEADS, DK, DV), hre_s.reshape(ns, S5_GROUPS, S5_STATE),
                      him_s.reshape(ns, S5_GROUPS, S5_STATE), sh_s.reshape(ns, HEADS, DK, DV)))

        route = jnp.concatenate([rt_p, rt_s], axis=0)
        src, pos, tile_expert, n_used = _moe_plan(route)
        xg = jnp.take(jnp.concatenate([h2_p, h2_s], axis=0), src, axis=0)
        y = _grouped_ffn(tile_expert, n_used, xg, wm_gate[i], wm_up[i], wm_down[i])
        rows_of = lambda pp, slot: jnp.take(y, pp[:, slot], axis=0)
        pos_p, pos_s = pos[:n_p], pos[n_p:]

        final = i == DEPTH - 1
        ple_w = (npl[i], w_pg[i], w_pp[i], nfin)
        xp = _ple(xm_p, rows_of(pos_p, 0), rows_of(pos_p, 1), rt_p, p_prompt[i].reshape(n_p, PLE_DIM), *ple_w,
                  final=final, tm=512)
        xs = _ple(xm_s, rows_of(pos_s, 0), rows_of(pos_s, 1), rt_s, p_sample[i].reshape(ns, PLE_DIM), *ple_w,
                  final=final, tm=ns)

    stack = lambda items, j: jnp.stack([it[j] for it in items])
    return (xp.reshape(nb, length, D_MODEL), xs.reshape(ns, 1, D_MODEL),
            stack(new_p, 0), stack(new_p, 1), stack(new_p, 2), stack(new_p, 3),
            stack(new_s, 0), stack(new_s, 1), stack(new_s, 2), stack(new_s, 3))
```

```python
import functools

import jax
import jax.numpy as jnp
import numpy as np
from jax import lax
from jax.experimental import pallas as pl
from jax.experimental.compute_on import compute_on
from jax.experimental.pallas import tpu as pltpu

F32, BF16, I32 = jnp.float32, jnp.bfloat16, jnp.int32

D_MODEL = 1024
DEPTH = 2
HEADS, DK, DV = 4, 64, 128
QK, VW = HEADS * DK, HEADS * DV
GLA_RANK, GLA_TAU = 16, 16.0
S5_WIDTH, S5_GROUP, S5_GROUPS, S5_STATE = 512, 16, 32, 64
S5_LANES = S5_GROUPS * S5_STATE
S5_SLABS = 4
MOE_GROUPS, MOE_PER_GROUP, MOE_EXPERTS, MOE_FF = 4, 8, 32, 256
PLE_DIM = 256
EPS = 1e-6

LANES = 128
CHUNK = 64
S5_CHUNK = 64
MOE_TILE = 256
SAFE_EXP = 80.0
VMEM_LIMIT = 56 * 1024 * 1024

C_GLA, C_S5, C_HG, C_GLR, C_END = 0, 1536, 2048, 3584, 3712
R_E1, R_E2, R_W1, R_W2, R_RANK1, R_RANK2 = 0, 1, 2, 3, 4, 5


def _cparams(*sem):
    return pltpu.CompilerParams(dimension_semantics=sem, vmem_limit_bytes=VMEM_LIMIT)


def _layer_spec(shape, layer):
    return pl.BlockSpec((None,) + tuple(shape), lambda *_: (layer,) + (0,) * len(shape))


def _dot(a, b):
    return jnp.dot(a, b, preferred_element_type=F32)


def _rms(x, g):
    return x * lax.rsqrt(jnp.mean(x * x, axis=-1, keepdims=True) + EPS) * g


def _log_sigmoid(x):
    return jnp.minimum(x, 0.0) - jnp.log1p(jnp.exp(-jnp.abs(x)))


def _sigmoid(x):
    return 1.0 / (1.0 + jnp.exp(-x))


def _silu(x):
    return x * _sigmoid(x)


def _split3(x):
    hi = x.astype(BF16)
    r1 = x - hi.astype(F32)
    mid = r1.astype(BF16)
    lo = (r1 - mid.astype(F32)).astype(BF16)
    return hi, mid, lo


def _dot01(m, parts):
    out = _dot(m, parts[0])
    for p in parts[1:]:
        out = out + _dot(m, p)
    return out


def _inproj_kernel(x_ref, nm_ref, w_ref, wg2_ref, bg_ref, lbl_ref, qk_ref, v_ref, gate_ref, g_ref, su_ref, *, layer):
    hb = _rms(x_ref[...], nm_ref[...]).astype(BF16)

    def proj(lo, hi):
        return _dot(hb, w_ref[:, lo:hi])

    qk_ref[0, :, 0:QK] = (proj(0, 256) * (DK ** -0.5)).astype(BF16)
    qk_ref[0, :, QK:2 * QK] = proj(256, 512).astype(BF16)
    v_ref[0] = proj(512, 1024).astype(BF16)
    gate_ref[0] = _silu(proj(1024, 1536)).astype(BF16)
    glr = proj(C_GLR, C_END).astype(BF16)
    g_ref[0] = _log_sigmoid(_dot(glr, wg2_ref[...]) + bg_ref[...]) * (1.0 / GLA_TAU)

    su_ref[...] = proj(C_S5, C_HG)

    lg = lbl_ref[...]
    mx = jnp.max(lg, axis=0, keepdims=True)
    ex = jnp.exp(lg - mx)
    sm = ex / jnp.sum(ex, axis=0, keepdims=True)
    cs = sm[0:1]
    for j in range(1, layer + 1):
        cs = cs + sm[j:j + 1]
    lb = cs - sm[0:1]
    log_lb, log_1mlb = jnp.log(lb), jnp.log1p(-lb)
    z = proj(C_HG + 256, C_HG + 512)
    a, c = log_lb, log_1mlb + _log_sigmoid(z)
    g_ref[1] = jnp.maximum(a, c) + jnp.log1p(jnp.exp(-jnp.abs(a - c)))
    qk_ref[1, :, 0:QK] = _silu(proj(C_HG, C_HG + 256)).astype(BF16)
    qk_ref[1, :, QK:2 * QK] = ((1.0 - lb) * _sigmoid(-z)).astype(BF16)
    v_ref[1] = proj(C_HG + 512, C_HG + 1024).astype(BF16)
    gate_ref[1] = _silu(proj(C_HG + 1024, C_HG + 1536)).astype(BF16)


def _inproj(x, nm, w, wg2, bg, lbl, *, layer, tm):
    rows = x.shape[0]
    return pl.pallas_call(
        functools.partial(_inproj_kernel, layer=layer),
        grid=(rows // tm,),
        in_specs=[pl.BlockSpec((tm, D_MODEL), lambda i: (i, 0)),
                  _layer_spec((1, D_MODEL), layer),
                  _layer_spec((D_MODEL, C_END), layer),
                  _layer_spec((LANES, QK), layer),
                  _layer_spec((1, QK), layer),
                  pl.BlockSpec((DEPTH, QK), lambda i: (0, 0))],
        out_specs=[pl.BlockSpec((2, tm, 2 * QK), lambda i: (0, i, 0)),
                   pl.BlockSpec((2, tm, VW), lambda i: (0, i, 0)),
                   pl.BlockSpec((2, tm, VW), lambda i: (0, i, 0)),
                   pl.BlockSpec((2, tm, QK), lambda i: (0, i, 0)),
                   pl.BlockSpec((tm, S5_WIDTH), lambda i: (i, 0))],
        out_shape=[jax.ShapeDtypeStruct((2, rows, 2 * QK), BF16),
                   jax.ShapeDtypeStruct((2, rows, VW), BF16),
                   jax.ShapeDtypeStruct((2, rows, VW), BF16),
                   jax.ShapeDtypeStruct((2, rows, QK), F32),
                   jax.ShapeDtypeStruct((rows, S5_WIDTH), F32)],
        compiler_params=_cparams("parallel"),
        name="inproj",
    )(x, nm, w, wg2, bg, lbl)


def _head_stack(x):
    head = lax.broadcasted_iota(I32, x.shape, 1) // DK
    return jnp.concatenate([jnp.where(head == h, x, 0.0) for h in range(HEADS)], axis=0).astype(BF16)


def _stack_scores(qt, kt):
    return lax.dot_general(_head_stack(qt), kt.astype(BF16), (((1,), (1,)), ((), ())), preferred_element_type=F32)


def _chunk_kernel(qk_ref, v_ref, g_ref, o_ref, st_ref, bc_ref, sc_ref, *, nb, c):
    @pl.when(pl.program_id(1) == 0)
    def _():
        st_ref[...] = jnp.zeros_like(st_ref)

    row = lax.broadcasted_iota(I32, (c, c), 0)
    col = lax.broadcasted_iota(I32, (c, c), 1)
    tri = jnp.where(col <= row, 1.0, 0.0).astype(BF16)
    srow = lax.broadcasted_iota(I32, (HEADS * c, c), 0) & (c - 1)
    scol = lax.broadcasted_iota(I32, (HEADS * c, c), 1)
    mid = c // 2 - 1

    spread = None
    for b in range(nb):
        bc = _dot01(tri, _split3(g_ref[0, b]))
        bc_ref[b] = bc
        ref, last = bc[mid:mid + 1, :], bc[c - 1:c, :]
        s = jnp.maximum(jnp.max(-ref), jnp.max(ref - last))
        spread = s if spread is None else jnp.maximum(spread, s)

    def qk_of(b):
        return qk_ref[0, b, :, 0:QK].astype(F32), qk_ref[0, b, :, QK:2 * QK].astype(F32)

    def scores_one_reference():
        for b in range(nb):
            q, k = qk_of(b)
            bc = bc_ref[b]
            ref = bc[mid:mid + 1, :]
            s = _stack_scores(q * jnp.exp(bc - ref), k * jnp.exp(ref - bc))
            sc_ref[b] = jnp.where(scol <= srow, s, 0.0).astype(BF16)

    def scores_by_levels():
        qrow = lax.broadcasted_iota(I32, (c, QK), 0)
        for b in range(nb):
            q, k = qk_of(b)
            bc = bc_ref[b]
            parts = _split3(g_ref[0, b])
            acc = jnp.where(scol == srow, _stack_scores(q, k), 0.0)
            half = c // 2
            while half >= 1:
                blk = 2 * half
                last_low = (row & ~(blk - 1)) + (half - 1)
                ref = _dot01(jnp.where(col <= last_low, 1.0, 0.0).astype(BF16), parts)
                upper = (qrow & (blk - 1)) >= half
                dq = jnp.minimum(jnp.where(upper, bc - ref, 0.0), 0.0)
                dk = jnp.minimum(jnp.where(upper, 0.0, ref - bc), 0.0)
                s = _stack_scores(q * jnp.exp(dq), k * jnp.exp(dk))
                pair = ((srow & ~(blk - 1)) == (scol & ~(blk - 1))) & ((srow & (blk - 1)) >= half) & ((scol & (blk - 1)) < half)
                acc = acc + jnp.where(pair, s, 0.0)
                half //= 2
            sc_ref[b] = acc.astype(BF16)

    lax.cond(spread <= SAFE_EXP, scores_one_reference, scores_by_levels)

    for b in range(nb):
        q, k = qk_of(b)
        bc = bc_ref[b]
        last = bc[c - 1:c, :]
        v = v_ref[0, b]
        state = st_ref[0, b]
        o_intra = _dot(sc_ref[b], v)
        o_inter = _dot(_head_stack(q * jnp.exp(bc)), state.astype(BF16))
        upd = lax.dot_general((k * jnp.exp(last - bc)).astype(BF16), v, (((0,), (0,)), ((), ())),
                              preferred_element_type=F32)
        decay = jnp.transpose(jnp.broadcast_to(jnp.exp(last), (DV, QK)))
        for h in range(HEADS):
            rs, ls = slice(h * c, (h + 1) * c), slice(h * DV, (h + 1) * DV)
            o_ref[0, b, :, ls] = (o_intra[rs, ls] + o_inter[rs, :]).astype(BF16)
            ks = slice(h * DK, (h + 1) * DK)
            st_ref[0, b, ks, :] = decay[ks, :] * state[ks, :] + upd[ks, ls]


def _chunk_scan(qk, v, g, *, nb, length):
    c = CHUNK
    blk = lambda w: pl.BlockSpec((1, nb, c, w), lambda br, i: (br, 0, i, 0))
    return pl.pallas_call(
        functools.partial(_chunk_kernel, nb=nb, c=c),
        grid=(2, length // c),
        in_specs=[blk(2 * QK), blk(VW), blk(QK)],
        out_specs=[blk(VW), pl.BlockSpec((1, nb, QK, DV), lambda br, i: (br, 0, 0, 0))],
        out_shape=[jax.ShapeDtypeStruct((2, nb, length, VW), BF16),
                   jax.ShapeDtypeStruct((2, nb, QK, DV), F32)],
        scratch_shapes=[pltpu.VMEM((nb, c, QK), F32), pltpu.VMEM((nb, HEADS * c, c), BF16)],
        compiler_params=_cparams("arbitrary", "arbitrary"),
        name="chunk_scan",
    )(qk, v, g)


def _decode_kernel(qk_ref, v_ref, g_ref, s0_ref, s1_ref, *rest, nt):
    o_ref, n0_ref, n1_ref = rest[-3:]
    for br, (s_ref, n_ref) in enumerate(((s0_ref, n0_ref), (s1_ref, n1_ref))):
        for j in range(nt):
            d = jnp.exp(g_ref[br, j:j + 1, :])
            q = qk_ref[br, j:j + 1, 0:QK].astype(F32)
            k = qk_ref[br, j:j + 1, QK:2 * QK].astype(F32)
            cols = jnp.transpose(jnp.concatenate([d, k, q, jnp.zeros((5, QK), F32)], axis=0))
            vrow = v_ref[br, j:j + 1, :].astype(F32)
            vfull = jnp.concatenate([jnp.broadcast_to(vrow[:, h * DV:(h + 1) * DV], (DK, DV)) for h in range(HEADS)], axis=0)
            new = cols[:, 0:1] * s_ref[j] + cols[:, 1:2] * vfull
            n_ref[j] = new
            t = cols[:, 2:3] * new
            for h in range(HEADS):
                o_ref[br, j:j + 1, h * DV:(h + 1) * DV] = jnp.sum(t[h * DK:(h + 1) * DK, :], axis=0, keepdims=True).astype(BF16)


def _decode(qk, v, g, s_gla, s_hg, prev, *, layer):
    n = qk.shape[1]
    nt = 8
    row = lambda w: pl.BlockSpec((2, nt, w), lambda i: (0, i, 0))
    st = pl.BlockSpec((None, nt, QK, DV), lambda i: (layer, i, 0, 0))
    carried = [] if prev is None else list(prev)
    first = 5
    return pl.pallas_call(
        functools.partial(_decode_kernel, nt=nt),
        grid=(n // nt,),
        in_specs=[row(2 * QK), row(VW), row(QK), st, st] + [pl.BlockSpec(memory_space=pl.ANY)] * len(carried),
        out_specs=[row(VW), st, st],
        out_shape=[jax.ShapeDtypeStruct((2, n, VW), BF16),
                   jax.ShapeDtypeStruct((DEPTH, n, QK, DV), F32),
                   jax.ShapeDtypeStruct((DEPTH, n, QK, DV), F32)],
        input_output_aliases={first + j: 1 + j for j in range(len(carried))},
        compiler_params=_cparams("parallel"),
        name="decode_step",
    )(qk, v, g, s_gla, s_hg, *carried)


def _s5_disc_kernel(lr_ref, li_ref, ldt_ref, br_ref, bi_ref, abr_ref, abi_ref, bbr_ref, bbi_ref):
    lr, li, dt = lr_ref[...], li_ref[...], jnp.exp(ldt_ref[...])
    mag = jnp.exp(lr * dt)
    ab_re, ab_im = mag * jnp.cos(li * dt), mag * jnp.sin(li * dt)
    den = lr * lr + li * li
    num_re = ab_re - 1.0
    coef_re = (num_re * lr + ab_im * li) / den
    coef_im = (ab_im * lr - num_re * li) / den
    br, bi = br_ref[...], bi_ref[...]
    abr_ref[...] = ab_re
    abi_ref[...] = ab_im
    bbr_ref[...] = coef_re * br - coef_im * bi
    bbi_ref[...] = coef_re * bi + coef_im * br


def _s5_discretise(a_re, a_im, log_dt, b_re, b_im):
    n = DEPTH * S5_GROUPS
    rep = lambda t: jnp.repeat(t.reshape(n, S5_STATE), S5_GROUP, axis=1)
    ldt = jnp.broadcast_to(log_dt.reshape(n, 1), (n, S5_STATE * S5_GROUP))
    shape = jax.ShapeDtypeStruct((n, S5_STATE * S5_GROUP), F32)
    ab_re, ab_im, bb_re, bb_im = pl.pallas_call(_s5_disc_kernel, out_shape=[shape] * 4, name="s5_discretise")(
        rep(a_re), rep(a_im), ldt, b_re.reshape(n, -1), b_im.reshape(n, -1))
    pole = lambda t: t[:, ::S5_GROUP].reshape(DEPTH, 1, S5_LANES)
    eye = jnp.eye(8, dtype=F32)

    def blockdiag_in(bb):
        t = bb.reshape(DEPTH, S5_SLABS, 8, S5_STATE, S5_GROUP).transpose(0, 1, 2, 4, 3)
        return jnp.einsum("dcgmp,gh->dcgmhp", t, eye).reshape(DEPTH, S5_SLABS, LANES, 8 * S5_STATE)

    w_in = jnp.concatenate([blockdiag_in(bb_re), blockdiag_in(bb_im)], axis=-1).astype(BF16)
    return w_in, pole(ab_re), pole(ab_im)


def _s5_blockdiag_out(c):
    t = c.reshape(DEPTH, S5_SLABS, 8, S5_GROUP, S5_STATE).transpose(0, 1, 2, 4, 3)
    return jnp.einsum("dcgpm,gh->dcgphm", t, jnp.eye(8, dtype=F32)).reshape(DEPTH, S5_SLABS, 8 * S5_STATE, LANES).astype(BF16)


def _time_major_perm(nb, ct):
    r = np.arange(nb * ct)
    p = np.zeros((nb * ct, nb * ct), np.float32)
    p[r, (r % nb) * ct + r // nb] = 1.0
    return p


def _gelu_tanh(y):
    return 0.5 * y * (1.0 + jnp.tanh(0.7978845608028654 * (y + 0.044715 * (y * y * y))))


def _s5_input(ub, win_ref, xre_ref, xim_ref):
    half = 8 * S5_STATE
    for s in range(S5_SLABS):
        r = _dot(ub[:, s * LANES:(s + 1) * LANES], win_ref[s])
        xre_ref[:, s * half:(s + 1) * half] = r[:, :half]
        xim_ref[:, s * half:(s + 1) * half] = r[:, half:]


def _s5_readout(xre_ref, xim_ref, cre_ref, cim_ref):
    half = 8 * S5_STATE
    ys = []
    for s in range(S5_SLABS):
        ls = slice(s * half, (s + 1) * half)
        ys.append(_dot(xre_ref[:, ls].astype(BF16), cre_ref[s]) - _dot(xim_ref[:, ls].astype(BF16), cim_ref[s]))
    return jnp.concatenate(ys, axis=-1)


def _s5_glu(y, u, d_ref, wglu_ref, bglu_ref):
    z = _gelu_tanh(y + d_ref[...] * u)
    return z * _sigmoid(_dot(z.astype(BF16), wglu_ref[...]) + bglu_ref[...])


def _s5_scan_kernel(su_ref, perm_ref, permt_ref, win_ref, are_ref, aim_ref, cre_ref, cim_ref, d_ref, wglu_ref, bglu_ref,
                    o_ref, hre_ref, him_ref, xre_ref, xim_ref, *, nb, ct):
    @pl.when(pl.program_id(0) == 0)
    def _():
        hre_ref[...] = jnp.zeros_like(hre_ref)
        him_ref[...] = jnp.zeros_like(him_ref)

    u = su_ref[...].reshape(nb * ct, S5_WIDTH)
    _s5_input(_dot(perm_ref[...], u.astype(BF16)).astype(BF16), win_ref, xre_ref, xim_ref)

    width = 512
    for lc in range(S5_LANES // width):
        ls = slice(lc * width, (lc + 1) * width)
        ar = jnp.broadcast_to(are_ref[:, ls], (nb, width))
        ai = jnp.broadcast_to(aim_ref[:, ls], (nb, width))

        def step(t, carry):
            hr, hi = carry
            rows = pl.ds(pl.multiple_of(t * nb, nb), nb)
            nr = ar * hr - ai * hi + xre_ref[rows, ls]
            ni = ar * hi + ai * hr + xim_ref[rows, ls]
            xre_ref[rows, ls] = nr
            xim_ref[rows, ls] = ni
            return nr, ni

        hr, hi = lax.fori_loop(0, ct, step, (hre_ref[:, ls], him_ref[:, ls]), unroll=4)
        hre_ref[:, ls] = hr
        him_ref[:, ls] = hi

    y = _s5_readout(xre_ref, xim_ref, cre_ref, cim_ref)
    y_hi = y.astype(BF16)
    y_lo = (y - y_hi.astype(F32)).astype(BF16)
    y = _dot01(permt_ref[...], (y_hi, y_lo))
    o_ref[...] = _s5_glu(y, u, d_ref, wglu_ref, bglu_ref).reshape(nb, ct, S5_WIDTH).astype(BF16)


def _s5_weight_specs(layer):
    return [_layer_spec((S5_SLABS, LANES, 2 * 8 * S5_STATE), layer),
            _layer_spec((1, S5_LANES), layer), _layer_spec((1, S5_LANES), layer),
            _layer_spec((S5_SLABS, 8 * S5_STATE, LANES), layer), _layer_spec((S5_SLABS, 8 * S5_STATE, LANES), layer),
            _layer_spec((1, S5_WIDTH), layer), _layer_spec((S5_WIDTH, S5_WIDTH), layer),
            _layer_spec((1, S5_WIDTH), layer)]


def _s5_scan(su, weights, *, nb, length, layer):
    ct = S5_CHUNK
    perm = _time_major_perm(nb, ct)
    const = pl.BlockSpec((nb * ct, nb * ct), lambda i: (0, 0))
    return pl.pallas_call(
        functools.partial(_s5_scan_kernel, nb=nb, ct=ct),
        grid=(length // ct,),
        in_specs=[pl.BlockSpec((nb, ct, S5_WIDTH), lambda i: (0, i, 0)), const, const] + _s5_weight_specs(layer),
        out_specs=[pl.BlockSpec((nb, ct, S5_WIDTH), lambda i: (0, i, 0)),
                   pl.BlockSpec((nb, S5_LANES), lambda i: (0, 0)), pl.BlockSpec((nb, S5_LANES), lambda i: (0, 0))],
        out_shape=[jax.ShapeDtypeStruct((nb, length, S5_WIDTH), BF16),
                   jax.ShapeDtypeStruct((nb, S5_LANES), F32), jax.ShapeDtypeStruct((nb, S5_LANES), F32)],
        scratch_shapes=[pltpu.VMEM((nb * ct, S5_LANES), F32), pltpu.VMEM((nb * ct, S5_LANES), F32)],
        compiler_params=_cparams("arbitrary"),
        name="s5_scan",
    )(su, jnp.asarray(perm, BF16), jnp.asarray(perm.T, BF16), *weights)


def _s5_step_kernel(su_ref, h0r_ref, h0i_ref, win_ref, are_ref, aim_ref, cre_ref, cim_ref, d_ref, wglu_ref, bglu_ref,
                    o_ref, hre_ref, him_ref):
    u = su_ref[...]
    _s5_input(u.astype(BF16), win_ref, hre_ref, him_ref)
    ar, ai = are_ref[...], aim_ref[...]
    h0r, h0i = h0r_ref[...], h0i_ref[...]
    nr = ar * h0r - ai * h0i + hre_ref[...]
    ni = ar * h0i + ai * h0r + him_ref[...]
    hre_ref[...] = nr
    him_ref[...] = ni
    o_ref[...] = _s5_glu(_s5_readout(hre_ref, him_ref, cre_ref, cim_ref), u, d_ref, wglu_ref, bglu_ref).astype(BF16)


def _s5_step(su, h0r, h0i, weights, *, layer):
    n = su.shape[0]
    full = lambda w: pl.BlockSpec((n, w), lambda i: (0, 0))
    state = pl.BlockSpec((None, n, S5_LANES), lambda i: (layer, 0, 0))
    return pl.pallas_call(
        _s5_step_kernel,
        grid=(1,),
        in_specs=[full(S5_WIDTH), state, state] + _s5_weight_specs(layer),
        out_specs=[full(S5_WIDTH), full(S5_LANES), full(S5_LANES)],
        out_shape=[jax.ShapeDtypeStruct((n, S5_WIDTH), BF16),
                   jax.ShapeDtypeStruct((n, S5_LANES), F32), jax.ShapeDtypeStruct((n, S5_LANES), F32)],
        compiler_params=_cparams("arbitrary"),
        name="s5_step",
    )(su, h0r, h0i, *weights)


def _head_norm(o, g):
    parts = []
    for h in range(HEADS):
        seg = o[:, h * DV:(h + 1) * DV]
        parts.append(seg * lax.rsqrt(jnp.mean(seg * seg, axis=-1, keepdims=True) + EPS))
    return jnp.concatenate(parts, axis=-1) * g


def _route(logits, count):
    rows = logits.shape[0]
    lane = lax.broadcasted_iota(I32, logits.shape, 1)
    neg = -jnp.inf
    first = lambda hit: jnp.min(jnp.where(hit, lane, LANES), axis=-1, keepdims=True)
    glog = jnp.where(lane < MOE_GROUPS, logits, neg)
    gmax = jnp.max(glog, axis=-1, keepdims=True)
    gidx = first(glog == gmax)
    gw = 1.0 / jnp.sum(jnp.where(lane < MOE_GROUPS, jnp.exp(logits - gmax), 0.0), axis=-1, keepdims=True)
    inside = (lane >= MOE_GROUPS) & (lane < MOE_GROUPS + MOE_EXPERTS) & (((lane - MOE_GROUPS) >> 3) == gidx)
    el = jnp.where(inside, logits, neg)
    v1 = jnp.max(el, axis=-1, keepdims=True)
    i1 = first(el == v1)
    el2 = jnp.where(lane == i1, neg, el)
    v2 = jnp.max(el2, axis=-1, keepdims=True)
    i2 = first(el2 == v2)
    p2 = jnp.exp(v2 - v1)
    w1 = gw / (1.0 + p2)
    w2 = gw * p2 / (1.0 + p2)
    hit1, hit2 = lane == i1, lane == i2
    onehot = jnp.where(hit1 | hit2, 1.0, 0.0)
    tr = lax.broadcasted_iota(I32, (rows, rows), 0)
    tc = lax.broadcasted_iota(I32, (rows, rows), 1)
    before = _dot(jnp.where(tc < tr, 1.0, 0.0).astype(BF16), onehot.astype(BF16)) + count
    rank1 = jnp.sum(jnp.where(hit1, before, 0.0), axis=-1, keepdims=True)
    rank2 = jnp.sum(jnp.where(hit2, before, 0.0), axis=-1, keepdims=True)
    slab = jnp.zeros(logits.shape, F32)
    for ln, val in ((R_E1, (i1 - MOE_GROUPS).astype(F32)), (R_E2, (i2 - MOE_GROUPS).astype(F32)), (R_W1, w1), (R_W2, w2),
                    (R_RANK1, rank1), (R_RANK2, rank2)):
        slab = jnp.where(lane == ln, val, slab)
    return slab, count + jnp.sum(onehot, axis=0, keepdims=True)


def _merge_kernel(x_ref, o_ref, gate_ref, os5_ref, cnt0_ref, nm_ref, wgm_ref, gn_ref, wbr_ref, wout_ref, nf_ref,
                  wrh_ref, wrl_ref, br_ref, *rest):
    xm_ref, h2_ref, rt_ref, cnt_ref = rest[-4:]

    @pl.when(pl.program_id(0) == 0)
    def _():
        cnt_ref[...] = cnt0_ref[...]

    x = x_ref[...]
    hb = _rms(x, nm_ref[...]).astype(BF16)
    mixed = None
    for i, src in enumerate((0, None, 1)):
        if src is None:
            branch = os5_ref[...]
        else:
            branch = (_head_norm(o_ref[src].astype(F32), gn_ref[src]) * gate_ref[src].astype(F32)).astype(BF16)
        gate = _sigmoid(_dot(hb, wgm_ref[:, i * D_MODEL:(i + 1) * D_MODEL]))
        term = gate * _dot(branch, wbr_ref[i])
        mixed = term if mixed is None else mixed + term
    xm = x + _dot(mixed.astype(BF16), wout_ref[...])
    xm_ref[...] = xm
    h2 = _rms(xm, nf_ref[...])
    h2_hi = h2.astype(BF16)
    h2_lo = (h2 - h2_hi.astype(F32)).astype(BF16)
    h2_ref[...] = h2_hi
    logits = _dot(h2_hi, wrh_ref[...]) + (_dot(h2_hi, wrl_ref[...]) + _dot(h2_lo, wrh_ref[...])) + br_ref[...]
    rt_ref[...], cnt_ref[...] = _route(logits, cnt_ref[...])


def _merge(x, o, gate, os5, cnt0, weights, carried, *, layer, tm, total_rows, block_offset):
    rows = x.shape[0]
    tile = lambda w: pl.BlockSpec((tm, w), lambda i: (i, 0))
    shared = lambda w: pl.BlockSpec((tm, w), lambda i: (i + block_offset, 0))
    carried = [] if carried is None else list(carried)
    w_specs = [_layer_spec((1, D_MODEL), layer), _layer_spec((D_MODEL, 3 * D_MODEL), layer),
               _layer_spec((2, 1, VW), layer), _layer_spec((3, VW, D_MODEL), layer),
               _layer_spec((D_MODEL, D_MODEL), layer), _layer_spec((1, D_MODEL), layer),
               _layer_spec((D_MODEL, LANES), layer), _layer_spec((D_MODEL, LANES), layer),
               _layer_spec((1, LANES), layer)]
    first = 5 + len(w_specs)
    return pl.pallas_call(
        _merge_kernel,
        grid=(rows // tm,),
        in_specs=[tile(D_MODEL),
                  pl.BlockSpec((2, tm, VW), lambda i: (0, i, 0)),
                  pl.BlockSpec((2, tm, VW), lambda i: (0, i, 0)),
                  tile(S5_WIDTH),
                  pl.BlockSpec((1, LANES), lambda i: (0, 0))] + w_specs + [pl.BlockSpec(memory_space=pl.ANY)] * len(carried),
        out_specs=[tile(D_MODEL), shared(D_MODEL), shared(LANES), pl.BlockSpec((1, LANES), lambda i: (0, 0))],
        out_shape=[jax.ShapeDtypeStruct((rows, D_MODEL), F32),
                   jax.ShapeDtypeStruct((total_rows, D_MODEL), BF16),
                   jax.ShapeDtypeStruct((total_rows, LANES), F32),
                   jax.ShapeDtypeStruct((1, LANES), F32)],
        input_output_aliases={first + j: 1 + j for j in range(len(carried))},
        compiler_params=_cparams("arbitrary"),
        name="merge_route",
    )(x, o, gate, os5, cnt0, *weights, *carried)


def _ffn_kernel(te_ref, nu_ref, x_ref, wg_ref, wu_ref, wd_ref, y_ref):
    @pl.when(pl.program_id(0) < nu_ref[0])
    def _():
        x = x_ref[...]
        act = _silu(_dot(x, wg_ref[...])) * _dot(x, wu_ref[...])
        y_ref[...] = _dot(act.astype(BF16), wd_ref[...]).astype(BF16)


def _grouped_ffn(tile_expert, n_used, xg, wg, wu, wd, *, layer):
    rows = xg.shape[0]
    expert = lambda a, b: pl.BlockSpec((None, None, a, b), lambda i, te, nu: (layer, te[i], 0, 0))
    grid_spec = pltpu.PrefetchScalarGridSpec(
        num_scalar_prefetch=2,
        grid=(rows // MOE_TILE,),
        in_specs=[pl.BlockSpec((MOE_TILE, D_MODEL), lambda i, te, nu: (i, 0)),
                  expert(D_MODEL, MOE_FF), expert(D_MODEL, MOE_FF), expert(MOE_FF, D_MODEL)],
        out_specs=pl.BlockSpec((MOE_TILE, D_MODEL), lambda i, te, nu: (i, 0)),
    )
    return pl.pallas_call(
        _ffn_kernel,
        grid_spec=grid_spec,
        out_shape=jax.ShapeDtypeStruct((rows, D_MODEL), BF16),
        compiler_params=_cparams("arbitrary"),
        name="expert_ffn",
    )(tile_expert, n_used, xg, wg, wu, wd)


@compute_on("tpu_sparsecore")
@jax.jit
def _take_rows(x, idx):
    return jnp.take(x, idx, axis=0, mode="clip")


def _moe_plan(route, count):
    tokens = route.shape[0]
    n_tiles = (2 * tokens + MOE_EXPERTS * MOE_TILE) // MOE_TILE
    n_rows = n_tiles * MOE_TILE
    counts = count[0, MOE_GROUPS:MOE_GROUPS + MOE_EXPERTS].astype(I32)
    padded = ((counts + MOE_TILE - 1) // MOE_TILE) * MOE_TILE
    gend = jnp.cumsum(padded)
    gstart = gend - padded
    e = route[:, R_E1:R_E2 + 1].astype(I32)
    rank = route[:, R_RANK1:R_RANK2 + 1].astype(I32)
    onehot = e[:, :, None] == jnp.arange(MOE_EXPERTS, dtype=I32)[None, None, :]
    pos = jnp.sum(jnp.where(onehot, gstart[None, None, :], 0), axis=-1) + rank
    token = jnp.broadcast_to(jnp.arange(tokens, dtype=I32)[:, None], (tokens, 2))
    marked = jnp.zeros((n_rows,), I32).at[pos.reshape(-1)].add(token.reshape(-1) + 1, unique_indices=True)
    filler = jnp.arange(n_rows, dtype=I32) % tokens
    src = jnp.where(marked > 0, marked - 1, filler)
    tile_start = jnp.arange(n_tiles, dtype=I32) * MOE_TILE
    tile_expert = jnp.minimum(jnp.sum((tile_start[:, None] >= gend[None, :]).astype(I32), axis=1), MOE_EXPERTS - 1)
    n_used = (gend[-1] // MOE_TILE).astype(I32).reshape(1)
    return src, pos, tile_expert, n_used


def _ple_kernel(xm_ref, y1_ref, y2_ref, rt_ref, p_ref, np_ref, wg_ref, wp_ref, nfin_ref, out_ref, *, final):
    rt = rt_ref[...]
    x1 = xm_ref[...] + rt[:, R_W1:R_W1 + 1] * y1_ref[...].astype(F32) + rt[:, R_W2:R_W2 + 1] * y2_ref[...].astype(F32)
    gate = _sigmoid(_dot(_rms(x1, np_ref[...]).astype(BF16), wg_ref[...]))
    x2 = x1 + gate * _dot(p_ref[...].astype(BF16), wp_ref[...])
    out_ref[...] = _rms(x2, nfin_ref[...]) if final else x2


def _ple(xm, y1, y2, route, p, npl, wg, wp, nfin, *, layer, final, tm, block_offset):
    rows = xm.shape[0]
    tile = lambda w: pl.BlockSpec((tm, w), lambda i: (i, 0))
    shared = lambda w: pl.BlockSpec((tm, w), lambda i: (i + block_offset, 0))
    return pl.pallas_call(
        functools.partial(_ple_kernel, final=final),
        grid=(rows // tm,),
        in_specs=[tile(D_MODEL), shared(D_MODEL), shared(D_MODEL), shared(LANES),
                  pl.BlockSpec((None, tm, PLE_DIM), lambda i: (layer, i, 0)),
                  _layer_spec((1, D_MODEL), layer), _layer_spec((D_MODEL, D_MODEL), layer),
                  _layer_spec((PLE_DIM, D_MODEL), layer), pl.BlockSpec((1, D_MODEL), lambda i: (0, 0))],
        out_specs=tile(D_MODEL),
        out_shape=jax.ShapeDtypeStruct((rows, D_MODEL), F32),
        compiler_params=_cparams("parallel"),
        name="combine_ple",
    )(xm, y1, y2, route, p, npl, wg, wp, nfin)


def kernel(x_prompt, x_sample, state_gla, state_s5_re, state_s5_im, state_hgrn, p_prompt, p_sample, norm_mix, w_in, gla_w_gate2, gla_b_gate, gla_norm, s5_a_re, s5_a_im, s5_log_dt, s5_b_re, s5_b_im, s5_c_re, s5_c_im, s5_d, s5_w_glu, s5_b_glu, hgrn_lb_logits, hgrn_norm, w_br_gla, w_br_s5, w_br_hgrn, w_out, norm_ffn, moe_w_group, moe_b_group, moe_w_expert, moe_b_expert, moe_w_gate, moe_w_up, moe_w_down, norm_ple, w_ple_gate, w_ple_proj, norm_final):
    nb, length, _ = x_prompt.shape
    ns = x_sample.shape[0]
    n_p = nb * length
    n_all = n_p + ns
    row = lambda t: t.reshape(DEPTH, 1, -1)

    o1, o2, o3, o4 = 1536, 1552, 2064, 3600
    w_a = jnp.concatenate([w_in[:, :, :o1], w_in[:, :, o2:o4], w_in[:, :, o1:o2],
                           jnp.zeros((DEPTH, D_MODEL, LANES - GLA_RANK), F32)], axis=-1).astype(BF16)
    wg2 = jnp.concatenate([gla_w_gate2, jnp.zeros((DEPTH, LANES - GLA_RANK, QK), F32)], axis=1).astype(BF16)
    s5_win, s5_are, s5_aim = _s5_discretise(s5_a_re, s5_a_im, s5_log_dt, s5_b_re, s5_b_im)
    s5_w = (s5_win, s5_are, s5_aim, _s5_blockdiag_out(s5_c_re), _s5_blockdiag_out(s5_c_im), row(s5_d),
            s5_w_glu.astype(BF16), row(s5_b_glu))
    head_gain = jnp.stack([jnp.tile(gla_norm, (1, HEADS)), jnp.tile(hgrn_norm, (1, HEADS))], axis=1).reshape(DEPTH, 2, 1, VW)
    w_router = jnp.concatenate([moe_w_group, moe_w_expert,
                                jnp.zeros((DEPTH, D_MODEL, LANES - MOE_GROUPS - MOE_EXPERTS), F32)], axis=-1)
    wr_hi = w_router.astype(BF16)
    wr_lo = (w_router - wr_hi.astype(F32)).astype(BF16)
    b_router = jnp.concatenate([moe_b_group, moe_b_expert,
                                jnp.zeros((DEPTH, LANES - MOE_GROUPS - MOE_EXPERTS), F32)], axis=-1).reshape(DEPTH, 1, LANES)
    merge_w = (row(norm_mix), w_in[:, :, o4:].astype(BF16), head_gain,
               jnp.stack([w_br_gla, w_br_s5, w_br_hgrn], axis=1).astype(BF16), w_out.astype(BF16), row(norm_ffn),
               wr_hi, wr_lo, b_router)
    wm_gate, wm_up, wm_down = moe_w_gate.astype(BF16), moe_w_up.astype(BF16), moe_w_down.astype(BF16)
    ple_w = (row(norm_ple), w_ple_gate.astype(BF16), w_ple_proj.astype(BF16), norm_final.reshape(1, D_MODEL))
    nm, bg = row(norm_mix), row(gla_b_gate)
    p_p, p_s = p_prompt.reshape(DEPTH, n_p, PLE_DIM), p_sample.reshape(DEPTH, ns, PLE_DIM)
    sg_in, sh_in = state_gla.reshape(DEPTH, ns, QK, DV), state_hgrn.reshape(DEPTH, ns, QK, DV)
    s5r_in, s5i_in = state_s5_re.reshape(DEPTH, ns, S5_LANES), state_s5_im.reshape(DEPTH, ns, S5_LANES)

    xp = x_prompt.reshape(n_p, D_MODEL)
    xs = x_sample.reshape(ns, D_MODEL)
    new_p, new_s5 = [], []
    new_s = None
    for i in range(DEPTH):
        qk, v, gate, g, su = _inproj(xp, nm, w_a, wg2, bg, hgrn_lb_logits, layer=i, tm=512)
        o_p, st_p = _chunk_scan(qk.reshape(2, nb, length, -1), v.reshape(2, nb, length, -1),
                                g.reshape(2, nb, length, -1), nb=nb, length=length)
        os5_p, hre_p, him_p = _s5_scan(su.reshape(nb, length, -1), s5_w, nb=nb, length=length, layer=i)
        xm_p, h2, rt, cnt = _merge(xp, o_p.reshape(2, n_p, VW), gate, os5_p.reshape(n_p, -1), jnp.zeros((1, LANES), F32),
                                   merge_w, None, layer=i, tm=256, total_rows=n_all, block_offset=0)
        new_p.append((st_p[0].reshape(nb, HEADS, DK, DV), hre_p.reshape(nb, S5_GROUPS, S5_STATE),
                      him_p.reshape(nb, S5_GROUPS, S5_STATE), st_p[1].reshape(nb, HEADS, DK, DV)))

        qk, v, gate, g, su = _inproj(xs, nm, w_a, wg2, bg, hgrn_lb_logits, layer=i, tm=ns)
        o_s, *new_s = _decode(qk, v, g, sg_in, sh_in, new_s, layer=i)
        os5_s, hre_s, him_s = _s5_step(su, s5r_in, s5i_in, s5_w, layer=i)
        xm_s, h2, rt, cnt = _merge(xs, o_s, gate, os5_s, cnt, merge_w, (h2, rt), layer=i, tm=ns, total_rows=n_all,
                                   block_offset=n_p // ns)
        new_s5.append((hre_s.reshape(ns, S5_GROUPS, S5_STATE), him_s.reshape(ns, S5_GROUPS, S5_STATE)))

        src, pos, tile_expert, n_used = _moe_plan(rt, cnt)
        y = _grouped_ffn(tile_expert, n_used, _take_rows(h2, src), wm_gate, wm_up, wm_down, layer=i)
        y1, y2 = _take_rows(y, pos[:, 0]), _take_rows(y, pos[:, 1])

        final = i == DEPTH - 1
        xp = _ple(xm_p, y1, y2, rt, p_p, *ple_w, layer=i, final=final, tm=512, block_offset=0)
        xs = _ple(xm_s, y1, y2, rt, p_s, *ple_w, layer=i, final=final, tm=ns, block_offset=n_p // ns)

    stack = lambda items, j: jnp.stack([it[j] for it in items])
    return (xp.reshape(nb, length, D_MODEL), xs.reshape(ns, 1, D_MODEL),
            stack(new_p, 0), stack(new_p, 1), stack(new_p, 2), stack(new_p, 3),
            new_s[0].reshape(DEPTH, ns, HEADS, DK, DV), stack(new_s5, 0), stack(new_s5, 1),
            new_s[1].reshape(DEPTH, ns, HEADS, DK, DV))
```

```python
import functools

import jax
import jax.numpy as jnp
import numpy as np
from jax import lax
from jax.experimental import pallas as pl
from jax.experimental.compute_on import compute_on
from jax.experimental.pallas import tpu as pltpu

F32, BF16, I32 = jnp.float32, jnp.bfloat16, jnp.int32

D_MODEL = 1024
DEPTH = 2
HEADS, DK, DV = 4, 64, 128
QK, VW = HEADS * DK, HEADS * DV
GLA_RANK, GLA_TAU = 16, 16.0
S5_WIDTH, S5_GROUP, S5_GROUPS, S5_STATE = 512, 16, 32, 64
S5_LANES = S5_GROUPS * S5_STATE
S5_SLABS = 4
MOE_GROUPS, MOE_PER_GROUP, MOE_EXPERTS, MOE_FF = 4, 8, 32, 256
PLE_DIM = 256
EPS = 1e-6

LANES = 128
CHUNK = 64
S5_CHUNK = 64
MOE_TILE = 256
SAFE_EXP = 80.0
VMEM_LIMIT = 56 * 1024 * 1024

C_GLA, C_S5, C_HG, C_GLR, C_END = 0, 1536, 2048, 3584, 3712
R_E1, R_E2, R_W1, R_W2, R_RANK1, R_RANK2 = 0, 1, 2, 3, 4, 5


def _cparams(*sem):
    return pltpu.CompilerParams(dimension_semantics=sem, vmem_limit_bytes=VMEM_LIMIT)


def _layer_spec(shape, layer):
    return pl.BlockSpec((None,) + tuple(shape), lambda *_: (layer,) + (0,) * len(shape))


def _dot(a, b):
    return jnp.dot(a, b, preferred_element_type=F32)


def _rms(x, g):
    return x * lax.rsqrt(jnp.mean(x * x, axis=-1, keepdims=True) + EPS) * g


def _log_sigmoid(x):
    return jnp.minimum(x, 0.0) - jnp.log1p(jnp.exp(-jnp.abs(x)))


def _sigmoid(x):
    return 1.0 / (1.0 + jnp.exp(-x))


def _silu(x):
    return x * _sigmoid(x)


def _split3(x):
    hi = x.astype(BF16)
    r1 = x - hi.astype(F32)
    mid = r1.astype(BF16)
    lo = (r1 - mid.astype(F32)).astype(BF16)
    return hi, mid, lo


def _dot01(m, parts):
    out = _dot(m, parts[0])
    for p in parts[1:]:
        out = out + _dot(m, p)
    return out


def _inproj_kernel(x_ref, nm_ref, w_ref, wg2_ref, bg_ref, lbl_ref, qk_ref, v_ref, gate_ref, g_ref, su_ref, *, layer):
    hb = _rms(x_ref[...], nm_ref[...]).astype(BF16)

    def proj(lo, hi):
        return _dot(hb, w_ref[:, lo:hi])

    qk_ref[0, :, 0:QK] = (proj(0, 256) * (DK ** -0.5)).astype(BF16)
    qk_ref[0, :, QK:2 * QK] = proj(256, 512).astype(BF16)
    v_ref[0] = proj(512, 1024).astype(BF16)
    gate_ref[0] = _silu(proj(1024, 1536)).astype(BF16)
    glr = proj(C_GLR, C_END).astype(BF16)
    g_ref[0] = _log_sigmoid(_dot(glr, wg2_ref[...]) + bg_ref[...]) * (1.0 / GLA_TAU)

    su_ref[...] = proj(C_S5, C_HG)

    lg = lbl_ref[...]
    mx = jnp.max(lg, axis=0, keepdims=True)
    ex = jnp.exp(lg - mx)
    sm = ex / jnp.sum(ex, axis=0, keepdims=True)
    cs = sm[0:1]
    for j in range(1, layer + 1):
        cs = cs + sm[j:j + 1]
    lb = cs - sm[0:1]
    log_lb, log_1mlb = jnp.log(lb), jnp.log1p(-lb)
    z = proj(C_HG + 256, C_HG + 512)
    a, c = log_lb, log_1mlb + _log_sigmoid(z)
    g_ref[1] = jnp.maximum(a, c) + jnp.log1p(jnp.exp(-jnp.abs(a - c)))
    qk_ref[1, :, 0:QK] = _silu(proj(C_HG, C_HG + 256)).astype(BF16)
    qk_ref[1, :, QK:2 * QK] = ((1.0 - lb) * _sigmoid(-z)).astype(BF16)
    v_ref[1] = proj(C_HG + 512, C_HG + 1024).astype(BF16)
    gate_ref[1] = _silu(proj(C_HG + 1024, C_HG + 1536)).astype(BF16)


def _inproj(x, nm, w, wg2, bg, lbl, *, layer, tm):
    rows = x.shape[0]
    return pl.pallas_call(
        functools.partial(_inproj_kernel, layer=layer),
        grid=(rows // tm,),
        in_specs=[pl.BlockSpec((tm, D_MODEL), lambda i: (i, 0)),
                  _layer_spec((1, D_MODEL), layer),
                  _layer_spec((D_MODEL, C_END), layer),
                  _layer_spec((LANES, QK), layer),
                  _layer_spec((1, QK), layer),
                  pl.BlockSpec((DEPTH, QK), lambda i: (0, 0))],
        out_specs=[pl.BlockSpec((2, tm, 2 * QK), lambda i: (0, i, 0)),
                   pl.BlockSpec((2, tm, VW), lambda i: (0, i, 0)),
                   pl.BlockSpec((2, tm, VW), lambda i: (0, i, 0)),
                   pl.BlockSpec((2, tm, QK), lambda i: (0, i, 0)),
                   pl.BlockSpec((tm, S5_WIDTH), lambda i: (i, 0))],
        out_shape=[jax.ShapeDtypeStruct((2, rows, 2 * QK), BF16),
                   jax.ShapeDtypeStruct((2, rows, VW), BF16),
                   jax.ShapeDtypeStruct((2, rows, VW), BF16),
                   jax.ShapeDtypeStruct((2, rows, QK), F32),
                   jax.ShapeDtypeStruct((rows, S5_WIDTH), F32)],
        compiler_params=_cparams("parallel"),
        name="inproj",
    )(x, nm, w, wg2, bg, lbl)


def _head_stack(x):
    head = lax.broadcasted_iota(I32, x.shape, 1) // DK
    return jnp.concatenate([jnp.where(head == h, x, 0.0) for h in range(HEADS)], axis=0).astype(BF16)


def _stack_scores(qt, kt):
    return lax.dot_general(_head_stack(qt), kt.astype(BF16), (((1,), (1,)), ((), ())), preferred_element_type=F32)


def _chunk_kernel(qk_ref, v_ref, g_ref, o_ref, st_ref, bc_ref, sc_ref, *, nb, c):
    @pl.when(pl.program_id(1) == 0)
    def _():
        st_ref[...] = jnp.zeros_like(st_ref)

    row = lax.broadcasted_iota(I32, (c, c), 0)
    col = lax.broadcasted_iota(I32, (c, c), 1)
    tri = jnp.where(col <= row, 1.0, 0.0).astype(BF16)
    srow = lax.broadcasted_iota(I32, (HEADS * c, c), 0) & (c - 1)
    scol = lax.broadcasted_iota(I32, (HEADS * c, c), 1)
    mid = c // 2 - 1

    spread = None
    for b in range(nb):
        bc = _dot01(tri, _split3(g_ref[0, b]))
        bc_ref[b] = bc
        ref, last = bc[mid:mid + 1, :], bc[c - 1:c, :]
        s = jnp.maximum(jnp.max(-ref), jnp.max(ref - last))
        spread = s if spread is None else jnp.maximum(spread, s)

    def qk_of(b):
        return qk_ref[0, b, :, 0:QK].astype(F32), qk_ref[0, b, :, QK:2 * QK].astype(F32)

    def scores_one_reference():
        for b in range(nb):
            q, k = qk_of(b)
            bc = bc_ref[b]
            ref = bc[mid:mid + 1, :]
            s = _stack_scores(q * jnp.exp(bc - ref), k * jnp.exp(ref - bc))
            sc_ref[b] = jnp.where(scol <= srow, s, 0.0).astype(BF16)

    def scores_by_levels():
        qrow = lax.broadcasted_iota(I32, (c, QK), 0)
        for b in range(nb):
            q, k = qk_of(b)
            bc = bc_ref[b]
            parts = _split3(g_ref[0, b])
            acc = jnp.where(scol == srow, _stack_scores(q, k), 0.0)
            half = c // 2
            while half >= 1:
                blk = 2 * half
                last_low = (row & ~(blk - 1)) + (half - 1)
                ref = _dot01(jnp.where(col <= last_low, 1.0, 0.0).astype(BF16), parts)
                upper = (qrow & (blk - 1)) >= half
                dq = jnp.minimum(jnp.where(upper, bc - ref, 0.0), 0.0)
                dk = jnp.minimum(jnp.where(upper, 0.0, ref - bc), 0.0)
                s = _stack_scores(q * jnp.exp(dq), k * jnp.exp(dk))
                pair = ((srow & ~(blk - 1)) == (scol & ~(blk - 1))) & ((srow & (blk - 1)) >= half) & ((scol & (blk - 1)) < half)
                acc = acc + jnp.where(pair, s, 0.0)
                half //= 2
            sc_ref[b] = acc.astype(BF16)

    lax.cond(spread <= SAFE_EXP, scores_one_reference, scores_by_levels)

    for b in range(nb):
        q, k = qk_of(b)
        bc = bc_ref[b]
        last = bc[c - 1:c, :]
        v = v_ref[0, b]
        state = st_ref[0, b]
        o_inter = _dot(_head_stack(q * jnp.exp(bc)), state.astype(BF16))
        k_out = jnp.transpose(k * jnp.exp(last - bc)).astype(BF16)
        decay = jnp.transpose(jnp.broadcast_to(jnp.exp(last), (DV, QK)))
        for h in range(HEADS):
            rs, ls, ks = slice(h * c, (h + 1) * c), slice(h * DV, (h + 1) * DV), slice(h * DK, (h + 1) * DK)
            o_ref[0, b, :, ls] = (_dot(sc_ref[b, rs, :], v[:, ls]) + o_inter[rs, :]).astype(BF16)
            st_ref[0, b, ks, :] = decay[ks, :] * state[ks, :] + _dot(k_out[ks, :], v[:, ls])


def _chunk_scan(qk, v, g, *, nb, length):
    c = CHUNK
    blk = lambda w: pl.BlockSpec((1, nb, c, w), lambda br, i: (br, 0, i, 0))
    return pl.pallas_call(
        functools.partial(_chunk_kernel, nb=nb, c=c),
        grid=(2, length // c),
        in_specs=[blk(2 * QK), blk(VW), blk(QK)],
        out_specs=[blk(VW), pl.BlockSpec((1, nb, QK, DV), lambda br, i: (br, 0, 0, 0))],
        out_shape=[jax.ShapeDtypeStruct((2, nb, length, VW), BF16),
                   jax.ShapeDtypeStruct((2, nb, QK, DV), F32)],
        scratch_shapes=[pltpu.VMEM((nb, c, QK), F32), pltpu.VMEM((nb, HEADS * c, c), BF16)],
        compiler_params=_cparams("arbitrary", "arbitrary"),
        name="chunk_scan",
    )(qk, v, g)


def _decode_kernel(qk_ref, v_ref, g_ref, s0_ref, s1_ref, *rest, nt):
    o_ref, n0_ref, n1_ref = rest[-3:]
    for br, (s_ref, n_ref) in enumerate(((s0_ref, n0_ref), (s1_ref, n1_ref))):
        for j in range(nt):
            d = jnp.exp(g_ref[br, j:j + 1, :])
            q = qk_ref[br, j:j + 1, 0:QK].astype(F32)
            k = qk_ref[br, j:j + 1, QK:2 * QK].astype(F32)
            cols = jnp.transpose(jnp.concatenate([d, k, q, jnp.zeros((5, QK), F32)], axis=0))
            vrow = v_ref[br, j:j + 1, :].astype(F32)
            vfull = jnp.concatenate([jnp.broadcast_to(vrow[:, h * DV:(h + 1) * DV], (DK, DV)) for h in range(HEADS)], axis=0)
            new = cols[:, 0:1] * s_ref[j] + cols[:, 1:2] * vfull
            n_ref[j] = new
            t = cols[:, 2:3] * new
            for h in range(HEADS):
                o_ref[br, j:j + 1, h * DV:(h + 1) * DV] = jnp.sum(t[h * DK:(h + 1) * DK, :], axis=0, keepdims=True).astype(BF16)


def _decode(qk, v, g, s_gla, s_hg, prev, *, layer):
    n = qk.shape[1]
    nt = 8
    row = lambda w: pl.BlockSpec((2, nt, w), lambda i: (0, i, 0))
    st = pl.BlockSpec((None, nt, QK, DV), lambda i: (layer, i, 0, 0))
    carried = [] if prev is None else list(prev)
    first = 5
    return pl.pallas_call(
        functools.partial(_decode_kernel, nt=nt),
        grid=(n // nt,),
        in_specs=[row(2 * QK), row(VW), row(QK), st, st] + [pl.BlockSpec(memory_space=pl.ANY)] * len(carried),
        out_specs=[row(VW), st, st],
        out_shape=[jax.ShapeDtypeStruct((2, n, VW), BF16),
                   jax.ShapeDtypeStruct((DEPTH, n, QK, DV), F32),
                   jax.ShapeDtypeStruct((DEPTH, n, QK, DV), F32)],
        input_output_aliases={first + j: 1 + j for j in range(len(carried))},
        compiler_params=_cparams("parallel"),
        name="decode_step",
    )(qk, v, g, s_gla, s_hg, *carried)


def _s5_disc_kernel(lr_ref, li_ref, ldt_ref, br_ref, bi_ref, abr_ref, abi_ref, bbr_ref, bbi_ref):
    lr, li, dt = lr_ref[...], li_ref[...], jnp.exp(ldt_ref[...])
    mag = jnp.exp(lr * dt)
    ab_re, ab_im = mag * jnp.cos(li * dt), mag * jnp.sin(li * dt)
    den = lr * lr + li * li
    num_re = ab_re - 1.0
    coef_re = (num_re * lr + ab_im * li) / den
    coef_im = (ab_im * lr - num_re * li) / den
    br, bi = br_ref[...], bi_ref[...]
    abr_ref[...] = ab_re
    abi_ref[...] = ab_im
    bbr_ref[...] = coef_re * br - coef_im * bi
    bbi_ref[...] = coef_re * bi + coef_im * br


def _s5_discretise(a_re, a_im, log_dt, b_re, b_im):
    n = DEPTH * S5_GROUPS
    rep = lambda t: jnp.repeat(t.reshape(n, S5_STATE), S5_GROUP, axis=1)
    ldt = jnp.broadcast_to(log_dt.reshape(n, 1), (n, S5_STATE * S5_GROUP))
    shape = jax.ShapeDtypeStruct((n, S5_STATE * S5_GROUP), F32)
    ab_re, ab_im, bb_re, bb_im = pl.pallas_call(_s5_disc_kernel, out_shape=[shape] * 4, name="s5_discretise")(
        rep(a_re), rep(a_im), ldt, b_re.reshape(n, -1), b_im.reshape(n, -1))
    pole = lambda t: t[:, ::S5_GROUP].reshape(DEPTH, 1, S5_LANES)
    eye = jnp.eye(8, dtype=F32)

    def blockdiag_in(bb):
        t = bb.reshape(DEPTH, S5_SLABS, 8, S5_STATE, S5_GROUP).transpose(0, 1, 2, 4, 3)
        return jnp.einsum("dcgmp,gh->dcgmhp", t, eye).reshape(DEPTH, S5_SLABS, LANES, 8 * S5_STATE)

    w_in = jnp.concatenate([blockdiag_in(bb_re), blockdiag_in(bb_im)], axis=-1).astype(BF16)
    return w_in, pole(ab_re), pole(ab_im)


def _s5_blockdiag_out(c):
    t = c.reshape(DEPTH, S5_SLABS, 8, S5_GROUP, S5_STATE).transpose(0, 1, 2, 4, 3)
    return jnp.einsum("dcgpm,gh->dcgphm", t, jnp.eye(8, dtype=F32)).reshape(DEPTH, S5_SLABS, 8 * S5_STATE, LANES).astype(BF16)


def _time_major_perm(nb, ct):
    r = np.arange(nb * ct)
    p = np.zeros((nb * ct, nb * ct), np.float32)
    p[r, (r % nb) * ct + r // nb] = 1.0
    return p


def _gelu_tanh(y):
    return 0.5 * y * (1.0 + jnp.tanh(0.7978845608028654 * (y + 0.044715 * (y * y * y))))


def _s5_input(ub, win_ref, xre_ref, xim_ref):
    half = 8 * S5_STATE
    for s in range(S5_SLABS):
        r = _dot(ub[:, s * LANES:(s + 1) * LANES], win_ref[s])
        xre_ref[:, s * half:(s + 1) * half] = r[:, :half]
        xim_ref[:, s * half:(s + 1) * half] = r[:, half:]


def _s5_readout(xre_ref, xim_ref, cre_ref, cim_ref):
    half = 8 * S5_STATE
    ys = []
    for s in range(S5_SLABS):
        ls = slice(s * half, (s + 1) * half)
        ys.append(_dot(xre_ref[:, ls].astype(BF16), cre_ref[s]) - _dot(xim_ref[:, ls].astype(BF16), cim_ref[s]))
    return jnp.concatenate(ys, axis=-1)


def _s5_glu(y, u, d_ref, wglu_ref, bglu_ref):
    z = _gelu_tanh(y + d_ref[...] * u)
    return z * _sigmoid(_dot(z.astype(BF16), wglu_ref[...]) + bglu_ref[...])


def _s5_scan_kernel(su_ref, perm_ref, permt_ref, win_ref, are_ref, aim_ref, cre_ref, cim_ref, d_ref, wglu_ref, bglu_ref,
                    o_ref, hre_ref, him_ref, xre_ref, xim_ref, *, nb, ct):
    @pl.when(pl.program_id(0) == 0)
    def _():
        hre_ref[...] = jnp.zeros_like(hre_ref)
        him_ref[...] = jnp.zeros_like(him_ref)

    u = su_ref[...].reshape(nb * ct, S5_WIDTH)
    _s5_input(_dot(perm_ref[...], u.astype(BF16)).astype(BF16), win_ref, xre_ref, xim_ref)

    width = 512
    for lc in range(S5_LANES // width):
        ls = slice(lc * width, (lc + 1) * width)
        ar = jnp.broadcast_to(are_ref[:, ls], (nb, width))
        ai = jnp.broadcast_to(aim_ref[:, ls], (nb, width))

        def step(t, carry):
            hr, hi = carry
            rows = pl.ds(pl.multiple_of(t * nb, nb), nb)
            nr = ar * hr - ai * hi + xre_ref[rows, ls]
            ni = ar * hi + ai * hr + xim_ref[rows, ls]
            xre_ref[rows, ls] = nr
            xim_ref[rows, ls] = ni
            return nr, ni

        hr, hi = lax.fori_loop(0, ct, step, (hre_ref[:, ls], him_ref[:, ls]), unroll=4)
        hre_ref[:, ls] = hr
        him_ref[:, ls] = hi

    y = _s5_readout(xre_ref, xim_ref, cre_ref, cim_ref)
    y_hi = y.astype(BF16)
    y_lo = (y - y_hi.astype(F32)).astype(BF16)
    y = _dot01(permt_ref[...], (y_hi, y_lo))
    o_ref[...] = _s5_glu(y, u, d_ref, wglu_ref, bglu_ref).reshape(nb, ct, S5_WIDTH).astype(BF16)


def _s5_weight_specs(layer):
    return [_layer_spec((S5_SLABS, LANES, 2 * 8 * S5_STATE), layer),
            _layer_spec((1, S5_LANES), layer), _layer_spec((1, S5_LANES), layer),
            _layer_spec((S5_SLABS, 8 * S5_STATE, LANES), layer), _layer_spec((S5_SLABS, 8 * S5_STATE, LANES), layer),
            _layer_spec((1, S5_WIDTH), layer), _layer_spec((S5_WIDTH, S5_WIDTH), layer),
            _layer_spec((1, S5_WIDTH), layer)]


def _s5_scan(su, weights, *, nb, length, layer):
    ct = S5_CHUNK
    perm = _time_major_perm(nb, ct)
    const = pl.BlockSpec((nb * ct, nb * ct), lambda i: (0, 0))
    return pl.pallas_call(
        functools.partial(_s5_scan_kernel, nb=nb, ct=ct),
        grid=(length // ct,),
        in_specs=[pl.BlockSpec((nb, ct, S5_WIDTH), lambda i: (0, i, 0)), const, const] + _s5_weight_specs(layer),
        out_specs=[pl.BlockSpec((nb, ct, S5_WIDTH), lambda i: (0, i, 0)),
                   pl.BlockSpec((nb, S5_LANES), lambda i: (0, 0)), pl.BlockSpec((nb, S5_LANES), lambda i: (0, 0))],
        out_shape=[jax.ShapeDtypeStruct((nb, length, S5_WIDTH), BF16),
                   jax.ShapeDtypeStruct((nb, S5_LANES), F32), jax.ShapeDtypeStruct((nb, S5_LANES), F32)],
        scratch_shapes=[pltpu.VMEM((nb * ct, S5_LANES), F32), pltpu.VMEM((nb * ct, S5_LANES), F32)],
        compiler_params=_cparams("arbitrary"),
        name="s5_scan",
    )(su, jnp.asarray(perm, BF16), jnp.asarray(perm.T, BF16), *weights)


def _s5_step_kernel(su_ref, h0r_ref, h0i_ref, win_ref, are_ref, aim_ref, cre_ref, cim_ref, d_ref, wglu_ref, bglu_ref,
                    o_ref, hre_ref, him_ref):
    u = su_ref[...]
    _s5_input(u.astype(BF16), win_ref, hre_ref, him_ref)
    ar, ai = are_ref[...], aim_ref[...]
    h0r, h0i = h0r_ref[...], h0i_ref[...]
    nr = ar * h0r - ai * h0i + hre_ref[...]
    ni = ar * h0i + ai * h0r + him_ref[...]
    hre_ref[...] = nr
    him_ref[...] = ni
    o_ref[...] = _s5_glu(_s5_readout(hre_ref, him_ref, cre_ref, cim_ref), u, d_ref, wglu_ref, bglu_ref).astype(BF16)


def _s5_step(su, h0r, h0i, weights, *, layer):
    n = su.shape[0]
    full = lambda w: pl.BlockSpec((n, w), lambda i: (0, 0))
    state = pl.BlockSpec((None, n, S5_LANES), lambda i: (layer, 0, 0))
    return pl.pallas_call(
        _s5_step_kernel,
        grid=(1,),
        in_specs=[full(S5_WIDTH), state, state] + _s5_weight_specs(layer),
        out_specs=[full(S5_WIDTH), full(S5_LANES), full(S5_LANES)],
        out_shape=[jax.ShapeDtypeStruct((n, S5_WIDTH), BF16),
                   jax.ShapeDtypeStruct((n, S5_LANES), F32), jax.ShapeDtypeStruct((n, S5_LANES), F32)],
        compiler_params=_cparams("arbitrary"),
        name="s5_step",
    )(su, h0r, h0i, *weights)


def _head_norm(o, g):
    parts = []
    for h in range(HEADS):
        seg = o[:, h * DV:(h + 1) * DV]
        parts.append(seg * lax.rsqrt(jnp.mean(seg * seg, axis=-1, keepdims=True) + EPS))
    return jnp.concatenate(parts, axis=-1) * g


def _route(logits, count):
    rows = logits.shape[0]
    lane = lax.broadcasted_iota(I32, logits.shape, 1)
    neg = -jnp.inf
    first = lambda hit: jnp.min(jnp.where(hit, lane, LANES), axis=-1, keepdims=True)
    glog = jnp.where(lane < MOE_GROUPS, logits, neg)
    gmax = jnp.max(glog, axis=-1, keepdims=True)
    gidx = first(glog == gmax)
    gw = 1.0 / jnp.sum(jnp.where(lane < MOE_GROUPS, jnp.exp(logits - gmax), 0.0), axis=-1, keepdims=True)
    inside = (lane >= MOE_GROUPS) & (lane < MOE_GROUPS + MOE_EXPERTS) & (((lane - MOE_GROUPS) >> 3) == gidx)
    el = jnp.where(inside, logits, neg)
    v1 = jnp.max(el, axis=-1, keepdims=True)
    i1 = first(el == v1)
    el2 = jnp.where(lane == i1, neg, el)
    v2 = jnp.max(el2, axis=-1, keepdims=True)
    i2 = first(el2 == v2)
    p2 = jnp.exp(v2 - v1)
    w1 = gw / (1.0 + p2)
    w2 = gw * p2 / (1.0 + p2)
    hit1, hit2 = lane == i1, lane == i2
    onehot = jnp.where(hit1 | hit2, 1.0, 0.0)
    tr = lax.broadcasted_iota(I32, (rows, rows), 0)
    tc = lax.broadcasted_iota(I32, (rows, rows), 1)
    before = _dot(jnp.where(tc < tr, 1.0, 0.0).astype(BF16), onehot.astype(BF16)) + count
    rank1 = jnp.sum(jnp.where(hit1, before, 0.0), axis=-1, keepdims=True)
    rank2 = jnp.sum(jnp.where(hit2, before, 0.0), axis=-1, keepdims=True)
    slab = jnp.zeros(logits.shape, F32)
    for ln, val in ((R_E1, (i1 - MOE_GROUPS).astype(F32)), (R_E2, (i2 - MOE_GROUPS).astype(F32)), (R_W1, w1), (R_W2, w2),
                    (R_RANK1, rank1), (R_RANK2, rank2)):
        slab = jnp.where(lane == ln, val, slab)
    return slab, count + jnp.sum(onehot, axis=0, keepdims=True)


def _merge_kernel(x_ref, o_ref, gate_ref, os5_ref, cnt0_ref, nm_ref, wgm_ref, gn_ref, wbr_ref, wout_ref, nf_ref,
                  wrh_ref, wrl_ref, br_ref, *rest):
    xm_ref, h2_ref, rt_ref, rtt_ref, cnt_ref = rest[-5:]

    @pl.when(pl.program_id(0) == 0)
    def _():
        cnt_ref[...] = cnt0_ref[...]

    x = x_ref[...]
    hb = _rms(x, nm_ref[...]).astype(BF16)
    mixed = None
    for i, src in enumerate((0, None, 1)):
        if src is None:
            branch = os5_ref[...]
        else:
            branch = (_head_norm(o_ref[src].astype(F32), gn_ref[src]) * gate_ref[src].astype(F32)).astype(BF16)
        gate = _sigmoid(_dot(hb, wgm_ref[:, i * D_MODEL:(i + 1) * D_MODEL]))
        term = gate * _dot(branch, wbr_ref[i])
        mixed = term if mixed is None else mixed + term
    xm = x + _dot(mixed.astype(BF16), wout_ref[...])
    xm_ref[...] = xm
    h2 = _rms(xm, nf_ref[...])
    h2_hi = h2.astype(BF16)
    h2_lo = (h2 - h2_hi.astype(F32)).astype(BF16)
    h2_ref[...] = h2_hi
    logits = _dot(h2_hi, wrh_ref[...]) + (_dot(h2_hi, wrl_ref[...]) + _dot(h2_lo, wrh_ref[...])) + br_ref[...]
    slab, cnt_ref[...] = _route(logits, cnt_ref[...])
    rt_ref[...] = slab
    rtt_ref[...] = jnp.transpose(slab)[0:8, :]


def _merge(x, o, gate, os5, cnt0, weights, carried, *, layer, tm, total_rows, block_offset):
    rows = x.shape[0]
    tile = lambda w: pl.BlockSpec((tm, w), lambda i: (i, 0))
    shared = lambda w: pl.BlockSpec((tm, w), lambda i: (i + block_offset, 0))
    carried = [] if carried is None else list(carried)
    w_specs = [_layer_spec((1, D_MODEL), layer), _layer_spec((D_MODEL, 3 * D_MODEL), layer),
               _layer_spec((2, 1, VW), layer), _layer_spec((3, VW, D_MODEL), layer),
               _layer_spec((D_MODEL, D_MODEL), layer), _layer_spec((1, D_MODEL), layer),
               _layer_spec((D_MODEL, LANES), layer), _layer_spec((D_MODEL, LANES), layer),
               _layer_spec((1, LANES), layer)]
    first = 5 + len(w_specs)
    return pl.pallas_call(
        _merge_kernel,
        grid=(rows // tm,),
        in_specs=[tile(D_MODEL),
                  pl.BlockSpec((2, tm, VW), lambda i: (0, i, 0)),
                  pl.BlockSpec((2, tm, VW), lambda i: (0, i, 0)),
                  tile(S5_WIDTH),
                  pl.BlockSpec((1, LANES), lambda i: (0, 0))] + w_specs + [pl.BlockSpec(memory_space=pl.ANY)] * len(carried),
        out_specs=[tile(D_MODEL), shared(D_MODEL), shared(LANES),
                   pl.BlockSpec((8, tm), lambda i: (0, i + block_offset)), pl.BlockSpec((1, LANES), lambda i: (0, 0))],
        out_shape=[jax.ShapeDtypeStruct((rows, D_MODEL), F32),
                   jax.ShapeDtypeStruct((total_rows, D_MODEL), BF16),
                   jax.ShapeDtypeStruct((total_rows, LANES), F32),
                   jax.ShapeDtypeStruct((8, total_rows), F32),
                   jax.ShapeDtypeStruct((1, LANES), F32)],
        input_output_aliases={first + j: 1 + j for j in range(len(carried))},
        compiler_params=_cparams("arbitrary"),
        name="merge_route",
    )(x, o, gate, os5, cnt0, *weights, *carried)


def _ffn_kernel(te_ref, nu_ref, x_ref, wg_ref, wu_ref, wd_ref, y_ref, wg_bf, wu_bf, wd_bf):
    i = pl.program_id(0)

    @pl.when(i < nu_ref[0])
    def _():
        @pl.when((i == 0) | (te_ref[i] != te_ref[jnp.maximum(i - 1, 0)]))
        def _():
            wg_bf[...] = wg_ref[...].astype(BF16)
            wu_bf[...] = wu_ref[...].astype(BF16)
            wd_bf[...] = wd_ref[...].astype(BF16)

        x = x_ref[...]
        act = _silu(_dot(x, wg_bf[...])) * _dot(x, wu_bf[...])
        y_ref[...] = _dot(act.astype(BF16), wd_bf[...]).astype(BF16)


def _grouped_ffn(tile_expert, n_used, xg, wg, wu, wd, *, layer):
    rows = xg.shape[0]
    used = lambda i, nu: jnp.minimum(i, nu[0] - 1)
    expert = lambda a, b: pl.BlockSpec((None, None, a, b), lambda i, te, nu: (layer, te[used(i, nu)], 0, 0))
    grid_spec = pltpu.PrefetchScalarGridSpec(
        num_scalar_prefetch=2,
        grid=(rows // MOE_TILE,),
        in_specs=[pl.BlockSpec((MOE_TILE, D_MODEL), lambda i, te, nu: (used(i, nu), 0)),
                  expert(D_MODEL, MOE_FF), expert(D_MODEL, MOE_FF), expert(MOE_FF, D_MODEL)],
        out_specs=pl.BlockSpec((MOE_TILE, D_MODEL), lambda i, te, nu: (used(i, nu), 0)),
        scratch_shapes=[pltpu.VMEM((D_MODEL, MOE_FF), BF16), pltpu.VMEM((D_MODEL, MOE_FF), BF16),
                        pltpu.VMEM((MOE_FF, D_MODEL), BF16)],
    )
    return pl.pallas_call(
        _ffn_kernel,
        grid_spec=grid_spec,
        out_shape=jax.ShapeDtypeStruct((rows, D_MODEL), BF16),
        compiler_params=_cparams("arbitrary"),
        name="expert_ffn",
    )(tile_expert, n_used, xg, wg, wu, wd)


@compute_on("tpu_sparsecore")
@jax.jit
def _take_rows(x, idx):
    return jnp.take(x, idx, axis=0, mode="clip")


def _moe_plan(route_t, count):
    tokens = route_t.shape[1]
    n_tiles = (2 * tokens + MOE_EXPERTS * MOE_TILE) // MOE_TILE
    n_rows = n_tiles * MOE_TILE
    counts = count[0, MOE_GROUPS:MOE_GROUPS + MOE_EXPERTS].astype(I32)
    padded = ((counts + MOE_TILE - 1) // MOE_TILE) * MOE_TILE
    gend = jnp.cumsum(padded)
    gstart = gend - padded
    experts = jnp.arange(MOE_EXPERTS, dtype=I32)[:, None]

    def rows_of(e_lane, rank_lane):
        e = route_t[e_lane].astype(I32)
        return jnp.sum(jnp.where(e[None, :] == experts, gstart[:, None], 0), axis=0) + route_t[rank_lane].astype(I32)

    pos1, pos2 = rows_of(R_E1, R_RANK1), rows_of(R_E2, R_RANK2)
    token = jnp.arange(tokens, dtype=I32) + 1
    marked = jnp.zeros((n_rows,), I32).at[jnp.concatenate([pos1, pos2])].add(jnp.concatenate([token, token]),
                                                                             unique_indices=True)
    filler = jnp.arange(n_rows, dtype=I32) % tokens
    src = jnp.where(marked > 0, marked - 1, filler)
    tile_start = jnp.arange(n_tiles, dtype=I32) * MOE_TILE
    tile_expert = jnp.minimum(jnp.sum((tile_start[:, None] >= gend[None, :]).astype(I32), axis=1), MOE_EXPERTS - 1)
    n_used = (gend[-1] // MOE_TILE).astype(I32).reshape(1)
    return src, pos1, pos2, tile_expert, n_used


def _ple_kernel(xm_ref, y1_ref, y2_ref, rt_ref, p_ref, np_ref, wg_ref, wp_ref, nfin_ref, out_ref, *, final):
    rt = rt_ref[...]
    x1 = xm_ref[...] + rt[:, R_W1:R_W1 + 1] * y1_ref[...].astype(F32) + rt[:, R_W2:R_W2 + 1] * y2_ref[...].astype(F32)
    gate = _sigmoid(_dot(_rms(x1, np_ref[...]).astype(BF16), wg_ref[...]))
    x2 = x1 + gate * _dot(p_ref[...].astype(BF16), wp_ref[...])
    out_ref[...] = _rms(x2, nfin_ref[...]) if final else x2


def _ple(xm, y1, y2, route, p, npl, wg, wp, nfin, *, layer, final, tm, block_offset):
    rows = xm.shape[0]
    tile = lambda w: pl.BlockSpec((tm, w), lambda i: (i, 0))
    shared = lambda w: pl.BlockSpec((tm, w), lambda i: (i + block_offset, 0))
    return pl.pallas_call(
        functools.partial(_ple_kernel, final=final),
        grid=(rows // tm,),
        in_specs=[tile(D_MODEL), shared(D_MODEL), shared(D_MODEL), shared(LANES),
                  pl.BlockSpec((None, tm, PLE_DIM), lambda i: (layer, i, 0)),
                  _layer_spec((1, D_MODEL), layer), _layer_spec((D_MODEL, D_MODEL), layer),
                  _layer_spec((PLE_DIM, D_MODEL), layer), pl.BlockSpec((1, D_MODEL), lambda i: (0, 0))],
        out_specs=tile(D_MODEL),
        out_shape=jax.ShapeDtypeStruct((rows, D_MODEL), F32),
        compiler_params=_cparams("parallel"),
        name="combine_ple",
    )(xm, y1, y2, route, p, npl, wg, wp, nfin)


def kernel(x_prompt, x_sample, state_gla, state_s5_re, state_s5_im, state_hgrn, p_prompt, p_sample, norm_mix, w_in, gla_w_gate2, gla_b_gate, gla_norm, s5_a_re, s5_a_im, s5_log_dt, s5_b_re, s5_b_im, s5_c_re, s5_c_im, s5_d, s5_w_glu, s5_b_glu, hgrn_lb_logits, hgrn_norm, w_br_gla, w_br_s5, w_br_hgrn, w_out, norm_ffn, moe_w_group, moe_b_group, moe_w_expert, moe_b_expert, moe_w_gate, moe_w_up, moe_w_down, norm_ple, w_ple_gate, w_ple_proj, norm_final):
    nb, length, _ = x_prompt.shape
    ns = x_sample.shape[0]
    n_p = nb * length
    n_all = n_p + ns
    row = lambda t: t.reshape(DEPTH, 1, -1)

    o1, o2, o3, o4 = 1536, 1552, 2064, 3600
    w_a = jnp.concatenate([w_in[:, :, :o1], w_in[:, :, o2:o4], w_in[:, :, o1:o2],
                           jnp.zeros((DEPTH, D_MODEL, LANES - GLA_RANK), F32)], axis=-1).astype(BF16)
    wg2 = jnp.concatenate([gla_w_gate2, jnp.zeros((DEPTH, LANES - GLA_RANK, QK), F32)], axis=1).astype(BF16)
    s5_win, s5_are, s5_aim = _s5_discretise(s5_a_re, s5_a_im, s5_log_dt, s5_b_re, s5_b_im)
    s5_w = (s5_win, s5_are, s5_aim, _s5_blockdiag_out(s5_c_re), _s5_blockdiag_out(s5_c_im), row(s5_d),
            s5_w_glu.astype(BF16), row(s5_b_glu))
    head_gain = jnp.stack([jnp.tile(gla_norm, (1, HEADS)), jnp.tile(hgrn_norm, (1, HEADS))], axis=1).reshape(DEPTH, 2, 1, VW)
    w_router = jnp.concatenate([moe_w_group, moe_w_expert,
                                jnp.zeros((DEPTH, D_MODEL, LANES - MOE_GROUPS - MOE_EXPERTS), F32)], axis=-1)
    wr_hi = w_router.astype(BF16)
    wr_lo = (w_router - wr_hi.astype(F32)).astype(BF16)
    b_router = jnp.concatenate([moe_b_group, moe_b_expert,
                                jnp.zeros((DEPTH, LANES - MOE_GROUPS - MOE_EXPERTS), F32)], axis=-1).reshape(DEPTH, 1, LANES)
    merge_w = (row(norm_mix), w_in[:, :, o4:].astype(BF16), head_gain,
               jnp.stack([w_br_gla, w_br_s5, w_br_hgrn], axis=1).astype(BF16), w_out.astype(BF16), row(norm_ffn),
               wr_hi, wr_lo, b_router)
    ple_w = (row(norm_ple), w_ple_gate.astype(BF16), w_ple_proj.astype(BF16), norm_final.reshape(1, D_MODEL))
    nm, bg = row(norm_mix), row(gla_b_gate)
    p_p, p_s = p_prompt.reshape(DEPTH, n_p, PLE_DIM), p_sample.reshape(DEPTH, ns, PLE_DIM)
    sg_in, sh_in = state_gla.reshape(DEPTH, ns, QK, DV), state_hgrn.reshape(DEPTH, ns, QK, DV)
    s5r_in, s5i_in = state_s5_re.reshape(DEPTH, ns, S5_LANES), state_s5_im.reshape(DEPTH, ns, S5_LANES)

    xp = x_prompt.reshape(n_p, D_MODEL)
    xs = x_sample.reshape(ns, D_MODEL)
    new_p, new_s5 = [], []
    new_s = None
    for i in range(DEPTH):
        qk, v, gate, g, su = _inproj(xp, nm, w_a, wg2, bg, hgrn_lb_logits, layer=i, tm=512)
        o_p, st_p = _chunk_scan(qk.reshape(2, nb, length, -1), v.reshape(2, nb, length, -1),
                                g.reshape(2, nb, length, -1), nb=nb, length=length)
        os5_p, hre_p, him_p = _s5_scan(su.reshape(nb, length, -1), s5_w, nb=nb, length=length, layer=i)
        xm_p, h2, rt, rtt, cnt = _merge(xp, o_p.reshape(2, n_p, VW), gate, os5_p.reshape(n_p, -1),
                                        jnp.zeros((1, LANES), F32), merge_w, None, layer=i, tm=256, total_rows=n_all,
                                        block_offset=0)
        new_p.append((st_p[0].reshape(nb, HEADS, DK, DV), hre_p.reshape(nb, S5_GROUPS, S5_STATE),
                      him_p.reshape(nb, S5_GROUPS, S5_STATE), st_p[1].reshape(nb, HEADS, DK, DV)))

        qk, v, gate, g, su = _inproj(xs, nm, w_a, wg2, bg, hgrn_lb_logits, layer=i, tm=ns)
        o_s, *new_s = _decode(qk, v, g, sg_in, sh_in, new_s, layer=i)
        os5_s, hre_s, him_s = _s5_step(su, s5r_in, s5i_in, s5_w, layer=i)
        xm_s, h2, rt, rtt, cnt = _merge(xs, o_s, gate, os5_s, cnt, merge_w, (h2, rt, rtt), layer=i, tm=ns,
                                        total_rows=n_all, block_offset=n_p // ns)
        new_s5.append((hre_s.reshape(ns, S5_GROUPS, S5_STATE), him_s.reshape(ns, S5_GROUPS, S5_STATE)))

        src, pos1, pos2, tile_expert, n_used = _moe_plan(rtt, cnt)
        y = _grouped_ffn(tile_expert, n_used, _take_rows(h2, src), moe_w_gate, moe_w_up, moe_w_down, layer=i)
        y1, y2 = _take_rows(y, pos1), _take_rows(y, pos2)

        final = i == DEPTH - 1
        xp = _ple(xm_p, y1, y2, rt, p_p, *ple_w, layer=i, final=final, tm=512, block_offset=0)
        xs = _ple(xm_s, y1, y2, rt, p_s, *ple_w, layer=i, final=final, tm=ns, block_offset=n_p // ns)

    stack = lambda items, j: jnp.stack([it[j] for it in items])
    return (xp.reshape(nb, length, D_MODEL), xs.reshape(ns, 1, D_MODEL),
            stack(new_p, 0), stack(new_p, 1), stack(new_p, 2), stack(new_p, 3),
            new_s[0].reshape(DEPTH, ns, HEADS, DK, DV), stack(new_s5, 0), stack(new_s5, 1),
            new_s[1].reshape(DEPTH, ns, HEADS, DK, DV))
```

```python
import functools

import jax
import jax.numpy as jnp
import numpy as np
from jax import lax
from jax.experimental import pallas as pl
from jax.experimental.compute_on import compute_on
from jax.experimental.pallas import tpu as pltpu

F32, BF16, I32 = jnp.float32, jnp.bfloat16, jnp.int32

D_MODEL = 1024
DEPTH = 2
HEADS, DK, DV = 4, 64, 128
QK, VW = HEADS * DK, HEADS * DV
GLA_RANK, GLA_TAU = 16, 16.0
S5_WIDTH, S5_GROUP, S5_GROUPS, S5_STATE = 512, 16, 32, 64
S5_LANES = S5_GROUPS * S5_STATE
S5_SLABS = 4
MOE_GROUPS, MOE_PER_GROUP, MOE_EXPERTS, MOE_FF = 4, 8, 32, 256
PLE_DIM = 256
EPS = 1e-6

LANES = 128
CHUNK = 64
S5_CHUNK = 64
MOE_TILE = 256
PIPE = 2
SAFE_EXP = 80.0
VMEM_LIMIT = 56 * 1024 * 1024

C_GLA, C_S5, C_HG, C_GLR, C_END = 0, 1536, 2048, 3584, 3712
R_E1, R_E2, R_W1, R_W2, R_RANK1, R_RANK2 = 0, 1, 2, 3, 4, 5


def _cparams(*sem):
    return pltpu.CompilerParams(dimension_semantics=sem, vmem_limit_bytes=VMEM_LIMIT)


def _layer_spec(shape, layer):
    return pl.BlockSpec((None,) + tuple(shape), lambda *_: (layer,) + (0,) * len(shape))


def _dot(a, b):
    return jnp.dot(a, b, preferred_element_type=F32)


def _rms(x, g):
    return x * lax.rsqrt(jnp.mean(x * x, axis=-1, keepdims=True) + EPS) * g


def _log_sigmoid(x):
    return jnp.minimum(x, 0.0) - jnp.log1p(jnp.exp(-jnp.abs(x)))


def _sigmoid(x):
    return 1.0 / (1.0 + jnp.exp(-x))


def _silu(x):
    return x * _sigmoid(x)


def _split3(x):
    hi = x.astype(BF16)
    r1 = x - hi.astype(F32)
    mid = r1.astype(BF16)
    lo = (r1 - mid.astype(F32)).astype(BF16)
    return hi, mid, lo


def _dot01(m, parts):
    out = _dot(m, parts[0])
    for p in parts[1:]:
        out = out + _dot(m, p)
    return out


def _inproj_kernel(x_ref, nm_ref, w_ref, wg2_ref, bg_ref, lbl_ref, qk_ref, v_ref, gate_ref, g_ref, su_ref, *, layer):
    hb = _rms(x_ref[...], nm_ref[...]).astype(BF16)

    def proj(lo, hi):
        return _dot(hb, w_ref[:, lo:hi])

    qk_ref[0, :, 0:QK] = (proj(0, 256) * (DK ** -0.5)).astype(BF16)
    qk_ref[0, :, QK:2 * QK] = proj(256, 512).astype(BF16)
    v_ref[0] = proj(512, 1024).astype(BF16)
    gate_ref[0] = _silu(proj(1024, 1536)).astype(BF16)
    glr = proj(C_GLR, C_END).astype(BF16)
    g_ref[0] = _log_sigmoid(_dot(glr, wg2_ref[...]) + bg_ref[...]) * (1.0 / GLA_TAU)

    su_ref[...] = proj(C_S5, C_HG)

    lg = lbl_ref[...]
    mx = jnp.max(lg, axis=0, keepdims=True)
    ex = jnp.exp(lg - mx)
    sm = ex / jnp.sum(ex, axis=0, keepdims=True)
    cs = sm[0:1]
    for j in range(1, layer + 1):
        cs = cs + sm[j:j + 1]
    lb = cs - sm[0:1]
    log_lb, log_1mlb = jnp.log(lb), jnp.log1p(-lb)
    z = proj(C_HG + 256, C_HG + 512)
    a, c = log_lb, log_1mlb + _log_sigmoid(z)
    g_ref[1] = jnp.maximum(a, c) + jnp.log1p(jnp.exp(-jnp.abs(a - c)))
    qk_ref[1, :, 0:QK] = _silu(proj(C_HG, C_HG + 256)).astype(BF16)
    qk_ref[1, :, QK:2 * QK] = ((1.0 - lb) * _sigmoid(-z)).astype(BF16)
    v_ref[1] = proj(C_HG + 512, C_HG + 1024).astype(BF16)
    gate_ref[1] = _silu(proj(C_HG + 1024, C_HG + 1536)).astype(BF16)


def _inproj(x, nm, w, wg2, bg, lbl, *, layer, tm):
    rows = x.shape[0]
    return pl.pallas_call(
        functools.partial(_inproj_kernel, layer=layer),
        grid=(rows // tm,),
        in_specs=[pl.BlockSpec((tm, D_MODEL), lambda i: (i, 0)),
                  _layer_spec((1, D_MODEL), layer),
                  _layer_spec((D_MODEL, C_END), layer),
                  _layer_spec((LANES, QK), layer),
                  _layer_spec((1, QK), layer),
                  pl.BlockSpec((DEPTH, QK), lambda i: (0, 0))],
        out_specs=[pl.BlockSpec((2, tm, 2 * QK), lambda i: (0, i, 0)),
                   pl.BlockSpec((2, tm, VW), lambda i: (0, i, 0)),
                   pl.BlockSpec((2, tm, VW), lambda i: (0, i, 0)),
                   pl.BlockSpec((2, tm, QK), lambda i: (0, i, 0)),
                   pl.BlockSpec((tm, S5_WIDTH), lambda i: (i, 0))],
        out_shape=[jax.ShapeDtypeStruct((2, rows, 2 * QK), BF16),
                   jax.ShapeDtypeStruct((2, rows, VW), BF16),
                   jax.ShapeDtypeStruct((2, rows, VW), BF16),
                   jax.ShapeDtypeStruct((2, rows, QK), F32),
                   jax.ShapeDtypeStruct((rows, S5_WIDTH), F32)],
        compiler_params=_cparams("parallel"),
        name="inproj",
    )(x, nm, w, wg2, bg, lbl)


def _head_stack(x):
    head = lax.broadcasted_iota(I32, x.shape, 1) // DK
    return jnp.concatenate([jnp.where(head == h, x, 0.0) for h in range(HEADS)], axis=0).astype(BF16)


def _stack_scores(qt, kt):
    return lax.dot_general(_head_stack(qt), kt.astype(BF16), (((1,), (1,)), ((), ())), preferred_element_type=F32)


def _chunk_kernel(qk_ref, v_ref, g_ref, o_ref, st_ref, bc_ref, sc_ref, *, nb, c):
    @pl.when(pl.program_id(1) == 0)
    def _():
        st_ref[...] = jnp.zeros_like(st_ref)

    row = lax.broadcasted_iota(I32, (c, c), 0)
    col = lax.broadcasted_iota(I32, (c, c), 1)
    tri = jnp.where(col <= row, 1.0, 0.0).astype(BF16)
    srow = lax.broadcasted_iota(I32, (HEADS * c, c), 0) & (c - 1)
    scol = lax.broadcasted_iota(I32, (HEADS * c, c), 1)
    mid = c // 2 - 1

    spread = None
    for b in range(nb):
        bc = _dot01(tri, _split3(g_ref[0, b]))
        bc_ref[b] = bc
        ref, last = bc[mid:mid + 1, :], bc[c - 1:c, :]
        s = jnp.maximum(jnp.max(-ref), jnp.max(ref - last))
        spread = s if spread is None else jnp.maximum(spread, s)

    def qk_of(b):
        return qk_ref[0, b, :, 0:QK].astype(F32), qk_ref[0, b, :, QK:2 * QK].astype(F32)

    def scores_one_reference():
        for b in range(nb):
            q, k = qk_of(b)
            bc = bc_ref[b]
            ref = bc[mid:mid + 1, :]
            s = _stack_scores(q * jnp.exp(bc - ref), k * jnp.exp(ref - bc))
            sc_ref[b] = jnp.where(scol <= srow, s, 0.0).astype(BF16)

    def scores_by_levels():
        qrow = lax.broadcasted_iota(I32, (c, QK), 0)
        for b in range(nb):
            q, k = qk_of(b)
            bc = bc_ref[b]
            parts = _split3(g_ref[0, b])
            acc = jnp.where(scol == srow, _stack_scores(q, k), 0.0)
            half = c // 2
            while half >= 1:
                blk = 2 * half
                last_low = (row & ~(blk - 1)) + (half - 1)
                ref = _dot01(jnp.where(col <= last_low, 1.0, 0.0).astype(BF16), parts)
                upper = (qrow & (blk - 1)) >= half
                dq = jnp.minimum(jnp.where(upper, bc - ref, 0.0), 0.0)
                dk = jnp.minimum(jnp.where(upper, 0.0, ref - bc), 0.0)
                s = _stack_scores(q * jnp.exp(dq), k * jnp.exp(dk))
                pair = ((srow & ~(blk - 1)) == (scol & ~(blk - 1))) & ((srow & (blk - 1)) >= half) & ((scol & (blk - 1)) < half)
                acc = acc + jnp.where(pair, s, 0.0)
                half //= 2
            sc_ref[b] = acc.astype(BF16)

    lax.cond(spread <= SAFE_EXP, scores_one_reference, scores_by_levels)

    for b in range(nb):
        q, k = qk_of(b)
        bc = bc_ref[b]
        last = bc[c - 1:c, :]
        v = v_ref[0, b]
        state = st_ref[0, b]
        o_inter = _dot(_head_stack(q * jnp.exp(bc)), state.astype(BF16))
        k_out = jnp.transpose(k * jnp.exp(last - bc)).astype(BF16)
        decay = jnp.transpose(jnp.broadcast_to(jnp.exp(last), (DV, QK)))
        for h in range(HEADS):
            rs, ls, ks = slice(h * c, (h + 1) * c), slice(h * DV, (h + 1) * DV), slice(h * DK, (h + 1) * DK)
            o_ref[0, b, :, ls] = (_dot(sc_ref[b, rs, :], v[:, ls]) + o_inter[rs, :]).astype(BF16)
            st_ref[0, b, ks, :] = decay[ks, :] * state[ks, :] + _dot(k_out[ks, :], v[:, ls])


def _chunk_scan(qk, v, g, *, nb, length):
    c = CHUNK
    blk = lambda w: pl.BlockSpec((1, nb, c, w), lambda br, i: (br, 0, i, 0))
    return pl.pallas_call(
        functools.partial(_chunk_kernel, nb=nb, c=c),
        grid=(2, length // c),
        in_specs=[blk(2 * QK), blk(VW), blk(QK)],
        out_specs=[blk(VW), pl.BlockSpec((1, nb, QK, DV), lambda br, i: (br, 0, 0, 0))],
        out_shape=[jax.ShapeDtypeStruct((2, nb, length, VW), BF16),
                   jax.ShapeDtypeStruct((2, nb, QK, DV), F32)],
        scratch_shapes=[pltpu.VMEM((nb, c, QK), F32), pltpu.VMEM((nb, HEADS * c, c), BF16)],
        compiler_params=_cparams("arbitrary", "arbitrary"),
        name="chunk_scan",
    )(qk, v, g)


def _decode_kernel(qk_ref, v_ref, g_ref, s0_ref, s1_ref, *rest, nt):
    o_ref, n0_ref, n1_ref = rest[-3:]
    for br, (s_ref, n_ref) in enumerate(((s0_ref, n0_ref), (s1_ref, n1_ref))):
        for j in range(nt):
            d = jnp.exp(g_ref[br, j:j + 1, :])
            q = qk_ref[br, j:j + 1, 0:QK].astype(F32)
            k = qk_ref[br, j:j + 1, QK:2 * QK].astype(F32)
            cols = jnp.transpose(jnp.concatenate([d, k, q, jnp.zeros((5, QK), F32)], axis=0))
            vrow = v_ref[br, j:j + 1, :].astype(F32)
            vfull = jnp.concatenate([jnp.broadcast_to(vrow[:, h * DV:(h + 1) * DV], (DK, DV)) for h in range(HEADS)], axis=0)
            new = cols[:, 0:1] * s_ref[j] + cols[:, 1:2] * vfull
            n_ref[j] = new
            t = cols[:, 2:3] * new
            for h in range(HEADS):
                o_ref[br, j:j + 1, h * DV:(h + 1) * DV] = jnp.sum(t[h * DK:(h + 1) * DK, :], axis=0, keepdims=True).astype(BF16)


def _decode(qk, v, g, s_gla, s_hg, prev, *, layer):
    n = qk.shape[1]
    nt = 8
    row = lambda w: pl.BlockSpec((2, nt, w), lambda i: (0, i, 0))
    st = pl.BlockSpec((None, nt, QK, DV), lambda i: (layer, i, 0, 0))
    carried = [] if prev is None else list(prev)
    first = 5
    return pl.pallas_call(
        functools.partial(_decode_kernel, nt=nt),
        grid=(n // nt,),
        in_specs=[row(2 * QK), row(VW), row(QK), st, st] + [pl.BlockSpec(memory_space=pl.ANY)] * len(carried),
        out_specs=[row(VW), st, st],
        out_shape=[jax.ShapeDtypeStruct((2, n, VW), BF16),
                   jax.ShapeDtypeStruct((DEPTH, n, QK, DV), F32),
                   jax.ShapeDtypeStruct((DEPTH, n, QK, DV), F32)],
        input_output_aliases={first + j: 1 + j for j in range(len(carried))},
        compiler_params=_cparams("parallel"),
        name="decode_step",
    )(qk, v, g, s_gla, s_hg, *carried)


def _s5_disc_kernel(lr_ref, li_ref, ldt_ref, br_ref, bi_ref, abr_ref, abi_ref, bbr_ref, bbi_ref):
    lr, li, dt = lr_ref[...], li_ref[...], jnp.exp(ldt_ref[...])
    mag = jnp.exp(lr * dt)
    ab_re, ab_im = mag * jnp.cos(li * dt), mag * jnp.sin(li * dt)
    den = lr * lr + li * li
    num_re = ab_re - 1.0
    coef_re = (num_re * lr + ab_im * li) / den
    coef_im = (ab_im * lr - num_re * li) / den
    br, bi = br_ref[...], bi_ref[...]
    abr_ref[...] = ab_re
    abi_ref[...] = ab_im
    bbr_ref[...] = coef_re * br - coef_im * bi
    bbi_ref[...] = coef_re * bi + coef_im * br


def _s5_discretise(a_re, a_im, log_dt, b_re, b_im):
    n = DEPTH * S5_GROUPS
    rep = lambda t: jnp.repeat(t.reshape(n, S5_STATE), S5_GROUP, axis=1)
    ldt = jnp.broadcast_to(log_dt.reshape(n, 1), (n, S5_STATE * S5_GROUP))
    shape = jax.ShapeDtypeStruct((n, S5_STATE * S5_GROUP), F32)
    ab_re, ab_im, bb_re, bb_im = pl.pallas_call(_s5_disc_kernel, out_shape=[shape] * 4, name="s5_discretise")(
        rep(a_re), rep(a_im), ldt, b_re.reshape(n, -1), b_im.reshape(n, -1))
    pole = lambda t: t[:, ::S5_GROUP].reshape(DEPTH, 1, S5_LANES)
    eye = jnp.eye(8, dtype=F32)

    def blockdiag_in(bb):
        t = bb.reshape(DEPTH, S5_SLABS, 8, S5_STATE, S5_GROUP).transpose(0, 1, 2, 4, 3)
        return jnp.einsum("dcgmp,gh->dcgmhp", t, eye).reshape(DEPTH, S5_SLABS, LANES, 8 * S5_STATE)

    w_in = jnp.concatenate([blockdiag_in(bb_re), blockdiag_in(bb_im)], axis=-1).astype(BF16)
    return w_in, pole(ab_re), pole(ab_im)


def _s5_blockdiag_out(c):
    t = c.reshape(DEPTH, S5_SLABS, 8, S5_GROUP, S5_STATE).transpose(0, 1, 2, 4, 3)
    return jnp.einsum("dcgpm,gh->dcgphm", t, jnp.eye(8, dtype=F32)).reshape(DEPTH, S5_SLABS, 8 * S5_STATE, LANES).astype(BF16)


def _time_major_perm(nb, ct):
    r = np.arange(nb * ct)
    p = np.zeros((nb * ct, nb * ct), np.float32)
    p[r, (r % nb) * ct + r // nb] = 1.0
    return p


def _gelu_tanh(y):
    return 0.5 * y * (1.0 + jnp.tanh(0.7978845608028654 * (y + 0.044715 * (y * y * y))))


def _s5_input(ub, win_ref, xre_ref, xim_ref):
    half = 8 * S5_STATE
    for s in range(S5_SLABS):
        r = _dot(ub[:, s * LANES:(s + 1) * LANES], win_ref[s])
        xre_ref[:, s * half:(s + 1) * half] = r[:, :half]
        xim_ref[:, s * half:(s + 1) * half] = r[:, half:]


def _s5_readout(xre_ref, xim_ref, cre_ref, cim_ref):
    half = 8 * S5_STATE
    ys = []
    for s in range(S5_SLABS):
        ls = slice(s * half, (s + 1) * half)
        ys.append(_dot(xre_ref[:, ls].astype(BF16), cre_ref[s]) - _dot(xim_ref[:, ls].astype(BF16), cim_ref[s]))
    return jnp.concatenate(ys, axis=-1)


def _s5_glu(y, u, d_ref, wglu_ref, bglu_ref):
    z = _gelu_tanh(y + d_ref[...] * u)
    return z * _sigmoid(_dot(z.astype(BF16), wglu_ref[...]) + bglu_ref[...])


def _s5_scan_kernel(su_ref, perm_ref, permt_ref, win_ref, are_ref, aim_ref, cre_ref, cim_ref, d_ref, wglu_ref, bglu_ref,
                    o_ref, hre_ref, him_ref, xre_ref, xim_ref, *, nb, ct):
    @pl.when(pl.program_id(0) == 0)
    def _():
        hre_ref[...] = jnp.zeros_like(hre_ref)
        him_ref[...] = jnp.zeros_like(him_ref)

    u = su_ref[...].reshape(nb * ct, S5_WIDTH)
    ub = _dot(perm_ref[...], u.astype(BF16)).astype(BF16)

    half = 8 * S5_STATE
    ys = []
    for s in range(S5_SLABS):
        ls = slice(s * half, (s + 1) * half)
        r = _dot(ub[:, s * LANES:(s + 1) * LANES], win_ref[s])
        xre_ref[:, ls] = r[:, :half]
        xim_ref[:, ls] = r[:, half:]
        ar = jnp.broadcast_to(are_ref[:, ls], (nb, half))
        ai = jnp.broadcast_to(aim_ref[:, ls], (nb, half))
        hr, hi = hre_ref[:, ls], him_ref[:, ls]
        for t in range(ct):
            rows = slice(t * nb, (t + 1) * nb)
            hr, hi = ar * hr - ai * hi + xre_ref[rows, ls], ar * hi + ai * hr + xim_ref[rows, ls]
            xre_ref[rows, ls] = hr
            xim_ref[rows, ls] = hi
        hre_ref[:, ls] = hr
        him_ref[:, ls] = hi
        ys.append(_dot(xre_ref[:, ls].astype(BF16), cre_ref[s]) - _dot(xim_ref[:, ls].astype(BF16), cim_ref[s]))

    y = jnp.concatenate(ys, axis=-1)
    y_hi = y.astype(BF16)
    y_lo = (y - y_hi.astype(F32)).astype(BF16)
    y = _dot01(permt_ref[...], (y_hi, y_lo))
    o_ref[...] = _s5_glu(y, u, d_ref, wglu_ref, bglu_ref).reshape(nb, ct, S5_WIDTH).astype(BF16)


def _s5_weight_specs(layer):
    return [_layer_spec((S5_SLABS, LANES, 2 * 8 * S5_STATE), layer),
            _layer_spec((1, S5_LANES), layer), _layer_spec((1, S5_LANES), layer),
            _layer_spec((S5_SLABS, 8 * S5_STATE, LANES), layer), _layer_spec((S5_SLABS, 8 * S5_STATE, LANES), layer),
            _layer_spec((1, S5_WIDTH), layer), _layer_spec((S5_WIDTH, S5_WIDTH), layer),
            _layer_spec((1, S5_WIDTH), layer)]


def _s5_scan(su, weights, *, nb, length, layer):
    ct = S5_CHUNK
    perm = _time_major_perm(nb, ct)
    const = pl.BlockSpec((nb * ct, nb * ct), lambda i: (0, 0))
    return pl.pallas_call(
        functools.partial(_s5_scan_kernel, nb=nb, ct=ct),
        grid=(length // ct,),
        in_specs=[pl.BlockSpec((nb, ct, S5_WIDTH), lambda i: (0, i, 0)), const, const] + _s5_weight_specs(layer),
        out_specs=[pl.BlockSpec((nb, ct, S5_WIDTH), lambda i: (0, i, 0)),
                   pl.BlockSpec((nb, S5_LANES), lambda i: (0, 0)), pl.BlockSpec((nb, S5_LANES), lambda i: (0, 0))],
        out_shape=[jax.ShapeDtypeStruct((nb, length, S5_WIDTH), BF16),
                   jax.ShapeDtypeStruct((nb, S5_LANES), F32), jax.ShapeDtypeStruct((nb, S5_LANES), F32)],
        scratch_shapes=[pltpu.VMEM((nb * ct, S5_LANES), F32), pltpu.VMEM((nb * ct, S5_LANES), F32)],
        compiler_params=_cparams("arbitrary"),
        name="s5_scan",
    )(su, jnp.asarray(perm, BF16), jnp.asarray(perm.T, BF16), *weights)


def _s5_step_kernel(su_ref, h0r_ref, h0i_ref, win_ref, are_ref, aim_ref, cre_ref, cim_ref, d_ref, wglu_ref, bglu_ref,
                    o_ref, hre_ref, him_ref):
    u = su_ref[...]
    _s5_input(u.astype(BF16), win_ref, hre_ref, him_ref)
    ar, ai = are_ref[...], aim_ref[...]
    h0r, h0i = h0r_ref[...], h0i_ref[...]
    nr = ar * h0r - ai * h0i + hre_ref[...]
    ni = ar * h0i + ai * h0r + him_ref[...]
    hre_ref[...] = nr
    him_ref[...] = ni
    o_ref[...] = _s5_glu(_s5_readout(hre_ref, him_ref, cre_ref, cim_ref), u, d_ref, wglu_ref, bglu_ref).astype(BF16)


def _s5_step(su, h0r, h0i, weights, *, layer):
    n = su.shape[0]
    full = lambda w: pl.BlockSpec((n, w), lambda i: (0, 0))
    state = pl.BlockSpec((None, n, S5_LANES), lambda i: (layer, 0, 0))
    return pl.pallas_call(
        _s5_step_kernel,
        grid=(1,),
        in_specs=[full(S5_WIDTH), state, state] + _s5_weight_specs(layer),
        out_specs=[full(S5_WIDTH), full(S5_LANES), full(S5_LANES)],
        out_shape=[jax.ShapeDtypeStruct((n, S5_WIDTH), BF16),
                   jax.ShapeDtypeStruct((n, S5_LANES), F32), jax.ShapeDtypeStruct((n, S5_LANES), F32)],
        compiler_params=_cparams("arbitrary"),
        name="s5_step",
    )(su, h0r, h0i, *weights)


def _head_norm(o, g):
    parts = []
    for h in range(HEADS):
        seg = o[:, h * DV:(h + 1) * DV]
        parts.append(seg * lax.rsqrt(jnp.mean(seg * seg, axis=-1, keepdims=True) + EPS))
    return jnp.concatenate(parts, axis=-1) * g


def _route(logits, count):
    rows = logits.shape[0]
    lane = lax.broadcasted_iota(I32, logits.shape, 1)
    neg = -jnp.inf
    first = lambda hit: jnp.min(jnp.where(hit, lane, LANES), axis=-1, keepdims=True)
    glog = jnp.where(lane < MOE_GROUPS, logits, neg)
    gmax = jnp.max(glog, axis=-1, keepdims=True)
    gidx = first(glog == gmax)
    gw = 1.0 / jnp.sum(jnp.where(lane < MOE_GROUPS, jnp.exp(logits - gmax), 0.0), axis=-1, keepdims=True)
    inside = (lane >= MOE_GROUPS) & (lane < MOE_GROUPS + MOE_EXPERTS) & (((lane - MOE_GROUPS) >> 3) == gidx)
    el = jnp.where(inside, logits, neg)
    v1 = jnp.max(el, axis=-1, keepdims=True)
    i1 = first(el == v1)
    el2 = jnp.where(lane == i1, neg, el)
    v2 = jnp.max(el2, axis=-1, keepdims=True)
    i2 = first(el2 == v2)
    p2 = jnp.exp(v2 - v1)
    w1 = gw / (1.0 + p2)
    w2 = gw * p2 / (1.0 + p2)
    hit1, hit2 = lane == i1, lane == i2
    onehot = jnp.where(hit1 | hit2, 1.0, 0.0)
    tr = lax.broadcasted_iota(I32, (rows, rows), 0)
    tc = lax.broadcasted_iota(I32, (rows, rows), 1)
    before = _dot(jnp.where(tc < tr, 1.0, 0.0).astype(BF16), onehot.astype(BF16)) + count
    rank1 = jnp.sum(jnp.where(hit1, before, 0.0), axis=-1, keepdims=True)
    rank2 = jnp.sum(jnp.where(hit2, before, 0.0), axis=-1, keepdims=True)
    slab = jnp.zeros(logits.shape, F32)
    for ln, val in ((R_E1, (i1 - MOE_GROUPS).astype(F32)), (R_E2, (i2 - MOE_GROUPS).astype(F32)), (R_W1, w1), (R_W2, w2),
                    (R_RANK1, rank1), (R_RANK2, rank2)):
        slab = jnp.where(lane == ln, val, slab)
    return slab, count + jnp.sum(onehot, axis=0, keepdims=True)


def _merge_kernel(x_ref, o_ref, gate_ref, os5_ref, cnt0_ref, nm_ref, wgm_ref, gn_ref, wbr_ref, wout_ref, nf_ref,
                  wrh_ref, wrl_ref, br_ref, *rest, sub):
    xm_ref, h2_ref, rt_ref, rtt_ref, cnt_ref = rest[-5:]

    @pl.when(pl.program_id(0) == 0)
    def _():
        cnt_ref[...] = cnt0_ref[...]

    logits = []
    for r0 in range(0, x_ref.shape[0], sub):
        rs = slice(r0, r0 + sub)
        x = x_ref[rs, :]
        hb = _rms(x, nm_ref[...]).astype(BF16)
        mixed = None
        for i, src in enumerate((0, None, 1)):
            if src is None:
                branch = os5_ref[rs, :]
            else:
                branch = (_head_norm(o_ref[src, rs, :].astype(F32), gn_ref[src]) * gate_ref[src, rs, :].astype(F32)).astype(BF16)
            gate = _sigmoid(_dot(hb, wgm_ref[:, i * D_MODEL:(i + 1) * D_MODEL]))
            term = gate * _dot(branch, wbr_ref[i])
            mixed = term if mixed is None else mixed + term
        xm = x + _dot(mixed.astype(BF16), wout_ref[...])
        xm_ref[rs, :] = xm
        h2 = _rms(xm, nf_ref[...])
        h2_hi = h2.astype(BF16)
        h2_lo = (h2 - h2_hi.astype(F32)).astype(BF16)
        h2_ref[rs, :] = h2_hi
        logits.append(_dot(h2_hi, wrh_ref[...]) + (_dot(h2_hi, wrl_ref[...]) + _dot(h2_lo, wrh_ref[...])) + br_ref[...])
    slab, cnt_ref[...] = _route(jnp.concatenate(logits, axis=0), cnt_ref[...])
    rt_ref[...] = slab
    rtt_ref[...] = jnp.transpose(slab)[0:8, :]


def _merge(x, o, gate, os5, cnt0, weights, carried, *, layer, tm, total_rows, block_offset):
    rows = x.shape[0]
    tile = lambda w: pl.BlockSpec((tm, w), lambda i: (i, 0))
    shared = lambda w: pl.BlockSpec((tm, w), lambda i: (i + block_offset, 0))
    carried = [] if carried is None else list(carried)
    w_specs = [_layer_spec((1, D_MODEL), layer), _layer_spec((D_MODEL, 3 * D_MODEL), layer),
               _layer_spec((2, 1, VW), layer), _layer_spec((3, VW, D_MODEL), layer),
               _layer_spec((D_MODEL, D_MODEL), layer), _layer_spec((1, D_MODEL), layer),
               _layer_spec((D_MODEL, LANES), layer), _layer_spec((D_MODEL, LANES), layer),
               _layer_spec((1, LANES), layer)]
    first = 5 + len(w_specs)
    return pl.pallas_call(
        functools.partial(_merge_kernel, sub=min(tm, 256)),
        grid=(rows // tm,),
        in_specs=[tile(D_MODEL),
                  pl.BlockSpec((2, tm, VW), lambda i: (0, i, 0)),
                  pl.BlockSpec((2, tm, VW), lambda i: (0, i, 0)),
                  tile(S5_WIDTH),
                  pl.BlockSpec((1, LANES), lambda i: (0, 0))] + w_specs + [pl.BlockSpec(memory_space=pl.ANY)] * len(carried),
        out_specs=[tile(D_MODEL), shared(D_MODEL), shared(LANES),
                   pl.BlockSpec((8, tm), lambda i: (0, i + block_offset)), pl.BlockSpec((1, LANES), lambda i: (0, 0))],
        out_shape=[jax.ShapeDtypeStruct((rows, D_MODEL), F32),
                   jax.ShapeDtypeStruct((total_rows, D_MODEL), BF16),
                   jax.ShapeDtypeStruct((total_rows, LANES), F32),
                   jax.ShapeDtypeStruct((8, total_rows), F32),
                   jax.ShapeDtypeStruct((1, LANES), F32)],
        input_output_aliases={first + j: 1 + j for j in range(len(carried))},
        compiler_params=_cparams("arbitrary"),
        name="merge_route",
    )(x, o, gate, os5, cnt0, *weights, *carried)


def _ffn_kernel(te_ref, nu_ref, x_ref, wg_ref, wu_ref, wd_ref, *rest, tile0):
    y_ref, wg_bf, wu_bf, wd_bf = rest[-4:]
    i = pl.program_id(0)
    tile = i + tile0

    @pl.when(tile < nu_ref[0])
    def _():
        @pl.when((i == 0) | (te_ref[tile] != te_ref[jnp.maximum(tile - 1, 0)]))
        def _():
            wg_bf[...] = wg_ref[...].astype(BF16)
            wu_bf[...] = wu_ref[...].astype(BF16)
            wd_bf[...] = wd_ref[...].astype(BF16)

        x = x_ref[...]
        act = _silu(_dot(x, wg_bf[...])) * _dot(x, wu_bf[...])
        y_ref[...] = _dot(act.astype(BF16), wd_bf[...]).astype(BF16)


def _grouped_ffn(tile_expert, n_used, xg, y_prev, wg, wu, wd, *, layer, tile0, total_rows):
    used = lambda i, nu: jnp.maximum(jnp.minimum(i + tile0, nu[0] - 1), tile0)
    expert = lambda a, b: pl.BlockSpec((None, None, a, b), lambda i, te, nu: (layer, te[used(i, nu)], 0, 0))
    carried = [] if y_prev is None else [y_prev]
    grid_spec = pltpu.PrefetchScalarGridSpec(
        num_scalar_prefetch=2,
        grid=(xg.shape[0] // MOE_TILE,),
        in_specs=[pl.BlockSpec((MOE_TILE, D_MODEL), lambda i, te, nu: (used(i, nu) - tile0, 0)),
                  expert(D_MODEL, MOE_FF), expert(D_MODEL, MOE_FF), expert(MOE_FF, D_MODEL)]
        + [pl.BlockSpec(memory_space=pl.ANY)] * len(carried),
        out_specs=pl.BlockSpec((MOE_TILE, D_MODEL), lambda i, te, nu: (used(i, nu), 0)),
        scratch_shapes=[pltpu.VMEM((D_MODEL, MOE_FF), BF16), pltpu.VMEM((D_MODEL, MOE_FF), BF16),
                        pltpu.VMEM((MOE_FF, D_MODEL), BF16)],
    )
    return pl.pallas_call(
        functools.partial(_ffn_kernel, tile0=tile0),
        grid_spec=grid_spec,
        out_shape=jax.ShapeDtypeStruct((total_rows, D_MODEL), BF16),
        input_output_aliases={6: 0} if carried else {},
        compiler_params=_cparams("arbitrary"),
        name="expert_ffn",
    )(tile_expert, n_used, xg, wg, wu, wd, *carried)


@compute_on("tpu_sparsecore")
@jax.jit
def _take_rows(x, idx):
    return jnp.take(x, idx, axis=0, mode="clip")


def _moe_plan(route_t, count):
    tokens = route_t.shape[1]
    n_tiles = (2 * tokens + MOE_EXPERTS * MOE_TILE) // MOE_TILE
    n_rows = n_tiles * MOE_TILE
    counts = count[0, MOE_GROUPS:MOE_GROUPS + MOE_EXPERTS].astype(I32)
    padded = ((counts + MOE_TILE - 1) // MOE_TILE) * MOE_TILE
    gend = jnp.cumsum(padded)
    gstart = gend - padded
    experts = jnp.arange(MOE_EXPERTS, dtype=I32)[:, None]

    def rows_of(e_lane, rank_lane):
        e = route_t[e_lane].astype(I32)
        return jnp.sum(jnp.where(e[None, :] == experts, gstart[:, None], 0), axis=0) + route_t[rank_lane].astype(I32)

    pos1, pos2 = rows_of(R_E1, R_RANK1), rows_of(R_E2, R_RANK2)
    token = jnp.arange(tokens, dtype=I32) + 1
    marked = jnp.zeros((n_rows,), I32).at[jnp.concatenate([pos1, pos2])].add(jnp.concatenate([token, token]),
                                                                             unique_indices=True)
    filler = jnp.arange(n_rows, dtype=I32) % tokens
    src = jnp.where(marked > 0, marked - 1, filler)
    tile_start = jnp.arange(n_tiles, dtype=I32) * MOE_TILE
    tile_expert = jnp.minimum(jnp.sum((tile_start[:, None] >= gend[None, :]).astype(I32), axis=1), MOE_EXPERTS - 1)
    n_used = (gend[-1] // MOE_TILE).astype(I32).reshape(1)
    return src, pos1, pos2, tile_expert, n_used


def _ple_kernel(xm_ref, y1_ref, y2_ref, rt_ref, p_ref, np_ref, wg_ref, wp_ref, nfin_ref, *rest, final):
    out_ref = rest[-1]
    rt = rt_ref[...]
    x1 = xm_ref[...] + rt[:, R_W1:R_W1 + 1] * y1_ref[...].astype(F32) + rt[:, R_W2:R_W2 + 1] * y2_ref[...].astype(F32)
    gate = _sigmoid(_dot(_rms(x1, np_ref[...]).astype(BF16), wg_ref[...]))
    x2 = x1 + gate * _dot(p_ref[...].astype(BF16), wp_ref[...])
    out_ref[...] = _rms(x2, nfin_ref[...]) if final else x2


def _ple(xm, y1, y2, route, p, x_prev, npl, wg, wp, nfin, *, layer, final, tm, n_blocks, x_off, y_off, rt_off):
    at = lambda w, off: pl.BlockSpec((tm, w), lambda i: (i + off, 0))
    carried = [] if x_prev is None else [x_prev]
    return pl.pallas_call(
        functools.partial(_ple_kernel, final=final),
        grid=(n_blocks,),
        in_specs=[at(D_MODEL, x_off), at(D_MODEL, y_off), at(D_MODEL, y_off), at(LANES, rt_off),
                  pl.BlockSpec((None, tm, PLE_DIM), lambda i: (layer, i + x_off, 0)),
                  _layer_spec((1, D_MODEL), layer), _layer_spec((D_MODEL, D_MODEL), layer),
                  _layer_spec((PLE_DIM, D_MODEL), layer), pl.BlockSpec((1, D_MODEL), lambda i: (0, 0))]
        + [pl.BlockSpec(memory_space=pl.ANY)] * len(carried),
        out_specs=at(D_MODEL, x_off),
        out_shape=jax.ShapeDtypeStruct(xm.shape, F32),
        input_output_aliases={9: 0} if carried else {},
        compiler_params=_cparams("parallel"),
        name="combine_ple",
    )(xm, y1, y2, route, p, npl, wg, wp, nfin, *carried)


def kernel(x_prompt, x_sample, state_gla, state_s5_re, state_s5_im, state_hgrn, p_prompt, p_sample, norm_mix, w_in, gla_w_gate2, gla_b_gate, gla_norm, s5_a_re, s5_a_im, s5_log_dt, s5_b_re, s5_b_im, s5_c_re, s5_c_im, s5_d, s5_w_glu, s5_b_glu, hgrn_lb_logits, hgrn_norm, w_br_gla, w_br_s5, w_br_hgrn, w_out, norm_ffn, moe_w_group, moe_b_group, moe_w_expert, moe_b_expert, moe_w_gate, moe_w_up, moe_w_down, norm_ple, w_ple_gate, w_ple_proj, norm_final):
    nb, length, _ = x_prompt.shape
    ns = x_sample.shape[0]
    n_p = nb * length
    n_all = n_p + ns
    row = lambda t: t.reshape(DEPTH, 1, -1)

    o1, o2, o3, o4 = 1536, 1552, 2064, 3600
    w_a = jnp.concatenate([w_in[:, :, :o1], w_in[:, :, o2:o4], w_in[:, :, o1:o2],
                           jnp.zeros((DEPTH, D_MODEL, LANES - GLA_RANK), F32)], axis=-1).astype(BF16)
    wg2 = jnp.concatenate([gla_w_gate2, jnp.zeros((DEPTH, LANES - GLA_RANK, QK), F32)], axis=1).astype(BF16)
    s5_win, s5_are, s5_aim = _s5_discretise(s5_a_re, s5_a_im, s5_log_dt, s5_b_re, s5_b_im)
    s5_w = (s5_win, s5_are, s5_aim, _s5_blockdiag_out(s5_c_re), _s5_blockdiag_out(s5_c_im), row(s5_d),
            s5_w_glu.astype(BF16), row(s5_b_glu))
    head_gain = jnp.stack([jnp.tile(gla_norm, (1, HEADS)), jnp.tile(hgrn_norm, (1, HEADS))], axis=1).reshape(DEPTH, 2, 1, VW)
    w_router = jnp.concatenate([moe_w_group, moe_w_expert,
                                jnp.zeros((DEPTH, D_MODEL, LANES - MOE_GROUPS - MOE_EXPERTS), F32)], axis=-1)
    wr_hi = w_router.astype(BF16)
    wr_lo = (w_router - wr_hi.astype(F32)).astype(BF16)
    b_router = jnp.concatenate([moe_b_group, moe_b_expert,
                                jnp.zeros((DEPTH, LANES - MOE_GROUPS - MOE_EXPERTS), F32)], axis=-1).reshape(DEPTH, 1, LANES)
    merge_w = (row(norm_mix), w_in[:, :, o4:].astype(BF16), head_gain,
               jnp.stack([w_br_gla, w_br_s5, w_br_hgrn], axis=1).astype(BF16), w_out.astype(BF16), row(norm_ffn),
               wr_hi, wr_lo, b_router)
    ple_w = (row(norm_ple), w_ple_gate.astype(BF16), w_ple_proj.astype(BF16), norm_final.reshape(1, D_MODEL))
    nm, bg = row(norm_mix), row(gla_b_gate)
    p_p, p_s = p_prompt.reshape(DEPTH, n_p, PLE_DIM), p_sample.reshape(DEPTH, ns, PLE_DIM)
    sg_in, sh_in = state_gla.reshape(DEPTH, ns, QK, DV), state_hgrn.reshape(DEPTH, ns, QK, DV)
    s5r_in, s5i_in = state_s5_re.reshape(DEPTH, ns, S5_LANES), state_s5_im.reshape(DEPTH, ns, S5_LANES)

    xp = x_prompt.reshape(n_p, D_MODEL)
    xs = x_sample.reshape(ns, D_MODEL)
    new_p, new_s5 = [], []
    new_s = None
    for i in range(DEPTH):
        qk, v, gate, g, su = _inproj(xp, nm, w_a, wg2, bg, hgrn_lb_logits, layer=i, tm=512)
        o_p, st_p = _chunk_scan(qk.reshape(2, nb, length, -1), v.reshape(2, nb, length, -1),
                                g.reshape(2, nb, length, -1), nb=nb, length=length)
        os5_p, hre_p, him_p = _s5_scan(su.reshape(nb, length, -1), s5_w, nb=nb, length=length, layer=i)
        xm_p, h2, rt, rtt, cnt = _merge(xp, o_p.reshape(2, n_p, VW), gate, os5_p.reshape(n_p, -1),
                                        jnp.zeros((1, LANES), F32), merge_w, None, layer=i, tm=512, total_rows=n_all,
                                        block_offset=0)
        new_p.append((st_p[0].reshape(nb, HEADS, DK, DV), hre_p.reshape(nb, S5_GROUPS, S5_STATE),
                      him_p.reshape(nb, S5_GROUPS, S5_STATE), st_p[1].reshape(nb, HEADS, DK, DV)))

        qk, v, gate, g, su = _inproj(xs, nm, w_a, wg2, bg, hgrn_lb_logits, layer=i, tm=ns)
        o_s, *new_s = _decode(qk, v, g, sg_in, sh_in, new_s, layer=i)
        os5_s, hre_s, him_s = _s5_step(su, s5r_in, s5i_in, s5_w, layer=i)
        xm_s, h2, rt, rtt, cnt = _merge(xs, o_s, gate, os5_s, cnt, merge_w, (h2, rt, rtt), layer=i, tm=ns,
                                        total_rows=n_all, block_offset=n_p // ns)
        new_s5.append((hre_s.reshape(ns, S5_GROUPS, S5_STATE), him_s.reshape(ns, S5_GROUPS, S5_STATE)))

        src, pos1, pos2, tile_expert, n_used = _moe_plan(rtt, cnt)
        n_tiles = tile_expert.shape[0]
        y = None
        for k in range(PIPE):
            t0, t1 = k * n_tiles // PIPE, (k + 1) * n_tiles // PIPE
            y = _grouped_ffn(tile_expert, n_used, _take_rows(h2, src[t0 * MOE_TILE:t1 * MOE_TILE]), y,
                             moe_w_gate, moe_w_up, moe_w_down, layer=i, tile0=t0, total_rows=n_tiles * MOE_TILE)

        final = i == DEPTH - 1
        tm = 512
        blocks = n_p // tm
        pieces = min(PIPE, blocks)
        x_new = None
        for k in range(pieces):
            b0, b1 = k * blocks // pieces, (k + 1) * blocks // pieces
            r0, r1 = b0 * tm, (n_all if k == pieces - 1 else b1 * tm)
            y1, y2 = _take_rows(y, pos1[r0:r1]), _take_rows(y, pos2[r0:r1])
            x_new = _ple(xm_p, y1, y2, rt, p_p, x_new, *ple_w, layer=i, final=final, tm=tm, n_blocks=b1 - b0,
                         x_off=b0, y_off=0, rt_off=b0)
        xp = x_new
        xs = _ple(xm_s, y1, y2, rt, p_s, None, *ple_w, layer=i, final=final, tm=ns, n_blocks=1,
                  x_off=0, y_off=(n_p - r0) // ns, rt_off=n_p // ns)

    stack = lambda items, j: jnp.stack([it[j] for it in items])
    return (xp.reshape(nb, length, D_MODEL), xs.reshape(ns, 1, D_MODEL),
            stack(new_p, 0), stack(new_p, 1), stack(new_p, 2), stack(new_p, 3),
            new_s[0].reshape(DEPTH, ns, HEADS, DK, DV), stack(new_s5, 0), stack(new_s5, 1),
            new_s[1].reshape(DEPTH, ns, HEADS, DK, DV))
```

```python
import functools

import jax
import jax.numpy as jnp
import numpy as np
from jax import lax
from jax.experimental import pallas as pl
from jax.experimental.compute_on import compute_on
from jax.experimental.pallas import tpu as pltpu

F32, BF16, I32 = jnp.float32, jnp.bfloat16, jnp.int32

D_MODEL = 1024
DEPTH = 2
HEADS, DK, DV = 4, 64, 128
QK, VW = HEADS * DK, HEADS * DV
GLA_RANK, GLA_TAU = 16, 16.0
S5_WIDTH, S5_GROUP, S5_GROUPS, S5_STATE = 512, 16, 32, 64
S5_LANES = S5_GROUPS * S5_STATE
S5_SLABS = 4
MOE_GROUPS, MOE_PER_GROUP, MOE_EXPERTS, MOE_FF = 4, 8, 32, 256
PLE_DIM = 256
EPS = 1e-6

LANES = 128
CHUNK = 64
S5_CHUNK = 64
MOE_TILE = 512
PIPE = 2
SAFE_EXP = 80.0
VMEM_LIMIT = 56 * 1024 * 1024

C_GLA, C_S5, C_HG, C_GLR, C_END = 0, 1536, 2048, 3584, 3712
R_E1, R_E2, R_W1, R_W2, R_RANK1, R_RANK2 = 0, 1, 2, 3, 4, 5


def _cparams(*sem):
    return pltpu.CompilerParams(dimension_semantics=sem, vmem_limit_bytes=VMEM_LIMIT)


def _layer_spec(shape, layer):
    return pl.BlockSpec((None,) + tuple(shape), lambda *_: (layer,) + (0,) * len(shape), pipeline_mode=pl.Buffered(1))


def _dot(a, b):
    return jnp.dot(a, b, preferred_element_type=F32)


def _rms(x, g):
    return x * lax.rsqrt(jnp.mean(x * x, axis=-1, keepdims=True) + EPS) * g


def _log_sigmoid(x):
    return jnp.minimum(x, 0.0) - jnp.log1p(jnp.exp(-jnp.abs(x)))


def _sigmoid(x):
    return 1.0 / (1.0 + jnp.exp(-x))


def _silu(x):
    return x * _sigmoid(x)


def _split3(x):
    hi = x.astype(BF16)
    r1 = x - hi.astype(F32)
    mid = r1.astype(BF16)
    lo = (r1 - mid.astype(F32)).astype(BF16)
    return hi, mid, lo


def _dot01(m, parts):
    out = _dot(m, parts[0])
    for p in parts[1:]:
        out = out + _dot(m, p)
    return out


def _inproj_kernel(x_ref, nm_ref, w_ref, wg2_ref, bg_ref, lbl_ref, qk_ref, v_ref, gate_ref, g_ref, su_ref, *, layer, sub):
    lg = lbl_ref[...]
    mx = jnp.max(lg, axis=0, keepdims=True)
    ex = jnp.exp(lg - mx)
    sm = ex / jnp.sum(ex, axis=0, keepdims=True)
    cs = sm[0:1]
    for j in range(1, layer + 1):
        cs = cs + sm[j:j + 1]
    lb = cs - sm[0:1]
    log_lb, log_1mlb = jnp.log(lb), jnp.log1p(-lb)

    for r0 in range(0, x_ref.shape[0], sub):
        rs = slice(r0, r0 + sub)
        hb = _rms(x_ref[rs, :], nm_ref[...]).astype(BF16)

        def proj(lo, hi):
            return _dot(hb, w_ref[:, lo:hi])

        qk_ref[0, rs, 0:QK] = (proj(0, 256) * (DK ** -0.5)).astype(BF16)
        qk_ref[0, rs, QK:2 * QK] = proj(256, 512).astype(BF16)
        v_ref[0, rs, :] = proj(512, 1024).astype(BF16)
        gate_ref[0, rs, :] = _silu(proj(1024, 1536)).astype(BF16)
        glr = proj(C_GLR, C_END).astype(BF16)
        g_ref[0, rs, :] = _log_sigmoid(_dot(glr, wg2_ref[...]) + bg_ref[...]) * (1.0 / GLA_TAU)

        su_ref[rs, :] = proj(C_S5, C_HG)

        z = proj(C_HG + 256, C_HG + 512)
        a, c = log_lb, log_1mlb + _log_sigmoid(z)
        g_ref[1, rs, :] = jnp.maximum(a, c) + jnp.log1p(jnp.exp(-jnp.abs(a - c)))
        qk_ref[1, rs, 0:QK] = _silu(proj(C_HG, C_HG + 256)).astype(BF16)
        qk_ref[1, rs, QK:2 * QK] = ((1.0 - lb) * _sigmoid(-z)).astype(BF16)
        v_ref[1, rs, :] = proj(C_HG + 512, C_HG + 1024).astype(BF16)
        gate_ref[1, rs, :] = _silu(proj(C_HG + 1024, C_HG + 1536)).astype(BF16)


def _inproj(x, nm, w, wg2, bg, lbl, *, layer, tm):
    rows = x.shape[0]
    return pl.pallas_call(
        functools.partial(_inproj_kernel, layer=layer, sub=min(tm, 256)),
        grid=(rows // tm,),
        in_specs=[pl.BlockSpec((tm, D_MODEL), lambda i: (i, 0)),
                  _layer_spec((1, D_MODEL), layer),
                  _layer_spec((D_MODEL, C_END), layer),
                  _layer_spec((LANES, QK), layer),
                  _layer_spec((1, QK), layer),
                  pl.BlockSpec((DEPTH, QK), lambda i: (0, 0))],
        out_specs=[pl.BlockSpec((2, tm, 2 * QK), lambda i: (0, i, 0)),
                   pl.BlockSpec((2, tm, VW), lambda i: (0, i, 0)),
                   pl.BlockSpec((2, tm, VW), lambda i: (0, i, 0)),
                   pl.BlockSpec((2, tm, QK), lambda i: (0, i, 0)),
                   pl.BlockSpec((tm, S5_WIDTH), lambda i: (i, 0))],
        out_shape=[jax.ShapeDtypeStruct((2, rows, 2 * QK), BF16),
                   jax.ShapeDtypeStruct((2, rows, VW), BF16),
                   jax.ShapeDtypeStruct((2, rows, VW), BF16),
                   jax.ShapeDtypeStruct((2, rows, QK), F32),
                   jax.ShapeDtypeStruct((rows, S5_WIDTH), F32)],
        compiler_params=_cparams("parallel"),
        name="inproj",
    )(x, nm, w, wg2, bg, lbl)


def _head_stack(x):
    head = lax.broadcasted_iota(I32, x.shape, 1) // DK
    return jnp.concatenate([jnp.where(head == h, x, 0.0) for h in range(HEADS)], axis=0).astype(BF16)


def _stack_scores(qt, kt):
    return lax.dot_general(_head_stack(qt), kt.astype(BF16), (((1,), (1,)), ((), ())), preferred_element_type=F32)


def _chunk_kernel(qk_ref, v_ref, g_ref, o_ref, st_ref, bc_ref, sc_ref, *, nb, c):
    @pl.when(pl.program_id(1) == 0)
    def _():
        st_ref[...] = jnp.zeros_like(st_ref)

    row = lax.broadcasted_iota(I32, (c, c), 0)
    col = lax.broadcasted_iota(I32, (c, c), 1)
    tri = jnp.where(col <= row, 1.0, 0.0).astype(BF16)
    srow = lax.broadcasted_iota(I32, (HEADS * c, c), 0) & (c - 1)
    scol = lax.broadcasted_iota(I32, (HEADS * c, c), 1)
    mid = c // 2 - 1

    spread = None
    for b in range(nb):
        bc = _dot01(tri, _split3(g_ref[0, b]))
        bc_ref[b] = bc
        ref, last = bc[mid:mid + 1, :], bc[c - 1:c, :]
        s = jnp.maximum(jnp.max(-ref), jnp.max(ref - last))
        spread = s if spread is None else jnp.maximum(spread, s)

    def qk_of(b):
        return qk_ref[0, b, :, 0:QK].astype(F32), qk_ref[0, b, :, QK:2 * QK].astype(F32)

    def scores_one_reference():
        for b in range(nb):
            q, k = qk_of(b)
            bc = bc_ref[b]
            ref = bc[mid:mid + 1, :]
            s = _stack_scores(q * jnp.exp(bc - ref), k * jnp.exp(ref - bc))
            sc_ref[b] = jnp.where(scol <= srow, s, 0.0).astype(BF16)

    def scores_by_levels():
        qrow = lax.broadcasted_iota(I32, (c, QK), 0)
        for b in range(nb):
            q, k = qk_of(b)
            bc = bc_ref[b]
            parts = _split3(g_ref[0, b])
            acc = jnp.where(scol == srow, _stack_scores(q, k), 0.0)
            half = c // 2
            while half >= 1:
                blk = 2 * half
                last_low = (row & ~(blk - 1)) + (half - 1)
                ref = _dot01(jnp.where(col <= last_low, 1.0, 0.0).astype(BF16), parts)
                upper = (qrow & (blk - 1)) >= half
                dq = jnp.minimum(jnp.where(upper, bc - ref, 0.0), 0.0)
                dk = jnp.minimum(jnp.where(upper, 0.0, ref - bc), 0.0)
                s = _stack_scores(q * jnp.exp(dq), k * jnp.exp(dk))
                pair = ((srow & ~(blk - 1)) == (scol & ~(blk - 1))) & ((srow & (blk - 1)) >= half) & ((scol & (blk - 1)) < half)
                acc = acc + jnp.where(pair, s, 0.0)
                half //= 2
            sc_ref[b] = acc.astype(BF16)

    lax.cond(spread <= SAFE_EXP, scores_one_reference, scores_by_levels)

    for b in range(nb):
        q, k = qk_of(b)
        bc = bc_ref[b]
        last = bc[c - 1:c, :]
        v = v_ref[0, b]
        state = st_ref[0, b]
        o_inter = _dot(_head_stack(q * jnp.exp(bc)), state.astype(BF16))
        k_out = jnp.transpose(k * jnp.exp(last - bc)).astype(BF16)
        decay = jnp.transpose(jnp.broadcast_to(jnp.exp(last), (DV, QK)))
        for h in range(HEADS):
            rs, ls, ks = slice(h * c, (h + 1) * c), slice(h * DV, (h + 1) * DV), slice(h * DK, (h + 1) * DK)
            o_ref[0, b, :, ls] = (_dot(sc_ref[b, rs, :], v[:, ls]) + o_inter[rs, :]).astype(BF16)
            st_ref[0, b, ks, :] = decay[ks, :] * state[ks, :] + _dot(k_out[ks, :], v[:, ls])


def _chunk_scan(qk, v, g, *, nb, length):
    c = CHUNK
    blk = lambda w: pl.BlockSpec((1, nb, c, w), lambda br, i: (br, 0, i, 0))
    return pl.pallas_call(
        functools.partial(_chunk_kernel, nb=nb, c=c),
        grid=(2, length // c),
        in_specs=[blk(2 * QK), blk(VW), blk(QK)],
        out_specs=[blk(VW), pl.BlockSpec((1, nb, QK, DV), lambda br, i: (br, 0, 0, 0))],
        out_shape=[jax.ShapeDtypeStruct((2, nb, length, VW), BF16),
                   jax.ShapeDtypeStruct((2, nb, QK, DV), F32)],
        scratch_shapes=[pltpu.VMEM((nb, c, QK), F32), pltpu.VMEM((nb, HEADS * c, c), BF16)],
        compiler_params=_cparams("arbitrary", "arbitrary"),
        name="chunk_scan",
    )(qk, v, g)


def _decode_kernel(qk_ref, v_ref, g_ref, s0_ref, s1_ref, *rest, nt):
    o_ref, n0_ref, n1_ref = rest[-3:]
    for br, (s_ref, n_ref) in enumerate(((s0_ref, n0_ref), (s1_ref, n1_ref))):
        for j in range(nt):
            d = jnp.exp(g_ref[br, j:j + 1, :])
            q = qk_ref[br, j:j + 1, 0:QK].astype(F32)
            k = qk_ref[br, j:j + 1, QK:2 * QK].astype(F32)
            cols = jnp.transpose(jnp.concatenate([d, k, q, jnp.zeros((5, QK), F32)], axis=0))
            vrow = v_ref[br, j:j + 1, :].astype(F32)
            vfull = jnp.concatenate([jnp.broadcast_to(vrow[:, h * DV:(h + 1) * DV], (DK, DV)) for h in range(HEADS)], axis=0)
            new = cols[:, 0:1] * s_ref[j] + cols[:, 1:2] * vfull
            n_ref[j] = new
            t = cols[:, 2:3] * new
            for h in range(HEADS):
                o_ref[br, j:j + 1, h * DV:(h + 1) * DV] = jnp.sum(t[h * DK:(h + 1) * DK, :], axis=0, keepdims=True).astype(BF16)


def _decode(qk, v, g, s_gla, s_hg, prev, *, layer):
    n = qk.shape[1]
    nt = 8
    row = lambda w: pl.BlockSpec((2, nt, w), lambda i: (0, i, 0))
    st = pl.BlockSpec((None, nt, QK, DV), lambda i: (layer, i, 0, 0))
    carried = [] if prev is None else list(prev)
    first = 5
    return pl.pallas_call(
        functools.partial(_decode_kernel, nt=nt),
        grid=(n // nt,),
        in_specs=[row(2 * QK), row(VW), row(QK), st, st] + [pl.BlockSpec(memory_space=pl.ANY)] * len(carried),
        out_specs=[row(VW), st, st],
        out_shape=[jax.ShapeDtypeStruct((2, n, VW), BF16),
                   jax.ShapeDtypeStruct((DEPTH, n, QK, DV), F32),
                   jax.ShapeDtypeStruct((DEPTH, n, QK, DV), F32)],
        input_output_aliases={first + j: 1 + j for j in range(len(carried))},
        compiler_params=_cparams("parallel"),
        name="decode_step",
    )(qk, v, g, s_gla, s_hg, *carried)


def _s5_disc_kernel(lr_ref, li_ref, ldt_ref, br_ref, bi_ref, abr_ref, abi_ref, bbr_ref, bbi_ref):
    lr, li, dt = lr_ref[...], li_ref[...], jnp.exp(ldt_ref[...])
    mag = jnp.exp(lr * dt)
    ab_re, ab_im = mag * jnp.cos(li * dt), mag * jnp.sin(li * dt)
    den = lr * lr + li * li
    num_re = ab_re - 1.0
    coef_re = (num_re * lr + ab_im * li) / den
    coef_im = (ab_im * lr - num_re * li) / den
    br, bi = br_ref[...], bi_ref[...]
    abr_ref[...] = ab_re
    abi_ref[...] = ab_im
    bbr_ref[...] = coef_re * br - coef_im * bi
    bbi_ref[...] = coef_re * bi + coef_im * br


def _s5_discretise(a_re, a_im, log_dt, b_re, b_im):
    n = DEPTH * S5_GROUPS
    rep = lambda t: jnp.repeat(t.reshape(n, S5_STATE), S5_GROUP, axis=1)
    ldt = jnp.broadcast_to(log_dt.reshape(n, 1), (n, S5_STATE * S5_GROUP))
    shape = jax.ShapeDtypeStruct((n, S5_STATE * S5_GROUP), F32)
    ab_re, ab_im, bb_re, bb_im = pl.pallas_call(_s5_disc_kernel, out_shape=[shape] * 4, name="s5_discretise")(
        rep(a_re), rep(a_im), ldt, b_re.reshape(n, -1), b_im.reshape(n, -1))
    pole = lambda t: t[:, ::S5_GROUP].reshape(DEPTH, 1, S5_LANES)
    eye = jnp.eye(8, dtype=F32)

    def blockdiag_in(bb):
        t = bb.reshape(DEPTH, S5_SLABS, 8, S5_STATE, S5_GROUP).transpose(0, 1, 2, 4, 3)
        return jnp.einsum("dcgmp,gh->dcgmhp", t, eye).reshape(DEPTH, S5_SLABS, LANES, 8 * S5_STATE)

    w_in = jnp.concatenate([blockdiag_in(bb_re), blockdiag_in(bb_im)], axis=-1).astype(BF16)
    return w_in, pole(ab_re), pole(ab_im)


def _s5_blockdiag_out(c):
    t = c.reshape(DEPTH, S5_SLABS, 8, S5_GROUP, S5_STATE).transpose(0, 1, 2, 4, 3)
    return jnp.einsum("dcgpm,gh->dcgphm", t, jnp.eye(8, dtype=F32)).reshape(DEPTH, S5_SLABS, 8 * S5_STATE, LANES).astype(BF16)


def _time_major_perm(nb, ct):
    r = np.arange(nb * ct)
    p = np.zeros((nb * ct, nb * ct), np.float32)
    p[r, (r % nb) * ct + r // nb] = 1.0
    return p


def _gelu_tanh(y):
    return 0.5 * y * (1.0 + jnp.tanh(0.7978845608028654 * (y + 0.044715 * (y * y * y))))


def _s5_input(ub, win_ref, xre_ref, xim_ref):
    half = 8 * S5_STATE
    for s in range(S5_SLABS):
        r = _dot(ub[:, s * LANES:(s + 1) * LANES], win_ref[s])
        xre_ref[:, s * half:(s + 1) * half] = r[:, :half]
        xim_ref[:, s * half:(s + 1) * half] = r[:, half:]


def _s5_readout(xre_ref, xim_ref, cre_ref, cim_ref):
    half = 8 * S5_STATE
    ys = []
    for s in range(S5_SLABS):
        ls = slice(s * half, (s + 1) * half)
        ys.append(_dot(xre_ref[:, ls].astype(BF16), cre_ref[s]) - _dot(xim_ref[:, ls].astype(BF16), cim_ref[s]))
    return jnp.concatenate(ys, axis=-1)


def _s5_glu(y, u, d_ref, wglu_ref, bglu_ref):
    z = _gelu_tanh(y + d_ref[...] * u)
    return z * _sigmoid(_dot(z.astype(BF16), wglu_ref[...]) + bglu_ref[...])


def _s5_scan_kernel(su_ref, perm_ref, permt_ref, win_ref, are_ref, aim_ref, cre_ref, cim_ref, d_ref, wglu_ref, bglu_ref,
                    o_ref, hre_ref, him_ref, xre_ref, xim_ref, *, nb, ct):
    @pl.when(pl.program_id(0) == 0)
    def _():
        hre_ref[...] = jnp.zeros_like(hre_ref)
        him_ref[...] = jnp.zeros_like(him_ref)

    u = su_ref[...].reshape(nb * ct, S5_WIDTH)
    ub = _dot(perm_ref[...], u.astype(BF16)).astype(BF16)

    half = 8 * S5_STATE
    ys = []
    for s in range(S5_SLABS):
        ls = slice(s * half, (s + 1) * half)
        r = _dot(ub[:, s * LANES:(s + 1) * LANES], win_ref[s])
        xre_ref[:, ls] = r[:, :half]
        xim_ref[:, ls] = r[:, half:]
        ar = jnp.broadcast_to(are_ref[:, ls], (nb, half))
        ai = jnp.broadcast_to(aim_ref[:, ls], (nb, half))
        hr, hi = hre_ref[:, ls], him_ref[:, ls]
        for t in range(ct):
            rows = slice(t * nb, (t + 1) * nb)
            hr, hi = ar * hr - ai * hi + xre_ref[rows, ls], ar * hi + ai * hr + xim_ref[rows, ls]
            xre_ref[rows, ls] = hr
            xim_ref[rows, ls] = hi
        hre_ref[:, ls] = hr
        him_ref[:, ls] = hi
        ys.append(_dot(xre_ref[:, ls].astype(BF16), cre_ref[s]) - _dot(xim_ref[:, ls].astype(BF16), cim_ref[s]))

    y = jnp.concatenate(ys, axis=-1)
    y_hi = y.astype(BF16)
    y_lo = (y - y_hi.astype(F32)).astype(BF16)
    y = _dot01(permt_ref[...], (y_hi, y_lo))
    o_ref[...] = _s5_glu(y, u, d_ref, wglu_ref, bglu_ref).reshape(nb, ct, S5_WIDTH).astype(BF16)


def _s5_weight_specs(layer):
    return [_layer_spec((S5_SLABS, LANES, 2 * 8 * S5_STATE), layer),
            _layer_spec((1, S5_LANES), layer), _layer_spec((1, S5_LANES), layer),
            _layer_spec((S5_SLABS, 8 * S5_STATE, LANES), layer), _layer_spec((S5_SLABS, 8 * S5_STATE, LANES), layer),
            _layer_spec((1, S5_WIDTH), layer), _layer_spec((S5_WIDTH, S5_WIDTH), layer),
            _layer_spec((1, S5_WIDTH), layer)]


def _s5_scan(su, weights, *, nb, length, layer):
    ct = S5_CHUNK
    perm = _time_major_perm(nb, ct)
    const = pl.BlockSpec((nb * ct, nb * ct), lambda i: (0, 0))
    return pl.pallas_call(
        functools.partial(_s5_scan_kernel, nb=nb, ct=ct),
        grid=(length // ct,),
        in_specs=[pl.BlockSpec((nb, ct, S5_WIDTH), lambda i: (0, i, 0)), const, const] + _s5_weight_specs(layer),
        out_specs=[pl.BlockSpec((nb, ct, S5_WIDTH), lambda i: (0, i, 0)),
                   pl.BlockSpec((nb, S5_LANES), lambda i: (0, 0)), pl.BlockSpec((nb, S5_LANES), lambda i: (0, 0))],
        out_shape=[jax.ShapeDtypeStruct((nb, length, S5_WIDTH), BF16),
                   jax.ShapeDtypeStruct((nb, S5_LANES), F32), jax.ShapeDtypeStruct((nb, S5_LANES), F32)],
        scratch_shapes=[pltpu.VMEM((nb * ct, S5_LANES), F32), pltpu.VMEM((nb * ct, S5_LANES), F32)],
        compiler_params=_cparams("arbitrary"),
        name="s5_scan",
    )(su, jnp.asarray(perm, BF16), jnp.asarray(perm.T, BF16), *weights)


def _s5_step_kernel(su_ref, h0r_ref, h0i_ref, win_ref, are_ref, aim_ref, cre_ref, cim_ref, d_ref, wglu_ref, bglu_ref,
                    o_ref, hre_ref, him_ref):
    u = su_ref[...]
    _s5_input(u.astype(BF16), win_ref, hre_ref, him_ref)
    ar, ai = are_ref[...], aim_ref[...]
    h0r, h0i = h0r_ref[...], h0i_ref[...]
    nr = ar * h0r - ai * h0i + hre_ref[...]
    ni = ar * h0i + ai * h0r + him_ref[...]
    hre_ref[...] = nr
    him_ref[...] = ni
    o_ref[...] = _s5_glu(_s5_readout(hre_ref, him_ref, cre_ref, cim_ref), u, d_ref, wglu_ref, bglu_ref).astype(BF16)


def _s5_step(su, h0r, h0i, weights, *, layer):
    n = su.shape[0]
    full = lambda w: pl.BlockSpec((n, w), lambda i: (0, 0))
    state = pl.BlockSpec((None, n, S5_LANES), lambda i: (layer, 0, 0))
    return pl.pallas_call(
        _s5_step_kernel,
        grid=(1,),
        in_specs=[full(S5_WIDTH), state, state] + _s5_weight_specs(layer),
        out_specs=[full(S5_WIDTH), full(S5_LANES), full(S5_LANES)],
        out_shape=[jax.ShapeDtypeStruct((n, S5_WIDTH), BF16),
                   jax.ShapeDtypeStruct((n, S5_LANES), F32), jax.ShapeDtypeStruct((n, S5_LANES), F32)],
        compiler_params=_cparams("arbitrary"),
        name="s5_step",
    )(su, h0r, h0i, *weights)


def _head_norm(o, g):
    parts = []
    for h in range(HEADS):
        seg = o[:, h * DV:(h + 1) * DV]
        parts.append(seg * lax.rsqrt(jnp.mean(seg * seg, axis=-1, keepdims=True) + EPS))
    return jnp.concatenate(parts, axis=-1) * g


def _route(logits, count):
    rows = logits.shape[0]
    lane = lax.broadcasted_iota(I32, logits.shape, 1)
    neg = -jnp.inf
    first = lambda hit: jnp.min(jnp.where(hit, lane, LANES), axis=-1, keepdims=True)
    glog = jnp.where(lane < MOE_GROUPS, logits, neg)
    gmax = jnp.max(glog, axis=-1, keepdims=True)
    gidx = first(glog == gmax)
    gw = 1.0 / jnp.sum(jnp.where(lane < MOE_GROUPS, jnp.exp(logits - gmax), 0.0), axis=-1, keepdims=True)
    inside = (lane >= MOE_GROUPS) & (lane < MOE_GROUPS + MOE_EXPERTS) & (((lane - MOE_GROUPS) >> 3) == gidx)
    el = jnp.where(inside, logits, neg)
    v1 = jnp.max(el, axis=-1, keepdims=True)
    i1 = first(el == v1)
    el2 = jnp.where(lane == i1, neg, el)
    v2 = jnp.max(el2, axis=-1, keepdims=True)
    i2 = first(el2 == v2)
    p2 = jnp.exp(v2 - v1)
    w1 = gw / (1.0 + p2)
    w2 = gw * p2 / (1.0 + p2)
    hit1, hit2 = lane == i1, lane == i2
    onehot = jnp.where(hit1 | hit2, 1.0, 0.0)
    tr = lax.broadcasted_iota(I32, (rows, rows), 0)
    tc = lax.broadcasted_iota(I32, (rows, rows), 1)
    before = _dot(jnp.where(tc < tr, 1.0, 0.0).astype(BF16), onehot.astype(BF16)) + count
    rank1 = jnp.sum(jnp.where(hit1, before, 0.0), axis=-1, keepdims=True)
    rank2 = jnp.sum(jnp.where(hit2, before, 0.0), axis=-1, keepdims=True)
    slab = jnp.zeros(logits.shape, F32)
    for ln, val in ((R_E1, (i1 - MOE_GROUPS).astype(F32)), (R_E2, (i2 - MOE_GROUPS).astype(F32)), (R_W1, w1), (R_W2, w2),
                    (R_RANK1, rank1), (R_RANK2, rank2)):
        slab = jnp.where(lane == ln, val, slab)
    return slab, count + jnp.sum(onehot, axis=0, keepdims=True)


def _merge_kernel(x_ref, o_ref, gate_ref, os5_ref, cnt0_ref, nm_ref, wgm_ref, gn_ref, wbr_ref, wout_ref, nf_ref,
                  wrh_ref, wrl_ref, br_ref, *rest, sub):
    xm_ref, h2_ref, rt_ref, rtt_ref, cnt_ref = rest[-5:]

    @pl.when(pl.program_id(0) == 0)
    def _():
        cnt_ref[...] = cnt0_ref[...]

    logits = []
    for r0 in range(0, x_ref.shape[0], sub):
        rs = slice(r0, r0 + sub)
        x = x_ref[rs, :]
        hb = _rms(x, nm_ref[...]).astype(BF16)
        mixed = None
        for i, src in enumerate((0, None, 1)):
            if src is None:
                branch = os5_ref[rs, :]
            else:
                branch = (_head_norm(o_ref[src, rs, :].astype(F32), gn_ref[src]) * gate_ref[src, rs, :].astype(F32)).astype(BF16)
            gate = _sigmoid(_dot(hb, wgm_ref[:, i * D_MODEL:(i + 1) * D_MODEL]))
            term = gate * _dot(branch, wbr_ref[i])
            mixed = term if mixed is None else mixed + term
        xm = x + _dot(mixed.astype(BF16), wout_ref[...])
        xm_ref[rs, :] = xm
        h2 = _rms(xm, nf_ref[...])
        h2_hi = h2.astype(BF16)
        h2_lo = (h2 - h2_hi.astype(F32)).astype(BF16)
        h2_ref[rs, :] = h2_hi
        logits.append(_dot(h2_hi, wrh_ref[...]) + (_dot(h2_hi, wrl_ref[...]) + _dot(h2_lo, wrh_ref[...])) + br_ref[...])
    slab, cnt_ref[...] = _route(jnp.concatenate(logits, axis=0), cnt_ref[...])
    rt_ref[...] = slab
    rtt_ref[...] = jnp.transpose(slab)[0:8, :]


def _merge(x, o, gate, os5, cnt0, weights, carried, *, layer, tm, total_rows, block_offset):
    rows = x.shape[0]
    tile = lambda w: pl.BlockSpec((tm, w), lambda i: (i, 0))
    shared = lambda w: pl.BlockSpec((tm, w), lambda i: (i + block_offset, 0))
    carried = [] if carried is None else list(carried)
    w_specs = [_layer_spec((1, D_MODEL), layer), _layer_spec((D_MODEL, 3 * D_MODEL), layer),
               _layer_spec((2, 1, VW), layer), _layer_spec((3, VW, D_MODEL), layer),
               _layer_spec((D_MODEL, D_MODEL), layer), _layer_spec((1, D_MODEL), layer),
               _layer_spec((D_MODEL, LANES), layer), _layer_spec((D_MODEL, LANES), layer),
               _layer_spec((1, LANES), layer)]
    first = 5 + len(w_specs)
    return pl.pallas_call(
        functools.partial(_merge_kernel, sub=min(tm, 256)),
        grid=(rows // tm,),
        in_specs=[tile(D_MODEL),
                  pl.BlockSpec((2, tm, VW), lambda i: (0, i, 0)),
                  pl.BlockSpec((2, tm, VW), lambda i: (0, i, 0)),
                  tile(S5_WIDTH),
                  pl.BlockSpec((1, LANES), lambda i: (0, 0))] + w_specs + [pl.BlockSpec(memory_space=pl.ANY)] * len(carried),
        out_specs=[tile(D_MODEL), shared(D_MODEL), shared(LANES),
                   pl.BlockSpec((8, tm), lambda i: (0, i + block_offset)), pl.BlockSpec((1, LANES), lambda i: (0, 0))],
        out_shape=[jax.ShapeDtypeStruct((rows, D_MODEL), F32),
                   jax.ShapeDtypeStruct((total_rows, D_MODEL), BF16),
                   jax.ShapeDtypeStruct((total_rows, LANES), F32),
                   jax.ShapeDtypeStruct((8, total_rows), F32),
                   jax.ShapeDtypeStruct((1, LANES), F32)],
        input_output_aliases={first + j: 1 + j for j in range(len(carried))},
        compiler_params=_cparams("arbitrary"),
        name="merge_route",
    )(x, o, gate, os5, cnt0, *weights, *carried)


def _ffn_kernel(te_ref, nu_ref, x_ref, wg_ref, wu_ref, wd_ref, *rest, tile0):
    y_ref, wg_bf, wu_bf, wd_bf = rest[-4:]
    i = pl.program_id(0)
    tile = i + tile0

    @pl.when(tile < nu_ref[0])
    def _():
        @pl.when((i == 0) | (te_ref[tile] != te_ref[jnp.maximum(tile - 1, 0)]))
        def _():
            wg_bf[...] = wg_ref[...].astype(BF16)
            wu_bf[...] = wu_ref[...].astype(BF16)
            wd_bf[...] = wd_ref[...].astype(BF16)

        x = x_ref[...]
        act = _silu(_dot(x, wg_bf[...])) * _dot(x, wu_bf[...])
        y_ref[...] = _dot(act.astype(BF16), wd_bf[...]).astype(BF16)


def _grouped_ffn(tile_expert, n_used, xg, y_prev, wg, wu, wd, *, layer, tile0, total_rows):
    used = lambda i, nu: jnp.maximum(jnp.minimum(i + tile0, nu[0] - 1), tile0)
    expert = lambda a, b: pl.BlockSpec((None, None, a, b), lambda i, te, nu: (layer, te[used(i, nu)], 0, 0))
    carried = [] if y_prev is None else [y_prev]
    grid_spec = pltpu.PrefetchScalarGridSpec(
        num_scalar_prefetch=2,
        grid=(xg.shape[0] // MOE_TILE,),
        in_specs=[pl.BlockSpec((MOE_TILE, D_MODEL), lambda i, te, nu: (used(i, nu) - tile0, 0)),
                  expert(D_MODEL, MOE_FF), expert(D_MODEL, MOE_FF), expert(MOE_FF, D_MODEL)]
        + [pl.BlockSpec(memory_space=pl.ANY)] * len(carried),
        out_specs=pl.BlockSpec((MOE_TILE, D_MODEL), lambda i, te, nu: (used(i, nu), 0)),
        scratch_shapes=[pltpu.VMEM((D_MODEL, MOE_FF), BF16), pltpu.VMEM((D_MODEL, MOE_FF), BF16),
                        pltpu.VMEM((MOE_FF, D_MODEL), BF16)],
    )
    return pl.pallas_call(
        functools.partial(_ffn_kernel, tile0=tile0),
        grid_spec=grid_spec,
        out_shape=jax.ShapeDtypeStruct((total_rows, D_MODEL), BF16),
        input_output_aliases={6: 0} if carried else {},
        compiler_params=_cparams("arbitrary"),
        name="expert_ffn",
    )(tile_expert, n_used, xg, wg, wu, wd, *carried)


@compute_on("tpu_sparsecore")
@jax.jit
def _take_rows(x, idx):
    return jnp.take(x, idx, axis=0, mode="clip")


def _moe_plan(route_t, count):
    tokens = route_t.shape[1]
    n_tiles = -(-(2 * tokens + MOE_EXPERTS * (MOE_TILE - 1)) // MOE_TILE)
    n_rows = n_tiles * MOE_TILE
    counts = count[0, MOE_GROUPS:MOE_GROUPS + MOE_EXPERTS].astype(I32)
    padded = ((counts + MOE_TILE - 1) // MOE_TILE) * MOE_TILE
    gend = jnp.cumsum(padded)
    gstart = gend - padded
    experts = jnp.arange(MOE_EXPERTS, dtype=I32)[:, None]

    def rows_of(e_lane, rank_lane):
        e = route_t[e_lane].astype(I32)
        return jnp.sum(jnp.where(e[None, :] == experts, gstart[:, None], 0), axis=0) + route_t[rank_lane].astype(I32)

    pos1, pos2 = rows_of(R_E1, R_RANK1), rows_of(R_E2, R_RANK2)
    token = jnp.arange(tokens, dtype=I32) + 1
    marked = jnp.zeros((n_rows,), I32).at[jnp.concatenate([pos1, pos2])].add(jnp.concatenate([token, token]),
                                                                             unique_indices=True)
    filler = jnp.arange(n_rows, dtype=I32) % tokens
    src = jnp.where(marked > 0, marked - 1, filler)
    tile_start = jnp.arange(n_tiles, dtype=I32) * MOE_TILE
    tile_expert = jnp.minimum(jnp.sum((tile_start[:, None] >= gend[None, :]).astype(I32), axis=1), MOE_EXPERTS - 1)
    n_used = (gend[-1] // MOE_TILE).astype(I32).reshape(1)
    return src, pos1, pos2, tile_expert, n_used


def _ple_kernel(xm_ref, y1_ref, y2_ref, rt_ref, p_ref, np_ref, wg_ref, wp_ref, nfin_ref, *rest, final):
    out_ref = rest[-1]
    rt = rt_ref[...]
    x1 = xm_ref[...] + rt[:, R_W1:R_W1 + 1] * y1_ref[...].astype(F32) + rt[:, R_W2:R_W2 + 1] * y2_ref[...].astype(F32)
    gate = _sigmoid(_dot(_rms(x1, np_ref[...]).astype(BF16), wg_ref[...]))
    x2 = x1 + gate * _dot(p_ref[...].astype(BF16), wp_ref[...])
    out_ref[...] = _rms(x2, nfin_ref[...]) if final else x2


def _ple(xm, y1, y2, route, p, x_prev, npl, wg, wp, nfin, *, layer, final, tm, n_blocks, x_off, y_off, rt_off):
    at = lambda w, off: pl.BlockSpec((tm, w), lambda i: (i + off, 0))
    carried = [] if x_prev is None else [x_prev]
    return pl.pallas_call(
        functools.partial(_ple_kernel, final=final),
        grid=(n_blocks,),
        in_specs=[at(D_MODEL, x_off), at(D_MODEL, y_off), at(D_MODEL, y_off), at(LANES, rt_off),
                  pl.BlockSpec((None, tm, PLE_DIM), lambda i: (layer, i + x_off, 0)),
                  _layer_spec((1, D_MODEL), layer), _layer_spec((D_MODEL, D_MODEL), layer),
                  _layer_spec((PLE_DIM, D_MODEL), layer), pl.BlockSpec((1, D_MODEL), lambda i: (0, 0))]
        + [pl.BlockSpec(memory_space=pl.ANY)] * len(carried),
        out_specs=at(D_MODEL, x_off),
        out_shape=jax.ShapeDtypeStruct(xm.shape, F32),
        input_output_aliases={9: 0} if carried else {},
        compiler_params=_cparams("parallel"),
        name="combine_ple",
    )(xm, y1, y2, route, p, npl, wg, wp, nfin, *carried)


def kernel(x_prompt, x_sample, state_gla, state_s5_re, state_s5_im, state_hgrn, p_prompt, p_sample, norm_mix, w_in, gla_w_gate2, gla_b_gate, gla_norm, s5_a_re, s5_a_im, s5_log_dt, s5_b_re, s5_b_im, s5_c_re, s5_c_im, s5_d, s5_w_glu, s5_b_glu, hgrn_lb_logits, hgrn_norm, w_br_gla, w_br_s5, w_br_hgrn, w_out, norm_ffn, moe_w_group, moe_b_group, moe_w_expert, moe_b_expert, moe_w_gate, moe_w_up, moe_w_down, norm_ple, w_ple_gate, w_ple_proj, norm_final):
    nb, length, _ = x_prompt.shape
    ns = x_sample.shape[0]
    n_p = nb * length
    n_all = n_p + ns
    row = lambda t: t.reshape(DEPTH, 1, -1)

    o1, o2, o3, o4 = 1536, 1552, 2064, 3600
    w_a = jnp.concatenate([w_in[:, :, :o1], w_in[:, :, o2:o4], w_in[:, :, o1:o2],
                           jnp.zeros((DEPTH, D_MODEL, LANES - GLA_RANK), F32)], axis=-1).astype(BF16)
    wg2 = jnp.concatenate([gla_w_gate2, jnp.zeros((DEPTH, LANES - GLA_RANK, QK), F32)], axis=1).astype(BF16)
    s5_win, s5_are, s5_aim = _s5_discretise(s5_a_re, s5_a_im, s5_log_dt, s5_b_re, s5_b_im)
    s5_w = (s5_win, s5_are, s5_aim, _s5_blockdiag_out(s5_c_re), _s5_blockdiag_out(s5_c_im), row(s5_d),
            s5_w_glu.astype(BF16), row(s5_b_glu))
    head_gain = jnp.stack([jnp.tile(gla_norm, (1, HEADS)), jnp.tile(hgrn_norm, (1, HEADS))], axis=1).reshape(DEPTH, 2, 1, VW)
    w_router = jnp.concatenate([moe_w_group, moe_w_expert,
                                jnp.zeros((DEPTH, D_MODEL, LANES - MOE_GROUPS - MOE_EXPERTS), F32)], axis=-1)
    wr_hi = w_router.astype(BF16)
    wr_lo = (w_router - wr_hi.astype(F32)).astype(BF16)
    b_router = jnp.concatenate([moe_b_group, moe_b_expert,
                                jnp.zeros((DEPTH, LANES - MOE_GROUPS - MOE_EXPERTS), F32)], axis=-1).reshape(DEPTH, 1, LANES)
    merge_w = (row(norm_mix), w_in[:, :, o4:].astype(BF16), head_gain,
               jnp.stack([w_br_gla, w_br_s5, w_br_hgrn], axis=1).astype(BF16), w_out.astype(BF16), row(norm_ffn),
               wr_hi, wr_lo, b_router)
    ple_w = (row(norm_ple), w_ple_gate.astype(BF16), w_ple_proj.astype(BF16), norm_final.reshape(1, D_MODEL))
    nm, bg = row(norm_mix), row(gla_b_gate)
    p_p, p_s = p_prompt.reshape(DEPTH, n_p, PLE_DIM), p_sample.reshape(DEPTH, ns, PLE_DIM)
    sg_in, sh_in = state_gla.reshape(DEPTH, ns, QK, DV), state_hgrn.reshape(DEPTH, ns, QK, DV)
    s5r_in, s5i_in = state_s5_re.reshape(DEPTH, ns, S5_LANES), state_s5_im.reshape(DEPTH, ns, S5_LANES)

    xp = x_prompt.reshape(n_p, D_MODEL)
    xs = x_sample.reshape(ns, D_MODEL)
    new_p, new_s5 = [], []
    new_s = None
    for i in range(DEPTH):
        qk, v, gate, g, su = _inproj(xp, nm, w_a, wg2, bg, hgrn_lb_logits, layer=i, tm=1024)
        o_p, st_p = _chunk_scan(qk.reshape(2, nb, length, -1), v.reshape(2, nb, length, -1),
                                g.reshape(2, nb, length, -1), nb=nb, length=length)
        os5_p, hre_p, him_p = _s5_scan(su.reshape(nb, length, -1), s5_w, nb=nb, length=length, layer=i)
        xm_p, h2, rt, rtt, cnt = _merge(xp, o_p.reshape(2, n_p, VW), gate, os5_p.reshape(n_p, -1),
                                        jnp.zeros((1, LANES), F32), merge_w, None, layer=i, tm=1024, total_rows=n_all,
                                        block_offset=0)
        new_p.append((st_p[0].reshape(nb, HEADS, DK, DV), hre_p.reshape(nb, S5_GROUPS, S5_STATE),
                      him_p.reshape(nb, S5_GROUPS, S5_STATE), st_p[1].reshape(nb, HEADS, DK, DV)))

        qk, v, gate, g, su = _inproj(xs, nm, w_a, wg2, bg, hgrn_lb_logits, layer=i, tm=ns)
        o_s, *new_s = _decode(qk, v, g, sg_in, sh_in, new_s, layer=i)
        os5_s, hre_s, him_s = _s5_step(su, s5r_in, s5i_in, s5_w, layer=i)
        xm_s, h2, rt, rtt, cnt = _merge(xs, o_s, gate, os5_s, cnt, merge_w, (h2, rt, rtt), layer=i, tm=ns,
                                        total_rows=n_all, block_offset=n_p // ns)
        new_s5.append((hre_s.reshape(ns, S5_GROUPS, S5_STATE), him_s.reshape(ns, S5_GROUPS, S5_STATE)))

        src, pos1, pos2, tile_expert, n_used = _moe_plan(rtt, cnt)
        n_tiles = tile_expert.shape[0]
        y = None
        for k in range(PIPE):
            t0, t1 = k * n_tiles // PIPE, (k + 1) * n_tiles // PIPE
            y = _grouped_ffn(tile_expert, n_used, _take_rows(h2, src[t0 * MOE_TILE:t1 * MOE_TILE]), y,
                             moe_w_gate, moe_w_up, moe_w_down, layer=i, tile0=t0, total_rows=n_tiles * MOE_TILE)

        final = i == DEPTH - 1
        tm = 512
        blocks = n_p // tm
        pieces = min(PIPE, blocks)
        x_new = None
        for k in range(pieces):
            b0, b1 = k * blocks // pieces, (k + 1) * blocks // pieces
            r0, r1 = b0 * tm, (n_all if k == pieces - 1 else b1 * tm)
            y1, y2 = _take_rows(y, pos1[r0:r1]), _take_rows(y, pos2[r0:r1])
            x_new = _ple(xm_p, y1, y2, rt, p_p, x_new, *ple_w, layer=i, final=final, tm=tm, n_blocks=b1 - b0,
                         x_off=b0, y_off=0, rt_off=b0)
        xp = x_new
        xs = _ple(xm_s, y1, y2, rt, p_s, None, *ple_w, layer=i, final=final, tm=ns, n_blocks=1,
                  x_off=0, y_off=(n_p - r0) // ns, rt_off=n_p // ns)

    stack = lambda items, j: jnp.stack([it[j] for it in items])
    return (xp.reshape(nb, length, D_MODEL), xs.reshape(ns, 1, D_MODEL),
            stack(new_p, 0), stack(new_p, 1), stack(new_p, 2), stack(new_p, 3),
            new_s[0].reshape(DEPTH, ns, HEADS, DK, DV), stack(new_s5, 0), stack(new_s5, 1),
            new_s[1].reshape(DEPTH, ns, HEADS, DK, DV))
```

```python
import functools

import jax
import jax.numpy as jnp
import numpy as np
from jax import lax
from jax.experimental import pallas as pl
from jax.experimental.compute_on import compute_on
from jax.experimental.pallas import tpu as pltpu

F32, BF16, I32 = jnp.float32, jnp.bfloat16, jnp.int32

D_MODEL = 1024
DEPTH = 2
HEADS, DK, DV = 4, 64, 128
QK, VW = HEADS * DK, HEADS * DV
GLA_RANK, GLA_TAU = 16, 16.0
S5_WIDTH, S5_GROUP, S5_GROUPS, S5_STATE = 512, 16, 32, 64
S5_LANES = S5_GROUPS * S5_STATE
S5_SLABS = 4
MOE_GROUPS, MOE_PER_GROUP, MOE_EXPERTS, MOE_FF = 4, 8, 32, 256
PLE_DIM = 256
EPS = 1e-6

LANES = 128
CHUNK = 64
S5_CHUNK = 64
MOE_TILE = 512
PIPE = 3
SAFE_EXP = 80.0
VMEM_LIMIT = 56 * 1024 * 1024

C_GLA, C_S5, C_HG, C_GLR, C_END = 0, 1536, 2048, 3584, 3712
R_E1, R_E2, R_W1, R_W2, R_RANK1, R_RANK2 = 0, 1, 2, 3, 4, 5


def _cparams(*sem):
    return pltpu.CompilerParams(dimension_semantics=sem, vmem_limit_bytes=VMEM_LIMIT)


def _layer_spec(shape, layer):
    return pl.BlockSpec((None,) + tuple(shape), lambda *_: (layer,) + (0,) * len(shape), pipeline_mode=pl.Buffered(1))


def _dot(a, b):
    return jnp.dot(a, b, preferred_element_type=F32)


def _rms(x, g):
    return x * lax.rsqrt(jnp.mean(x * x, axis=-1, keepdims=True) + EPS) * g


def _log_sigmoid(x):
    return jnp.minimum(x, 0.0) - jnp.log1p(jnp.exp(-jnp.abs(x)))


def _sigmoid(x):
    return 1.0 / (1.0 + jnp.exp(-x))


def _silu(x):
    return x * _sigmoid(x)


def _split3(x):
    hi = x.astype(BF16)
    r1 = x - hi.astype(F32)
    mid = r1.astype(BF16)
    lo = (r1 - mid.astype(F32)).astype(BF16)
    return hi, mid, lo


def _dot01(m, parts):
    out = _dot(m, parts[0])
    for p in parts[1:]:
        out = out + _dot(m, p)
    return out


W_GLA, W_GLR, W_S5HG, W_END = 1536, 1552, 3600, 6672


def _w_in_layout_kernel(w_ref, wa_ref, wgm_ref):
    wa_ref[:, C_GLA:C_S5] = w_ref[:, 0:W_GLA].astype(BF16)
    wa_ref[:, C_S5:C_GLR] = w_ref[:, W_GLR:W_S5HG].astype(BF16)
    lane = lax.broadcasted_iota(I32, (w_ref.shape[0], LANES), 1)
    wa_ref[:, C_GLR:C_END] = jnp.where(lane < GLA_RANK, w_ref[:, W_GLA:W_GLA + LANES], 0.0).astype(BF16)
    wgm_ref[...] = w_ref[:, W_S5HG:W_END].astype(BF16)


def _w_in_layout(w_in):
    tr = 256
    return pl.pallas_call(
        _w_in_layout_kernel,
        grid=(DEPTH, D_MODEL // tr),
        in_specs=[pl.BlockSpec((None, tr, W_END), lambda d, i: (d, i, 0))],
        out_specs=[pl.BlockSpec((None, tr, C_END), lambda d, i: (d, i, 0)),
                   pl.BlockSpec((None, tr, 3 * D_MODEL), lambda d, i: (d, i, 0))],
        out_shape=[jax.ShapeDtypeStruct((DEPTH, D_MODEL, C_END), BF16),
                   jax.ShapeDtypeStruct((DEPTH, D_MODEL, 3 * D_MODEL), BF16)],
        compiler_params=_cparams("parallel", "parallel"),
        name="w_in_layout",
    )(w_in)
def _inproj_kernel(x_ref, nm_ref, w_ref, wg2_ref, bg_ref, lbl_ref, qk_ref, v_ref, gate_ref, g_ref, su_ref, *, layer, sub):
    lg = lbl_ref[...]
    mx = jnp.max(lg, axis=0, keepdims=True)
    ex = jnp.exp(lg - mx)
    sm = ex / jnp.sum(ex, axis=0, keepdims=True)
    cs = sm[0:1]
    for j in range(1, layer + 1):
        cs = cs + sm[j:j + 1]
    lb = cs - sm[0:1]
    log_lb, log_1mlb = jnp.log(lb), jnp.log1p(-lb)

    for r0 in range(0, x_ref.shape[0], sub):
        rs = slice(r0, r0 + sub)
        hb = _rms(x_ref[rs, :], nm_ref[...]).astype(BF16)

        def proj(lo, hi):
            return _dot(hb, w_ref[:, lo:hi])

        qk_ref[0, rs, 0:QK] = (proj(0, 256) * (DK ** -0.5)).astype(BF16)
        qk_ref[0, rs, QK:2 * QK] = proj(256, 512).astype(BF16)
        v_ref[0, rs, :] = proj(512, 1024).astype(BF16)
        gate_ref[0, rs, :] = _silu(proj(1024, 1536)).astype(BF16)
        glr = proj(C_GLR, C_END).astype(BF16)
        g_ref[0, rs, :] = _log_sigmoid(_dot(glr, wg2_ref[...]) + bg_ref[...]) * (1.0 / GLA_TAU)

        su_ref[rs, :] = proj(C_S5, C_HG)

        z = proj(C_HG + 256, C_HG + 512)
        a, c = log_lb, log_1mlb + _log_sigmoid(z)
        g_ref[1, rs, :] = jnp.maximum(a, c) + jnp.log1p(jnp.exp(-jnp.abs(a - c)))
        qk_ref[1, rs, 0:QK] = _silu(proj(C_HG, C_HG + 256)).astype(BF16)
        qk_ref[1, rs, QK:2 * QK] = ((1.0 - lb) * _sigmoid(-z)).astype(BF16)
        v_ref[1, rs, :] = proj(C_HG + 512, C_HG + 1024).astype(BF16)
        gate_ref[1, rs, :] = _silu(proj(C_HG + 1024, C_HG + 1536)).astype(BF16)


def _inproj(x, nm, w, wg2, bg, lbl, *, layer, tm):
    rows = x.shape[0]
    return pl.pallas_call(
        functools.partial(_inproj_kernel, layer=layer, sub=min(tm, 256)),
        grid=(rows // tm,),
        in_specs=[pl.BlockSpec((tm, D_MODEL), lambda i: (i, 0)),
                  _layer_spec((1, D_MODEL), layer),
                  _layer_spec((D_MODEL, C_END), layer),
                  _layer_spec((LANES, QK), layer),
                  _layer_spec((1, QK), layer),
                  pl.BlockSpec((DEPTH, QK), lambda i: (0, 0))],
        out_specs=[pl.BlockSpec((2, tm, 2 * QK), lambda i: (0, i, 0)),
                   pl.BlockSpec((2, tm, VW), lambda i: (0, i, 0)),
                   pl.BlockSpec((2, tm, VW), lambda i: (0, i, 0)),
                   pl.BlockSpec((2, tm, QK), lambda i: (0, i, 0)),
                   pl.BlockSpec((tm, S5_WIDTH), lambda i: (i, 0))],
        out_shape=[jax.ShapeDtypeStruct((2, rows, 2 * QK), BF16),
                   jax.ShapeDtypeStruct((2, rows, VW), BF16),
                   jax.ShapeDtypeStruct((2, rows, VW), BF16),
                   jax.ShapeDtypeStruct((2, rows, QK), F32),
                   jax.ShapeDtypeStruct((rows, S5_WIDTH), F32)],
        compiler_params=_cparams("parallel"),
        name="inproj",
    )(x, nm, w, wg2, bg, lbl)


def _head_stack(x):
    head = lax.broadcasted_iota(I32, x.shape, 1) // DK
    return jnp.concatenate([jnp.where(head == h, x, 0.0) for h in range(HEADS)], axis=0).astype(BF16)


def _stack_scores(qt, kt):
    return lax.dot_general(_head_stack(qt), kt.astype(BF16), (((1,), (1,)), ((), ())), preferred_element_type=F32)


def _chunk_kernel(qk_ref, v_ref, g_ref, o_ref, st_ref, bc_ref, sc_ref, *, nb, c):
    @pl.when(pl.program_id(1) == 0)
    def _():
        st_ref[...] = jnp.zeros_like(st_ref)

    row = lax.broadcasted_iota(I32, (c, c), 0)
    col = lax.broadcasted_iota(I32, (c, c), 1)
    tri = jnp.where(col <= row, 1.0, 0.0).astype(BF16)
    srow = lax.broadcasted_iota(I32, (HEADS * c, c), 0) & (c - 1)
    scol = lax.broadcasted_iota(I32, (HEADS * c, c), 1)
    mid = c // 2 - 1

    spread = None
    for b in range(nb):
        bc = _dot01(tri, _split3(g_ref[0, b]))
        bc_ref[b] = bc
        ref, last = bc[mid:mid + 1, :], bc[c - 1:c, :]
        s = jnp.maximum(jnp.max(-ref), jnp.max(ref - last))
        spread = s if spread is None else jnp.maximum(spread, s)

    def qk_of(b):
        return qk_ref[0, b, :, 0:QK].astype(F32), qk_ref[0, b, :, QK:2 * QK].astype(F32)

    def scores_one_reference():
        for b in range(nb):
            q, k = qk_of(b)
            bc = bc_ref[b]
            ref = bc[mid:mid + 1, :]
            s = _stack_scores(q * jnp.exp(bc - ref), k * jnp.exp(ref - bc))
            sc_ref[b] = jnp.where(scol <= srow, s, 0.0).astype(BF16)

    def scores_by_levels():
        qrow = lax.broadcasted_iota(I32, (c, QK), 0)
        for b in range(nb):
            q, k = qk_of(b)
            bc = bc_ref[b]
            parts = _split3(g_ref[0, b])
            acc = jnp.where(scol == srow, _stack_scores(q, k), 0.0)
            half = c // 2
            while half >= 1:
                blk = 2 * half
                last_low = (row & ~(blk - 1)) + (half - 1)
                ref = _dot01(jnp.where(col <= last_low, 1.0, 0.0).astype(BF16), parts)
                upper = (qrow & (blk - 1)) >= half
                dq = jnp.minimum(jnp.where(upper, bc - ref, 0.0), 0.0)
                dk = jnp.minimum(jnp.where(upper, 0.0, ref - bc), 0.0)
                s = _stack_scores(q * jnp.exp(dq), k * jnp.exp(dk))
                pair = ((srow & ~(blk - 1)) == (scol & ~(blk - 1))) & ((srow & (blk - 1)) >= half) & ((scol & (blk - 1)) < half)
                acc = acc + jnp.where(pair, s, 0.0)
                half //= 2
            sc_ref[b] = acc.astype(BF16)

    lax.cond(spread <= SAFE_EXP, scores_one_reference, scores_by_levels)

    for b in range(nb):
        q, k = qk_of(b)
        bc = bc_ref[b]
        last = bc[c - 1:c, :]
        v = v_ref[0, b]
        state = st_ref[0, b]
        o_inter = _dot(_head_stack(q * jnp.exp(bc)), state.astype(BF16))
        k_out = jnp.transpose(k * jnp.exp(last - bc)).astype(BF16)
        decay = jnp.transpose(jnp.broadcast_to(jnp.exp(last), (DV, QK)))
        for h in range(HEADS):
            rs, ls, ks = slice(h * c, (h + 1) * c), slice(h * DV, (h + 1) * DV), slice(h * DK, (h + 1) * DK)
            o_ref[0, b, :, ls] = (_dot(sc_ref[b, rs, :], v[:, ls]) + o_inter[rs, :]).astype(BF16)
            st_ref[0, b, ks, :] = decay[ks, :] * state[ks, :] + _dot(k_out[ks, :], v[:, ls])


def _chunk_scan(qk, v, g, *, nb, length):
    c = CHUNK
    blk = lambda w: pl.BlockSpec((1, nb, c, w), lambda br, i: (br, 0, i, 0))
    return pl.pallas_call(
        functools.partial(_chunk_kernel, nb=nb, c=c),
        grid=(2, length // c),
        in_specs=[blk(2 * QK), blk(VW), blk(QK)],
        out_specs=[blk(VW), pl.BlockSpec((1, nb, QK, DV), lambda br, i: (br, 0, 0, 0))],
        out_shape=[jax.ShapeDtypeStruct((2, nb, length, VW), BF16),
                   jax.ShapeDtypeStruct((2, nb, QK, DV), F32)],
        scratch_shapes=[pltpu.VMEM((nb, c, QK), F32), pltpu.VMEM((nb, HEADS * c, c), BF16)],
        compiler_params=_cparams("arbitrary", "arbitrary"),
        name="chunk_scan",
    )(qk, v, g)


def _decode_kernel(qk_ref, v_ref, g_ref, s0_ref, s1_ref, *rest, nt):
    o_ref, n0_ref, n1_ref = rest[-3:]
    for br, (s_ref, n_ref) in enumerate(((s0_ref, n0_ref), (s1_ref, n1_ref))):
        for j in range(nt):
            d = jnp.exp(g_ref[br, j:j + 1, :])
            q = qk_ref[br, j:j + 1, 0:QK].astype(F32)
            k = qk_ref[br, j:j + 1, QK:2 * QK].astype(F32)
            cols = jnp.transpose(jnp.concatenate([d, k, q, jnp.zeros((5, QK), F32)], axis=0))
            vrow = v_ref[br, j:j + 1, :].astype(F32)
            vfull = jnp.concatenate([jnp.broadcast_to(vrow[:, h * DV:(h + 1) * DV], (DK, DV)) for h in range(HEADS)], axis=0)
            new = cols[:, 0:1] * s_ref[j] + cols[:, 1:2] * vfull
            n_ref[j] = new
            t = cols[:, 2:3] * new
            for h in range(HEADS):
                o_ref[br, j:j + 1, h * DV:(h + 1) * DV] = jnp.sum(t[h * DK:(h + 1) * DK, :], axis=0, keepdims=True).astype(BF16)


def _decode(qk, v, g, s_gla, s_hg, prev, *, layer):
    n = qk.shape[1]
    nt = 8
    row = lambda w: pl.BlockSpec((2, nt, w), lambda i: (0, i, 0))
    st = pl.BlockSpec((None, nt, QK, DV), lambda i: (layer, i, 0, 0))
    carried = [] if prev is None else list(prev)
    first = 5
    return pl.pallas_call(
        functools.partial(_decode_kernel, nt=nt),
        grid=(n // nt,),
        in_specs=[row(2 * QK), row(VW), row(QK), st, st] + [pl.BlockSpec(memory_space=pl.ANY)] * len(carried),
        out_specs=[row(VW), st, st],
        out_shape=[jax.ShapeDtypeStruct((2, n, VW), BF16),
                   jax.ShapeDtypeStruct((DEPTH, n, QK, DV), F32),
                   jax.ShapeDtypeStruct((DEPTH, n, QK, DV), F32)],
        input_output_aliases={first + j: 1 + j for j in range(len(carried))},
        compiler_params=_cparams("parallel"),
        name="decode_step",
    )(qk, v, g, s_gla, s_hg, *carried)


def _s5_disc_kernel(lr_ref, li_ref, ldt_ref, br_ref, bi_ref, abr_ref, abi_ref, bbr_ref, bbi_ref):
    lr, li, dt = lr_ref[...], li_ref[...], jnp.exp(ldt_ref[...])
    mag = jnp.exp(lr * dt)
    ab_re, ab_im = mag * jnp.cos(li * dt), mag * jnp.sin(li * dt)
    den = lr * lr + li * li
    num_re = ab_re - 1.0
    coef_re = (num_re * lr + ab_im * li) / den
    coef_im = (ab_im * lr - num_re * li) / den
    br, bi = br_ref[...], bi_ref[...]
    abr_ref[...] = ab_re
    abi_ref[...] = ab_im
    bbr_ref[...] = coef_re * br - coef_im * bi
    bbi_ref[...] = coef_re * bi + coef_im * br


def _s5_discretise(a_re, a_im, log_dt, b_re, b_im):
    n = DEPTH * S5_GROUPS
    rep = lambda t: jnp.repeat(t.reshape(n, S5_STATE), S5_GROUP, axis=1)
    ldt = jnp.broadcast_to(log_dt.reshape(n, 1), (n, S5_STATE * S5_GROUP))
    shape = jax.ShapeDtypeStruct((n, S5_STATE * S5_GROUP), F32)
    ab_re, ab_im, bb_re, bb_im = pl.pallas_call(_s5_disc_kernel, out_shape=[shape] * 4, name="s5_discretise")(
        rep(a_re), rep(a_im), ldt, b_re.reshape(n, -1), b_im.reshape(n, -1))
    pole = lambda t: t[:, ::S5_GROUP].reshape(DEPTH, 1, S5_LANES)
    eye = jnp.eye(8, dtype=F32)

    def blockdiag_in(bb):
        t = bb.reshape(DEPTH, S5_SLABS, 8, S5_STATE, S5_GROUP).transpose(0, 1, 2, 4, 3)
        return jnp.einsum("dcgmp,gh->dcgmhp", t, eye).reshape(DEPTH, S5_SLABS, LANES, 8 * S5_STATE)

    w_in = jnp.concatenate([blockdiag_in(bb_re), blockdiag_in(bb_im)], axis=-1).astype(BF16)
    return w_in, pole(ab_re), pole(ab_im)


def _s5_blockdiag_out(c):
    t = c.reshape(DEPTH, S5_SLABS, 8, S5_GROUP, S5_STATE).transpose(0, 1, 2, 4, 3)
    return jnp.einsum("dcgpm,gh->dcgphm", t, jnp.eye(8, dtype=F32)).reshape(DEPTH, S5_SLABS, 8 * S5_STATE, LANES).astype(BF16)


def _time_major_perm(nb, ct):
    r = np.arange(nb * ct)
    p = np.zeros((nb * ct, nb * ct), np.float32)
    p[r, (r % nb) * ct + r // nb] = 1.0
    return p


def _gelu_tanh(y):
    return 0.5 * y * (1.0 + jnp.tanh(0.7978845608028654 * (y + 0.044715 * (y * y * y))))


def _s5_input(ub, win_ref, xre_ref, xim_ref):
    half = 8 * S5_STATE
    for s in range(S5_SLABS):
        r = _dot(ub[:, s * LANES:(s + 1) * LANES], win_ref[s])
        xre_ref[:, s * half:(s + 1) * half] = r[:, :half]
        xim_ref[:, s * half:(s + 1) * half] = r[:, half:]


def _s5_readout(xre_ref, xim_ref, cre_ref, cim_ref):
    half = 8 * S5_STATE
    ys = []
    for s in range(S5_SLABS):
        ls = slice(s * half, (s + 1) * half)
        ys.append(_dot(xre_ref[:, ls].astype(BF16), cre_ref[s]) - _dot(xim_ref[:, ls].astype(BF16), cim_ref[s]))
    return jnp.concatenate(ys, axis=-1)


def _s5_glu(y, u, d_ref, wglu_ref, bglu_ref):
    z = _gelu_tanh(y + d_ref[...] * u)
    return z * _sigmoid(_dot(z.astype(BF16), wglu_ref[...]) + bglu_ref[...])


def _s5_scan_kernel(su_ref, perm_ref, permt_ref, win_ref, are_ref, aim_ref, cre_ref, cim_ref, d_ref, wglu_ref, bglu_ref,
                    o_ref, hre_ref, him_ref, xre_ref, xim_ref, *, nb, ct):
    @pl.when(pl.program_id(0) == 0)
    def _():
        hre_ref[...] = jnp.zeros_like(hre_ref)
        him_ref[...] = jnp.zeros_like(him_ref)

    u = su_ref[...].reshape(nb * ct, S5_WIDTH)
    ub = _dot(perm_ref[...], u.astype(BF16)).astype(BF16)

    half = 8 * S5_STATE
    ys = []
    for s in range(S5_SLABS):
        ls = slice(s * half, (s + 1) * half)
        r = _dot(ub[:, s * LANES:(s + 1) * LANES], win_ref[s])
        xre_ref[:, ls] = r[:, :half]
        xim_ref[:, ls] = r[:, half:]
        ar = jnp.broadcast_to(are_ref[:, ls], (nb, half))
        ai = jnp.broadcast_to(aim_ref[:, ls], (nb, half))
        hr, hi = hre_ref[:, ls], him_ref[:, ls]
        for t in range(ct):
            rows = slice(t * nb, (t + 1) * nb)
            hr, hi = ar * hr - ai * hi + xre_ref[rows, ls], ar * hi + ai * hr + xim_ref[rows, ls]
            xre_ref[rows, ls] = hr
            xim_ref[rows, ls] = hi
        hre_ref[:, ls] = hr
        him_ref[:, ls] = hi
        ys.append(_dot(xre_ref[:, ls].astype(BF16), cre_ref[s]) - _dot(xim_ref[:, ls].astype(BF16), cim_ref[s]))

    y = jnp.concatenate(ys, axis=-1)
    y_hi = y.astype(BF16)
    y_lo = (y - y_hi.astype(F32)).astype(BF16)
    y = _dot01(permt_ref[...], (y_hi, y_lo))
    o_ref[...] = _s5_glu(y, u, d_ref, wglu_ref, bglu_ref).reshape(nb, ct, S5_WIDTH).astype(BF16)


def _s5_weight_specs(layer):
    return [_layer_spec((S5_SLABS, LANES, 2 * 8 * S5_STATE), layer),
            _layer_spec((1, S5_LANES), layer), _layer_spec((1, S5_LANES), layer),
            _layer_spec((S5_SLABS, 8 * S5_STATE, LANES), layer), _layer_spec((S5_SLABS, 8 * S5_STATE, LANES), layer),
            _layer_spec((1, S5_WIDTH), layer), _layer_spec((S5_WIDTH, S5_WIDTH), layer),
            _layer_spec((1, S5_WIDTH), layer)]


def _s5_scan(su, weights, *, nb, length, layer):
    ct = S5_CHUNK
    perm = _time_major_perm(nb, ct)
    const = pl.BlockSpec((nb * ct, nb * ct), lambda i: (0, 0))
    return pl.pallas_call(
        functools.partial(_s5_scan_kernel, nb=nb, ct=ct),
        grid=(length // ct,),
        in_specs=[pl.BlockSpec((nb, ct, S5_WIDTH), lambda i: (0, i, 0)), const, const] + _s5_weight_specs(layer),
        out_specs=[pl.BlockSpec((nb, ct, S5_WIDTH), lambda i: (0, i, 0)),
                   pl.BlockSpec((nb, S5_LANES), lambda i: (0, 0)), pl.BlockSpec((nb, S5_LANES), lambda i: (0, 0))],
        out_shape=[jax.ShapeDtypeStruct((nb, length, S5_WIDTH), BF16),
                   jax.ShapeDtypeStruct((nb, S5_LANES), F32), jax.ShapeDtypeStruct((nb, S5_LANES), F32)],
        scratch_shapes=[pltpu.VMEM((nb * ct, S5_LANES), F32), pltpu.VMEM((nb * ct, S5_LANES), F32)],
        compiler_params=_cparams("arbitrary"),
        name="s5_scan",
    )(su, jnp.asarray(perm, BF16), jnp.asarray(perm.T, BF16), *weights)


def _s5_step_kernel(su_ref, h0r_ref, h0i_ref, win_ref, are_ref, aim_ref, cre_ref, cim_ref, d_ref, wglu_ref, bglu_ref,
                    o_ref, hre_ref, him_ref):
    u = su_ref[...]
    _s5_input(u.astype(BF16), win_ref, hre_ref, him_ref)
    ar, ai = are_ref[...], aim_ref[...]
    h0r, h0i = h0r_ref[...], h0i_ref[...]
    nr = ar * h0r - ai * h0i + hre_ref[...]
    ni = ar * h0i + ai * h0r + him_ref[...]
    hre_ref[...] = nr
    him_ref[...] = ni
    o_ref[...] = _s5_glu(_s5_readout(hre_ref, him_ref, cre_ref, cim_ref), u, d_ref, wglu_ref, bglu_ref).astype(BF16)


def _s5_step(su, h0r, h0i, weights, *, layer):
    n = su.shape[0]
    full = lambda w: pl.BlockSpec((n, w), lambda i: (0, 0))
    state = pl.BlockSpec((None, n, S5_LANES), lambda i: (layer, 0, 0))
    return pl.pallas_call(
        _s5_step_kernel,
        grid=(1,),
        in_specs=[full(S5_WIDTH), state, state] + _s5_weight_specs(layer),
        out_specs=[full(S5_WIDTH), full(S5_LANES), full(S5_LANES)],
        out_shape=[jax.ShapeDtypeStruct((n, S5_WIDTH), BF16),
                   jax.ShapeDtypeStruct((n, S5_LANES), F32), jax.ShapeDtypeStruct((n, S5_LANES), F32)],
        compiler_params=_cparams("arbitrary"),
        name="s5_step",
    )(su, h0r, h0i, *weights)


def _head_norm(o, g):
    parts = []
    for h in range(HEADS):
        seg = o[:, h * DV:(h + 1) * DV]
        parts.append(seg * lax.rsqrt(jnp.mean(seg * seg, axis=-1, keepdims=True) + EPS))
    return jnp.concatenate(parts, axis=-1) * g


def _route(logits, count):
    rows = logits.shape[0]
    lane = lax.broadcasted_iota(I32, logits.shape, 1)
    neg = -jnp.inf
    first = lambda hit: jnp.min(jnp.where(hit, lane, LANES), axis=-1, keepdims=True)
    glog = jnp.where(lane < MOE_GROUPS, logits, neg)
    gmax = jnp.max(glog, axis=-1, keepdims=True)
    gidx = first(glog == gmax)
    gw = 1.0 / jnp.sum(jnp.where(lane < MOE_GROUPS, jnp.exp(logits - gmax), 0.0), axis=-1, keepdims=True)
    inside = (lane >= MOE_GROUPS) & (lane < MOE_GROUPS + MOE_EXPERTS) & (((lane - MOE_GROUPS) >> 3) == gidx)
    el = jnp.where(inside, logits, neg)
    v1 = jnp.max(el, axis=-1, keepdims=True)
    i1 = first(el == v1)
    el2 = jnp.where(lane == i1, neg, el)
    v2 = jnp.max(el2, axis=-1, keepdims=True)
    i2 = first(el2 == v2)
    p2 = jnp.exp(v2 - v1)
    w1 = gw / (1.0 + p2)
    w2 = gw * p2 / (1.0 + p2)
    hit1, hit2 = lane == i1, lane == i2
    onehot = jnp.where(hit1 | hit2, 1.0, 0.0)
    tr = lax.broadcasted_iota(I32, (rows, rows), 0)
    tc = lax.broadcasted_iota(I32, (rows, rows), 1)
    before = _dot(jnp.where(tc < tr, 1.0, 0.0).astype(BF16), onehot.astype(BF16)) + count
    rank1 = jnp.sum(jnp.where(hit1, before, 0.0), axis=-1, keepdims=True)
    rank2 = jnp.sum(jnp.where(hit2, before, 0.0), axis=-1, keepdims=True)
    slab = jnp.zeros(logits.shape, F32)
    for ln, val in ((R_E1, (i1 - MOE_GROUPS).astype(F32)), (R_E2, (i2 - MOE_GROUPS).astype(F32)), (R_W1, w1), (R_W2, w2),
                    (R_RANK1, rank1), (R_RANK2, rank2)):
        slab = jnp.where(lane == ln, val, slab)
    return slab, count + jnp.sum(onehot, axis=0, keepdims=True)


def _merge_kernel(x_ref, o_ref, gate_ref, os5_ref, cnt0_ref, nm_ref, wgm_ref, gn_ref, wbr_ref, wout_ref, nf_ref,
                  wr_ref, br_ref, *rest, sub):
    xm_ref, h2_ref, rt_ref, rtt_ref, cnt_ref = rest[-5:]

    @pl.when(pl.program_id(0) == 0)
    def _():
        cnt_ref[...] = cnt0_ref[...]

    logits = []
    for r0 in range(0, x_ref.shape[0], sub):
        rs = slice(r0, r0 + sub)
        x = x_ref[rs, :]
        hb = _rms(x, nm_ref[...]).astype(BF16)
        mixed = None
        for i, src in enumerate((0, None, 1)):
            if src is None:
                branch = os5_ref[rs, :]
            else:
                branch = (_head_norm(o_ref[src, rs, :].astype(F32), gn_ref[src]) * gate_ref[src, rs, :].astype(F32)).astype(BF16)
            gate = _sigmoid(_dot(hb, wgm_ref[:, i * D_MODEL:(i + 1) * D_MODEL]))
            term = gate * _dot(branch, wbr_ref[i])
            mixed = term if mixed is None else mixed + term
        xm = x + _dot(mixed.astype(BF16), wout_ref[...])
        xm_ref[rs, :] = xm
        h2 = _rms(xm, nf_ref[...])
        h2_hi = h2.astype(BF16)
        h2_lo = (h2 - h2_hi.astype(F32)).astype(BF16)
        h2_ref[rs, :] = h2_hi
        both = _dot(h2_hi, wr_ref[...])
        logits.append(both[:, :LANES] + (both[:, LANES:] + _dot(h2_lo, wr_ref[:, :LANES])) + br_ref[...])
    slab, cnt_ref[...] = _route(jnp.concatenate(logits, axis=0), cnt_ref[...])
    rt_ref[...] = slab
    rtt_ref[...] = jnp.transpose(slab)[0:8, :]


def _merge(x, o, gate, os5, cnt0, weights, carried, *, layer, tm, total_rows, block_offset):
    rows = x.shape[0]
    tile = lambda w: pl.BlockSpec((tm, w), lambda i: (i, 0))
    shared = lambda w: pl.BlockSpec((tm, w), lambda i: (i + block_offset, 0))
    carried = [] if carried is None else list(carried)
    w_specs = [_layer_spec((1, D_MODEL), layer), _layer_spec((D_MODEL, 3 * D_MODEL), layer),
               _layer_spec((2, 1, VW), layer), _layer_spec((3, VW, D_MODEL), layer),
               _layer_spec((D_MODEL, D_MODEL), layer), _layer_spec((1, D_MODEL), layer),
               _layer_spec((D_MODEL, 2 * LANES), layer), _layer_spec((1, LANES), layer)]
    first = 5 + len(w_specs)
    return pl.pallas_call(
        functools.partial(_merge_kernel, sub=min(tm, 256)),
        grid=(rows // tm,),
        in_specs=[tile(D_MODEL),
                  pl.BlockSpec((2, tm, VW), lambda i: (0, i, 0)),
                  pl.BlockSpec((2, tm, VW), lambda i: (0, i, 0)),
                  tile(S5_WIDTH),
                  pl.BlockSpec((1, LANES), lambda i: (0, 0))] + w_specs + [pl.BlockSpec(memory_space=pl.ANY)] * len(carried),
        out_specs=[tile(D_MODEL), shared(D_MODEL), shared(LANES),
                   pl.BlockSpec((8, tm), lambda i: (0, i + block_offset)), pl.BlockSpec((1, LANES), lambda i: (0, 0))],
        out_shape=[jax.ShapeDtypeStruct((rows, D_MODEL), F32),
                   jax.ShapeDtypeStruct((total_rows, D_MODEL), BF16),
                   jax.ShapeDtypeStruct((total_rows, LANES), F32),
                   jax.ShapeDtypeStruct((8, total_rows), F32),
                   jax.ShapeDtypeStruct((1, LANES), F32)],
        input_output_aliases={first + j: 1 + j for j in range(len(carried))},
        compiler_params=_cparams("arbitrary"),
        name="merge_route",
    )(x, o, gate, os5, cnt0, *weights, *carried)


def _ffn_kernel(te_ref, nu_ref, x_ref, wg_ref, wu_ref, wd_ref, *rest, tile0):
    y_ref, wg_bf, wu_bf, wd_bf = rest[-4:]
    i = pl.program_id(0)
    tile = i + tile0

    @pl.when(tile < nu_ref[0])
    def _():
        @pl.when((i == 0) | (te_ref[tile] != te_ref[jnp.maximum(tile - 1, 0)]))
        def _():
            wg_bf[...] = wg_ref[...].astype(BF16)
            wu_bf[...] = wu_ref[...].astype(BF16)
            wd_bf[...] = wd_ref[...].astype(BF16)

        x = x_ref[...]
        act = _silu(_dot(x, wg_bf[...])) * _dot(x, wu_bf[...])
        y_ref[...] = _dot(act.astype(BF16), wd_bf[...]).astype(BF16)


def _grouped_ffn(tile_expert, n_used, xg, y_prev, wg, wu, wd, *, layer, tile0, total_rows):
    used = lambda i, nu: jnp.maximum(jnp.minimum(i + tile0, nu[0] - 1), tile0)
    expert = lambda a, b: pl.BlockSpec((None, None, a, b), lambda i, te, nu: (layer, te[used(i, nu)], 0, 0))
    carried = [] if y_prev is None else [y_prev]
    grid_spec = pltpu.PrefetchScalarGridSpec(
        num_scalar_prefetch=2,
        grid=(xg.shape[0] // MOE_TILE,),
        in_specs=[pl.BlockSpec((MOE_TILE, D_MODEL), lambda i, te, nu: (used(i, nu) - tile0, 0)),
                  expert(D_MODEL, MOE_FF), expert(D_MODEL, MOE_FF), expert(MOE_FF, D_MODEL)]
        + [pl.BlockSpec(memory_space=pl.ANY)] * len(carried),
        out_specs=pl.BlockSpec((MOE_TILE, D_MODEL), lambda i, te, nu: (used(i, nu), 0)),
        scratch_shapes=[pltpu.VMEM((D_MODEL, MOE_FF), BF16), pltpu.VMEM((D_MODEL, MOE_FF), BF16),
                        pltpu.VMEM((MOE_FF, D_MODEL), BF16)],
    )
    return pl.pallas_call(
        functools.partial(_ffn_kernel, tile0=tile0),
        grid_spec=grid_spec,
        out_shape=jax.ShapeDtypeStruct((total_rows, D_MODEL), BF16),
        input_output_aliases={6: 0} if carried else {},
        compiler_params=_cparams("arbitrary"),
        name="expert_ffn",
    )(tile_expert, n_used, xg, wg, wu, wd, *carried)


@compute_on("tpu_sparsecore")
@jax.jit
def _take_rows(x, idx):
    return jnp.take(x, idx, axis=0, mode="clip")


def _moe_plan(route_t, count):
    tokens = route_t.shape[1]
    n_tiles = -(-(2 * tokens + MOE_EXPERTS * (MOE_TILE - 1)) // MOE_TILE)
    n_rows = n_tiles * MOE_TILE
    counts = count[0, MOE_GROUPS:MOE_GROUPS + MOE_EXPERTS].astype(I32)
    padded = ((counts + MOE_TILE - 1) // MOE_TILE) * MOE_TILE
    gend = jnp.cumsum(padded)
    gstart = gend - padded
    experts = jnp.arange(MOE_EXPERTS, dtype=I32)[:, None]

    def rows_of(e_lane, rank_lane):
        e = route_t[e_lane].astype(I32)
        return jnp.sum(jnp.where(e[None, :] == experts, gstart[:, None], 0), axis=0) + route_t[rank_lane].astype(I32)

    pos1, pos2 = rows_of(R_E1, R_RANK1), rows_of(R_E2, R_RANK2)
    token = jnp.arange(tokens, dtype=I32) + 1
    marked = jnp.zeros((n_rows,), I32).at[jnp.concatenate([pos1, pos2])].add(jnp.concatenate([token, token]),
                                                                             unique_indices=True)
    filler = jnp.arange(n_rows, dtype=I32) % tokens
    src = jnp.where(marked > 0, marked - 1, filler)
    tile_start = jnp.arange(n_tiles, dtype=I32) * MOE_TILE
    tile_expert = jnp.minimum(jnp.sum((tile_start[:, None] >= gend[None, :]).astype(I32), axis=1), MOE_EXPERTS - 1)
    n_used = (gend[-1] // MOE_TILE).astype(I32).reshape(1)
    return src, pos1, pos2, tile_expert, n_used


def _ple_kernel(xm_ref, y1_ref, y2_ref, rt_ref, p_ref, np_ref, wg_ref, wp_ref, nfin_ref, *rest, final):
    out_ref = rest[-1]
    rt = rt_ref[...]
    x1 = xm_ref[...] + rt[:, R_W1:R_W1 + 1] * y1_ref[...].astype(F32) + rt[:, R_W2:R_W2 + 1] * y2_ref[...].astype(F32)
    gate = _sigmoid(_dot(_rms(x1, np_ref[...]).astype(BF16), wg_ref[...]))
    x2 = x1 + gate * _dot(p_ref[...].astype(BF16), wp_ref[...])
    out_ref[...] = _rms(x2, nfin_ref[...]) if final else x2


def _ple(xm, y1, y2, route, p, x_prev, npl, wg, wp, nfin, *, layer, final, tm, n_blocks, x_off, y_off, rt_off):
    at = lambda w, off: pl.BlockSpec((tm, w), lambda i: (i + off, 0))
    carried = [] if x_prev is None else [x_prev]
    return pl.pallas_call(
        functools.partial(_ple_kernel, final=final),
        grid=(n_blocks,),
        in_specs=[at(D_MODEL, x_off), at(D_MODEL, y_off), at(D_MODEL, y_off), at(LANES, rt_off),
                  pl.BlockSpec((None, tm, PLE_DIM), lambda i: (layer, i + x_off, 0)),
                  _layer_spec((1, D_MODEL), layer), _layer_spec((D_MODEL, D_MODEL), layer),
                  _layer_spec((PLE_DIM, D_MODEL), layer), pl.BlockSpec((1, D_MODEL), lambda i: (0, 0))]
        + [pl.BlockSpec(memory_space=pl.ANY)] * len(carried),
        out_specs=at(D_MODEL, x_off),
        out_shape=jax.ShapeDtypeStruct(xm.shape, F32),
        input_output_aliases={9: 0} if carried else {},
        compiler_params=_cparams("parallel"),
        name="combine_ple",
    )(xm, y1, y2, route, p, npl, wg, wp, nfin, *carried)


def kernel(x_prompt, x_sample, state_gla, state_s5_re, state_s5_im, state_hgrn, p_prompt, p_sample, norm_mix, w_in, gla_w_gate2, gla_b_gate, gla_norm, s5_a_re, s5_a_im, s5_log_dt, s5_b_re, s5_b_im, s5_c_re, s5_c_im, s5_d, s5_w_glu, s5_b_glu, hgrn_lb_logits, hgrn_norm, w_br_gla, w_br_s5, w_br_hgrn, w_out, norm_ffn, moe_w_group, moe_b_group, moe_w_expert, moe_b_expert, moe_w_gate, moe_w_up, moe_w_down, norm_ple, w_ple_gate, w_ple_proj, norm_final):
    nb, length, _ = x_prompt.shape
    ns = x_sample.shape[0]
    n_p = nb * length
    n_all = n_p + ns
    row = lambda t: t.reshape(DEPTH, 1, -1)

    w_a, w_gm = _w_in_layout(w_in)
    wg2 =jnp.concatenate([gla_w_gate2, jnp.zeros((DEPTH, LANES - GLA_RANK, QK), F32)], axis=1).astype(BF16)
    s5_win, s5_are, s5_aim = _s5_discretise(s5_a_re, s5_a_im, s5_log_dt, s5_b_re, s5_b_im)
    s5_w = (s5_win, s5_are, s5_aim, _s5_blockdiag_out(s5_c_re), _s5_blockdiag_out(s5_c_im), row(s5_d),
            s5_w_glu.astype(BF16), row(s5_b_glu))
    head_gain = jnp.stack([jnp.tile(gla_norm, (1, HEADS)), jnp.tile(hgrn_norm, (1, HEADS))], axis=1).reshape(DEPTH, 2, 1, VW)
    w_router = jnp.concatenate([moe_w_group, moe_w_expert,
                                jnp.zeros((DEPTH, D_MODEL, LANES - MOE_GROUPS - MOE_EXPERTS), F32)], axis=-1)
    wr_hi = w_router.astype(BF16)
    wr_lo = (w_router - wr_hi.astype(F32)).astype(BF16)
    wr_hilo = jnp.concatenate([wr_hi, wr_lo], axis=-1)
    b_router = jnp.concatenate([moe_b_group, moe_b_expert,
                                jnp.zeros((DEPTH, LANES - MOE_GROUPS - MOE_EXPERTS), F32)], axis=-1).reshape(DEPTH, 1, LANES)
    merge_w = (row(norm_mix), w_gm, head_gain,
               jnp.stack([w_br_gla, w_br_s5, w_br_hgrn], axis=1).astype(BF16), w_out.astype(BF16), row(norm_ffn),
               wr_hilo, b_router)
    ple_w = (row(norm_ple), w_ple_gate.astype(BF16), w_ple_proj.astype(BF16), norm_final.reshape(1, D_MODEL))
    nm, bg = row(norm_mix), row(gla_b_gate)
    p_p, p_s = p_prompt.reshape(DEPTH, n_p, PLE_DIM), p_sample.reshape(DEPTH, ns, PLE_DIM)
    sg_in, sh_in = state_gla.reshape(DEPTH, ns, QK, DV), state_hgrn.reshape(DEPTH, ns, QK, DV)
    s5r_in, s5i_in = state_s5_re.reshape(DEPTH, ns, S5_LANES), state_s5_im.reshape(DEPTH, ns, S5_LANES)

    xp = x_prompt.reshape(n_p, D_MODEL)
    xs = x_sample.reshape(ns, D_MODEL)
    new_p, new_s5 = [], []
    new_s = None
    for i in range(DEPTH):
        qk, v, gate, g, su = _inproj(xp, nm, w_a, wg2, bg, hgrn_lb_logits, layer=i, tm=1024)
        o_p, st_p = _chunk_scan(qk.reshape(2, nb, length, -1), v.reshape(2, nb, length, -1),
                                g.reshape(2, nb, length, -1), nb=nb, length=length)
        os5_p, hre_p, him_p = _s5_scan(su.reshape(nb, length, -1), s5_w, nb=nb, length=length, layer=i)
        xm_p, h2, rt, rtt, cnt = _merge(xp, o_p.reshape(2, n_p, VW), gate, os5_p.reshape(n_p, -1),
                                        jnp.zeros((1, LANES), F32), merge_w, None, layer=i, tm=1024, total_rows=n_all,
                                        block_offset=0)
        new_p.append((st_p[0].reshape(nb, HEADS, DK, DV), hre_p.reshape(nb, S5_GROUPS, S5_STATE),
                      him_p.reshape(nb, S5_GROUPS, S5_STATE), st_p[1].reshape(nb, HEADS, DK, DV)))

        qk, v, gate, g, su = _inproj(xs, nm, w_a, wg2, bg, hgrn_lb_logits, layer=i, tm=ns)
        o_s, *new_s = _decode(qk, v, g, sg_in, sh_in, new_s, layer=i)
        os5_s, hre_s, him_s = _s5_step(su, s5r_in, s5i_in, s5_w, layer=i)
        xm_s, h2, rt, rtt, cnt = _merge(xs, o_s, gate, os5_s, cnt, merge_w, (h2, rt, rtt), layer=i, tm=ns,
                                        total_rows=n_all, block_offset=n_p // ns)
        new_s5.append((hre_s.reshape(ns, S5_GROUPS, S5_STATE), him_s.reshape(ns, S5_GROUPS, S5_STATE)))

        src, pos1, pos2, tile_expert, n_used = _moe_plan(rtt, cnt)
        n_tiles = tile_expert.shape[0]
        y = None
        for k in range(PIPE):
            t0, t1 = k * n_tiles // PIPE, (k + 1) * n_tiles // PIPE
            y = _grouped_ffn(tile_expert, n_used, _take_rows(h2, src[t0 * MOE_TILE:t1 * MOE_TILE]), y,
                             moe_w_gate, moe_w_up, moe_w_down, layer=i, tile0=t0, total_rows=n_tiles * MOE_TILE)

        final = i == DEPTH - 1
        tm = 512
        blocks = n_p // tm
        pieces = min(PIPE, blocks)
        x_new = None
        for k in range(pieces):
            b0, b1 = k * blocks // pieces, (k + 1) * blocks // pieces
            r0, r1 = b0 * tm, (n_all if k == pieces - 1 else b1 * tm)
            y1, y2 = _take_rows(y, pos1[r0:r1]), _take_rows(y, pos2[r0:r1])
            x_new = _ple(xm_p, y1, y2, rt, p_p, x_new, *ple_w, layer=i, final=final, tm=tm, n_blocks=b1 - b0,
                         x_off=b0, y_off=0, rt_off=b0)
        xp = x_new
        xs = _ple(xm_s, y1, y2, rt, p_s, None, *ple_w, layer=i, final=final, tm=ns, n_blocks=1,
                  x_off=0, y_off=(n_p - r0) // ns, rt_off=n_p // ns)

    stack = lambda items, j: jnp.stack([it[j] for it in items])
    return (xp.reshape(nb, length, D_MODEL), xs.reshape(ns, 1, D_MODEL),
            stack(new_p, 0), stack(new_p, 1), stack(new_p, 2), stack(new_p, 3),
            new_s[0].reshape(DEPTH, ns, HEADS, DK, DV), stack(new_s5, 0), stack(new_s5, 1),
            new_s[1].reshape(DEPTH, ns, HEADS, DK, DV))
```

```python
import functools

import jax
import jax.numpy as jnp
import numpy as np
from jax import lax
from jax.experimental import pallas as pl
from jax.experimental.compute_on import compute_on
from jax.experimental.pallas import tpu as pltpu

F32, BF16, I32 = jnp.float32, jnp.bfloat16, jnp.int32

D_MODEL = 1024
DEPTH = 2
HEADS, DK, DV = 4, 64, 128
QK, VW = HEADS * DK, HEADS * DV
GLA_RANK, GLA_TAU = 16, 16.0
S5_WIDTH, S5_GROUP, S5_GROUPS, S5_STATE = 512, 16, 32, 64
S5_LANES = S5_GROUPS * S5_STATE
S5_SLABS = 4
MOE_GROUPS, MOE_PER_GROUP, MOE_EXPERTS, MOE_FF = 4, 8, 32, 256
PLE_DIM = 256
EPS = 1e-6

LANES = 128
CHUNK = 64
S5_CHUNK = 64
MOE_TILE = 512
PIPE = 2
SAFE_EXP = 80.0
VMEM_LIMIT = 56 * 1024 * 1024

C_GLA, C_S5, C_HG, C_GLR, C_END = 0, 1536, 2048, 3584, 3712
R_E1, R_E2, R_W1, R_W2, R_RANK1, R_RANK2 = 0, 1, 2, 3, 4, 5


def _cparams(*sem):
    return pltpu.CompilerParams(dimension_semantics=sem, vmem_limit_bytes=VMEM_LIMIT)


def _layer_spec(shape, layer):
    return pl.BlockSpec((None,) + tuple(shape), lambda *_: (layer,) + (0,) * len(shape), pipeline_mode=pl.Buffered(1))


def _dot(a, b):
    return jnp.dot(a, b, preferred_element_type=F32)


def _rms(x, g):
    return x * lax.rsqrt(jnp.mean(x * x, axis=-1, keepdims=True) + EPS) * g


def _log_sigmoid(x):
    return jnp.minimum(x, 0.0) - jnp.log1p(jnp.exp(-jnp.abs(x)))


def _sigmoid(x):
    return 1.0 / (1.0 + jnp.exp(-x))


def _silu(x):
    return x * _sigmoid(x)


def _split3(x):
    hi = x.astype(BF16)
    r1 = x - hi.astype(F32)
    mid = r1.astype(BF16)
    lo = (r1 - mid.astype(F32)).astype(BF16)
    return hi, mid, lo


def _dot01(m, parts):
    out = _dot(m, parts[0])
    for p in parts[1:]:
        out = out + _dot(m, p)
    return out


W_GLA, W_GLR, W_S5HG, W_END = 1536, 1552, 3600, 6672


def _w_in_layout_kernel(wt_ref, wa_ref, wgm_ref):
    cols = lambda lo, hi: jnp.transpose(wt_ref[lo:hi, :])
    wa_ref[:, C_GLA:C_S5] = cols(0, W_GLA).astype(BF16)
    wa_ref[:, C_S5:C_GLR] = cols(W_GLR, W_S5HG).astype(BF16)
    lane = lax.broadcasted_iota(I32, (LANES, LANES), 1)
    wa_ref[:, C_GLR:C_END] = jnp.where(lane < GLA_RANK, cols(W_GLA, W_GLA + LANES), 0.0).astype(BF16)
    wgm_ref[...] = cols(W_S5HG, W_END).astype(BF16)


def _w_in_layout(w_in):
    wt = jnp.swapaxes(w_in, 1, 2)
    return pl.pallas_call(
        _w_in_layout_kernel,
        grid=(DEPTH, D_MODEL // LANES),
        in_specs=[pl.BlockSpec((None, W_END, LANES), lambda d, i: (d, 0, i))],
        out_specs=[pl.BlockSpec((None, LANES, C_END), lambda d, i: (d, i, 0)),
                   pl.BlockSpec((None, LANES, 3 * D_MODEL), lambda d, i: (d, i, 0))],
        out_shape=[jax.ShapeDtypeStruct((DEPTH, D_MODEL, C_END), BF16),
                   jax.ShapeDtypeStruct((DEPTH, D_MODEL, 3 * D_MODEL), BF16)],
        compiler_params=_cparams("parallel", "parallel"),
        name="w_in_layout",
    )(wt)
def _inproj_kernel(x_ref, nm_ref, w_ref, wg2_ref, bg_ref, lbl_ref, qk_ref, v_ref, gate_ref, g_ref, su_ref, *, layer, sub):
    lg = lbl_ref[...]
    mx = jnp.max(lg, axis=0, keepdims=True)
    ex = jnp.exp(lg - mx)
    sm = ex / jnp.sum(ex, axis=0, keepdims=True)
    cs = sm[0:1]
    for j in range(1, layer + 1):
        cs = cs + sm[j:j + 1]
    lb = cs - sm[0:1]
    log_lb, log_1mlb = jnp.log(lb), jnp.log1p(-lb)

    for r0 in range(0, x_ref.shape[0], sub):
        rs = slice(r0, r0 + sub)
        hb = _rms(x_ref[rs, :], nm_ref[...]).astype(BF16)

        def proj(lo, hi):
            return _dot(hb, w_ref[:, lo:hi])

        qk_ref[0, rs, 0:QK] = (proj(0, 256) * (DK ** -0.5)).astype(BF16)
        qk_ref[0, rs, QK:2 * QK] = proj(256, 512).astype(BF16)
        v_ref[0, rs, :] = proj(512, 1024).astype(BF16)
        gate_ref[0, rs, :] = _silu(proj(1024, 1536)).astype(BF16)
        glr = proj(C_GLR, C_END).astype(BF16)
        g_ref[0, rs, :] = _log_sigmoid(_dot(glr, wg2_ref[...]) + bg_ref[...]) * (1.0 / GLA_TAU)

        su_ref[rs, :] = proj(C_S5, C_HG)

        z = proj(C_HG + 256, C_HG + 512)
        a, c = log_lb, log_1mlb + _log_sigmoid(z)
        g_ref[1, rs, :] = jnp.maximum(a, c) + jnp.log1p(jnp.exp(-jnp.abs(a - c)))
        qk_ref[1, rs, 0:QK] = _silu(proj(C_HG, C_HG + 256)).astype(BF16)
        qk_ref[1, rs, QK:2 * QK] = ((1.0 - lb) * _sigmoid(-z)).astype(BF16)
        v_ref[1, rs, :] = proj(C_HG + 512, C_HG + 1024).astype(BF16)
        gate_ref[1, rs, :] = _silu(proj(C_HG + 1024, C_HG + 1536)).astype(BF16)


def _inproj(x, nm, w, wg2, bg, lbl, *, layer, tm):
    rows = x.shape[0]
    return pl.pallas_call(
        functools.partial(_inproj_kernel, layer=layer, sub=min(tm, 256)),
        grid=(rows // tm,),
        in_specs=[pl.BlockSpec((tm, D_MODEL), lambda i: (i, 0)),
                  _layer_spec((1, D_MODEL), layer),
                  _layer_spec((D_MODEL, C_END), layer),
                  _layer_spec((LANES, QK), layer),
                  _layer_spec((1, QK), layer),
                  pl.BlockSpec((DEPTH, QK), lambda i: (0, 0))],
        out_specs=[pl.BlockSpec((2, tm, 2 * QK), lambda i: (0, i, 0)),
                   pl.BlockSpec((2, tm, VW), lambda i: (0, i, 0)),
                   pl.BlockSpec((2, tm, VW), lambda i: (0, i, 0)),
                   pl.BlockSpec((2, tm, QK), lambda i: (0, i, 0)),
                   pl.BlockSpec((tm, S5_WIDTH), lambda i: (i, 0))],
        out_shape=[jax.ShapeDtypeStruct((2, rows, 2 * QK), BF16),
                   jax.ShapeDtypeStruct((2, rows, VW), BF16),
                   jax.ShapeDtypeStruct((2, rows, VW), BF16),
                   jax.ShapeDtypeStruct((2, rows, QK), F32),
                   jax.ShapeDtypeStruct((rows, S5_WIDTH), F32)],
        compiler_params=_cparams("parallel"),
        name="inproj",
    )(x, nm, w, wg2, bg, lbl)


def _head_stack(x):
    head = lax.broadcasted_iota(I32, x.shape, 1) // DK
    return jnp.concatenate([jnp.where(head == h, x, 0.0) for h in range(HEADS)], axis=0).astype(BF16)


def _stack_scores(qt, kt):
    return lax.dot_general(_head_stack(qt), kt.astype(BF16), (((1,), (1,)), ((), ())), preferred_element_type=F32)


def _chunk_kernel(qk_ref, v_ref, g_ref, o_ref, st_ref, bc_ref, sc_ref, *, nb, c):
    @pl.when(pl.program_id(1) == 0)
    def _():
        st_ref[...] = jnp.zeros_like(st_ref)

    row = lax.broadcasted_iota(I32, (c, c), 0)
    col = lax.broadcasted_iota(I32, (c, c), 1)
    tri = jnp.where(col <= row, 1.0, 0.0).astype(BF16)
    srow = lax.broadcasted_iota(I32, (HEADS * c, c), 0) & (c - 1)
    scol = lax.broadcasted_iota(I32, (HEADS * c, c), 1)
    mid = c // 2 - 1

    spread = None
    for b in range(nb):
        bc = _dot01(tri, _split3(g_ref[0, b]))
        bc_ref[b] = bc
        ref, last = bc[mid:mid + 1, :], bc[c - 1:c, :]
        s = jnp.maximum(jnp.max(-ref), jnp.max(ref - last))
        spread = s if spread is None else jnp.maximum(spread, s)

    def qk_of(b):
        return qk_ref[0, b, :, 0:QK].astype(F32), qk_ref[0, b, :, QK:2 * QK].astype(F32)

    def scores_one_reference():
        for b in range(nb):
            q, k = qk_of(b)
            bc = bc_ref[b]
            ref = bc[mid:mid + 1, :]
            s = _stack_scores(q * jnp.exp(bc - ref), k * jnp.exp(ref - bc))
            sc_ref[b] = jnp.where(scol <= srow, s, 0.0).astype(BF16)

    def scores_by_levels():
        qrow = lax.broadcasted_iota(I32, (c, QK), 0)
        for b in range(nb):
            q, k = qk_of(b)
            bc = bc_ref[b]
            parts = _split3(g_ref[0, b])
            acc = jnp.where(scol == srow, _stack_scores(q, k), 0.0)
            half = c // 2
            while half >= 1:
                blk = 2 * half
                last_low = (row & ~(blk - 1)) + (half - 1)
                ref = _dot01(jnp.where(col <= last_low, 1.0, 0.0).astype(BF16), parts)
                upper = (qrow & (blk - 1)) >= half
                dq = jnp.minimum(jnp.where(upper, bc - ref, 0.0), 0.0)
                dk = jnp.minimum(jnp.where(upper, 0.0, ref - bc), 0.0)
                s = _stack_scores(q * jnp.exp(dq), k * jnp.exp(dk))
                pair = ((srow & ~(blk - 1)) == (scol & ~(blk - 1))) & ((srow & (blk - 1)) >= half) & ((scol & (blk - 1)) < half)
                acc = acc + jnp.where(pair, s, 0.0)
                half //= 2
            sc_ref[b] = acc.astype(BF16)

    lax.cond(spread <= SAFE_EXP, scores_one_reference, scores_by_levels)

    for b in range(nb):
        q, k = qk_of(b)
        bc = bc_ref[b]
        last = bc[c - 1:c, :]
        v = v_ref[0, b]
        state = st_ref[0, b]
        o_inter = _dot(_head_stack(q * jnp.exp(bc)), state.astype(BF16))
        k_out = jnp.transpose(k * jnp.exp(last - bc)).astype(BF16)
        decay = jnp.transpose(jnp.broadcast_to(jnp.exp(last), (DV, QK)))
        for h in range(HEADS):
            rs, ls, ks = slice(h * c, (h + 1) * c), slice(h * DV, (h + 1) * DV), slice(h * DK, (h + 1) * DK)
            o_ref[0, b, :, ls] = (_dot(sc_ref[b, rs, :], v[:, ls]) + o_inter[rs, :]).astype(BF16)
            st_ref[0, b, ks, :] = decay[ks, :] * state[ks, :] + _dot(k_out[ks, :], v[:, ls])


def _chunk_scan(qk, v, g, *, nb, length):
    c = CHUNK
    blk = lambda w: pl.BlockSpec((1, nb, c, w), lambda br, i: (br, 0, i, 0))
    return pl.pallas_call(
        functools.partial(_chunk_kernel, nb=nb, c=c),
        grid=(2, length // c),
        in_specs=[blk(2 * QK), blk(VW), blk(QK)],
        out_specs=[blk(VW), pl.BlockSpec((1, nb, QK, DV), lambda br, i: (br, 0, 0, 0))],
        out_shape=[jax.ShapeDtypeStruct((2, nb, length, VW), BF16),
                   jax.ShapeDtypeStruct((2, nb, QK, DV), F32)],
        scratch_shapes=[pltpu.VMEM((nb, c, QK), F32), pltpu.VMEM((nb, HEADS * c, c), BF16)],
        compiler_params=_cparams("arbitrary", "arbitrary"),
        name="chunk_scan",
    )(qk, v, g)


def _decode_kernel(qk_ref, v_ref, g_ref, s0_ref, s1_ref, *rest, nt):
    o_ref, n0_ref, n1_ref = rest[-3:]
    for br, (s_ref, n_ref) in enumerate(((s0_ref, n0_ref), (s1_ref, n1_ref))):
        for j in range(nt):
            d = jnp.exp(g_ref[br, j:j + 1, :])
            q = qk_ref[br, j:j + 1, 0:QK].astype(F32)
            k = qk_ref[br, j:j + 1, QK:2 * QK].astype(F32)
            cols = jnp.transpose(jnp.concatenate([d, k, q, jnp.zeros((5, QK), F32)], axis=0))
            vrow = v_ref[br, j:j + 1, :].astype(F32)
            vfull = jnp.concatenate([jnp.broadcast_to(vrow[:, h * DV:(h + 1) * DV], (DK, DV)) for h in range(HEADS)], axis=0)
            new = cols[:, 0:1] * s_ref[j] + cols[:, 1:2] * vfull
            n_ref[j] = new
            t = cols[:, 2:3] * new
            for h in range(HEADS):
                o_ref[br, j:j + 1, h * DV:(h + 1) * DV] = jnp.sum(t[h * DK:(h + 1) * DK, :], axis=0, keepdims=True).astype(BF16)


def _decode(qk, v, g, s_gla, s_hg, prev, *, layer):
    n = qk.shape[1]
    nt = 8
    row = lambda w: pl.BlockSpec((2, nt, w), lambda i: (0, i, 0))
    st = pl.BlockSpec((None, nt, QK, DV), lambda i: (layer, i, 0, 0))
    carried = [] if prev is None else list(prev)
    first = 5
    return pl.pallas_call(
        functools.partial(_decode_kernel, nt=nt),
        grid=(n // nt,),
        in_specs=[row(2 * QK), row(VW), row(QK), st, st] + [pl.BlockSpec(memory_space=pl.ANY)] * len(carried),
        out_specs=[row(VW), st, st],
        out_shape=[jax.ShapeDtypeStruct((2, n, VW), BF16),
                   jax.ShapeDtypeStruct((DEPTH, n, QK, DV), F32),
                   jax.ShapeDtypeStruct((DEPTH, n, QK, DV), F32)],
        input_output_aliases={first + j: 1 + j for j in range(len(carried))},
        compiler_params=_cparams("parallel"),
        name="decode_step",
    )(qk, v, g, s_gla, s_hg, *carried)


def _s5_disc_kernel(lr_ref, li_ref, ldt_ref, br_ref, bi_ref, abr_ref, abi_ref, bbr_ref, bbi_ref):
    lr, li, dt = lr_ref[...], li_ref[...], jnp.exp(ldt_ref[...])
    mag = jnp.exp(lr * dt)
    ab_re, ab_im = mag * jnp.cos(li * dt), mag * jnp.sin(li * dt)
    den = lr * lr + li * li
    num_re = ab_re - 1.0
    coef_re = (num_re * lr + ab_im * li) / den
    coef_im = (ab_im * lr - num_re * li) / den
    br, bi = br_ref[...], bi_ref[...]
    abr_ref[...] = ab_re
    abi_ref[...] = ab_im
    bbr_ref[...] = coef_re * br - coef_im * bi
    bbi_ref[...] = coef_re * bi + coef_im * br


def _s5_discretise(a_re, a_im, log_dt, b_re, b_im):
    n = DEPTH * S5_GROUPS
    rep = lambda t: jnp.repeat(t.reshape(n, S5_STATE), S5_GROUP, axis=1)
    ldt = jnp.broadcast_to(log_dt.reshape(n, 1), (n, S5_STATE * S5_GROUP))
    shape = jax.ShapeDtypeStruct((n, S5_STATE * S5_GROUP), F32)
    ab_re, ab_im, bb_re, bb_im = pl.pallas_call(_s5_disc_kernel, out_shape=[shape] * 4, name="s5_discretise")(
        rep(a_re), rep(a_im), ldt, b_re.reshape(n, -1), b_im.reshape(n, -1))
    pole = lambda t: t[:, ::S5_GROUP].reshape(DEPTH, 1, S5_LANES)
    eye = jnp.eye(8, dtype=F32)

    def blockdiag_in(bb):
        t = bb.reshape(DEPTH, S5_SLABS, 8, S5_STATE, S5_GROUP).transpose(0, 1, 2, 4, 3)
        return jnp.einsum("dcgmp,gh->dcgmhp", t, eye).reshape(DEPTH, S5_SLABS, LANES, 8 * S5_STATE)

    w_in = jnp.concatenate([blockdiag_in(bb_re), blockdiag_in(bb_im)], axis=-1).astype(BF16)
    return w_in, pole(ab_re), pole(ab_im)


def _s5_blockdiag_out(c):
    t = c.reshape(DEPTH, S5_SLABS, 8, S5_GROUP, S5_STATE).transpose(0, 1, 2, 4, 3)
    return jnp.einsum("dcgpm,gh->dcgphm", t, jnp.eye(8, dtype=F32)).reshape(DEPTH, S5_SLABS, 8 * S5_STATE, LANES).astype(BF16)


def _time_major_perm(nb, ct):
    r = np.arange(nb * ct)
    p = np.zeros((nb * ct, nb * ct), np.float32)
    p[r, (r % nb) * ct + r // nb] = 1.0
    return p


def _gelu_tanh(y):
    return 0.5 * y * (1.0 + jnp.tanh(0.7978845608028654 * (y + 0.044715 * (y * y * y))))


def _s5_input(ub, win_ref, xre_ref, xim_ref):
    half = 8 * S5_STATE
    for s in range(S5_SLABS):
        r = _dot(ub[:, s * LANES:(s + 1) * LANES], win_ref[s])
        xre_ref[:, s * half:(s + 1) * half] = r[:, :half]
        xim_ref[:, s * half:(s + 1) * half] = r[:, half:]


def _s5_readout(xre_ref, xim_ref, cre_ref, cim_ref):
    half = 8 * S5_STATE
    ys = []
    for s in range(S5_SLABS):
        ls = slice(s * half, (s + 1) * half)
        ys.append(_dot(xre_ref[:, ls].astype(BF16), cre_ref[s]) - _dot(xim_ref[:, ls].astype(BF16), cim_ref[s]))
    return jnp.concatenate(ys, axis=-1)


def _s5_glu(y, u, d_ref, wglu_ref, bglu_ref):
    z = _gelu_tanh(y + d_ref[...] * u)
    return z * _sigmoid(_dot(z.astype(BF16), wglu_ref[...]) + bglu_ref[...])


def _s5_scan_kernel(su_ref, perm_ref, permt_ref, win_ref, are_ref, aim_ref, cre_ref, cim_ref, d_ref, wglu_ref, bglu_ref,
                    o_ref, hre_ref, him_ref, xre_ref, xim_ref, *, nb, ct):
    @pl.when(pl.program_id(0) == 0)
    def _():
        hre_ref[...] = jnp.zeros_like(hre_ref)
        him_ref[...] = jnp.zeros_like(him_ref)

    u = su_ref[...].reshape(nb * ct, S5_WIDTH)
    ub = _dot(perm_ref[...], u.astype(BF16)).astype(BF16)

    half = 8 * S5_STATE
    ys = []
    for s in range(S5_SLABS):
        ls = slice(s * half, (s + 1) * half)
        r = _dot(ub[:, s * LANES:(s + 1) * LANES], win_ref[s])
        xre_ref[:, ls] = r[:, :half]
        xim_ref[:, ls] = r[:, half:]
        ar = jnp.broadcast_to(are_ref[:, ls], (nb, half))
        ai = jnp.broadcast_to(aim_ref[:, ls], (nb, half))
        hr, hi = hre_ref[:, ls], him_ref[:, ls]
        for t in range(ct):
            rows = slice(t * nb, (t + 1) * nb)
            hr, hi = ar * hr - ai * hi + xre_ref[rows, ls], ar * hi + ai * hr + xim_ref[rows, ls]
            xre_ref[rows, ls] = hr
            xim_ref[rows, ls] = hi
        hre_ref[:, ls] = hr
        him_ref[:, ls] = hi
        ys.append(_dot(xre_ref[:, ls].astype(BF16), cre_ref[s]) - _dot(xim_ref[:, ls].astype(BF16), cim_ref[s]))

    y = jnp.concatenate(ys, axis=-1)
    y_hi = y.astype(BF16)
    y_lo = (y - y_hi.astype(F32)).astype(BF16)
    y = _dot01(permt_ref[...], (y_hi, y_lo))
    o_ref[...] = _s5_glu(y, u, d_ref, wglu_ref, bglu_ref).reshape(nb, ct, S5_WIDTH).astype(BF16)


def _s5_weight_specs(layer):
    return [_layer_spec((S5_SLABS, LANES, 2 * 8 * S5_STATE), layer),
            _layer_spec((1, S5_LANES), layer), _layer_spec((1, S5_LANES), layer),
            _layer_spec((S5_SLABS, 8 * S5_STATE, LANES), layer), _layer_spec((S5_SLABS, 8 * S5_STATE, LANES), layer),
            _layer_spec((1, S5_WIDTH), layer), _layer_spec((S5_WIDTH, S5_WIDTH), layer),
            _layer_spec((1, S5_WIDTH), layer)]


def _s5_scan(su, weights, *, nb, length, layer):
    ct = S5_CHUNK
    perm = _time_major_perm(nb, ct)
    const = pl.BlockSpec((nb * ct, nb * ct), lambda i: (0, 0))
    return pl.pallas_call(
        functools.partial(_s5_scan_kernel, nb=nb, ct=ct),
        grid=(length // ct,),
        in_specs=[pl.BlockSpec((nb, ct, S5_WIDTH), lambda i: (0, i, 0)), const, const] + _s5_weight_specs(layer),
        out_specs=[pl.BlockSpec((nb, ct, S5_WIDTH), lambda i: (0, i, 0)),
                   pl.BlockSpec((nb, S5_LANES), lambda i: (0, 0)), pl.BlockSpec((nb, S5_LANES), lambda i: (0, 0))],
        out_shape=[jax.ShapeDtypeStruct((nb, length, S5_WIDTH), BF16),
                   jax.ShapeDtypeStruct((nb, S5_LANES), F32), jax.ShapeDtypeStruct((nb, S5_LANES), F32)],
        scratch_shapes=[pltpu.VMEM((nb * ct, S5_LANES), F32), pltpu.VMEM((nb * ct, S5_LANES), F32)],
        compiler_params=_cparams("arbitrary"),
        name="s5_scan",
    )(su, jnp.asarray(perm, BF16), jnp.asarray(perm.T, BF16), *weights)


def _s5_step_kernel(su_ref, h0r_ref, h0i_ref, win_ref, are_ref, aim_ref, cre_ref, cim_ref, d_ref, wglu_ref, bglu_ref,
                    o_ref, hre_ref, him_ref):
    u = su_ref[...]
    _s5_input(u.astype(BF16), win_ref, hre_ref, him_ref)
    ar, ai = are_ref[...], aim_ref[...]
    h0r, h0i = h0r_ref[...], h0i_ref[...]
    nr = ar * h0r - ai * h0i + hre_ref[...]
    ni = ar * h0i + ai * h0r + him_ref[...]
    hre_ref[...] = nr
    him_ref[...] = ni
    o_ref[...] = _s5_glu(_s5_readout(hre_ref, him_ref, cre_ref, cim_ref), u, d_ref, wglu_ref, bglu_ref).astype(BF16)


def _s5_step(su, h0r, h0i, weights, *, layer):
    n = su.shape[0]
    full = lambda w: pl.BlockSpec((n, w), lambda i: (0, 0))
    state = pl.BlockSpec((None, n, S5_LANES), lambda i: (layer, 0, 0))
    return pl.pallas_call(
        _s5_step_kernel,
        grid=(1,),
        in_specs=[full(S5_WIDTH), state, state] + _s5_weight_specs(layer),
        out_specs=[full(S5_WIDTH), full(S5_LANES), full(S5_LANES)],
        out_shape=[jax.ShapeDtypeStruct((n, S5_WIDTH), BF16),
                   jax.ShapeDtypeStruct((n, S5_LANES), F32), jax.ShapeDtypeStruct((n, S5_LANES), F32)],
        compiler_params=_cparams("arbitrary"),
        name="s5_step",
    )(su, h0r, h0i, *weights)


def _head_norm(o, g):
    parts = []
    for h in range(HEADS):
        seg = o[:, h * DV:(h + 1) * DV]
        parts.append(seg * lax.rsqrt(jnp.mean(seg * seg, axis=-1, keepdims=True) + EPS))
    return jnp.concatenate(parts, axis=-1) * g


def _route(logits, count):
    rows = logits.shape[0]
    lane = lax.broadcasted_iota(I32, logits.shape, 1)
    neg = -jnp.inf
    first = lambda hit: jnp.min(jnp.where(hit, lane, LANES), axis=-1, keepdims=True)
    glog = jnp.where(lane < MOE_GROUPS, logits, neg)
    gmax = jnp.max(glog, axis=-1, keepdims=True)
    gidx = first(glog == gmax)
    gw = 1.0 / jnp.sum(jnp.where(lane < MOE_GROUPS, jnp.exp(logits - gmax), 0.0), axis=-1, keepdims=True)
    inside = (lane >= MOE_GROUPS) & (lane < MOE_GROUPS + MOE_EXPERTS) & (((lane - MOE_GROUPS) >> 3) == gidx)
    el = jnp.where(inside, logits, neg)
    v1 = jnp.max(el, axis=-1, keepdims=True)
    i1 = first(el == v1)
    el2 = jnp.where(lane == i1, neg, el)
    v2 = jnp.max(el2, axis=-1, keepdims=True)
    i2 = first(el2 == v2)
    p2 = jnp.exp(v2 - v1)
    w1 = gw / (1.0 + p2)
    w2 = gw * p2 / (1.0 + p2)
    hit1, hit2 = lane == i1, lane == i2
    onehot = jnp.where(hit1 | hit2, 1.0, 0.0)
    tr = lax.broadcasted_iota(I32, (rows, rows), 0)
    tc = lax.broadcasted_iota(I32, (rows, rows), 1)
    before = _dot(jnp.where(tc < tr, 1.0, 0.0).astype(BF16), onehot.astype(BF16)) + count
    rank1 = jnp.sum(jnp.where(hit1, before, 0.0), axis=-1, keepdims=True)
    rank2 = jnp.sum(jnp.where(hit2, before, 0.0), axis=-1, keepdims=True)
    slab = jnp.zeros(logits.shape, F32)
    for ln, val in ((R_E1, (i1 - MOE_GROUPS).astype(F32)), (R_E2, (i2 - MOE_GROUPS).astype(F32)), (R_W1, w1), (R_W2, w2),
                    (R_RANK1, rank1), (R_RANK2, rank2)):
        slab = jnp.where(lane == ln, val, slab)
    return slab, count + jnp.sum(onehot, axis=0, keepdims=True)


def _merge_kernel(x_ref, o_ref, gate_ref, os5_ref, cnt0_ref, nm_ref, wgm_ref, gn_ref, wbr_ref, wout_ref, nf_ref,
                  wr_ref, br_ref, *rest, sub):
    xm_ref, h2_ref, rt_ref, rtt_ref, cnt_ref = rest[-5:]

    @pl.when(pl.program_id(0) == 0)
    def _():
        cnt_ref[...] = cnt0_ref[...]

    logits = []
    for r0 in range(0, x_ref.shape[0], sub):
        rs = slice(r0, r0 + sub)
        x = x_ref[rs, :]
        hb = _rms(x, nm_ref[...]).astype(BF16)
        mixed = None
        for i, src in enumerate((0, None, 1)):
            if src is None:
                branch = os5_ref[rs, :]
            else:
                branch = (_head_norm(o_ref[src, rs, :].astype(F32), gn_ref[src]) * gate_ref[src, rs, :].astype(F32)).astype(BF16)
            gate = _sigmoid(_dot(hb, wgm_ref[:, i * D_MODEL:(i + 1) * D_MODEL]))
            term = gate * _dot(branch, wbr_ref[i])
            mixed = term if mixed is None else mixed + term
        xm = x + _dot(mixed.astype(BF16), wout_ref[...])
        xm_ref[rs, :] = xm
        h2 = _rms(xm, nf_ref[...])
        h2_hi = h2.astype(BF16)
        h2_lo = (h2 - h2_hi.astype(F32)).astype(BF16)
        h2_ref[rs, :] = h2_hi
        both = _dot(h2_hi, wr_ref[...])
        logits.append(both[:, :LANES] + (both[:, LANES:] + _dot(h2_lo, wr_ref[:, :LANES])) + br_ref[...])
    slab, cnt_ref[...] = _route(jnp.concatenate(logits, axis=0), cnt_ref[...])
    rt_ref[...] = slab
    rtt_ref[...] = jnp.transpose(slab)[0:8, :]


def _merge(x, o, gate, os5, cnt0, weights, carried, *, layer, tm, total_rows, block_offset):
    rows = x.shape[0]
    tile = lambda w: pl.BlockSpec((tm, w), lambda i: (i, 0))
    shared = lambda w: pl.BlockSpec((tm, w), lambda i: (i + block_offset, 0))
    carried = [] if carried is None else list(carried)
    w_specs = [_layer_spec((1, D_MODEL), layer), _layer_spec((D_MODEL, 3 * D_MODEL), layer),
               _layer_spec((2, 1, VW), layer), _layer_spec((3, VW, D_MODEL), layer),
               _layer_spec((D_MODEL, D_MODEL), layer), _layer_spec((1, D_MODEL), layer),
               _layer_spec((D_MODEL, 2 * LANES), layer), _layer_spec((1, LANES), layer)]
    first = 5 + len(w_specs)
    return pl.pallas_call(
        functools.partial(_merge_kernel, sub=min(tm, 256)),
        grid=(rows // tm,),
        in_specs=[tile(D_MODEL),
                  pl.BlockSpec((2, tm, VW), lambda i: (0, i, 0)),
                  pl.BlockSpec((2, tm, VW), lambda i: (0, i, 0)),
                  tile(S5_WIDTH),
                  pl.BlockSpec((1, LANES), lambda i: (0, 0))] + w_specs + [pl.BlockSpec(memory_space=pl.ANY)] * len(carried),
        out_specs=[tile(D_MODEL), shared(D_MODEL), shared(LANES),
                   pl.BlockSpec((8, tm), lambda i: (0, i + block_offset)), pl.BlockSpec((1, LANES), lambda i: (0, 0))],
        out_shape=[jax.ShapeDtypeStruct((rows, D_MODEL), F32),
                   jax.ShapeDtypeStruct((total_rows, D_MODEL), BF16),
                   jax.ShapeDtypeStruct((total_rows, LANES), F32),
                   jax.ShapeDtypeStruct((8, total_rows), F32),
                   jax.ShapeDtypeStruct((1, LANES), F32)],
        input_output_aliases={first + j: 1 + j for j in range(len(carried))},
        compiler_params=_cparams("arbitrary"),
        name="merge_route",
    )(x, o, gate, os5, cnt0, *weights, *carried)


def _ffn_kernel(te_ref, nu_ref, x_ref, wg_ref, wu_ref, wd_ref, *rest, tile0):
    y_ref, wg_bf, wu_bf, wd_bf = rest[-4:]
    i = pl.program_id(0)
    tile = i + tile0

    @pl.when(tile < nu_ref[0])
    def _():
        @pl.when((i == 0) | (te_ref[tile] != te_ref[jnp.maximum(tile - 1, 0)]))
        def _():
            wg_bf[...] = wg_ref[...].astype(BF16)
            wu_bf[...] = wu_ref[...].astype(BF16)
            wd_bf[...] = wd_ref[...].astype(BF16)

        x = x_ref[...]
        act = _silu(_dot(x, wg_bf[...])) * _dot(x, wu_bf[...])
        y_ref[...] = _dot(act.astype(BF16), wd_bf[...]).astype(BF16)


def _grouped_ffn(tile_expert, n_used, xg, y_prev, wg, wu, wd, *, layer, tile0, total_rows):
    used = lambda i, nu: jnp.maximum(jnp.minimum(i + tile0, nu[0] - 1), tile0)
    expert = lambda a, b: pl.BlockSpec((None, None, a, b), lambda i, te, nu: (layer, te[used(i, nu)], 0, 0))
    carried = [] if y_prev is None else [y_prev]
    grid_spec = pltpu.PrefetchScalarGridSpec(
        num_scalar_prefetch=2,
        grid=(xg.shape[0] // MOE_TILE,),
        in_specs=[pl.BlockSpec((MOE_TILE, D_MODEL), lambda i, te, nu: (used(i, nu) - tile0, 0)),
                  expert(D_MODEL, MOE_FF), expert(D_MODEL, MOE_FF), expert(MOE_FF, D_MODEL)]
        + [pl.BlockSpec(memory_space=pl.ANY)] * len(carried),
        out_specs=pl.BlockSpec((MOE_TILE, D_MODEL), lambda i, te, nu: (used(i, nu), 0)),
        scratch_shapes=[pltpu.VMEM((D_MODEL, MOE_FF), BF16), pltpu.VMEM((D_MODEL, MOE_FF), BF16),
                        pltpu.VMEM((MOE_FF, D_MODEL), BF16)],
    )
    return pl.pallas_call(
        functools.partial(_ffn_kernel, tile0=tile0),
        grid_spec=grid_spec,
        out_shape=jax.ShapeDtypeStruct((total_rows, D_MODEL), BF16),
        input_output_aliases={6: 0} if carried else {},
        compiler_params=_cparams("arbitrary"),
        name="expert_ffn",
    )(tile_expert, n_used, xg, wg, wu, wd, *carried)


@compute_on("tpu_sparsecore")
@jax.jit
def _take_rows(x, idx):
    return jnp.take(x, idx, axis=0, mode="clip")


def _moe_plan(route_t, count):
    tokens = route_t.shape[1]
    n_tiles = -(-(2 * tokens + MOE_EXPERTS * (MOE_TILE - 1)) // MOE_TILE)
    n_rows = n_tiles * MOE_TILE
    counts = count[0, MOE_GROUPS:MOE_GROUPS + MOE_EXPERTS].astype(I32)
    padded = ((counts + MOE_TILE - 1) // MOE_TILE) * MOE_TILE
    gend = jnp.cumsum(padded)
    gstart = gend - padded
    experts = jnp.arange(MOE_EXPERTS, dtype=I32)[:, None]

    def rows_of(e_lane, rank_lane):
        e = route_t[e_lane].astype(I32)
        return jnp.sum(jnp.where(e[None, :] == experts, gstart[:, None], 0), axis=0) + route_t[rank_lane].astype(I32)

    pos1, pos2 = rows_of(R_E1, R_RANK1), rows_of(R_E2, R_RANK2)
    token = jnp.arange(tokens, dtype=I32) + 1
    marked = jnp.zeros((n_rows,), I32).at[jnp.concatenate([pos1, pos2])].add(jnp.concatenate([token, token]),
                                                                             unique_indices=True)
    filler = jnp.arange(n_rows, dtype=I32) % tokens
    src = jnp.where(marked > 0, marked - 1, filler)
    tile_start = jnp.arange(n_tiles, dtype=I32) * MOE_TILE
    tile_expert = jnp.minimum(jnp.sum((tile_start[:, None] >= gend[None, :]).astype(I32), axis=1), MOE_EXPERTS - 1)
    n_used = (gend[-1] // MOE_TILE).astype(I32).reshape(1)
    return src, pos1, pos2, tile_expert, n_used


def _ple_kernel(xm_ref, y1_ref, y2_ref, rt_ref, p_ref, np_ref, wg_ref, wp_ref, nfin_ref, *rest, final):
    out_ref = rest[-1]
    rt = rt_ref[...]
    x1 = xm_ref[...] + rt[:, R_W1:R_W1 + 1] * y1_ref[...].astype(F32) + rt[:, R_W2:R_W2 + 1] * y2_ref[...].astype(F32)
    gate = _sigmoid(_dot(_rms(x1, np_ref[...]).astype(BF16), wg_ref[...]))
    x2 = x1 + gate * _dot(p_ref[...].astype(BF16), wp_ref[...])
    out_ref[...] = _rms(x2, nfin_ref[...]) if final else x2


def _ple(xm, y1, y2, route, p, x_prev, npl, wg, wp, nfin, *, layer, final, tm, n_blocks, x_off, y_off, rt_off):
    at = lambda w, off: pl.BlockSpec((tm, w), lambda i: (i + off, 0))
    carried = [] if x_prev is None else [x_prev]
    return pl.pallas_call(
        functools.partial(_ple_kernel, final=final),
        grid=(n_blocks,),
        in_specs=[at(D_MODEL, x_off), at(D_MODEL, y_off), at(D_MODEL, y_off), at(LANES, rt_off),
                  pl.BlockSpec((None, tm, PLE_DIM), lambda i: (layer, i + x_off, 0)),
                  _layer_spec((1, D_MODEL), layer), _layer_spec((D_MODEL, D_MODEL), layer),
                  _layer_spec((PLE_DIM, D_MODEL), layer), pl.BlockSpec((1, D_MODEL), lambda i: (0, 0))]
        + [pl.BlockSpec(memory_space=pl.ANY)] * len(carried),
        out_specs=at(D_MODEL, x_off),
        out_shape=jax.ShapeDtypeStruct(xm.shape, F32),
        input_output_aliases={9: 0} if carried else {},
        compiler_params=_cparams("parallel"),
        name="combine_ple",
    )(xm, y1, y2, route, p, npl, wg, wp, nfin, *carried)


def kernel(x_prompt, x_sample, state_gla, state_s5_re, state_s5_im, state_hgrn, p_prompt, p_sample, norm_mix, w_in, gla_w_gate2, gla_b_gate, gla_norm, s5_a_re, s5_a_im, s5_log_dt, s5_b_re, s5_b_im, s5_c_re, s5_c_im, s5_d, s5_w_glu, s5_b_glu, hgrn_lb_logits, hgrn_norm, w_br_gla, w_br_s5, w_br_hgrn, w_out, norm_ffn, moe_w_group, moe_b_group, moe_w_expert, moe_b_expert, moe_w_gate, moe_w_up, moe_w_down, norm_ple, w_ple_gate, w_ple_proj, norm_final):
    nb, length, _ = x_prompt.shape
    ns = x_sample.shape[0]
    n_p = nb * length
    n_all = n_p + ns
    row = lambda t: t.reshape(DEPTH, 1, -1)

    w_a, w_gm = _w_in_layout(w_in)
    wg2 =jnp.concatenate([gla_w_gate2, jnp.zeros((DEPTH, LANES - GLA_RANK, QK), F32)], axis=1).astype(BF16)
    s5_win, s5_are, s5_aim = _s5_discretise(s5_a_re, s5_a_im, s5_log_dt, s5_b_re, s5_b_im)
    s5_w = (s5_win, s5_are, s5_aim, _s5_blockdiag_out(s5_c_re), _s5_blockdiag_out(s5_c_im), row(s5_d),
            s5_w_glu.astype(BF16), row(s5_b_glu))
    head_gain = jnp.stack([jnp.tile(gla_norm, (1, HEADS)), jnp.tile(hgrn_norm, (1, HEADS))], axis=1).reshape(DEPTH, 2, 1, VW)
    w_router = jnp.concatenate([moe_w_group, moe_w_expert,
                                jnp.zeros((DEPTH, D_MODEL, LANES - MOE_GROUPS - MOE_EXPERTS), F32)], axis=-1)
    wr_hi = w_router.astype(BF16)
    wr_lo = (w_router - wr_hi.astype(F32)).astype(BF16)
    wr_hilo = jnp.concatenate([wr_hi, wr_lo], axis=-1)
    b_router = jnp.concatenate([moe_b_group, moe_b_expert,
                                jnp.zeros((DEPTH, LANES - MOE_GROUPS - MOE_EXPERTS), F32)], axis=-1).reshape(DEPTH, 1, LANES)
    merge_w = (row(norm_mix), w_gm, head_gain,
               jnp.stack([w_br_gla, w_br_s5, w_br_hgrn], axis=1).astype(BF16), w_out.astype(BF16), row(norm_ffn),
               wr_hilo, b_router)
    ple_w = (row(norm_ple), w_ple_gate.astype(BF16), w_ple_proj.astype(BF16), norm_final.reshape(1, D_MODEL))
    nm, bg = row(norm_mix), row(gla_b_gate)
    p_p, p_s = p_prompt.reshape(DEPTH, n_p, PLE_DIM), p_sample.reshape(DEPTH, ns, PLE_DIM)
    sg_in, sh_in = state_gla.reshape(DEPTH, ns, QK, DV), state_hgrn.reshape(DEPTH, ns, QK, DV)
    s5r_in, s5i_in = state_s5_re.reshape(DEPTH, ns, S5_LANES), state_s5_im.reshape(DEPTH, ns, S5_LANES)

    xp = x_prompt.reshape(n_p, D_MODEL)
    xs = x_sample.reshape(ns, D_MODEL)
    new_p, new_s5 = [], []
    new_s = None
    for i in range(DEPTH):
        qk, v, gate, g, su = _inproj(xp, nm, w_a, wg2, bg, hgrn_lb_logits, layer=i, tm=1024)
        o_p, st_p = _chunk_scan(qk.reshape(2, nb, length, -1), v.reshape(2, nb, length, -1),
                                g.reshape(2, nb, length, -1), nb=nb, length=length)
        os5_p, hre_p, him_p = _s5_scan(su.reshape(nb, length, -1), s5_w, nb=nb, length=length, layer=i)
        xm_p, h2, rt, rtt, cnt = _merge(xp, o_p.reshape(2, n_p, VW), gate, os5_p.reshape(n_p, -1),
                                        jnp.zeros((1, LANES), F32), merge_w, None, layer=i, tm=1024, total_rows=n_all,
                                        block_offset=0)
        new_p.append((st_p[0].reshape(nb, HEADS, DK, DV), hre_p.reshape(nb, S5_GROUPS, S5_STATE),
                      him_p.reshape(nb, S5_GROUPS, S5_STATE), st_p[1].reshape(nb, HEADS, DK, DV)))

        qk, v, gate, g, su = _inproj(xs, nm, w_a, wg2, bg, hgrn_lb_logits, layer=i, tm=ns)
        o_s, *new_s = _decode(qk, v, g, sg_in, sh_in, new_s, layer=i)
        os5_s, hre_s, him_s = _s5_step(su, s5r_in, s5i_in, s5_w, layer=i)
        xm_s, h2, rt, rtt, cnt = _merge(xs, o_s, gate, os5_s, cnt, merge_w, (h2, rt, rtt), layer=i, tm=ns,
                                        total_rows=n_all, block_offset=n_p // ns)
        new_s5.append((hre_s.reshape(ns, S5_GROUPS, S5_STATE), him_s.reshape(ns, S5_GROUPS, S5_STATE)))

        src, pos1, pos2, tile_expert, n_used = _moe_plan(rtt, cnt)
        n_tiles = tile_expert.shape[0]
        y = None
        for k in range(PIPE):
            t0, t1 = k * n_tiles // PIPE, (k + 1) * n_tiles // PIPE
            y = _grouped_ffn(tile_expert, n_used, _take_rows(h2, src[t0 * MOE_TILE:t1 * MOE_TILE]), y,
                             moe_w_gate, moe_w_up, moe_w_down, layer=i, tile0=t0, total_rows=n_tiles * MOE_TILE)

        final = i == DEPTH - 1
        tm = 512
        blocks = n_p // tm
        pieces = min(PIPE, blocks)
        x_new = None
        for k in range(pieces):
            b0, b1 = k * blocks // pieces, (k + 1) * blocks // pieces
            r0, r1 = b0 * tm, (n_all if k == pieces - 1 else b1 * tm)
            y1, y2 = _take_rows(y, pos1[r0:r1]), _take_rows(y, pos2[r0:r1])
            x_new = _ple(xm_p, y1, y2, rt, p_p, x_new, *ple_w, layer=i, final=final, tm=tm, n_blocks=b1 - b0,
                         x_off=b0, y_off=0, rt_off=b0)
        xp = x_new
        xs = _ple(xm_s, y1, y2, rt, p_s, None, *ple_w, layer=i, final=final, tm=ns, n_blocks=1,
                  x_off=0, y_off=(n_p - r0) // ns, rt_off=n_p // ns)

    stack = lambda items, j: jnp.stack([it[j] for it in items])
    return (xp.reshape(nb, length, D_MODEL), xs.reshape(ns, 1, D_MODEL),
            stack(new_p, 0), stack(new_p, 1), stack(new_p, 2), stack(new_p, 3),
            new_s[0].reshape(DEPTH, ns, HEADS, DK, DV), stack(new_s5, 0), stack(new_s5, 1),
            new_s[1].reshape(DEPTH, ns, HEADS, DK, DV))
```

```python
import functools

import jax
import jax.numpy as jnp
import numpy as np
from jax import lax
from jax.experimental import pallas as pl
from jax.experimental.compute_on import compute_on
from jax.experimental.pallas import tpu as pltpu

F32, BF16, I32 = jnp.float32, jnp.bfloat16, jnp.int32

D_MODEL = 1024
DEPTH = 2
HEADS, DK, DV = 4, 64, 128
QK, VW = HEADS * DK, HEADS * DV
GLA_RANK, GLA_TAU = 16, 16.0
S5_WIDTH, S5_GROUP, S5_GROUPS, S5_STATE = 512, 16, 32, 64
S5_LANES = S5_GROUPS * S5_STATE
S5_SLABS = 4
MOE_GROUPS, MOE_PER_GROUP, MOE_EXPERTS, MOE_FF = 4, 8, 32, 256
PLE_DIM = 256
EPS = 1e-6

LANES = 128
CHUNK = 64
S5_CHUNK = 64
MOE_TILE = 512
PIPE = 2
SAFE_EXP = 80.0
VMEM_LIMIT = 56 * 1024 * 1024

C_GLA, C_S5, C_HG, C_GLR, C_END = 0, 1536, 2048, 3584, 3712
R_E1, R_E2, R_W1, R_W2, R_RANK1, R_RANK2 = 0, 1, 2, 3, 4, 5


def _cparams(*sem):
    return pltpu.CompilerParams(dimension_semantics=sem, vmem_limit_bytes=VMEM_LIMIT)


def _layer_spec(shape, layer):
    return pl.BlockSpec((None,) + tuple(shape), lambda *_: (layer,) + (0,) * len(shape), pipeline_mode=pl.Buffered(1))


def _dot(a, b):
    return jnp.dot(a, b, preferred_element_type=F32)


def _rms(x, g):
    return x * lax.rsqrt(jnp.mean(x * x, axis=-1, keepdims=True) + EPS) * g


def _log_sigmoid(x):
    return jnp.minimum(x, 0.0) - jnp.log1p(jnp.exp(-jnp.abs(x)))


def _sigmoid(x):
    return 1.0 / (1.0 + jnp.exp(-x))


def _silu(x):
    return x * _sigmoid(x)


def _split3(x):
    hi = x.astype(BF16)
    r1 = x - hi.astype(F32)
    mid = r1.astype(BF16)
    lo = (r1 - mid.astype(F32)).astype(BF16)
    return hi, mid, lo


def _dot01(m, parts):
    out = _dot(m, parts[0])
    for p in parts[1:]:
        out = out + _dot(m, p)
    return out


W_GLA, W_GLR, W_S5HG, W_END = 1536, 1552, 3600, 6672


def _w_in_layout_kernel(wt_ref, wa_ref, wgm_ref):
    cols = lambda lo, hi: jnp.transpose(wt_ref[lo:hi, :])
    wa_ref[:, C_GLA:C_S5] = cols(0, W_GLA).astype(BF16)
    wa_ref[:, C_S5:C_GLR] = cols(W_GLR, W_S5HG).astype(BF16)
    lane = lax.broadcasted_iota(I32, (LANES, LANES), 1)
    wa_ref[:, C_GLR:C_END] = jnp.where(lane < GLA_RANK, cols(W_GLA, W_GLA + LANES), 0.0).astype(BF16)
    wgm_ref[...] = cols(W_S5HG, W_END).astype(BF16)


def _w_in_layout(w_in):
    wt = jnp.swapaxes(w_in, 1, 2)
    return pl.pallas_call(
        _w_in_layout_kernel,
        grid=(DEPTH, D_MODEL // LANES),
        in_specs=[pl.BlockSpec((None, W_END, LANES), lambda d, i: (d, 0, i))],
        out_specs=[pl.BlockSpec((None, LANES, C_END), lambda d, i: (d, i, 0)),
                   pl.BlockSpec((None, LANES, 3 * D_MODEL), lambda d, i: (d, i, 0))],
        out_shape=[jax.ShapeDtypeStruct((DEPTH, D_MODEL, C_END), BF16),
                   jax.ShapeDtypeStruct((DEPTH, D_MODEL, 3 * D_MODEL), BF16)],
        compiler_params=_cparams("parallel", "parallel"),
        name="w_in_layout",
    )(wt)
def _inproj_kernel(x_ref, nm_ref, w_ref, wg2_ref, bg_ref, lbl_ref, qk_ref, v_ref, gate_ref, g_ref, su_ref, *, layer, sub):
    lg = lbl_ref[...]
    mx = jnp.max(lg, axis=0, keepdims=True)
    ex = jnp.exp(lg - mx)
    sm = ex / jnp.sum(ex, axis=0, keepdims=True)
    cs = sm[0:1]
    for j in range(1, layer + 1):
        cs = cs + sm[j:j + 1]
    lb = cs - sm[0:1]
    log_lb, log_1mlb = jnp.log(lb), jnp.log1p(-lb)

    for r0 in range(0, x_ref.shape[0], sub):
        rs = slice(r0, r0 + sub)
        hb = _rms(x_ref[rs, :], nm_ref[...]).astype(BF16)

        def proj(lo, hi):
            return _dot(hb, w_ref[:, lo:hi])

        qk_ref[0, rs, 0:QK] = (proj(0, 256) * (DK ** -0.5)).astype(BF16)
        qk_ref[0, rs, QK:2 * QK] = proj(256, 512).astype(BF16)
        v_ref[0, rs, :] = proj(512, 1024).astype(BF16)
        gate_ref[0, rs, :] = _silu(proj(1024, 1536)).astype(BF16)
        glr = proj(C_GLR, C_END).astype(BF16)
        g_ref[0, rs, :] = _log_sigmoid(_dot(glr, wg2_ref[...]) + bg_ref[...]) * (1.0 / GLA_TAU)

        su_ref[rs, :] = proj(C_S5, C_HG)

        z = proj(C_HG + 256, C_HG + 512)
        a, c = log_lb, log_1mlb + _log_sigmoid(z)
        g_ref[1, rs, :] = jnp.maximum(a, c) + jnp.log1p(jnp.exp(-jnp.abs(a - c)))
        qk_ref[1, rs, 0:QK] = _silu(proj(C_HG, C_HG + 256)).astype(BF16)
        qk_ref[1, rs, QK:2 * QK] = ((1.0 - lb) * _sigmoid(-z)).astype(BF16)
        v_ref[1, rs, :] = proj(C_HG + 512, C_HG + 1024).astype(BF16)
        gate_ref[1, rs, :] = _silu(proj(C_HG + 1024, C_HG + 1536)).astype(BF16)


def _inproj(x, nm, w, wg2, bg, lbl, *, layer, tm):
    rows = x.shape[0]
    return pl.pallas_call(
        functools.partial(_inproj_kernel, layer=layer, sub=min(tm, 256)),
        grid=(rows // tm,),
        in_specs=[pl.BlockSpec((tm, D_MODEL), lambda i: (i, 0)),
                  _layer_spec((1, D_MODEL), layer),
                  _layer_spec((D_MODEL, C_END), layer),
                  _layer_spec((LANES, QK), layer),
                  _layer_spec((1, QK), layer),
                  pl.BlockSpec((DEPTH, QK), lambda i: (0, 0))],
        out_specs=[pl.BlockSpec((2, tm, 2 * QK), lambda i: (0, i, 0)),
                   pl.BlockSpec((2, tm, VW), lambda i: (0, i, 0)),
                   pl.BlockSpec((2, tm, VW), lambda i: (0, i, 0)),
                   pl.BlockSpec((2, tm, QK), lambda i: (0, i, 0)),
                   pl.BlockSpec((tm, S5_WIDTH), lambda i: (i, 0))],
        out_shape=[jax.ShapeDtypeStruct((2, rows, 2 * QK), BF16),
                   jax.ShapeDtypeStruct((2, rows, VW), BF16),
                   jax.ShapeDtypeStruct((2, rows, VW), BF16),
                   jax.ShapeDtypeStruct((2, rows, QK), F32),
                   jax.ShapeDtypeStruct((rows, S5_WIDTH), F32)],
        compiler_params=_cparams("parallel"),
        name="inproj",
    )(x, nm, w, wg2, bg, lbl)


def _head_stack(x):
    head = lax.broadcasted_iota(I32, x.shape, 1) // DK
    return jnp.concatenate([jnp.where(head == h, x, 0.0) for h in range(HEADS)], axis=0).astype(BF16)


def _stack_scores(qt, kt):
    kt = kt.astype(BF16)
    return lax.dot_general(_head_stack(qt), jnp.concatenate([kt, kt], axis=0), (((1,), (1,)), ((), ())),
                           preferred_element_type=F32)


def _chunk_kernel(qk_ref, v_ref, g_ref, o_ref, st_ref, bc_ref, sc_ref, *, nb, c):
    assert c == DK and 2 * c == LANES

    @pl.when(pl.program_id(1) == 0)
    def _():
        st_ref[...] = jnp.zeros_like(st_ref)

    row = lax.broadcasted_iota(I32, (c, c), 0)
    col = lax.broadcasted_iota(I32, (c, c), 1)
    tri = jnp.where(col <= row, 1.0, 0.0).astype(BF16)
    srow = lax.broadcasted_iota(I32, (HEADS * c, 2 * c), 0) & (c - 1)
    scol = lax.broadcasted_iota(I32, (HEADS * c, 2 * c), 1) & (c - 1)
    mid = c // 2 - 1

    spread = None
    for b in range(nb):
        bc = _dot01(tri, _split3(g_ref[0, b]))
        bc_ref[b] = bc
        ref, last = bc[mid:mid + 1, :], bc[c - 1:c, :]
        s = jnp.maximum(jnp.max(-ref), jnp.max(ref - last))
        spread = s if spread is None else jnp.maximum(spread, s)

    def qk_of(b):
        return qk_ref[0, b, :, 0:QK].astype(F32), qk_ref[0, b, :, QK:2 * QK].astype(F32)

    def scores_one_reference():
        for b in range(nb):
            q, k = qk_of(b)
            bc = bc_ref[b]
            ref = bc[mid:mid + 1, :]
            s = _stack_scores(q * jnp.exp(bc - ref), k * jnp.exp(ref - bc))
            sc_ref[b] = jnp.where(scol <= srow, s, 0.0).astype(BF16)

    def scores_by_levels():
        qrow = lax.broadcasted_iota(I32, (c, QK), 0)
        for b in range(nb):
            q, k = qk_of(b)
            bc = bc_ref[b]
            parts = _split3(g_ref[0, b])
            acc = jnp.where(scol == srow, _stack_scores(q, k), 0.0)
            half = c // 2
            while half >= 1:
                blk = 2 * half
                last_low = (row & ~(blk - 1)) + (half - 1)
                ref = _dot01(jnp.where(col <= last_low, 1.0, 0.0).astype(BF16), parts)
                upper = (qrow & (blk - 1)) >= half
                dq = jnp.minimum(jnp.where(upper, bc - ref, 0.0), 0.0)
                dk = jnp.minimum(jnp.where(upper, 0.0, ref - bc), 0.0)
                s = _stack_scores(q * jnp.exp(dq), k * jnp.exp(dk))
                pair = ((srow & ~(blk - 1)) == (scol & ~(blk - 1))) & ((srow & (blk - 1)) >= half) & ((scol & (blk - 1)) < half)
                acc = acc + jnp.where(pair, s, 0.0)
                half //= 2
            sc_ref[b] = acc.astype(BF16)

    lax.cond(spread <= SAFE_EXP, scores_one_reference, scores_by_levels)

    for b in range(nb):
        q, k = qk_of(b)
        bc = bc_ref[b]
        last = bc[c - 1:c, :]
        v = v_ref[0, b]
        state = st_ref[0, b]
        state_bf = state.astype(BF16)
        q_in = (q * jnp.exp(bc)).astype(BF16)
        k_out = jnp.transpose(k * jnp.exp(last - bc)).astype(BF16)
        decay = jnp.transpose(jnp.broadcast_to(jnp.exp(last), (DV, QK)))
        lane = lax.broadcasted_iota(I32, (c, LANES), 1)
        for h in range(HEADS):
            rs, ls, ks = slice(h * c, (h + 1) * c), slice(h * DV, (h + 1) * DV), slice(h * DK, (h + 1) * DK)
            pair = q_in[:, (h // 2) * LANES:(h // 2 + 1) * LANES]
            if h % 2 == 0:
                lhs = jnp.where(lane < DK, pair, sc_ref[b, rs, :])
                rhs = jnp.concatenate([state_bf[ks, :], v[:, ls]], axis=0)
            else:
                lhs = jnp.where(lane >= DK, pair, sc_ref[b, rs, :])
                rhs = jnp.concatenate([v[:, ls], state_bf[ks, :]], axis=0)
            o_ref[0, b, :, ls] = _dot(lhs, rhs).astype(BF16)
            st_ref[0, b, ks, :] = decay[ks, :] * state[ks, :] + _dot(k_out[ks, :], v[:, ls])


def _chunk_scan(qk, v, g, *, nb, length):
    c = CHUNK
    blk = lambda w: pl.BlockSpec((1, nb, c, w), lambda br, i: (br, 0, i, 0))
    return pl.pallas_call(
        functools.partial(_chunk_kernel, nb=nb, c=c),
        grid=(2, length // c),
        in_specs=[blk(2 * QK), blk(VW), blk(QK)],
        out_specs=[blk(VW), pl.BlockSpec((1, nb, QK, DV), lambda br, i: (br, 0, 0, 0))],
        out_shape=[jax.ShapeDtypeStruct((2, nb, length, VW), BF16),
                   jax.ShapeDtypeStruct((2, nb, QK, DV), F32)],
        scratch_shapes=[pltpu.VMEM((nb, c, QK), F32), pltpu.VMEM((nb, HEADS * c, 2 * c), BF16)],
        compiler_params=_cparams("arbitrary", "arbitrary"),
        name="chunk_scan",
    )(qk, v, g)


def _decode_kernel(qk_ref, v_ref, g_ref, s0_ref, s1_ref, *rest, nt):
    o_ref, n0_ref, n1_ref = rest[-3:]
    for br, (s_ref, n_ref) in enumerate(((s0_ref, n0_ref), (s1_ref, n1_ref))):
        for j in range(nt):
            d = jnp.exp(g_ref[br, j:j + 1, :])
            q = qk_ref[br, j:j + 1, 0:QK].astype(F32)
            k = qk_ref[br, j:j + 1, QK:2 * QK].astype(F32)
            cols = jnp.transpose(jnp.concatenate([d, k, q, jnp.zeros((5, QK), F32)], axis=0))
            vrow = v_ref[br, j:j + 1, :].astype(F32)
            vfull = jnp.concatenate([jnp.broadcast_to(vrow[:, h * DV:(h + 1) * DV], (DK, DV)) for h in range(HEADS)], axis=0)
            new = cols[:, 0:1] * s_ref[j] + cols[:, 1:2] * vfull
            n_ref[j] = new
            t = cols[:, 2:3] * new
            for h in range(HEADS):
                o_ref[br, j:j + 1, h * DV:(h + 1) * DV] = jnp.sum(t[h * DK:(h + 1) * DK, :], axis=0, keepdims=True).astype(BF16)


def _decode(qk, v, g, s_gla, s_hg, prev, *, layer):
    n = qk.shape[1]
    nt = 8
    row = lambda w: pl.BlockSpec((2, nt, w), lambda i: (0, i, 0))
    st = pl.BlockSpec((None, nt, QK, DV), lambda i: (layer, i, 0, 0))
    carried = [] if prev is None else list(prev)
    first = 5
    return pl.pallas_call(
        functools.partial(_decode_kernel, nt=nt),
        grid=(n // nt,),
        in_specs=[row(2 * QK), row(VW), row(QK), st, st] + [pl.BlockSpec(memory_space=pl.ANY)] * len(carried),
        out_specs=[row(VW), st, st],
        out_shape=[jax.ShapeDtypeStruct((2, n, VW), BF16),
                   jax.ShapeDtypeStruct((DEPTH, n, QK, DV), F32),
                   jax.ShapeDtypeStruct((DEPTH, n, QK, DV), F32)],
        input_output_aliases={first + j: 1 + j for j in range(len(carried))},
        compiler_params=_cparams("parallel"),
        name="decode_step",
    )(qk, v, g, s_gla, s_hg, *carried)


def _s5_disc_kernel(lr_ref, li_ref, ldt_ref, br_ref, bi_ref, abr_ref, abi_ref, bbr_ref, bbi_ref):
    lr, li, dt = lr_ref[...], li_ref[...], jnp.exp(ldt_ref[...])
    mag = jnp.exp(lr * dt)
    ab_re, ab_im = mag * jnp.cos(li * dt), mag * jnp.sin(li * dt)
    den = lr * lr + li * li
    num_re = ab_re - 1.0
    coef_re = (num_re * lr + ab_im * li) / den
    coef_im = (ab_im * lr - num_re * li) / den
    br, bi = br_ref[...], bi_ref[...]
    abr_ref[...] = ab_re
    abi_ref[...] = ab_im
    bbr_ref[...] = coef_re * br - coef_im * bi
    bbi_ref[...] = coef_re * bi + coef_im * br


def _s5_discretise(a_re, a_im, log_dt, b_re, b_im):
    n = DEPTH * S5_GROUPS
    rep = lambda t: jnp.repeat(t.reshape(n, S5_STATE), S5_GROUP, axis=1)
    ldt = jnp.broadcast_to(log_dt.reshape(n, 1), (n, S5_STATE * S5_GROUP))
    shape = jax.ShapeDtypeStruct((n, S5_STATE * S5_GROUP), F32)
    ab_re, ab_im, bb_re, bb_im = pl.pallas_call(_s5_disc_kernel, out_shape=[shape] * 4, name="s5_discretise")(
        rep(a_re), rep(a_im), ldt, b_re.reshape(n, -1), b_im.reshape(n, -1))
    pole = lambda t: t[:, ::S5_GROUP].reshape(DEPTH, 1, S5_LANES)
    eye = jnp.eye(8, dtype=F32)

    def blockdiag_in(bb):
        t = bb.reshape(DEPTH, S5_SLABS, 8, S5_STATE, S5_GROUP).transpose(0, 1, 2, 4, 3)
        return jnp.einsum("dcgmp,gh->dcgmhp", t, eye).reshape(DEPTH, S5_SLABS, LANES, 8 * S5_STATE)

    w_in = jnp.concatenate([blockdiag_in(bb_re), blockdiag_in(bb_im)], axis=-1).astype(BF16)
    return w_in, pole(ab_re), pole(ab_im)


def _s5_blockdiag_out(c):
    t = c.reshape(DEPTH, S5_SLABS, 8, S5_GROUP, S5_STATE).transpose(0, 1, 2, 4, 3)
    return jnp.einsum("dcgpm,gh->dcgphm", t, jnp.eye(8, dtype=F32)).reshape(DEPTH, S5_SLABS, 8 * S5_STATE, LANES).astype(BF16)


def _time_major_perm(nb, ct):
    r = np.arange(nb * ct)
    p = np.zeros((nb * ct, nb * ct), np.float32)
    p[r, (r % nb) * ct + r // nb] = 1.0
    return p


def _gelu_tanh(y):
    return 0.5 * y * (1.0 + jnp.tanh(0.7978845608028654 * (y + 0.044715 * (y * y * y))))


def _s5_input(ub, win_ref, xre_ref, xim_ref):
    half = 8 * S5_STATE
    for s in range(S5_SLABS):
        r = _dot(ub[:, s * LANES:(s + 1) * LANES], win_ref[s])
        xre_ref[:, s * half:(s + 1) * half] = r[:, :half]
        xim_ref[:, s * half:(s + 1) * half] = r[:, half:]


def _s5_readout(xre_ref, xim_ref, cre_ref, cim_ref):
    half = 8 * S5_STATE
    ys = []
    for s in range(S5_SLABS):
        ls = slice(s * half, (s + 1) * half)
        ys.append(_dot(xre_ref[:, ls].astype(BF16), cre_ref[s]) - _dot(xim_ref[:, ls].astype(BF16), cim_ref[s]))
    return jnp.concatenate(ys, axis=-1)


def _s5_glu(y, u, d_ref, wglu_ref, bglu_ref):
    z = _gelu_tanh(y + d_ref[...] * u)
    return z * _sigmoid(_dot(z.astype(BF16), wglu_ref[...]) + bglu_ref[...])


def _s5_scan_kernel(su_ref, perm_ref, permt_ref, win_ref, are_ref, aim_ref, cre_ref, cim_ref, d_ref, wglu_ref, bglu_ref,
                    o_ref, hre_ref, him_ref, xre_ref, xim_ref, *, nb, ct):
    @pl.when(pl.program_id(0) == 0)
    def _():
        hre_ref[...] = jnp.zeros_like(hre_ref)
        him_ref[...] = jnp.zeros_like(him_ref)

    u = su_ref[...].reshape(nb * ct, S5_WIDTH)
    ub = _dot(perm_ref[...], u.astype(BF16)).astype(BF16)

    half = 8 * S5_STATE
    ys = []
    for s in range(S5_SLABS):
        ls = slice(s * half, (s + 1) * half)
        r = _dot(ub[:, s * LANES:(s + 1) * LANES], win_ref[s])
        xre_ref[:, ls] = r[:, :half]
        xim_ref[:, ls] = r[:, half:]
        ar = jnp.broadcast_to(are_ref[:, ls], (nb, half))
        ai = jnp.broadcast_to(aim_ref[:, ls], (nb, half))
        hr, hi = hre_ref[:, ls], him_ref[:, ls]
        for t in range(ct):
            rows = slice(t * nb, (t + 1) * nb)
            hr, hi = ar * hr - ai * hi + xre_ref[rows, ls], ar * hi + ai * hr + xim_ref[rows, ls]
            xre_ref[rows, ls] = hr
            xim_ref[rows, ls] = hi
        hre_ref[:, ls] = hr
        him_ref[:, ls] = hi
        ys.append(_dot(xre_ref[:, ls].astype(BF16), cre_ref[s]) - _dot(xim_ref[:, ls].astype(BF16), cim_ref[s]))

    y = jnp.concatenate(ys, axis=-1)
    y_hi = y.astype(BF16)
    y_lo = (y - y_hi.astype(F32)).astype(BF16)
    y = _dot01(permt_ref[...], (y_hi, y_lo))
    o_ref[...] = _s5_glu(y, u, d_ref, wglu_ref, bglu_ref).reshape(nb, ct, S5_WIDTH).astype(BF16)


def _s5_weight_specs(layer):
    return [_layer_spec((S5_SLABS, LANES, 2 * 8 * S5_STATE), layer),
            _layer_spec((1, S5_LANES), layer), _layer_spec((1, S5_LANES), layer),
            _layer_spec((S5_SLABS, 8 * S5_STATE, LANES), layer), _layer_spec((S5_SLABS, 8 * S5_STATE, LANES), layer),
            _layer_spec((1, S5_WIDTH), layer), _layer_spec((S5_WIDTH, S5_WIDTH), layer),
            _layer_spec((1, S5_WIDTH), layer)]


def _s5_scan(su, weights, *, nb, length, layer):
    ct = S5_CHUNK
    perm = _time_major_perm(nb, ct)
    const = pl.BlockSpec((nb * ct, nb * ct), lambda i: (0, 0))
    return pl.pallas_call(
        functools.partial(_s5_scan_kernel, nb=nb, ct=ct),
        grid=(length // ct,),
        in_specs=[pl.BlockSpec((nb, ct, S5_WIDTH), lambda i: (0, i, 0)), const, const] + _s5_weight_specs(layer),
        out_specs=[pl.BlockSpec((nb, ct, S5_WIDTH), lambda i: (0, i, 0)),
                   pl.BlockSpec((nb, S5_LANES), lambda i: (0, 0)), pl.BlockSpec((nb, S5_LANES), lambda i: (0, 0))],
        out_shape=[jax.ShapeDtypeStruct((nb, length, S5_WIDTH), BF16),
                   jax.ShapeDtypeStruct((nb, S5_LANES), F32), jax.ShapeDtypeStruct((nb, S5_LANES), F32)],
        scratch_shapes=[pltpu.VMEM((nb * ct, S5_LANES), F32), pltpu.VMEM((nb * ct, S5_LANES), F32)],
        compiler_params=_cparams("arbitrary"),
        name="s5_scan",
    )(su, jnp.asarray(perm, BF16), jnp.asarray(perm.T, BF16), *weights)


def _s5_step_kernel(su_ref, h0r_ref, h0i_ref, win_ref, are_ref, aim_ref, cre_ref, cim_ref, d_ref, wglu_ref, bglu_ref,
                    o_ref, hre_ref, him_ref):
    u = su_ref[...]
    _s5_input(u.astype(BF16), win_ref, hre_ref, him_ref)
    ar, ai = are_ref[...], aim_ref[...]
    h0r, h0i = h0r_ref[...], h0i_ref[...]
    nr = ar * h0r - ai * h0i + hre_ref[...]
    ni = ar * h0i + ai * h0r + him_ref[...]
    hre_ref[...] = nr
    him_ref[...] = ni
    o_ref[...] = _s5_glu(_s5_readout(hre_ref, him_ref, cre_ref, cim_ref), u, d_ref, wglu_ref, bglu_ref).astype(BF16)


def _s5_step(su, h0r, h0i, weights, *, layer):
    n = su.shape[0]
    full = lambda w: pl.BlockSpec((n, w), lambda i: (0, 0))
    state = pl.BlockSpec((None, n, S5_LANES), lambda i: (layer, 0, 0))
    return pl.pallas_call(
        _s5_step_kernel,
        grid=(1,),
        in_specs=[full(S5_WIDTH), state, state] + _s5_weight_specs(layer),
        out_specs=[full(S5_WIDTH), full(S5_LANES), full(S5_LANES)],
        out_shape=[jax.ShapeDtypeStruct((n, S5_WIDTH), BF16),
                   jax.ShapeDtypeStruct((n, S5_LANES), F32), jax.ShapeDtypeStruct((n, S5_LANES), F32)],
        compiler_params=_cparams("arbitrary"),
        name="s5_step",
    )(su, h0r, h0i, *weights)


def _head_norm(o, g):
    parts = []
    for h in range(HEADS):
        seg = o[:, h * DV:(h + 1) * DV]
        parts.append(seg * lax.rsqrt(jnp.mean(seg * seg, axis=-1, keepdims=True) + EPS))
    return jnp.concatenate(parts, axis=-1) * g


def _route(logits, count):
    rows = logits.shape[0]
    lane = lax.broadcasted_iota(I32, logits.shape, 1)
    neg = -jnp.inf
    first = lambda hit: jnp.min(jnp.where(hit, lane, LANES), axis=-1, keepdims=True)
    glog = jnp.where(lane < MOE_GROUPS, logits, neg)
    gmax = jnp.max(glog, axis=-1, keepdims=True)
    gidx = first(glog == gmax)
    gw = 1.0 / jnp.sum(jnp.where(lane < MOE_GROUPS, jnp.exp(logits - gmax), 0.0), axis=-1, keepdims=True)
    inside = (lane >= MOE_GROUPS) & (lane < MOE_GROUPS + MOE_EXPERTS) & (((lane - MOE_GROUPS) >> 3) == gidx)
    el = jnp.where(inside, logits, neg)
    v1 = jnp.max(el, axis=-1, keepdims=True)
    i1 = first(el == v1)
    el2 = jnp.where(lane == i1, neg, el)
    v2 = jnp.max(el2, axis=-1, keepdims=True)
    i2 = first(el2 == v2)
    p2 = jnp.exp(v2 - v1)
    w1 = gw / (1.0 + p2)
    w2 = gw * p2 / (1.0 + p2)
    hit1, hit2 = lane == i1, lane == i2
    onehot = jnp.where(hit1 | hit2, 1.0, 0.0)
    tr = lax.broadcasted_iota(I32, (rows, rows), 0)
    tc = lax.broadcasted_iota(I32, (rows, rows), 1)
    before = _dot(jnp.where(tc < tr, 1.0, 0.0).astype(BF16), onehot.astype(BF16)) + count
    rank1 = jnp.sum(jnp.where(hit1, before, 0.0), axis=-1, keepdims=True)
    rank2 = jnp.sum(jnp.where(hit2, before, 0.0), axis=-1, keepdims=True)
    slab = jnp.zeros(logits.shape, F32)
    for ln, val in ((R_E1, (i1 - MOE_GROUPS).astype(F32)), (R_E2, (i2 - MOE_GROUPS).astype(F32)), (R_W1, w1), (R_W2, w2),
                    (R_RANK1, rank1), (R_RANK2, rank2)):
        slab = jnp.where(lane == ln, val, slab)
    return slab, count + jnp.sum(onehot, axis=0, keepdims=True)


def _merge_kernel(x_ref, o_ref, gate_ref, os5_ref, cnt0_ref, nm_ref, wgm_ref, gn_ref, wbr_ref, wout_ref, nf_ref,
                  wr_ref, br_ref, *rest, sub):
    xm_ref, h2_ref, rt_ref, rtt_ref, cnt_ref = rest[-5:]

    @pl.when(pl.program_id(0) == 0)
    def _():
        cnt_ref[...] = cnt0_ref[...]

    logits = []
    for r0 in range(0, x_ref.shape[0], sub):
        rs = slice(r0, r0 + sub)
        x = x_ref[rs, :]
        hb = _rms(x, nm_ref[...]).astype(BF16)
        mixed = None
        for i, src in enumerate((0, None, 1)):
            if src is None:
                branch = os5_ref[rs, :]
            else:
                branch = (_head_norm(o_ref[src, rs, :].astype(F32), gn_ref[src]) * gate_ref[src, rs, :].astype(F32)).astype(BF16)
            gate = _sigmoid(_dot(hb, wgm_ref[:, i * D_MODEL:(i + 1) * D_MODEL]))
            term = gate * _dot(branch, wbr_ref[i])
            mixed = term if mixed is None else mixed + term
        xm = x + _dot(mixed.astype(BF16), wout_ref[...])
        xm_ref[rs, :] = xm
        h2 = _rms(xm, nf_ref[...])
        h2_hi = h2.astype(BF16)
        h2_lo = (h2 - h2_hi.astype(F32)).astype(BF16)
        h2_ref[rs, :] = h2_hi
        both = _dot(h2_hi, wr_ref[...])
        logits.append(both[:, :LANES] + (both[:, LANES:] + _dot(h2_lo, wr_ref[:, :LANES])) + br_ref[...])
    slab, cnt_ref[...] = _route(jnp.concatenate(logits, axis=0), cnt_ref[...])
    rt_ref[...] = slab
    rtt_ref[...] = jnp.transpose(slab)[0:8, :]


def _merge(x, o, gate, os5, cnt0, weights, carried, *, layer, tm, total_rows, block_offset):
    rows = x.shape[0]
    tile = lambda w: pl.BlockSpec((tm, w), lambda i: (i, 0))
    shared = lambda w: pl.BlockSpec((tm, w), lambda i: (i + block_offset, 0))
    carried = [] if carried is None else list(carried)
    w_specs = [_layer_spec((1, D_MODEL), layer), _layer_spec((D_MODEL, 3 * D_MODEL), layer),
               _layer_spec((2, 1, VW), layer), _layer_spec((3, VW, D_MODEL), layer),
               _layer_spec((D_MODEL, D_MODEL), layer), _layer_spec((1, D_MODEL), layer),
               _layer_spec((D_MODEL, 2 * LANES), layer), _layer_spec((1, LANES), layer)]
    first = 5 + len(w_specs)
    return pl.pallas_call(
        functools.partial(_merge_kernel, sub=min(tm, 256)),
        grid=(rows // tm,),
        in_specs=[tile(D_MODEL),
                  pl.BlockSpec((2, tm, VW), lambda i: (0, i, 0)),
                  pl.BlockSpec((2, tm, VW), lambda i: (0, i, 0)),
                  tile(S5_WIDTH),
                  pl.BlockSpec((1, LANES), lambda i: (0, 0))] + w_specs + [pl.BlockSpec(memory_space=pl.ANY)] * len(carried),
        out_specs=[tile(D_MODEL), shared(D_MODEL), shared(LANES),
                   pl.BlockSpec((8, tm), lambda i: (0, i + block_offset)), pl.BlockSpec((1, LANES), lambda i: (0, 0))],
        out_shape=[jax.ShapeDtypeStruct((rows, D_MODEL), F32),
                   jax.ShapeDtypeStruct((total_rows, D_MODEL), BF16),
                   jax.ShapeDtypeStruct((total_rows, LANES), F32),
                   jax.ShapeDtypeStruct((8, total_rows), F32),
                   jax.ShapeDtypeStruct((1, LANES), F32)],
        input_output_aliases={first + j: 1 + j for j in range(len(carried))},
        compiler_params=_cparams("arbitrary"),
        name="merge_route",
    )(x, o, gate, os5, cnt0, *weights, *carried)


def _ffn_kernel(te_ref, nu_ref, x_ref, wg_ref, wu_ref, wd_ref, *rest, tile0):
    y_ref, wg_bf, wu_bf, wd_bf = rest[-4:]
    i = pl.program_id(0)
    tile = i + tile0

    @pl.when(tile < nu_ref[0])
    def _():
        @pl.when((i == 0) | (te_ref[tile] != te_ref[jnp.maximum(tile - 1, 0)]))
        def _():
            wg_bf[...] = wg_ref[...].astype(BF16)
            wu_bf[...] = wu_ref[...].astype(BF16)
            wd_bf[...] = wd_ref[...].astype(BF16)

        x = x_ref[...]
        act = _silu(_dot(x, wg_bf[...])) * _dot(x, wu_bf[...])
        y_ref[...] = _dot(act.astype(BF16), wd_bf[...]).astype(BF16)


def _grouped_ffn(tile_expert, n_used, xg, y_prev, wg, wu, wd, *, layer, tile0, total_rows):
    used = lambda i, nu: jnp.maximum(jnp.minimum(i + tile0, nu[0] - 1), tile0)
    expert = lambda a, b: pl.BlockSpec((None, None, a, b), lambda i, te, nu: (layer, te[used(i, nu)], 0, 0))
    carried = [] if y_prev is None else [y_prev]
    grid_spec = pltpu.PrefetchScalarGridSpec(
        num_scalar_prefetch=2,
        grid=(xg.shape[0] // MOE_TILE,),
        in_specs=[pl.BlockSpec((MOE_TILE, D_MODEL), lambda i, te, nu: (used(i, nu) - tile0, 0)),
                  expert(D_MODEL, MOE_FF), expert(D_MODEL, MOE_FF), expert(MOE_FF, D_MODEL)]
        + [pl.BlockSpec(memory_space=pl.ANY)] * len(carried),
        out_specs=pl.BlockSpec((MOE_TILE, D_MODEL), lambda i, te, nu: (used(i, nu), 0)),
        scratch_shapes=[pltpu.VMEM((D_MODEL, MOE_FF), BF16), pltpu.VMEM((D_MODEL, MOE_FF), BF16),
                        pltpu.VMEM((MOE_FF, D_MODEL), BF16)],
    )
    return pl.pallas_call(
        functools.partial(_ffn_kernel, tile0=tile0),
        grid_spec=grid_spec,
        out_shape=jax.ShapeDtypeStruct((total_rows, D_MODEL), BF16),
        input_output_aliases={6: 0} if carried else {},
        compiler_params=_cparams("arbitrary"),
        name="expert_ffn",
    )(tile_expert, n_used, xg, wg, wu, wd, *carried)


@compute_on("tpu_sparsecore")
@jax.jit
def _take_rows(x, idx):
    return jnp.take(x, idx, axis=0, mode="clip")


def _moe_plan(route_t, count):
    tokens = route_t.shape[1]
    n_tiles = -(-(2 * tokens + MOE_EXPERTS * (MOE_TILE - 1)) // MOE_TILE)
    n_rows = n_tiles * MOE_TILE
    counts = count[0, MOE_GROUPS:MOE_GROUPS + MOE_EXPERTS].astype(I32)
    padded = ((counts + MOE_TILE - 1) // MOE_TILE) * MOE_TILE
    gend = jnp.cumsum(padded)
    gstart = gend - padded
    experts = jnp.arange(MOE_EXPERTS, dtype=I32)[:, None]

    def rows_of(e_lane, rank_lane):
        e = route_t[e_lane].astype(I32)
        return jnp.sum(jnp.where(e[None, :] == experts, gstart[:, None], 0), axis=0) + route_t[rank_lane].astype(I32)

    pos1, pos2 = rows_of(R_E1, R_RANK1), rows_of(R_E2, R_RANK2)
    token = jnp.arange(tokens, dtype=I32) + 1
    marked = jnp.zeros((n_rows,), I32).at[jnp.concatenate([pos1, pos2])].add(jnp.concatenate([token, token]),
                                                                             unique_indices=True)
    filler = jnp.arange(n_rows, dtype=I32) % tokens
    src = jnp.where(marked > 0, marked - 1, filler)
    tile_start = jnp.arange(n_tiles, dtype=I32) * MOE_TILE
    tile_expert = jnp.minimum(jnp.sum((tile_start[:, None] >= gend[None, :]).astype(I32), axis=1), MOE_EXPERTS - 1)
    n_used = (gend[-1] // MOE_TILE).astype(I32).reshape(1)
    return src, pos1, pos2, tile_expert, n_used


def _ple_kernel(xm_ref, y1_ref, y2_ref, rt_ref, p_ref, np_ref, wg_ref, wp_ref, nfin_ref, *rest, final):
    out_ref = rest[-1]
    rt = rt_ref[...]
    x1 = xm_ref[...] + rt[:, R_W1:R_W1 + 1] * y1_ref[...].astype(F32) + rt[:, R_W2:R_W2 + 1] * y2_ref[...].astype(F32)
    gate = _sigmoid(_dot(_rms(x1, np_ref[...]).astype(BF16), wg_ref[...]))
    x2 = x1 + gate * _dot(p_ref[...].astype(BF16), wp_ref[...])
    out_ref[...] = _rms(x2, nfin_ref[...]) if final else x2


def _ple(xm, y1, y2, route, p, x_prev, npl, wg, wp, nfin, *, layer, final, tm, n_blocks, x_off, y_off, rt_off):
    at = lambda w, off: pl.BlockSpec((tm, w), lambda i: (i + off, 0))
    carried = [] if x_prev is None else [x_prev]
    return pl.pallas_call(
        functools.partial(_ple_kernel, final=final),
        grid=(n_blocks,),
        in_specs=[at(D_MODEL, x_off), at(D_MODEL, y_off), at(D_MODEL, y_off), at(LANES, rt_off),
                  pl.BlockSpec((None, tm, PLE_DIM), lambda i: (layer, i + x_off, 0)),
                  _layer_spec((1, D_MODEL), layer), _layer_spec((D_MODEL, D_MODEL), layer),
                  _layer_spec((PLE_DIM, D_MODEL), layer), pl.BlockSpec((1, D_MODEL), lambda i: (0, 0))]
        + [pl.BlockSpec(memory_space=pl.ANY)] * len(carried),
        out_specs=at(D_MODEL, x_off),
        out_shape=jax.ShapeDtypeStruct(xm.shape, F32),
        input_output_aliases={9: 0} if carried else {},
        compiler_params=_cparams("parallel"),
        name="combine_ple",
    )(xm, y1, y2, route, p, npl, wg, wp, nfin, *carried)


def kernel(x_prompt, x_sample, state_gla, state_s5_re, state_s5_im, state_hgrn, p_prompt, p_sample, norm_mix, w_in, gla_w_gate2, gla_b_gate, gla_norm, s5_a_re, s5_a_im, s5_log_dt, s5_b_re, s5_b_im, s5_c_re, s5_c_im, s5_d, s5_w_glu, s5_b_glu, hgrn_lb_logits, hgrn_norm, w_br_gla, w_br_s5, w_br_hgrn, w_out, norm_ffn, moe_w_group, moe_b_group, moe_w_expert, moe_b_expert, moe_w_gate, moe_w_up, moe_w_down, norm_ple, w_ple_gate, w_ple_proj, norm_final):
    nb, length, _ = x_prompt.shape
    ns = x_sample.shape[0]
    n_p = nb * length
    n_all = n_p + ns
    row = lambda t: t.reshape(DEPTH, 1, -1)

    w_a, w_gm = _w_in_layout(w_in)
    wg2 =jnp.concatenate([gla_w_gate2, jnp.zeros((DEPTH, LANES - GLA_RANK, QK), F32)], axis=1).astype(BF16)
    s5_win, s5_are, s5_aim = _s5_discretise(s5_a_re, s5_a_im, s5_log_dt, s5_b_re, s5_b_im)
    s5_w = (s5_win, s5_are, s5_aim, _s5_blockdiag_out(s5_c_re), _s5_blockdiag_out(s5_c_im), row(s5_d),
            s5_w_glu.astype(BF16), row(s5_b_glu))
    head_gain = jnp.stack([jnp.tile(gla_norm, (1, HEADS)), jnp.tile(hgrn_norm, (1, HEADS))], axis=1).reshape(DEPTH, 2, 1, VW)
    w_router = jnp.concatenate([moe_w_group, moe_w_expert,
                                jnp.zeros((DEPTH, D_MODEL, LANES - MOE_GROUPS - MOE_EXPERTS), F32)], axis=-1)
    wr_hi = w_router.astype(BF16)
    wr_lo = (w_router - wr_hi.astype(F32)).astype(BF16)
    wr_hilo = jnp.concatenate([wr_hi, wr_lo], axis=-1)
    b_router = jnp.concatenate([moe_b_group, moe_b_expert,
                                jnp.zeros((DEPTH, LANES - MOE_GROUPS - MOE_EXPERTS), F32)], axis=-1).reshape(DEPTH, 1, LANES)
    merge_w = (row(norm_mix), w_gm, head_gain,
               jnp.stack([w_br_gla, w_br_s5, w_br_hgrn], axis=1).astype(BF16), w_out.astype(BF16), row(norm_ffn),
               wr_hilo, b_router)
    ple_w = (row(norm_ple), w_ple_gate.astype(BF16), w_ple_proj.astype(BF16), norm_final.reshape(1, D_MODEL))
    nm, bg = row(norm_mix), row(gla_b_gate)
    p_p, p_s = p_prompt.reshape(DEPTH, n_p, PLE_DIM), p_sample.reshape(DEPTH, ns, PLE_DIM)
    sg_in, sh_in = state_gla.reshape(DEPTH, ns, QK, DV), state_hgrn.reshape(DEPTH, ns, QK, DV)
    s5r_in, s5i_in = state_s5_re.reshape(DEPTH, ns, S5_LANES), state_s5_im.reshape(DEPTH, ns, S5_LANES)

    xp = x_prompt.reshape(n_p, D_MODEL)
    xs = x_sample.reshape(ns, D_MODEL)
    new_p, new_s5 = [], []
    new_s = None
    for i in range(DEPTH):
        qk, v, gate, g, su = _inproj(xp, nm, w_a, wg2, bg, hgrn_lb_logits, layer=i, tm=1024)
        o_p, st_p = _chunk_scan(qk.reshape(2, nb, length, -1), v.reshape(2, nb, length, -1),
                                g.reshape(2, nb, length, -1), nb=nb, length=length)
        os5_p, hre_p, him_p = _s5_scan(su.reshape(nb, length, -1), s5_w, nb=nb, length=length, layer=i)
        xm_p, h2, rt, rtt, cnt = _merge(xp, o_p.reshape(2, n_p, VW), gate, os5_p.reshape(n_p, -1),
                                        jnp.zeros((1, LANES), F32), merge_w, None, layer=i, tm=1024, total_rows=n_all,
                                        block_offset=0)
        new_p.append((st_p[0].reshape(nb, HEADS, DK, DV), hre_p.reshape(nb, S5_GROUPS, S5_STATE),
                      him_p.reshape(nb, S5_GROUPS, S5_STATE), st_p[1].reshape(nb, HEADS, DK, DV)))

        qk, v, gate, g, su = _inproj(xs, nm, w_a, wg2, bg, hgrn_lb_logits, layer=i, tm=ns)
        o_s, *new_s = _decode(qk, v, g, sg_in, sh_in, new_s, layer=i)
        os5_s, hre_s, him_s = _s5_step(su, s5r_in, s5i_in, s5_w, layer=i)
        xm_s, h2, rt, rtt, cnt = _merge(xs, o_s, gate, os5_s, cnt, merge_w, (h2, rt, rtt), layer=i, tm=ns,
                                        total_rows=n_all, block_offset=n_p // ns)
        new_s5.append((hre_s.reshape(ns, S5_GROUPS, S5_STATE), him_s.reshape(ns, S5_GROUPS, S5_STATE)))

        src, pos1, pos2, tile_expert, n_used = _moe_plan(rtt, cnt)
        n_tiles = tile_expert.shape[0]
        y = None
        for k in range(PIPE):
            t0, t1 = k * n_tiles // PIPE, (k + 1) * n_tiles // PIPE
            y = _grouped_ffn(tile_expert, n_used, _take_rows(h2, src[t0 * MOE_TILE:t1 * MOE_TILE]), y,
                             moe_w_gate, moe_w_up, moe_w_down, layer=i, tile0=t0, total_rows=n_tiles * MOE_TILE)

        final = i == DEPTH - 1
        tm = 512
        blocks = n_p // tm
        pieces = min(PIPE, blocks)
        x_new = None
        for k in range(pieces):
            b0, b1 = k * blocks // pieces, (k + 1) * blocks // pieces
            r0, r1 = b0 * tm, (n_all if k == pieces - 1 else b1 * tm)
            y1, y2 = _take_rows(y, pos1[r0:r1]), _take_rows(y, pos2[r0:r1])
            x_new = _ple(xm_p, y1, y2, rt, p_p, x_new, *ple_w, layer=i, final=final, tm=tm, n_blocks=b1 - b0,
                         x_off=b0, y_off=0, rt_off=b0)
        xp = x_new
        xs = _ple(xm_s, y1, y2, rt, p_s, None, *ple_w, layer=i, final=final, tm=ns, n_blocks=1,
                  x_off=0, y_off=(n_p - r0) // ns, rt_off=n_p // ns)

    stack = lambda items, j: jnp.stack([it[j] for it in items])
    return (xp.reshape(nb, length, D_MODEL), xs.reshape(ns, 1, D_MODEL),
            stack(new_p, 0), stack(new_p, 1), stack(new_p, 2), stack(new_p, 3),
            new_s[0].reshape(DEPTH, ns, HEADS, DK, DV), stack(new_s5, 0), stack(new_s5, 1),
            new_s[1].reshape(DEPTH, ns, HEADS, DK, DV))
```

```python
import functools

import jax
import jax.numpy as jnp
import numpy as np
from jax import lax
from jax.experimental import pallas as pl
from jax.experimental.compute_on import compute_on
from jax.experimental.pallas import tpu as pltpu

F32, BF16, I32 = jnp.float32, jnp.bfloat16, jnp.int32

D_MODEL = 1024
DEPTH = 2
HEADS, DK, DV = 4, 64, 128
QK, VW = HEADS * DK, HEADS * DV
GLA_RANK, GLA_TAU = 16, 16.0
S5_WIDTH, S5_GROUP, S5_GROUPS, S5_STATE = 512, 16, 32, 64
S5_LANES = S5_GROUPS * S5_STATE
S5_SLABS = 4
MOE_GROUPS, MOE_PER_GROUP, MOE_EXPERTS, MOE_FF = 4, 8, 32, 256
PLE_DIM = 256
EPS = 1e-6

LANES = 128
CHUNK = 64
S5_CHUNK = 64
MOE_TILE = 512
PIPE = 2
SAFE_EXP = 80.0
VMEM_LIMIT = 56 * 1024 * 1024

C_GLA, C_S5, C_HG, C_GLR, C_END = 0, 1536, 2048, 3584, 3712
R_E1, R_E2, R_W1, R_W2, R_RANK1, R_RANK2 = 0, 1, 2, 3, 4, 5


def _cparams(*sem):
    return pltpu.CompilerParams(dimension_semantics=sem, vmem_limit_bytes=VMEM_LIMIT)


def _layer_spec(shape, layer):
    return pl.BlockSpec((None,) + tuple(shape), lambda *_: (layer,) + (0,) * len(shape), pipeline_mode=pl.Buffered(1))


def _dot(a, b):
    return jnp.dot(a, b, preferred_element_type=F32)


def _rms(x, g):
    return x * lax.rsqrt(jnp.mean(x * x, axis=-1, keepdims=True) + EPS) * g


def _log_sigmoid(x):
    return jnp.minimum(x, 0.0) - jnp.log1p(jnp.exp(-jnp.abs(x)))


def _sigmoid(x):
    return 0.5 * jnp.tanh(0.5 * x) + 0.5


def _silu(x):
    return x * _sigmoid(x)


def _split3(x):
    hi = x.astype(BF16)
    r1 = x - hi.astype(F32)
    mid = r1.astype(BF16)
    lo = (r1 - mid.astype(F32)).astype(BF16)
    return hi, mid, lo


def _dot01(m, parts):
    out = _dot(m, parts[0])
    for p in parts[1:]:
        out = out + _dot(m, p)
    return out


W_GLA, W_GLR, W_S5HG, W_END = 1536, 1552, 3600, 6672


def _w_in_layout_kernel(wt_ref, wa_ref, wgm_ref):
    cols = lambda lo, hi: jnp.transpose(wt_ref[lo:hi, :])
    wa_ref[:, C_GLA:C_S5] = cols(0, W_GLA).astype(BF16)
    wa_ref[:, C_S5:C_GLR] = cols(W_GLR, W_S5HG).astype(BF16)
    lane = lax.broadcasted_iota(I32, (LANES, LANES), 1)
    wa_ref[:, C_GLR:C_END] = jnp.where(lane < GLA_RANK, cols(W_GLA, W_GLA + LANES), 0.0).astype(BF16)
    wgm_ref[...] = cols(W_S5HG, W_END).astype(BF16)


def _w_in_layout(w_in):
    wt = jnp.swapaxes(w_in, 1, 2)
    return pl.pallas_call(
        _w_in_layout_kernel,
        grid=(DEPTH, D_MODEL // LANES),
        in_specs=[pl.BlockSpec((None, W_END, LANES), lambda d, i: (d, 0, i))],
        out_specs=[pl.BlockSpec((None, LANES, C_END), lambda d, i: (d, i, 0)),
                   pl.BlockSpec((None, LANES, 3 * D_MODEL), lambda d, i: (d, i, 0))],
        out_shape=[jax.ShapeDtypeStruct((DEPTH, D_MODEL, C_END), BF16),
                   jax.ShapeDtypeStruct((DEPTH, D_MODEL, 3 * D_MODEL), BF16)],
        compiler_params=_cparams("parallel", "parallel"),
        name="w_in_layout",
    )(wt)
def _inproj_kernel(x_ref, nm_ref, w_ref, wg2_ref, bg_ref, lbl_ref, qk_ref, v_ref, gate_ref, g_ref, su_ref, *, layer, sub):
    lg = lbl_ref[...]
    mx = jnp.max(lg, axis=0, keepdims=True)
    ex = jnp.exp(lg - mx)
    sm = ex / jnp.sum(ex, axis=0, keepdims=True)
    cs = sm[0:1]
    for j in range(1, layer + 1):
        cs = cs + sm[j:j + 1]
    lb = cs - sm[0:1]
    log_lb, log_1mlb = jnp.log(lb), jnp.log1p(-lb)

    for r0 in range(0, x_ref.shape[0], sub):
        rs = slice(r0, r0 + sub)
        hb = _rms(x_ref[rs, :], nm_ref[...]).astype(BF16)

        def proj(lo, hi):
            return _dot(hb, w_ref[:, lo:hi])

        qk_ref[0, rs, 0:QK] = (proj(0, 256) * (DK ** -0.5)).astype(BF16)
        qk_ref[0, rs, QK:2 * QK] = proj(256, 512).astype(BF16)
        v_ref[0, rs, :] = proj(512, 1024).astype(BF16)
        gate_ref[0, rs, :] = _silu(proj(1024, 1536)).astype(BF16)
        glr = proj(C_GLR, C_END).astype(BF16)
        g_ref[0, rs, :] = _log_sigmoid(_dot(glr, wg2_ref[...]) + bg_ref[...]) * (1.0 / GLA_TAU)

        su_ref[rs, :] = proj(C_S5, C_HG)

        z = proj(C_HG + 256, C_HG + 512)
        a, c = log_lb, log_1mlb + _log_sigmoid(z)
        g_ref[1, rs, :] = jnp.maximum(a, c) + jnp.log1p(jnp.exp(-jnp.abs(a - c)))
        qk_ref[1, rs, 0:QK] = _silu(proj(C_HG, C_HG + 256)).astype(BF16)
        qk_ref[1, rs, QK:2 * QK] = ((1.0 - lb) * _sigmoid(-z)).astype(BF16)
        v_ref[1, rs, :] = proj(C_HG + 512, C_HG + 1024).astype(BF16)
        gate_ref[1, rs, :] = _silu(proj(C_HG + 1024, C_HG + 1536)).astype(BF16)


def _inproj(x, nm, w, wg2, bg, lbl, *, layer, tm):
    rows = x.shape[0]
    return pl.pallas_call(
        functools.partial(_inproj_kernel, layer=layer, sub=min(tm, 256)),
        grid=(rows // tm,),
        in_specs=[pl.BlockSpec((tm, D_MODEL), lambda i: (i, 0)),
                  _layer_spec((1, D_MODEL), layer),
                  _layer_spec((D_MODEL, C_END), layer),
                  _layer_spec((LANES, QK), layer),
                  _layer_spec((1, QK), layer),
                  pl.BlockSpec((DEPTH, QK), lambda i: (0, 0))],
        out_specs=[pl.BlockSpec((2, tm, 2 * QK), lambda i: (0, i, 0)),
                   pl.BlockSpec((2, tm, VW), lambda i: (0, i, 0)),
                   pl.BlockSpec((2, tm, VW), lambda i: (0, i, 0)),
                   pl.BlockSpec((2, tm, QK), lambda i: (0, i, 0)),
                   pl.BlockSpec((tm, S5_WIDTH), lambda i: (i, 0))],
        out_shape=[jax.ShapeDtypeStruct((2, rows, 2 * QK), BF16),
                   jax.ShapeDtypeStruct((2, rows, VW), BF16),
                   jax.ShapeDtypeStruct((2, rows, VW), BF16),
                   jax.ShapeDtypeStruct((2, rows, QK), F32),
                   jax.ShapeDtypeStruct((rows, S5_WIDTH), F32)],
        compiler_params=_cparams("parallel"),
        name="inproj",
    )(x, nm, w, wg2, bg, lbl)


def _head_stack(x):
    head = lax.broadcasted_iota(I32, x.shape, 1) // DK
    return jnp.concatenate([jnp.where(head == h, x, 0.0) for h in range(HEADS)], axis=0).astype(BF16)


def _stack_scores(qt, kt):
    kt = kt.astype(BF16)
    return lax.dot_general(_head_stack(qt), jnp.concatenate([kt, kt], axis=0), (((1,), (1,)), ((), ())),
                           preferred_element_type=F32)


def _chunk_kernel(qk_ref, v_ref, g_ref, o_ref, st_ref, bc_ref, sc_ref, *, nb, c):
    assert c == DK and 2 * c == LANES

    @pl.when(pl.program_id(1) == 0)
    def _():
        st_ref[...] = jnp.zeros_like(st_ref)

    row = lax.broadcasted_iota(I32, (c, c), 0)
    col = lax.broadcasted_iota(I32, (c, c), 1)
    tri = jnp.where(col <= row, 1.0, 0.0).astype(BF16)
    srow = lax.broadcasted_iota(I32, (HEADS * c, 2 * c), 0) & (c - 1)
    scol = lax.broadcasted_iota(I32, (HEADS * c, 2 * c), 1) & (c - 1)
    mid = c // 2 - 1

    spread = None
    for b in range(nb):
        bc = _dot01(tri, _split3(g_ref[0, b]))
        bc_ref[b] = bc
        ref, last = bc[mid:mid + 1, :], bc[c - 1:c, :]
        s = jnp.maximum(jnp.max(-ref), jnp.max(ref - last))
        spread = s if spread is None else jnp.maximum(spread, s)

    def qk_of(b):
        return qk_ref[0, b, :, 0:QK].astype(F32), qk_ref[0, b, :, QK:2 * QK].astype(F32)

    def scores_one_reference():
        for b in range(nb):
            q, k = qk_of(b)
            bc = bc_ref[b]
            ref = bc[mid:mid + 1, :]
            s = _stack_scores(q * jnp.exp(bc - ref), k * jnp.exp(ref - bc))
            sc_ref[b] = jnp.where(scol <= srow, s, 0.0).astype(BF16)

    def scores_by_levels():
        qrow = lax.broadcasted_iota(I32, (c, QK), 0)
        for b in range(nb):
            q, k = qk_of(b)
            bc = bc_ref[b]
            parts = _split3(g_ref[0, b])
            acc = jnp.where(scol == srow, _stack_scores(q, k), 0.0)
            half = c // 2
            while half >= 1:
                blk = 2 * half
                last_low = (row & ~(blk - 1)) + (half - 1)
                ref = _dot01(jnp.where(col <= last_low, 1.0, 0.0).astype(BF16), parts)
                upper = (qrow & (blk - 1)) >= half
                dq = jnp.minimum(jnp.where(upper, bc - ref, 0.0), 0.0)
                dk = jnp.minimum(jnp.where(upper, 0.0, ref - bc), 0.0)
                s = _stack_scores(q * jnp.exp(dq), k * jnp.exp(dk))
                pair = ((srow & ~(blk - 1)) == (scol & ~(blk - 1))) & ((srow & (blk - 1)) >= half) & ((scol & (blk - 1)) < half)
                acc = acc + jnp.where(pair, s, 0.0)
                half //= 2
            sc_ref[b] = acc.astype(BF16)

    lax.cond(spread <= SAFE_EXP, scores_one_reference, scores_by_levels)

    for b in range(nb):
        q, k = qk_of(b)
        bc = bc_ref[b]
        last = bc[c - 1:c, :]
        v = v_ref[0, b]
        state = st_ref[0, b]
        state_bf = state.astype(BF16)
        q_in = (q * jnp.exp(bc)).astype(BF16)
        k_out = jnp.transpose(k * jnp.exp(last - bc)).astype(BF16)
        decay = jnp.transpose(jnp.broadcast_to(jnp.exp(last), (DV, QK)))
        lane = lax.broadcasted_iota(I32, (c, LANES), 1)
        for h in range(HEADS):
            rs, ls, ks = slice(h * c, (h + 1) * c), slice(h * DV, (h + 1) * DV), slice(h * DK, (h + 1) * DK)
            pair = q_in[:, (h // 2) * LANES:(h // 2 + 1) * LANES]
            if h % 2 == 0:
                lhs = jnp.where(lane < DK, pair, sc_ref[b, rs, :])
                rhs = jnp.concatenate([state_bf[ks, :], v[:, ls]], axis=0)
            else:
                lhs = jnp.where(lane >= DK, pair, sc_ref[b, rs, :])
                rhs = jnp.concatenate([v[:, ls], state_bf[ks, :]], axis=0)
            o_ref[0, b, :, ls] = _dot(lhs, rhs).astype(BF16)
            st_ref[0, b, ks, :] = decay[ks, :] * state[ks, :] + _dot(k_out[ks, :], v[:, ls])


def _chunk_scan(qk, v, g, *, nb, length):
    c = CHUNK
    blk = lambda w: pl.BlockSpec((1, nb, c, w), lambda br, i: (br, 0, i, 0))
    return pl.pallas_call(
        functools.partial(_chunk_kernel, nb=nb, c=c),
        grid=(2, length // c),
        in_specs=[blk(2 * QK), blk(VW), blk(QK)],
        out_specs=[blk(VW), pl.BlockSpec((1, nb, QK, DV), lambda br, i: (br, 0, 0, 0))],
        out_shape=[jax.ShapeDtypeStruct((2, nb, length, VW), BF16),
                   jax.ShapeDtypeStruct((2, nb, QK, DV), F32)],
        scratch_shapes=[pltpu.VMEM((nb, c, QK), F32), pltpu.VMEM((nb, HEADS * c, 2 * c), BF16)],
        compiler_params=_cparams("arbitrary", "arbitrary"),
        name="chunk_scan",
    )(qk, v, g)


def _decode_kernel(qk_ref, v_ref, g_ref, s0_ref, s1_ref, *rest, nt):
    o_ref, n0_ref, n1_ref = rest[-3:]
    for br, (s_ref, n_ref) in enumerate(((s0_ref, n0_ref), (s1_ref, n1_ref))):
        for j in range(nt):
            d = jnp.exp(g_ref[br, j:j + 1, :])
            q = qk_ref[br, j:j + 1, 0:QK].astype(F32)
            k = qk_ref[br, j:j + 1, QK:2 * QK].astype(F32)
            cols = jnp.transpose(jnp.concatenate([d, k, q, jnp.zeros((5, QK), F32)], axis=0))
            vrow = v_ref[br, j:j + 1, :].astype(F32)
            vfull = jnp.concatenate([jnp.broadcast_to(vrow[:, h * DV:(h + 1) * DV], (DK, DV)) for h in range(HEADS)], axis=0)
            new = cols[:, 0:1] * s_ref[j] + cols[:, 1:2] * vfull
            n_ref[j] = new
            t = cols[:, 2:3] * new
            for h in range(HEADS):
                o_ref[br, j:j + 1, h * DV:(h + 1) * DV] = jnp.sum(t[h * DK:(h + 1) * DK, :], axis=0, keepdims=True).astype(BF16)


def _decode(qk, v, g, s_gla, s_hg, prev, *, layer):
    n = qk.shape[1]
    nt = 8
    row = lambda w: pl.BlockSpec((2, nt, w), lambda i: (0, i, 0))
    st = pl.BlockSpec((None, nt, QK, DV), lambda i: (layer, i, 0, 0))
    carried = [] if prev is None else list(prev)
    first = 5
    return pl.pallas_call(
        functools.partial(_decode_kernel, nt=nt),
        grid=(n // nt,),
        in_specs=[row(2 * QK), row(VW), row(QK), st, st] + [pl.BlockSpec(memory_space=pl.ANY)] * len(carried),
        out_specs=[row(VW), st, st],
        out_shape=[jax.ShapeDtypeStruct((2, n, VW), BF16),
                   jax.ShapeDtypeStruct((DEPTH, n, QK, DV), F32),
                   jax.ShapeDtypeStruct((DEPTH, n, QK, DV), F32)],
        input_output_aliases={first + j: 1 + j for j in range(len(carried))},
        compiler_params=_cparams("parallel"),
        name="decode_step",
    )(qk, v, g, s_gla, s_hg, *carried)


def _s5_disc_kernel(lr_ref, li_ref, ldt_ref, br_ref, bi_ref, abr_ref, abi_ref, bbr_ref, bbi_ref):
    lr, li, dt = lr_ref[...], li_ref[...], jnp.exp(ldt_ref[...])
    mag = jnp.exp(lr * dt)
    ab_re, ab_im = mag * jnp.cos(li * dt), mag * jnp.sin(li * dt)
    den = lr * lr + li * li
    num_re = ab_re - 1.0
    coef_re = (num_re * lr + ab_im * li) / den
    coef_im = (ab_im * lr - num_re * li) / den
    br, bi = br_ref[...], bi_ref[...]
    abr_ref[...] = ab_re
    abi_ref[...] = ab_im
    bbr_ref[...] = coef_re * br - coef_im * bi
    bbi_ref[...] = coef_re * bi + coef_im * br


def _s5_discretise(a_re, a_im, log_dt, b_re, b_im):
    n = DEPTH * S5_GROUPS
    rep = lambda t: jnp.repeat(t.reshape(n, S5_STATE), S5_GROUP, axis=1)
    ldt = jnp.broadcast_to(log_dt.reshape(n, 1), (n, S5_STATE * S5_GROUP))
    shape = jax.ShapeDtypeStruct((n, S5_STATE * S5_GROUP), F32)
    ab_re, ab_im, bb_re, bb_im = pl.pallas_call(_s5_disc_kernel, out_shape=[shape] * 4, name="s5_discretise")(
        rep(a_re), rep(a_im), ldt, b_re.reshape(n, -1), b_im.reshape(n, -1))
    pole = lambda t: t[:, ::S5_GROUP].reshape(DEPTH, 1, S5_LANES)
    eye = jnp.eye(8, dtype=F32)

    def blockdiag_in(bb):
        t = bb.reshape(DEPTH, S5_SLABS, 8, S5_STATE, S5_GROUP).transpose(0, 1, 2, 4, 3)
        return jnp.einsum("dcgmp,gh->dcgmhp", t, eye).reshape(DEPTH, S5_SLABS, LANES, 8 * S5_STATE)

    w_in = jnp.concatenate([blockdiag_in(bb_re), blockdiag_in(bb_im)], axis=-1).astype(BF16)
    return w_in, pole(ab_re), pole(ab_im)


def _s5_blockdiag_out(c):
    t = c.reshape(DEPTH, S5_SLABS, 8, S5_GROUP, S5_STATE).transpose(0, 1, 2, 4, 3)
    return jnp.einsum("dcgpm,gh->dcgphm", t, jnp.eye(8, dtype=F32)).reshape(DEPTH, S5_SLABS, 8 * S5_STATE, LANES).astype(BF16)


def _time_major_perm(nb, ct):
    r = np.arange(nb * ct)
    p = np.zeros((nb * ct, nb * ct), np.float32)
    p[r, (r % nb) * ct + r // nb] = 1.0
    return p


def _gelu_tanh(y):
    return 0.5 * y * (1.0 + jnp.tanh(0.7978845608028654 * (y + 0.044715 * (y * y * y))))


def _s5_input(ub, win_ref, xre_ref, xim_ref):
    half = 8 * S5_STATE
    for s in range(S5_SLABS):
        r = _dot(ub[:, s * LANES:(s + 1) * LANES], win_ref[s])
        xre_ref[:, s * half:(s + 1) * half] = r[:, :half]
        xim_ref[:, s * half:(s + 1) * half] = r[:, half:]


def _s5_readout(xre_ref, xim_ref, cre_ref, cim_ref):
    half = 8 * S5_STATE
    ys = []
    for s in range(S5_SLABS):
        ls = slice(s * half, (s + 1) * half)
        ys.append(_dot(xre_ref[:, ls].astype(BF16), cre_ref[s]) - _dot(xim_ref[:, ls].astype(BF16), cim_ref[s]))
    return jnp.concatenate(ys, axis=-1)


def _s5_glu(y, u, d_ref, wglu_ref, bglu_ref):
    z = _gelu_tanh(y + d_ref[...] * u)
    return z * _sigmoid(_dot(z.astype(BF16), wglu_ref[...]) + bglu_ref[...])


def _s5_scan_kernel(su_ref, perm_ref, permt_ref, win_ref, are_ref, aim_ref, cre_ref, cim_ref, d_ref, wglu_ref, bglu_ref,
                    o_ref, hre_ref, him_ref, xre_ref, xim_ref, *, nb, ct):
    @pl.when(pl.program_id(0) == 0)
    def _():
        hre_ref[...] = jnp.zeros_like(hre_ref)
        him_ref[...] = jnp.zeros_like(him_ref)

    u = su_ref[...].reshape(nb * ct, S5_WIDTH)
    ub = _dot(perm_ref[...], u.astype(BF16)).astype(BF16)

    half = 8 * S5_STATE
    ys = []
    for s in range(S5_SLABS):
        ls = slice(s * half, (s + 1) * half)
        r = _dot(ub[:, s * LANES:(s + 1) * LANES], win_ref[s])
        xre_ref[:, ls] = r[:, :half]
        xim_ref[:, ls] = r[:, half:]
        ar = jnp.broadcast_to(are_ref[:, ls], (nb, half))
        ai = jnp.broadcast_to(aim_ref[:, ls], (nb, half))
        hr, hi = hre_ref[:, ls], him_ref[:, ls]
        for t in range(ct):
            rows = slice(t * nb, (t + 1) * nb)
            hr, hi = ar * hr - ai * hi + xre_ref[rows, ls], ar * hi + ai * hr + xim_ref[rows, ls]
            xre_ref[rows, ls] = hr
            xim_ref[rows, ls] = hi
        hre_ref[:, ls] = hr
        him_ref[:, ls] = hi
        ys.append(_dot(xre_ref[:, ls].astype(BF16), cre_ref[s]) - _dot(xim_ref[:, ls].astype(BF16), cim_ref[s]))

    y = jnp.concatenate(ys, axis=-1)
    y_hi = y.astype(BF16)
    y_lo = (y - y_hi.astype(F32)).astype(BF16)
    y = _dot01(permt_ref[...], (y_hi, y_lo))
    o_ref[...] = _s5_glu(y, u, d_ref, wglu_ref, bglu_ref).reshape(nb, ct, S5_WIDTH).astype(BF16)


def _s5_weight_specs(layer):
    return [_layer_spec((S5_SLABS, LANES, 2 * 8 * S5_STATE), layer),
            _layer_spec((1, S5_LANES), layer), _layer_spec((1, S5_LANES), layer),
            _layer_spec((S5_SLABS, 8 * S5_STATE, LANES), layer), _layer_spec((S5_SLABS, 8 * S5_STATE, LANES), layer),
            _layer_spec((1, S5_WIDTH), layer), _layer_spec((S5_WIDTH, S5_WIDTH), layer),
            _layer_spec((1, S5_WIDTH), layer)]


def _s5_scan(su, weights, *, nb, length, layer):
    ct = S5_CHUNK
    perm = _time_major_perm(nb, ct)
    const = pl.BlockSpec((nb * ct, nb * ct), lambda i: (0, 0))
    return pl.pallas_call(
        functools.partial(_s5_scan_kernel, nb=nb, ct=ct),
        grid=(length // ct,),
        in_specs=[pl.BlockSpec((nb, ct, S5_WIDTH), lambda i: (0, i, 0)), const, const] + _s5_weight_specs(layer),
        out_specs=[pl.BlockSpec((nb, ct, S5_WIDTH), lambda i: (0, i, 0)),
                   pl.BlockSpec((nb, S5_LANES), lambda i: (0, 0)), pl.BlockSpec((nb, S5_LANES), lambda i: (0, 0))],
        out_shape=[jax.ShapeDtypeStruct((nb, length, S5_WIDTH), BF16),
                   jax.ShapeDtypeStruct((nb, S5_LANES), F32), jax.ShapeDtypeStruct((nb, S5_LANES), F32)],
        scratch_shapes=[pltpu.VMEM((nb * ct, S5_LANES), F32), pltpu.VMEM((nb * ct, S5_LANES), F32)],
        compiler_params=_cparams("arbitrary"),
        name="s5_scan",
    )(su, jnp.asarray(perm, BF16), jnp.asarray(perm.T, BF16), *weights)


def _s5_step_kernel(su_ref, h0r_ref, h0i_ref, win_ref, are_ref, aim_ref, cre_ref, cim_ref, d_ref, wglu_ref, bglu_ref,
                    o_ref, hre_ref, him_ref):
    u = su_ref[...]
    _s5_input(u.astype(BF16), win_ref, hre_ref, him_ref)
    ar, ai = are_ref[...], aim_ref[...]
    h0r, h0i = h0r_ref[...], h0i_ref[...]
    nr = ar * h0r - ai * h0i + hre_ref[...]
    ni = ar * h0i + ai * h0r + him_ref[...]
    hre_ref[...] = nr
    him_ref[...] = ni
    o_ref[...] = _s5_glu(_s5_readout(hre_ref, him_ref, cre_ref, cim_ref), u, d_ref, wglu_ref, bglu_ref).astype(BF16)


def _s5_step(su, h0r, h0i, weights, *, layer):
    n = su.shape[0]
    full = lambda w: pl.BlockSpec((n, w), lambda i: (0, 0))
    state = pl.BlockSpec((None, n, S5_LANES), lambda i: (layer, 0, 0))
    return pl.pallas_call(
        _s5_step_kernel,
        grid=(1,),
        in_specs=[full(S5_WIDTH), state, state] + _s5_weight_specs(layer),
        out_specs=[full(S5_WIDTH), full(S5_LANES), full(S5_LANES)],
        out_shape=[jax.ShapeDtypeStruct((n, S5_WIDTH), BF16),
                   jax.ShapeDtypeStruct((n, S5_LANES), F32), jax.ShapeDtypeStruct((n, S5_LANES), F32)],
        compiler_params=_cparams("arbitrary"),
        name="s5_step",
    )(su, h0r, h0i, *weights)


def _head_norm(o, g):
    parts = []
    for h in range(HEADS):
        seg = o[:, h * DV:(h + 1) * DV]
        parts.append(seg * lax.rsqrt(jnp.mean(seg * seg, axis=-1, keepdims=True) + EPS))
    return jnp.concatenate(parts, axis=-1) * g


def _route(logits, count):
    rows = logits.shape[0]
    lane = lax.broadcasted_iota(I32, logits.shape, 1)
    neg = -jnp.inf
    first = lambda hit: jnp.min(jnp.where(hit, lane, LANES), axis=-1, keepdims=True)
    glog = jnp.where(lane < MOE_GROUPS, logits, neg)
    gmax = jnp.max(glog, axis=-1, keepdims=True)
    gidx = first(glog == gmax)
    gw = 1.0 / jnp.sum(jnp.where(lane < MOE_GROUPS, jnp.exp(logits - gmax), 0.0), axis=-1, keepdims=True)
    inside = (lane >= MOE_GROUPS) & (lane < MOE_GROUPS + MOE_EXPERTS) & (((lane - MOE_GROUPS) >> 3) == gidx)
    el = jnp.where(inside, logits, neg)
    v1 = jnp.max(el, axis=-1, keepdims=True)
    i1 = first(el == v1)
    el2 = jnp.where(lane == i1, neg, el)
    v2 = jnp.max(el2, axis=-1, keepdims=True)
    i2 = first(el2 == v2)
    p2 = jnp.exp(v2 - v1)
    w1 = gw / (1.0 + p2)
    w2 = gw * p2 / (1.0 + p2)
    hit1, hit2 = lane == i1, lane == i2
    onehot = jnp.where(hit1 | hit2, 1.0, 0.0)
    blk = min(rows, 256)
    tr = lax.broadcasted_iota(I32, (blk, blk), 0)
    tc = lax.broadcasted_iota(I32, (blk, blk), 1)
    lower = jnp.where(tc < tr, 1.0, 0.0).astype(BF16)
    before = []
    for r0 in range(0, rows, blk):
        part = onehot[r0:r0 + blk, :]
        before.append(_dot(lower, part.astype(BF16)) + count)
        count = count + jnp.sum(part, axis=0, keepdims=True)
    before = jnp.concatenate(before, axis=0)
    rank1 = jnp.sum(jnp.where(hit1, before, 0.0), axis=-1, keepdims=True)
    rank2 = jnp.sum(jnp.where(hit2, before, 0.0), axis=-1, keepdims=True)
    slab = jnp.zeros(logits.shape, F32)
    for ln, val in ((R_E1, (i1 - MOE_GROUPS).astype(F32)), (R_E2, (i2 - MOE_GROUPS).astype(F32)), (R_W1, w1), (R_W2, w2),
                    (R_RANK1, rank1), (R_RANK2, rank2)):
        slab = jnp.where(lane == ln, val, slab)
    return slab, count


def _merge_kernel(x_ref, o_ref, gate_ref, os5_ref, cnt0_ref, nm_ref, wgm_ref, gn_ref, wbr_ref, wout_ref, nf_ref,
                  wr_ref, br_ref, *rest, sub):
    xm_ref, h2_ref, rt_ref, rtt_ref, cnt_ref = rest[-5:]

    @pl.when(pl.program_id(0) == 0)
    def _():
        cnt_ref[...] = cnt0_ref[...]

    logits = []
    for r0 in range(0, x_ref.shape[0], sub):
        rs = slice(r0, r0 + sub)
        x = x_ref[rs, :]
        hb = _rms(x, nm_ref[...]).astype(BF16)
        mixed = None
        for i, src in enumerate((0, None, 1)):
            if src is None:
                branch = os5_ref[rs, :]
            else:
                branch = (_head_norm(o_ref[src, rs, :].astype(F32), gn_ref[src]) * gate_ref[src, rs, :].astype(F32)).astype(BF16)
            gate = _sigmoid(_dot(hb, wgm_ref[:, i * D_MODEL:(i + 1) * D_MODEL]))
            term = gate * _dot(branch, wbr_ref[i])
            mixed = term if mixed is None else mixed + term
        xm = x + _dot(mixed.astype(BF16), wout_ref[...])
        xm_ref[rs, :] = xm
        h2 = _rms(xm, nf_ref[...])
        h2_hi = h2.astype(BF16)
        h2_lo = (h2 - h2_hi.astype(F32)).astype(BF16)
        h2_ref[rs, :] = h2_hi
        both = _dot(h2_hi, wr_ref[...])
        logits.append(both[:, :LANES] + (both[:, LANES:] + _dot(h2_lo, wr_ref[:, :LANES])) + br_ref[...])
    slab, cnt_ref[...] = _route(jnp.concatenate(logits, axis=0), cnt_ref[...])
    rt_ref[...] = slab
    rtt_ref[...] = jnp.transpose(slab)[0:8, :]


def _merge(x, o, gate, os5, cnt0, weights, carried, *, layer, tm, total_rows, block_offset):
    rows = x.shape[0]
    tile = lambda w: pl.BlockSpec((tm, w), lambda i: (i, 0))
    shared = lambda w: pl.BlockSpec((tm, w), lambda i: (i + block_offset, 0))
    carried = [] if carried is None else list(carried)
    w_specs = [_layer_spec((1, D_MODEL), layer), _layer_spec((D_MODEL, 3 * D_MODEL), layer),
               _layer_spec((2, 1, VW), layer), _layer_spec((3, VW, D_MODEL), layer),
               _layer_spec((D_MODEL, D_MODEL), layer), _layer_spec((1, D_MODEL), layer),
               _layer_spec((D_MODEL, 2 * LANES), layer), _layer_spec((1, LANES), layer)]
    first = 5 + len(w_specs)
    return pl.pallas_call(
        functools.partial(_merge_kernel, sub=min(tm, 256)),
        grid=(rows // tm,),
        in_specs=[tile(D_MODEL),
                  pl.BlockSpec((2, tm, VW), lambda i: (0, i, 0)),
                  pl.BlockSpec((2, tm, VW), lambda i: (0, i, 0)),
                  tile(S5_WIDTH),
                  pl.BlockSpec((1, LANES), lambda i: (0, 0))] + w_specs + [pl.BlockSpec(memory_space=pl.ANY)] * len(carried),
        out_specs=[tile(D_MODEL), shared(D_MODEL), shared(LANES),
                   pl.BlockSpec((8, tm), lambda i: (0, i + block_offset)), pl.BlockSpec((1, LANES), lambda i: (0, 0))],
        out_shape=[jax.ShapeDtypeStruct((rows, D_MODEL), F32),
                   jax.ShapeDtypeStruct((total_rows, D_MODEL), BF16),
                   jax.ShapeDtypeStruct((total_rows, LANES), F32),
                   jax.ShapeDtypeStruct((8, total_rows), F32),
                   jax.ShapeDtypeStruct((1, LANES), F32)],
        input_output_aliases={first + j: 1 + j for j in range(len(carried))},
        compiler_params=_cparams("arbitrary"),
        name="merge_route",
    )(x, o, gate, os5, cnt0, *weights, *carried)


def _ffn_kernel(te_ref, nu_ref, x_ref, wg_ref, wu_ref, wd_ref, *rest, tile0):
    y_ref, wg_bf, wu_bf, wd_bf = rest[-4:]
    i = pl.program_id(0)
    tile = i + tile0

    @pl.when(tile < nu_ref[0])
    def _():
        @pl.when((i == 0) | (te_ref[tile] != te_ref[jnp.maximum(tile - 1, 0)]))
        def _():
            wg_bf[...] = wg_ref[...].astype(BF16)
            wu_bf[...] = wu_ref[...].astype(BF16)
            wd_bf[...] = wd_ref[...].astype(BF16)

        x = x_ref[...]
        act = _silu(_dot(x, wg_bf[...])) * _dot(x, wu_bf[...])
        y_ref[...] = _dot(act.astype(BF16), wd_bf[...]).astype(BF16)


def _grouped_ffn(tile_expert, n_used, xg, y_prev, wg, wu, wd, *, layer, tile0, total_rows):
    used = lambda i, nu: jnp.maximum(jnp.minimum(i + tile0, nu[0] - 1), tile0)
    expert = lambda a, b: pl.BlockSpec((None, None, a, b), lambda i, te, nu: (layer, te[used(i, nu)], 0, 0))
    carried = [] if y_prev is None else [y_prev]
    grid_spec = pltpu.PrefetchScalarGridSpec(
        num_scalar_prefetch=2,
        grid=(xg.shape[0] // MOE_TILE,),
        in_specs=[pl.BlockSpec((MOE_TILE, D_MODEL), lambda i, te, nu: (used(i, nu) - tile0, 0)),
                  expert(D_MODEL, MOE_FF), expert(D_MODEL, MOE_FF), expert(MOE_FF, D_MODEL)]
        + [pl.BlockSpec(memory_space=pl.ANY)] * len(carried),
        out_specs=pl.BlockSpec((MOE_TILE, D_MODEL), lambda i, te, nu: (used(i, nu), 0)),
        scratch_shapes=[pltpu.VMEM((D_MODEL, MOE_FF), BF16), pltpu.VMEM((D_MODEL, MOE_FF), BF16),
                        pltpu.VMEM((MOE_FF, D_MODEL), BF16)],
    )
    return pl.pallas_call(
        functools.partial(_ffn_kernel, tile0=tile0),
        grid_spec=grid_spec,
        out_shape=jax.ShapeDtypeStruct((total_rows, D_MODEL), BF16),
        input_output_aliases={6: 0} if carried else {},
        compiler_params=_cparams("arbitrary"),
        name="expert_ffn",
    )(tile_expert, n_used, xg, wg, wu, wd, *carried)


@compute_on("tpu_sparsecore")
@jax.jit
def _take_rows(x, idx):
    return jnp.take(x, idx, axis=0, mode="clip")


def _moe_plan(route_t, count):
    tokens = route_t.shape[1]
    n_tiles = -(-(2 * tokens + MOE_EXPERTS * (MOE_TILE - 1)) // MOE_TILE)
    n_rows = n_tiles * MOE_TILE
    counts = count[0, MOE_GROUPS:MOE_GROUPS + MOE_EXPERTS].astype(I32)
    padded = ((counts + MOE_TILE - 1) // MOE_TILE) * MOE_TILE
    gend = jnp.cumsum(padded)
    gstart = gend - padded
    experts = jnp.arange(MOE_EXPERTS, dtype=I32)[:, None]

    def rows_of(e_lane, rank_lane):
        e = route_t[e_lane].astype(I32)
        return jnp.sum(jnp.where(e[None, :] == experts, gstart[:, None], 0), axis=0) + route_t[rank_lane].astype(I32)

    pos1, pos2 = rows_of(R_E1, R_RANK1), rows_of(R_E2, R_RANK2)
    token = jnp.arange(tokens, dtype=I32) + 1
    marked = jnp.zeros((n_rows,), I32).at[jnp.concatenate([pos1, pos2])].add(jnp.concatenate([token, token]),
                                                                             unique_indices=True)
    filler = jnp.arange(n_rows, dtype=I32) % tokens
    src = jnp.where(marked > 0, marked - 1, filler)
    tile_start = jnp.arange(n_tiles, dtype=I32) * MOE_TILE
    tile_expert = jnp.minimum(jnp.sum((tile_start[:, None] >= gend[None, :]).astype(I32), axis=1), MOE_EXPERTS - 1)
    n_used = (gend[-1] // MOE_TILE).astype(I32).reshape(1)
    return src, pos1, pos2, tile_expert, n_used


def _ple_kernel(xm_ref, y1_ref, y2_ref, rt_ref, p_ref, np_ref, wg_ref, wp_ref, nfin_ref, *rest, final):
    out_ref = rest[-1]
    rt = rt_ref[...]
    x1 = xm_ref[...] + rt[:, R_W1:R_W1 + 1] * y1_ref[...].astype(F32) + rt[:, R_W2:R_W2 + 1] * y2_ref[...].astype(F32)
    gate = _sigmoid(_dot(_rms(x1, np_ref[...]).astype(BF16), wg_ref[...]))
    x2 = x1 + gate * _dot(p_ref[...].astype(BF16), wp_ref[...])
    out_ref[...] = _rms(x2, nfin_ref[...]) if final else x2


def _ple(xm, y1, y2, route, p, x_prev, npl, wg, wp, nfin, *, layer, final, tm, n_blocks, x_off, y_off, rt_off):
    at = lambda w, off: pl.BlockSpec((tm, w), lambda i: (i + off, 0))
    carried = [] if x_prev is None else [x_prev]
    return pl.pallas_call(
        functools.partial(_ple_kernel, final=final),
        grid=(n_blocks,),
        in_specs=[at(D_MODEL, x_off), at(D_MODEL, y_off), at(D_MODEL, y_off), at(LANES, rt_off),
                  pl.BlockSpec((None, tm, PLE_DIM), lambda i: (layer, i + x_off, 0)),
                  _layer_spec((1, D_MODEL), layer), _layer_spec((D_MODEL, D_MODEL), layer),
                  _layer_spec((PLE_DIM, D_MODEL), layer), pl.BlockSpec((1, D_MODEL), lambda i: (0, 0))]
        + [pl.BlockSpec(memory_space=pl.ANY)] * len(carried),
        out_specs=at(D_MODEL, x_off),
        out_shape=jax.ShapeDtypeStruct(xm.shape, F32),
        input_output_aliases={9: 0} if carried else {},
        compiler_params=_cparams("parallel"),
        name="combine_ple",
    )(xm, y1, y2, route, p, npl, wg, wp, nfin, *carried)


def kernel(x_prompt, x_sample, state_gla, state_s5_re, state_s5_im, state_hgrn, p_prompt, p_sample, norm_mix, w_in, gla_w_gate2, gla_b_gate, gla_norm, s5_a_re, s5_a_im, s5_log_dt, s5_b_re, s5_b_im, s5_c_re, s5_c_im, s5_d, s5_w_glu, s5_b_glu, hgrn_lb_logits, hgrn_norm, w_br_gla, w_br_s5, w_br_hgrn, w_out, norm_ffn, moe_w_group, moe_b_group, moe_w_expert, moe_b_expert, moe_w_gate, moe_w_up, moe_w_down, norm_ple, w_ple_gate, w_ple_proj, norm_final):
    nb, length, _ = x_prompt.shape
    ns = x_sample.shape[0]
    n_p = nb * length
    n_all = n_p + ns
    row = lambda t: t.reshape(DEPTH, 1, -1)

    w_a, w_gm = _w_in_layout(w_in)
    wg2 =jnp.concatenate([gla_w_gate2, jnp.zeros((DEPTH, LANES - GLA_RANK, QK), F32)], axis=1).astype(BF16)
    s5_win, s5_are, s5_aim = _s5_discretise(s5_a_re, s5_a_im, s5_log_dt, s5_b_re, s5_b_im)
    s5_w = (s5_win, s5_are, s5_aim, _s5_blockdiag_out(s5_c_re), _s5_blockdiag_out(s5_c_im), row(s5_d),
            s5_w_glu.astype(BF16), row(s5_b_glu))
    head_gain = jnp.stack([jnp.tile(gla_norm, (1, HEADS)), jnp.tile(hgrn_norm, (1, HEADS))], axis=1).reshape(DEPTH, 2, 1, VW)
    w_router = jnp.concatenate([moe_w_group, moe_w_expert,
                                jnp.zeros((DEPTH, D_MODEL, LANES - MOE_GROUPS - MOE_EXPERTS), F32)], axis=-1)
    wr_hi = w_router.astype(BF16)
    wr_lo = (w_router - wr_hi.astype(F32)).astype(BF16)
    wr_hilo = jnp.concatenate([wr_hi, wr_lo], axis=-1)
    b_router = jnp.concatenate([moe_b_group, moe_b_expert,
                                jnp.zeros((DEPTH, LANES - MOE_GROUPS - MOE_EXPERTS), F32)], axis=-1).reshape(DEPTH, 1, LANES)
    merge_w = (row(norm_mix), w_gm, head_gain,
               jnp.stack([w_br_gla, w_br_s5, w_br_hgrn], axis=1).astype(BF16), w_out.astype(BF16), row(norm_ffn),
               wr_hilo, b_router)
    ple_w = (row(norm_ple), w_ple_gate.astype(BF16), w_ple_proj.astype(BF16), norm_final.reshape(1, D_MODEL))
    nm, bg = row(norm_mix), row(gla_b_gate)
    p_p, p_s = p_prompt.reshape(DEPTH, n_p, PLE_DIM), p_sample.reshape(DEPTH, ns, PLE_DIM)
    sg_in, sh_in = state_gla.reshape(DEPTH, ns, QK, DV), state_hgrn.reshape(DEPTH, ns, QK, DV)
    s5r_in, s5i_in = state_s5_re.reshape(DEPTH, ns, S5_LANES), state_s5_im.reshape(DEPTH, ns, S5_LANES)

    xp = x_prompt.reshape(n_p, D_MODEL)
    xs = x_sample.reshape(ns, D_MODEL)
    new_p, new_s5 = [], []
    new_s = None
    for i in range(DEPTH):
        qk, v, gate, g, su = _inproj(xp, nm, w_a, wg2, bg, hgrn_lb_logits, layer=i, tm=1024)
        o_p, st_p = _chunk_scan(qk.reshape(2, nb, length, -1), v.reshape(2, nb, length, -1),
                                g.reshape(2, nb, length, -1), nb=nb, length=length)
        os5_p, hre_p, him_p = _s5_scan(su.reshape(nb, length, -1), s5_w, nb=nb, length=length, layer=i)
        xm_p, h2, rt, rtt, cnt = _merge(xp, o_p.reshape(2, n_p, VW), gate, os5_p.reshape(n_p, -1),
                                        jnp.zeros((1, LANES), F32), merge_w, None, layer=i, tm=1024, total_rows=n_all,
                                        block_offset=0)
        new_p.append((st_p[0].reshape(nb, HEADS, DK, DV), hre_p.reshape(nb, S5_GROUPS, S5_STATE),
                      him_p.reshape(nb, S5_GROUPS, S5_STATE), st_p[1].reshape(nb, HEADS, DK, DV)))

        qk, v, gate, g, su = _inproj(xs, nm, w_a, wg2, bg, hgrn_lb_logits, layer=i, tm=ns)
        o_s, *new_s = _decode(qk, v, g, sg_in, sh_in, new_s, layer=i)
        os5_s, hre_s, him_s = _s5_step(su, s5r_in, s5i_in, s5_w, layer=i)
        xm_s, h2, rt, rtt, cnt = _merge(xs, o_s, gate, os5_s, cnt, merge_w, (h2, rt, rtt), layer=i, tm=ns,
                                        total_rows=n_all, block_offset=n_p // ns)
        new_s5.append((hre_s.reshape(ns, S5_GROUPS, S5_STATE), him_s.reshape(ns, S5_GROUPS, S5_STATE)))

        src, pos1, pos2, tile_expert, n_used = _moe_plan(rtt, cnt)
        n_tiles = tile_expert.shape[0]
        y = None
        for k in range(PIPE):
            t0, t1 = k * n_tiles // PIPE, (k + 1) * n_tiles // PIPE
            y = _grouped_ffn(tile_expert, n_used, _take_rows(h2, src[t0 * MOE_TILE:t1 * MOE_TILE]), y,
                             moe_w_gate, moe_w_up, moe_w_down, layer=i, tile0=t0, total_rows=n_tiles * MOE_TILE)

        final = i == DEPTH - 1
        tm = 512
        blocks = n_p // tm
        pieces = min(PIPE, blocks)
        x_new = None
        for k in range(pieces):
            b0, b1 = k * blocks // pieces, (k + 1) * blocks // pieces
            r0, r1 = b0 * tm, (n_all if k == pieces - 1 else b1 * tm)
            y1, y2 = _take_rows(y, pos1[r0:r1]), _take_rows(y, pos2[r0:r1])
            x_new = _ple(xm_p, y1, y2, rt, p_p, x_new, *ple_w, layer=i, final=final, tm=tm, n_blocks=b1 - b0,
                         x_off=b0, y_off=0, rt_off=b0)
        xp = x_new
        xs = _ple(xm_s, y1, y2, rt, p_s, None, *ple_w, layer=i, final=final, tm=ns, n_blocks=1,
                  x_off=0, y_off=(n_p - r0) // ns, rt_off=n_p // ns)

    stack = lambda items, j: jnp.stack([it[j] for it in items])
    return (xp.reshape(nb, length, D_MODEL), xs.reshape(ns, 1, D_MODEL),
            stack(new_p, 0), stack(new_p, 1), stack(new_p, 2), stack(new_p, 3),
            new_s[0].reshape(DEPTH, ns, HEADS, DK, DV), stack(new_s5, 0), stack(new_s5, 1),
            new_s[1].reshape(DEPTH, ns, HEADS, DK, DV))
```

```python
import functools

import jax
import jax.numpy as jnp
import numpy as np
from jax import lax
from jax.experimental import pallas as pl
from jax.experimental.compute_on import compute_on
from jax.experimental.pallas import tpu as pltpu

F32, BF16, I32 = jnp.float32, jnp.bfloat16, jnp.int32

D_MODEL = 1024
DEPTH = 2
HEADS, DK, DV = 4, 64, 128
QK, VW = HEADS * DK, HEADS * DV
GLA_RANK, GLA_TAU = 16, 16.0
S5_WIDTH, S5_GROUP, S5_GROUPS, S5_STATE = 512, 16, 32, 64
S5_LANES = S5_GROUPS * S5_STATE
S5_SLABS = 4
MOE_GROUPS, MOE_PER_GROUP, MOE_EXPERTS, MOE_FF = 4, 8, 32, 256
PLE_DIM = 256
EPS = 1e-6

LANES = 128
CHUNK = 64
S5_CHUNK = 64
MOE_TILE = 512
PIPE_CUTS = (1.0 / 3.0,)
SAFE_EXP = 80.0
VMEM_LIMIT = 56 * 1024 * 1024

C_GLA, C_S5, C_HG, C_GLR, C_END = 0, 1536, 2048, 3584, 3712
R_E1, R_E2, R_W1, R_W2, R_RANK1, R_RANK2 = 0, 1, 2, 3, 4, 5


def _cparams(*sem):
    return pltpu.CompilerParams(dimension_semantics=sem, vmem_limit_bytes=VMEM_LIMIT)


def _layer_spec(shape, layer):
    return pl.BlockSpec((None,) + tuple(shape), lambda *_: (layer,) + (0,) * len(shape), pipeline_mode=pl.Buffered(1))


def _dot(a, b):
    return jnp.dot(a, b, preferred_element_type=F32)


def _rms(x, g):
    return x * lax.rsqrt(jnp.mean(x * x, axis=-1, keepdims=True) + EPS) * g


def _log_sigmoid(x):
    return jnp.minimum(x, 0.0) - jnp.log1p(jnp.exp(-jnp.abs(x)))


def _sigmoid(x):
    return 0.5 * jnp.tanh(0.5 * x) + 0.5


def _silu(x):
    return x * _sigmoid(x)


def _split3(x):
    hi = x.astype(BF16)
    r1 = x - hi.astype(F32)
    mid = r1.astype(BF16)
    lo = (r1 - mid.astype(F32)).astype(BF16)
    return hi, mid, lo


def _dot01(m, parts):
    out = _dot(m, parts[0])
    for p in parts[1:]:
        out = out + _dot(m, p)
    return out


W_GLA, W_GLR, W_S5HG, W_END = 1536, 1552, 3600, 6672


def _w_in_layout_kernel(wt_ref, wa_ref, wgm_ref):
    cols = lambda lo, hi: jnp.transpose(wt_ref[lo:hi, :])
    wa_ref[:, C_GLA:C_S5] = cols(0, W_GLA).astype(BF16)
    wa_ref[:, C_S5:C_GLR] = cols(W_GLR, W_S5HG).astype(BF16)
    lane = lax.broadcasted_iota(I32, (LANES, LANES), 1)
    wa_ref[:, C_GLR:C_END] = jnp.where(lane < GLA_RANK, cols(W_GLA, W_GLA + LANES), 0.0).astype(BF16)
    wgm_ref[...] = cols(W_S5HG, W_END).astype(BF16)


def _w_in_layout(w_in):
    wt = jnp.swapaxes(w_in, 1, 2)
    return pl.pallas_call(
        _w_in_layout_kernel,
        grid=(DEPTH, D_MODEL // LANES),
        in_specs=[pl.BlockSpec((None, W_END, LANES), lambda d, i: (d, 0, i))],
        out_specs=[pl.BlockSpec((None, LANES, C_END), lambda d, i: (d, i, 0)),
                   pl.BlockSpec((None, LANES, 3 * D_MODEL), lambda d, i: (d, i, 0))],
        out_shape=[jax.ShapeDtypeStruct((DEPTH, D_MODEL, C_END), BF16),
                   jax.ShapeDtypeStruct((DEPTH, D_MODEL, 3 * D_MODEL), BF16)],
        compiler_params=_cparams("parallel", "parallel"),
        name="w_in_layout",
    )(wt)
def _inproj_kernel(x_ref, nm_ref, w_ref, wg2_ref, bg_ref, lbl_ref, qk_ref, v_ref, gate_ref, g_ref, su_ref, *, layer, sub):
    lg = lbl_ref[...]
    mx = jnp.max(lg, axis=0, keepdims=True)
    ex = jnp.exp(lg - mx)
    sm = ex / jnp.sum(ex, axis=0, keepdims=True)
    cs = sm[0:1]
    for j in range(1, layer + 1):
        cs = cs + sm[j:j + 1]
    lb = cs - sm[0:1]
    log_lb, log_1mlb = jnp.log(lb), jnp.log1p(-lb)

    for r0 in range(0, x_ref.shape[0], sub):
        rs = slice(r0, r0 + sub)
        hb = _rms(x_ref[rs, :], nm_ref[...]).astype(BF16)

        def proj(lo, hi):
            return _dot(hb, w_ref[:, lo:hi])

        qk_ref[0, rs, 0:QK] = (proj(0, 256) * (DK ** -0.5)).astype(BF16)
        qk_ref[0, rs, QK:2 * QK] = proj(256, 512).astype(BF16)
        v_ref[0, rs, :] = proj(512, 1024).astype(BF16)
        gate_ref[0, rs, :] = _silu(proj(1024, 1536)).astype(BF16)
        glr = proj(C_GLR, C_END).astype(BF16)
        g_ref[0, rs, :] = _log_sigmoid(_dot(glr, wg2_ref[...]) + bg_ref[...]) * (1.0 / GLA_TAU)

        su_ref[rs, :] = proj(C_S5, C_HG)

        z = proj(C_HG + 256, C_HG + 512)
        a, c = log_lb, log_1mlb + _log_sigmoid(z)
        g_ref[1, rs, :] = jnp.maximum(a, c) + jnp.log1p(jnp.exp(-jnp.abs(a - c)))
        qk_ref[1, rs, 0:QK] = _silu(proj(C_HG, C_HG + 256)).astype(BF16)
        qk_ref[1, rs, QK:2 * QK] = ((1.0 - lb) * _sigmoid(-z)).astype(BF16)
        v_ref[1, rs, :] = proj(C_HG + 512, C_HG + 1024).astype(BF16)
        gate_ref[1, rs, :] = _silu(proj(C_HG + 1024, C_HG + 1536)).astype(BF16)


def _inproj(x, nm, w, wg2, bg, lbl, *, layer, tm):
    rows = x.shape[0]
    return pl.pallas_call(
        functools.partial(_inproj_kernel, layer=layer, sub=min(tm, 256)),
        grid=(rows // tm,),
        in_specs=[pl.BlockSpec((tm, D_MODEL), lambda i: (i, 0)),
                  _layer_spec((1, D_MODEL), layer),
                  _layer_spec((D_MODEL, C_END), layer),
                  _layer_spec((LANES, QK), layer),
                  _layer_spec((1, QK), layer),
                  pl.BlockSpec((DEPTH, QK), lambda i: (0, 0))],
        out_specs=[pl.BlockSpec((2, tm, 2 * QK), lambda i: (0, i, 0)),
                   pl.BlockSpec((2, tm, VW), lambda i: (0, i, 0)),
                   pl.BlockSpec((2, tm, VW), lambda i: (0, i, 0)),
                   pl.BlockSpec((2, tm, QK), lambda i: (0, i, 0)),
                   pl.BlockSpec((tm, S5_WIDTH), lambda i: (i, 0))],
        out_shape=[jax.ShapeDtypeStruct((2, rows, 2 * QK), BF16),
                   jax.ShapeDtypeStruct((2, rows, VW), BF16),
                   jax.ShapeDtypeStruct((2, rows, VW), BF16),
                   jax.ShapeDtypeStruct((2, rows, QK), F32),
                   jax.ShapeDtypeStruct((rows, S5_WIDTH), F32)],
        compiler_params=_cparams("parallel"),
        name="inproj",
    )(x, nm, w, wg2, bg, lbl)


def _head_stack(x):
    head = lax.broadcasted_iota(I32, x.shape, 1) // DK
    return jnp.concatenate([jnp.where(head == h, x, 0.0) for h in range(HEADS)], axis=0).astype(BF16)


def _stack_scores(qt, kt):
    kt = kt.astype(BF16)
    return lax.dot_general(_head_stack(qt), jnp.concatenate([kt, kt], axis=0), (((1,), (1,)), ((), ())),
                           preferred_element_type=F32)


def _chunk_kernel(qk_ref, v_ref, g_ref, o_ref, st_ref, bc_ref, sc_ref, *, nb, c):
    assert c == DK and 2 * c == LANES

    @pl.when(pl.program_id(1) == 0)
    def _():
        st_ref[...] = jnp.zeros_like(st_ref)

    row = lax.broadcasted_iota(I32, (c, c), 0)
    col = lax.broadcasted_iota(I32, (c, c), 1)
    tri = jnp.where(col <= row, 1.0, 0.0).astype(BF16)
    srow = lax.broadcasted_iota(I32, (HEADS * c, 2 * c), 0) & (c - 1)
    scol = lax.broadcasted_iota(I32, (HEADS * c, 2 * c), 1) & (c - 1)
    mid = c // 2 - 1

    spread = None
    for b in range(nb):
        bc = _dot01(tri, _split3(g_ref[0, b]))
        bc_ref[b] = bc
        ref, last = bc[mid:mid + 1, :], bc[c - 1:c, :]
        s = jnp.maximum(jnp.max(-ref), jnp.max(ref - last))
        spread = s if spread is None else jnp.maximum(spread, s)

    def qk_of(b):
        return qk_ref[0, b, :, 0:QK].astype(F32), qk_ref[0, b, :, QK:2 * QK].astype(F32)

    def scores_one_reference():
        for b in range(nb):
            q, k = qk_of(b)
            bc = bc_ref[b]
            ref = bc[mid:mid + 1, :]
            s = _stack_scores(q * jnp.exp(bc - ref), k * jnp.exp(ref - bc))
            sc_ref[b] = jnp.where(scol <= srow, s, 0.0).astype(BF16)

    def scores_by_levels():
        qrow = lax.broadcasted_iota(I32, (c, QK), 0)
        for b in range(nb):
            q, k = qk_of(b)
            bc = bc_ref[b]
            parts = _split3(g_ref[0, b])
            acc = jnp.where(scol == srow, _stack_scores(q, k), 0.0)
            half = c // 2
            while half >= 1:
                blk = 2 * half
                last_low = (row & ~(blk - 1)) + (half - 1)
                ref = _dot01(jnp.where(col <= last_low, 1.0, 0.0).astype(BF16), parts)
                upper = (qrow & (blk - 1)) >= half
                dq = jnp.minimum(jnp.where(upper, bc - ref, 0.0), 0.0)
                dk = jnp.minimum(jnp.where(upper, 0.0, ref - bc), 0.0)
                s = _stack_scores(q * jnp.exp(dq), k * jnp.exp(dk))
                pair = ((srow & ~(blk - 1)) == (scol & ~(blk - 1))) & ((srow & (blk - 1)) >= half) & ((scol & (blk - 1)) < half)
                acc = acc + jnp.where(pair, s, 0.0)
                half //= 2
            sc_ref[b] = acc.astype(BF16)

    lax.cond(spread <= SAFE_EXP, scores_one_reference, scores_by_levels)

    for b in range(nb):
        q, k = qk_of(b)
        bc = bc_ref[b]
        last = bc[c - 1:c, :]
        v = v_ref[0, b]
        state = st_ref[0, b]
        state_bf = state.astype(BF16)
        q_in = (q * jnp.exp(bc)).astype(BF16)
        k_out = jnp.transpose(k * jnp.exp(last - bc)).astype(BF16)
        decay = jnp.transpose(jnp.broadcast_to(jnp.exp(last), (DV, QK)))
        lane = lax.broadcasted_iota(I32, (c, LANES), 1)
        for h in range(HEADS):
            rs, ls, ks = slice(h * c, (h + 1) * c), slice(h * DV, (h + 1) * DV), slice(h * DK, (h + 1) * DK)
            pair = q_in[:, (h // 2) * LANES:(h // 2 + 1) * LANES]
            if h % 2 == 0:
                lhs = jnp.where(lane < DK, pair, sc_ref[b, rs, :])
                rhs = jnp.concatenate([state_bf[ks, :], v[:, ls]], axis=0)
            else:
                lhs = jnp.where(lane >= DK, pair, sc_ref[b, rs, :])
                rhs = jnp.concatenate([v[:, ls], state_bf[ks, :]], axis=0)
            o_ref[0, b, :, ls] = _dot(lhs, rhs).astype(BF16)
            st_ref[0, b, ks, :] = decay[ks, :] * state[ks, :] + _dot(k_out[ks, :], v[:, ls])


def _chunk_scan(qk, v, g, *, nb, length):
    c = CHUNK
    blk = lambda w: pl.BlockSpec((1, nb, c, w), lambda br, i: (br, 0, i, 0))
    return pl.pallas_call(
        functools.partial(_chunk_kernel, nb=nb, c=c),
        grid=(2, length // c),
        in_specs=[blk(2 * QK), blk(VW), blk(QK)],
        out_specs=[blk(VW), pl.BlockSpec((1, nb, QK, DV), lambda br, i: (br, 0, 0, 0))],
        out_shape=[jax.ShapeDtypeStruct((2, nb, length, VW), BF16),
                   jax.ShapeDtypeStruct((2, nb, QK, DV), F32)],
        scratch_shapes=[pltpu.VMEM((nb, c, QK), F32), pltpu.VMEM((nb, HEADS * c, 2 * c), BF16)],
        compiler_params=_cparams("arbitrary", "arbitrary"),
        name="chunk_scan",
    )(qk, v, g)


def _decode_kernel(qk_ref, v_ref, g_ref, s0_ref, s1_ref, *rest, nt):
    o_ref, n0_ref, n1_ref = rest[-3:]
    for br, (s_ref, n_ref) in enumerate(((s0_ref, n0_ref), (s1_ref, n1_ref))):
        for j in range(nt):
            d = jnp.exp(g_ref[br, j:j + 1, :])
            q = qk_ref[br, j:j + 1, 0:QK].astype(F32)
            k = qk_ref[br, j:j + 1, QK:2 * QK].astype(F32)
            cols = jnp.transpose(jnp.concatenate([d, k, q, jnp.zeros((5, QK), F32)], axis=0))
            vrow = v_ref[br, j:j + 1, :].astype(F32)
            vfull = jnp.concatenate([jnp.broadcast_to(vrow[:, h * DV:(h + 1) * DV], (DK, DV)) for h in range(HEADS)], axis=0)
            new = cols[:, 0:1] * s_ref[j] + cols[:, 1:2] * vfull
            n_ref[j] = new
            t = cols[:, 2:3] * new
            for h in range(HEADS):
                o_ref[br, j:j + 1, h * DV:(h + 1) * DV] = jnp.sum(t[h * DK:(h + 1) * DK, :], axis=0, keepdims=True).astype(BF16)


def _decode(qk, v, g, s_gla, s_hg, prev, *, layer):
    n = qk.shape[1]
    nt = 8
    row = lambda w: pl.BlockSpec((2, nt, w), lambda i: (0, i, 0))
    st = pl.BlockSpec((None, nt, QK, DV), lambda i: (layer, i, 0, 0))
    carried = [] if prev is None else list(prev)
    first = 5
    return pl.pallas_call(
        functools.partial(_decode_kernel, nt=nt),
        grid=(n // nt,),
        in_specs=[row(2 * QK), row(VW), row(QK), st, st] + [pl.BlockSpec(memory_space=pl.ANY)] * len(carried),
        out_specs=[row(VW), st, st],
        out_shape=[jax.ShapeDtypeStruct((2, n, VW), BF16),
                   jax.ShapeDtypeStruct((DEPTH, n, QK, DV), F32),
                   jax.ShapeDtypeStruct((DEPTH, n, QK, DV), F32)],
        input_output_aliases={first + j: 1 + j for j in range(len(carried))},
        compiler_params=_cparams("parallel"),
        name="decode_step",
    )(qk, v, g, s_gla, s_hg, *carried)


def _s5_disc_kernel(lr_ref, li_ref, ldt_ref, br_ref, bi_ref, abr_ref, abi_ref, bbr_ref, bbi_ref):
    lr, li, dt = lr_ref[...], li_ref[...], jnp.exp(ldt_ref[...])
    mag = jnp.exp(lr * dt)
    ab_re, ab_im = mag * jnp.cos(li * dt), mag * jnp.sin(li * dt)
    den = lr * lr + li * li
    num_re = ab_re - 1.0
    coef_re = (num_re * lr + ab_im * li) / den
    coef_im = (ab_im * lr - num_re * li) / den
    br, bi = br_ref[...], bi_ref[...]
    abr_ref[...] = ab_re
    abi_ref[...] = ab_im
    bbr_ref[...] = coef_re * br - coef_im * bi
    bbi_ref[...] = coef_re * bi + coef_im * br


def _s5_discretise(a_re, a_im, log_dt, b_re, b_im):
    n = DEPTH * S5_GROUPS
    rep = lambda t: jnp.repeat(t.reshape(n, S5_STATE), S5_GROUP, axis=1)
    ldt = jnp.broadcast_to(log_dt.reshape(n, 1), (n, S5_STATE * S5_GROUP))
    shape = jax.ShapeDtypeStruct((n, S5_STATE * S5_GROUP), F32)
    ab_re, ab_im, bb_re, bb_im = pl.pallas_call(_s5_disc_kernel, out_shape=[shape] * 4, name="s5_discretise")(
        rep(a_re), rep(a_im), ldt, b_re.reshape(n, -1), b_im.reshape(n, -1))
    pole = lambda t: t[:, ::S5_GROUP].reshape(DEPTH, 1, S5_LANES)
    eye = jnp.eye(8, dtype=F32)

    def blockdiag_in(bb):
        t = bb.reshape(DEPTH, S5_SLABS, 8, S5_STATE, S5_GROUP).transpose(0, 1, 2, 4, 3)
        return jnp.einsum("dcgmp,gh->dcgmhp", t, eye).reshape(DEPTH, S5_SLABS, LANES, 8 * S5_STATE)

    w_in = jnp.concatenate([blockdiag_in(bb_re), blockdiag_in(bb_im)], axis=-1).astype(BF16)
    return w_in, pole(ab_re), pole(ab_im)


def _s5_blockdiag_out(c):
    t = c.reshape(DEPTH, S5_SLABS, 8, S5_GROUP, S5_STATE).transpose(0, 1, 2, 4, 3)
    return jnp.einsum("dcgpm,gh->dcgphm", t, jnp.eye(8, dtype=F32)).reshape(DEPTH, S5_SLABS, 8 * S5_STATE, LANES).astype(BF16)


def _time_major_perm(nb, ct):
    r = np.arange(nb * ct)
    p = np.zeros((nb * ct, nb * ct), np.float32)
    p[r, (r % nb) * ct + r // nb] = 1.0
    return p


def _gelu_tanh(y):
    return 0.5 * y * (1.0 + jnp.tanh(0.7978845608028654 * (y + 0.044715 * (y * y * y))))


def _s5_input(ub, win_ref, xre_ref, xim_ref):
    half = 8 * S5_STATE
    for s in range(S5_SLABS):
        r = _dot(ub[:, s * LANES:(s + 1) * LANES], win_ref[s])
        xre_ref[:, s * half:(s + 1) * half] = r[:, :half]
        xim_ref[:, s * half:(s + 1) * half] = r[:, half:]


def _s5_readout(xre_ref, xim_ref, cre_ref, cim_ref):
    half = 8 * S5_STATE
    ys = []
    for s in range(S5_SLABS):
        ls = slice(s * half, (s + 1) * half)
        ys.append(_dot(xre_ref[:, ls].astype(BF16), cre_ref[s]) - _dot(xim_ref[:, ls].astype(BF16), cim_ref[s]))
    return jnp.concatenate(ys, axis=-1)


def _s5_glu(y, u, d_ref, wglu_ref, bglu_ref):
    z = _gelu_tanh(y + d_ref[...] * u)
    return z * _sigmoid(_dot(z.astype(BF16), wglu_ref[...]) + bglu_ref[...])


def _s5_scan_kernel(su_ref, perm_ref, permt_ref, win_ref, are_ref, aim_ref, cre_ref, cim_ref, d_ref, wglu_ref, bglu_ref,
                    o_ref, hre_ref, him_ref, xre_ref, xim_ref, *, nb, ct):
    @pl.when(pl.program_id(0) == 0)
    def _():
        hre_ref[...] = jnp.zeros_like(hre_ref)
        him_ref[...] = jnp.zeros_like(him_ref)

    u = su_ref[...].reshape(nb * ct, S5_WIDTH)
    ub = _dot(perm_ref[...], u.astype(BF16)).astype(BF16)

    half = 8 * S5_STATE
    ys = []
    for s in range(S5_SLABS):
        ls = slice(s * half, (s + 1) * half)
        r = _dot(ub[:, s * LANES:(s + 1) * LANES], win_ref[s])
        xre_ref[:, ls] = r[:, :half]
        xim_ref[:, ls] = r[:, half:]
        ar = jnp.broadcast_to(are_ref[:, ls], (nb, half))
        ai = jnp.broadcast_to(aim_ref[:, ls], (nb, half))
        hr, hi = hre_ref[:, ls], him_ref[:, ls]
        for t in range(ct):
            rows = slice(t * nb, (t + 1) * nb)
            hr, hi = ar * hr - ai * hi + xre_ref[rows, ls], ar * hi + ai * hr + xim_ref[rows, ls]
            xre_ref[rows, ls] = hr
            xim_ref[rows, ls] = hi
        hre_ref[:, ls] = hr
        him_ref[:, ls] = hi
        ys.append(_dot(xre_ref[:, ls].astype(BF16), cre_ref[s]) - _dot(xim_ref[:, ls].astype(BF16), cim_ref[s]))

    y = jnp.concatenate(ys, axis=-1)
    y_hi = y.astype(BF16)
    y_lo = (y - y_hi.astype(F32)).astype(BF16)
    y = _dot01(permt_ref[...], (y_hi, y_lo))
    o_ref[...] = _s5_glu(y, u, d_ref, wglu_ref, bglu_ref).reshape(nb, ct, S5_WIDTH).astype(BF16)


def _s5_weight_specs(layer):
    return [_layer_spec((S5_SLABS, LANES, 2 * 8 * S5_STATE), layer),
            _layer_spec((1, S5_LANES), layer), _layer_spec((1, S5_LANES), layer),
            _layer_spec((S5_SLABS, 8 * S5_STATE, LANES), layer), _layer_spec((S5_SLABS, 8 * S5_STATE, LANES), layer),
            _layer_spec((1, S5_WIDTH), layer), _layer_spec((S5_WIDTH, S5_WIDTH), layer),
            _layer_spec((1, S5_WIDTH), layer)]


def _s5_scan(su, weights, *, nb, length, layer):
    ct = S5_CHUNK
    perm = _time_major_perm(nb, ct)
    const = pl.BlockSpec((nb * ct, nb * ct), lambda i: (0, 0))
    return pl.pallas_call(
        functools.partial(_s5_scan_kernel, nb=nb, ct=ct),
        grid=(length // ct,),
        in_specs=[pl.BlockSpec((nb, ct, S5_WIDTH), lambda i: (0, i, 0)), const, const] + _s5_weight_specs(layer),
        out_specs=[pl.BlockSpec((nb, ct, S5_WIDTH), lambda i: (0, i, 0)),
                   pl.BlockSpec((nb, S5_LANES), lambda i: (0, 0)), pl.BlockSpec((nb, S5_LANES), lambda i: (0, 0))],
        out_shape=[jax.ShapeDtypeStruct((nb, length, S5_WIDTH), BF16),
                   jax.ShapeDtypeStruct((nb, S5_LANES), F32), jax.ShapeDtypeStruct((nb, S5_LANES), F32)],
        scratch_shapes=[pltpu.VMEM((nb * ct, S5_LANES), F32), pltpu.VMEM((nb * ct, S5_LANES), F32)],
        compiler_params=_cparams("arbitrary"),
        name="s5_scan",
    )(su, jnp.asarray(perm, BF16), jnp.asarray(perm.T, BF16), *weights)


def _s5_step_kernel(su_ref, h0r_ref, h0i_ref, win_ref, are_ref, aim_ref, cre_ref, cim_ref, d_ref, wglu_ref, bglu_ref,
                    o_ref, hre_ref, him_ref):
    u = su_ref[...]
    _s5_input(u.astype(BF16), win_ref, hre_ref, him_ref)
    ar, ai = are_ref[...], aim_ref[...]
    h0r, h0i = h0r_ref[...], h0i_ref[...]
    nr = ar * h0r - ai * h0i + hre_ref[...]
    ni = ar * h0i + ai * h0r + him_ref[...]
    hre_ref[...] = nr
    him_ref[...] = ni
    o_ref[...] = _s5_glu(_s5_readout(hre_ref, him_ref, cre_ref, cim_ref), u, d_ref, wglu_ref, bglu_ref).astype(BF16)


def _s5_step(su, h0r, h0i, weights, *, layer):
    n = su.shape[0]
    full = lambda w: pl.BlockSpec((n, w), lambda i: (0, 0))
    state = pl.BlockSpec((None, n, S5_LANES), lambda i: (layer, 0, 0))
    return pl.pallas_call(
        _s5_step_kernel,
        grid=(1,),
        in_specs=[full(S5_WIDTH), state, state] + _s5_weight_specs(layer),
        out_specs=[full(S5_WIDTH), full(S5_LANES), full(S5_LANES)],
        out_shape=[jax.ShapeDtypeStruct((n, S5_WIDTH), BF16),
                   jax.ShapeDtypeStruct((n, S5_LANES), F32), jax.ShapeDtypeStruct((n, S5_LANES), F32)],
        compiler_params=_cparams("arbitrary"),
        name="s5_step",
    )(su, h0r, h0i, *weights)


def _head_norm(o, g):
    parts = []
    for h in range(HEADS):
        seg = o[:, h * DV:(h + 1) * DV]
        parts.append(seg * lax.rsqrt(jnp.mean(seg * seg, axis=-1, keepdims=True) + EPS))
    return jnp.concatenate(parts, axis=-1) * g


def _route(logits, count):
    rows = logits.shape[0]
    lane = lax.broadcasted_iota(I32, logits.shape, 1)
    neg = -jnp.inf
    first = lambda hit: jnp.min(jnp.where(hit, lane, LANES), axis=-1, keepdims=True)
    glog = jnp.where(lane < MOE_GROUPS, logits, neg)
    gmax = jnp.max(glog, axis=-1, keepdims=True)
    gidx = first(glog == gmax)
    gw = 1.0 / jnp.sum(jnp.where(lane < MOE_GROUPS, jnp.exp(logits - gmax), 0.0), axis=-1, keepdims=True)
    inside = (lane >= MOE_GROUPS) & (lane < MOE_GROUPS + MOE_EXPERTS) & (((lane - MOE_GROUPS) >> 3) == gidx)
    el = jnp.where(inside, logits, neg)
    v1 = jnp.max(el, axis=-1, keepdims=True)
    i1 = first(el == v1)
    el2 = jnp.where(lane == i1, neg, el)
    v2 = jnp.max(el2, axis=-1, keepdims=True)
    i2 = first(el2 == v2)
    p2 = jnp.exp(v2 - v1)
    w1 = gw / (1.0 + p2)
    w2 = gw * p2 / (1.0 + p2)
    hit1, hit2 = lane == i1, lane == i2
    onehot = jnp.where(hit1 | hit2, 1.0, 0.0)
    blk = min(rows, 256)
    tr = lax.broadcasted_iota(I32, (blk, blk), 0)
    tc = lax.broadcasted_iota(I32, (blk, blk), 1)
    lower = jnp.where(tc < tr, 1.0, 0.0).astype(BF16)
    before = []
    for r0 in range(0, rows, blk):
        part = onehot[r0:r0 + blk, :]
        before.append(_dot(lower, part.astype(BF16)) + count)
        count = count + jnp.sum(part, axis=0, keepdims=True)
    before = jnp.concatenate(before, axis=0)
    rank1 = jnp.sum(jnp.where(hit1, before, 0.0), axis=-1, keepdims=True)
    rank2 = jnp.sum(jnp.where(hit2, before, 0.0), axis=-1, keepdims=True)
    slab = jnp.zeros(logits.shape, F32)
    for ln, val in ((R_E1, (i1 - MOE_GROUPS).astype(F32)), (R_E2, (i2 - MOE_GROUPS).astype(F32)), (R_W1, w1), (R_W2, w2),
                    (R_RANK1, rank1), (R_RANK2, rank2)):
        slab = jnp.where(lane == ln, val, slab)
    return slab, count


def _merge_kernel(x_ref, o_ref, gate_ref, os5_ref, cnt0_ref, nm_ref, wgm_ref, gn_ref, wbr_ref, wout_ref, nf_ref,
                  wr_ref, br_ref, *rest, sub):
    xm_ref, h2_ref, rt_ref, rtt_ref, cnt_ref = rest[-5:]

    @pl.when(pl.program_id(0) == 0)
    def _():
        cnt_ref[...] = cnt0_ref[...]

    logits = []
    for r0 in range(0, x_ref.shape[0], sub):
        rs = slice(r0, r0 + sub)
        x = x_ref[rs, :]
        hb = _rms(x, nm_ref[...]).astype(BF16)
        mixed = None
        for i, src in enumerate((0, None, 1)):
            if src is None:
                branch = os5_ref[rs, :]
            else:
                branch = (_head_norm(o_ref[src, rs, :].astype(F32), gn_ref[src]) * gate_ref[src, rs, :].astype(F32)).astype(BF16)
            gate = _sigmoid(_dot(hb, wgm_ref[:, i * D_MODEL:(i + 1) * D_MODEL]))
            term = gate * _dot(branch, wbr_ref[i])
            mixed = term if mixed is None else mixed + term
        xm = x + _dot(mixed.astype(BF16), wout_ref[...])
        xm_ref[rs, :] = xm
        h2 = _rms(xm, nf_ref[...])
        h2_hi = h2.astype(BF16)
        h2_lo = (h2 - h2_hi.astype(F32)).astype(BF16)
        h2_ref[rs, :] = h2_hi
        both = _dot(h2_hi, wr_ref[...])
        logits.append(both[:, :LANES] + (both[:, LANES:] + _dot(h2_lo, wr_ref[:, :LANES])) + br_ref[...])
    slab, cnt_ref[...] = _route(jnp.concatenate(logits, axis=0), cnt_ref[...])
    rt_ref[...] = slab
    rtt_ref[...] = jnp.transpose(slab)[0:8, :]


def _merge(x, o, gate, os5, cnt0, weights, carried, *, layer, tm, total_rows, block_offset):
    rows = x.shape[0]
    tile = lambda w: pl.BlockSpec((tm, w), lambda i: (i, 0))
    shared = lambda w: pl.BlockSpec((tm, w), lambda i: (i + block_offset, 0))
    carried = [] if carried is None else list(carried)
    w_specs = [_layer_spec((1, D_MODEL), layer), _layer_spec((D_MODEL, 3 * D_MODEL), layer),
               _layer_spec((2, 1, VW), layer), _layer_spec((3, VW, D_MODEL), layer),
               _layer_spec((D_MODEL, D_MODEL), layer), _layer_spec((1, D_MODEL), layer),
               _layer_spec((D_MODEL, 2 * LANES), layer), _layer_spec((1, LANES), layer)]
    first = 5 + len(w_specs)
    return pl.pallas_call(
        functools.partial(_merge_kernel, sub=min(tm, 256)),
        grid=(rows // tm,),
        in_specs=[tile(D_MODEL),
                  pl.BlockSpec((2, tm, VW), lambda i: (0, i, 0)),
                  pl.BlockSpec((2, tm, VW), lambda i: (0, i, 0)),
                  tile(S5_WIDTH),
                  pl.BlockSpec((1, LANES), lambda i: (0, 0))] + w_specs + [pl.BlockSpec(memory_space=pl.ANY)] * len(carried),
        out_specs=[tile(D_MODEL), shared(D_MODEL), shared(LANES),
                   pl.BlockSpec((8, tm), lambda i: (0, i + block_offset)), pl.BlockSpec((1, LANES), lambda i: (0, 0))],
        out_shape=[jax.ShapeDtypeStruct((rows, D_MODEL), F32),
                   jax.ShapeDtypeStruct((total_rows, D_MODEL), BF16),
                   jax.ShapeDtypeStruct((total_rows, LANES), F32),
                   jax.ShapeDtypeStruct((8, total_rows), F32),
                   jax.ShapeDtypeStruct((1, LANES), F32)],
        input_output_aliases={first + j: 1 + j for j in range(len(carried))},
        compiler_params=_cparams("arbitrary"),
        name="merge_route",
    )(x, o, gate, os5, cnt0, *weights, *carried)


def _ffn_kernel(te_ref, nu_ref, x_ref, wg_ref, wu_ref, wd_ref, *rest, tile0):
    y_ref, wg_bf, wu_bf, wd_bf = rest[-4:]
    i = pl.program_id(0)
    tile = i + tile0

    @pl.when(tile < nu_ref[0])
    def _():
        @pl.when((i == 0) | (te_ref[tile] != te_ref[jnp.maximum(tile - 1, 0)]))
        def _():
            wg_bf[...] = wg_ref[...].astype(BF16)
            wu_bf[...] = wu_ref[...].astype(BF16)
            wd_bf[...] = wd_ref[...].astype(BF16)

        x = x_ref[...]
        act = _silu(_dot(x, wg_bf[...])) * _dot(x, wu_bf[...])
        y_ref[...] = _dot(act.astype(BF16), wd_bf[...]).astype(BF16)


def _grouped_ffn(tile_expert, n_used, xg, y_prev, wg, wu, wd, *, layer, tile0, total_rows):
    used = lambda i, nu: jnp.maximum(jnp.minimum(i + tile0, nu[0] - 1), tile0)
    expert = lambda a, b: pl.BlockSpec((None, None, a, b), lambda i, te, nu: (layer, te[used(i, nu)], 0, 0))
    carried = [] if y_prev is None else [y_prev]
    grid_spec = pltpu.PrefetchScalarGridSpec(
        num_scalar_prefetch=2,
        grid=(xg.shape[0] // MOE_TILE,),
        in_specs=[pl.BlockSpec((MOE_TILE, D_MODEL), lambda i, te, nu: (used(i, nu) - tile0, 0)),
                  expert(D_MODEL, MOE_FF), expert(D_MODEL, MOE_FF), expert(MOE_FF, D_MODEL)]
        + [pl.BlockSpec(memory_space=pl.ANY)] * len(carried),
        out_specs=pl.BlockSpec((MOE_TILE, D_MODEL), lambda i, te, nu: (used(i, nu), 0)),
        scratch_shapes=[pltpu.VMEM((D_MODEL, MOE_FF), BF16), pltpu.VMEM((D_MODEL, MOE_FF), BF16),
                        pltpu.VMEM((MOE_FF, D_MODEL), BF16)],
    )
    return pl.pallas_call(
        functools.partial(_ffn_kernel, tile0=tile0),
        grid_spec=grid_spec,
        out_shape=jax.ShapeDtypeStruct((total_rows, D_MODEL), BF16),
        input_output_aliases={6: 0} if carried else {},
        compiler_params=_cparams("arbitrary"),
        name="expert_ffn",
    )(tile_expert, n_used, xg, wg, wu, wd, *carried)


@compute_on("tpu_sparsecore")
@jax.jit
def _take_rows(x, idx):
    return jnp.take(x, idx, axis=0, mode="clip")


def _piece_cuts(n):
    inner = sorted({min(max(round(f * n), 1), n - 1) for f in PIPE_CUTS}) if n > 1 else []
    return [0] + inner + [n]


def _moe_plan(route_t, count):
    tokens = route_t.shape[1]
    n_tiles = -(-(2 * tokens + MOE_EXPERTS * (MOE_TILE - 1)) // MOE_TILE)
    n_rows = n_tiles * MOE_TILE
    counts = count[0, MOE_GROUPS:MOE_GROUPS + MOE_EXPERTS].astype(I32)
    padded = ((counts + MOE_TILE - 1) // MOE_TILE) * MOE_TILE
    gend = jnp.cumsum(padded)
    gstart = gend - padded
    experts = jnp.arange(MOE_EXPERTS, dtype=I32)[:, None]

    def rows_of(e_lane, rank_lane):
        e = route_t[e_lane].astype(I32)
        return jnp.sum(jnp.where(e[None, :] == experts, gstart[:, None], 0), axis=0) + route_t[rank_lane].astype(I32)

    pos1, pos2 = rows_of(R_E1, R_RANK1), rows_of(R_E2, R_RANK2)
    token = jnp.arange(tokens, dtype=I32) + 1
    marked = jnp.zeros((n_rows,), I32).at[jnp.concatenate([pos1, pos2])].add(jnp.concatenate([token, token]),
                                                                             unique_indices=True)
    filler = jnp.arange(n_rows, dtype=I32) % tokens
    src = jnp.where(marked > 0, marked - 1, filler)
    tile_start = jnp.arange(n_tiles, dtype=I32) * MOE_TILE
    tile_expert = jnp.minimum(jnp.sum((tile_start[:, None] >= gend[None, :]).astype(I32), axis=1), MOE_EXPERTS - 1)
    n_used = (gend[-1] // MOE_TILE).astype(I32).reshape(1)
    return src, pos1, pos2, tile_expert, n_used


def _ple_kernel(xm_ref, y1_ref, y2_ref, rt_ref, p_ref, np_ref, wg_ref, wp_ref, nfin_ref, *rest, final):
    out_ref = rest[-1]
    rt = rt_ref[...]
    x1 = xm_ref[...] + rt[:, R_W1:R_W1 + 1] * y1_ref[...].astype(F32) + rt[:, R_W2:R_W2 + 1] * y2_ref[...].astype(F32)
    gate = _sigmoid(_dot(_rms(x1, np_ref[...]).astype(BF16), wg_ref[...]))
    x2 = x1 + gate * _dot(p_ref[...].astype(BF16), wp_ref[...])
    out_ref[...] = _rms(x2, nfin_ref[...]) if final else x2


def _ple(xm, y1, y2, route, p, x_prev, npl, wg, wp, nfin, *, layer, final, tm, n_blocks, x_off, y_off, rt_off):
    at = lambda w, off: pl.BlockSpec((tm, w), lambda i: (i + off, 0))
    carried = [] if x_prev is None else [x_prev]
    return pl.pallas_call(
        functools.partial(_ple_kernel, final=final),
        grid=(n_blocks,),
        in_specs=[at(D_MODEL, x_off), at(D_MODEL, y_off), at(D_MODEL, y_off), at(LANES, rt_off),
                  pl.BlockSpec((None, tm, PLE_DIM), lambda i: (layer, i + x_off, 0)),
                  _layer_spec((1, D_MODEL), layer), _layer_spec((D_MODEL, D_MODEL), layer),
                  _layer_spec((PLE_DIM, D_MODEL), layer), pl.BlockSpec((1, D_MODEL), lambda i: (0, 0))]
        + [pl.BlockSpec(memory_space=pl.ANY)] * len(carried),
        out_specs=at(D_MODEL, x_off),
        out_shape=jax.ShapeDtypeStruct(xm.shape, F32),
        input_output_aliases={9: 0} if carried else {},
        compiler_params=_cparams("parallel"),
        name="combine_ple",
    )(xm, y1, y2, route, p, npl, wg, wp, nfin, *carried)


def kernel(x_prompt, x_sample, state_gla, state_s5_re, state_s5_im, state_hgrn, p_prompt, p_sample, norm_mix, w_in, gla_w_gate2, gla_b_gate, gla_norm, s5_a_re, s5_a_im, s5_log_dt, s5_b_re, s5_b_im, s5_c_re, s5_c_im, s5_d, s5_w_glu, s5_b_glu, hgrn_lb_logits, hgrn_norm, w_br_gla, w_br_s5, w_br_hgrn, w_out, norm_ffn, moe_w_group, moe_b_group, moe_w_expert, moe_b_expert, moe_w_gate, moe_w_up, moe_w_down, norm_ple, w_ple_gate, w_ple_proj, norm_final):
    nb, length, _ = x_prompt.shape
    ns = x_sample.shape[0]
    n_p = nb * length
    n_all = n_p + ns
    row = lambda t: t.reshape(DEPTH, 1, -1)

    w_a, w_gm = _w_in_layout(w_in)
    wg2 =jnp.concatenate([gla_w_gate2, jnp.zeros((DEPTH, LANES - GLA_RANK, QK), F32)], axis=1).astype(BF16)
    s5_win, s5_are, s5_aim = _s5_discretise(s5_a_re, s5_a_im, s5_log_dt, s5_b_re, s5_b_im)
    s5_w = (s5_win, s5_are, s5_aim, _s5_blockdiag_out(s5_c_re), _s5_blockdiag_out(s5_c_im), row(s5_d),
            s5_w_glu.astype(BF16), row(s5_b_glu))
    head_gain = jnp.stack([jnp.tile(gla_norm, (1, HEADS)), jnp.tile(hgrn_norm, (1, HEADS))], axis=1).reshape(DEPTH, 2, 1, VW)
    w_router = jnp.concatenate([moe_w_group, moe_w_expert,
                                jnp.zeros((DEPTH, D_MODEL, LANES - MOE_GROUPS - MOE_EXPERTS), F32)], axis=-1)
    wr_hi = w_router.astype(BF16)
    wr_lo = (w_router - wr_hi.astype(F32)).astype(BF16)
    wr_hilo = jnp.concatenate([wr_hi, wr_lo], axis=-1)
    b_router = jnp.concatenate([moe_b_group, moe_b_expert,
                                jnp.zeros((DEPTH, LANES - MOE_GROUPS - MOE_EXPERTS), F32)], axis=-1).reshape(DEPTH, 1, LANES)
    merge_w = (row(norm_mix), w_gm, head_gain,
               jnp.stack([w_br_gla, w_br_s5, w_br_hgrn], axis=1).astype(BF16), w_out.astype(BF16), row(norm_ffn),
               wr_hilo, b_router)
    ple_w = (row(norm_ple), w_ple_gate.astype(BF16), w_ple_proj.astype(BF16), norm_final.reshape(1, D_MODEL))
    nm, bg = row(norm_mix), row(gla_b_gate)
    p_p, p_s = p_prompt.reshape(DEPTH, n_p, PLE_DIM), p_sample.reshape(DEPTH, ns, PLE_DIM)
    sg_in, sh_in = state_gla.reshape(DEPTH, ns, QK, DV), state_hgrn.reshape(DEPTH, ns, QK, DV)
    s5r_in, s5i_in = state_s5_re.reshape(DEPTH, ns, S5_LANES), state_s5_im.reshape(DEPTH, ns, S5_LANES)

    xp = x_prompt.reshape(n_p, D_MODEL)
    xs = x_sample.reshape(ns, D_MODEL)
    new_p, new_s5 = [], []
    new_s = None
    for i in range(DEPTH):
        qk, v, gate, g, su = _inproj(xp, nm, w_a, wg2, bg, hgrn_lb_logits, layer=i, tm=1024)
        o_p, st_p = _chunk_scan(qk.reshape(2, nb, length, -1), v.reshape(2, nb, length, -1),
                                g.reshape(2, nb, length, -1), nb=nb, length=length)
        os5_p, hre_p, him_p = _s5_scan(su.reshape(nb, length, -1), s5_w, nb=nb, length=length, layer=i)
        xm_p, h2, rt, rtt, cnt = _merge(xp, o_p.reshape(2, n_p, VW), gate, os5_p.reshape(n_p, -1),
                                        jnp.zeros((1, LANES), F32), merge_w, None, layer=i, tm=1024, total_rows=n_all,
                                        block_offset=0)
        new_p.append((st_p[0].reshape(nb, HEADS, DK, DV), hre_p.reshape(nb, S5_GROUPS, S5_STATE),
                      him_p.reshape(nb, S5_GROUPS, S5_STATE), st_p[1].reshape(nb, HEADS, DK, DV)))

        qk, v, gate, g, su = _inproj(xs, nm, w_a, wg2, bg, hgrn_lb_logits, layer=i, tm=ns)
        o_s, *new_s = _decode(qk, v, g, sg_in, sh_in, new_s, layer=i)
        os5_s, hre_s, him_s = _s5_step(su, s5r_in, s5i_in, s5_w, layer=i)
        xm_s, h2, rt, rtt, cnt = _merge(xs, o_s, gate, os5_s, cnt, merge_w, (h2, rt, rtt), layer=i, tm=ns,
                                        total_rows=n_all, block_offset=n_p // ns)
        new_s5.append((hre_s.reshape(ns, S5_GROUPS, S5_STATE), him_s.reshape(ns, S5_GROUPS, S5_STATE)))

        src, pos1, pos2, tile_expert, n_used = _moe_plan(rtt, cnt)
        n_tiles = tile_expert.shape[0]
        cuts = _piece_cuts(n_tiles)
        y = None
        for t0, t1 in zip(cuts[:-1], cuts[1:]):
            y = _grouped_ffn(tile_expert, n_used, _take_rows(h2, src[t0 * MOE_TILE:t1 * MOE_TILE]), y,
                             moe_w_gate, moe_w_up, moe_w_down, layer=i, tile0=t0, total_rows=n_tiles * MOE_TILE)

        final = i == DEPTH - 1
        tm = 512
        cuts = _piece_cuts(n_p // tm)
        x_new = None
        for b0, b1 in zip(cuts[:-1], cuts[1:]):
            r0, r1 = b0 * tm, (n_all if b1 == cuts[-1] else b1 * tm)
            y1, y2 = _take_rows(y, pos1[r0:r1]), _take_rows(y, pos2[r0:r1])
            x_new = _ple(xm_p, y1, y2, rt, p_p, x_new, *ple_w, layer=i, final=final, tm=tm, n_blocks=b1 - b0,
                         x_off=b0, y_off=0, rt_off=b0)
        xp = x_new
        xs = _ple(xm_s, y1, y2, rt, p_s, None, *ple_w, layer=i, final=final, tm=ns, n_blocks=1,
                  x_off=0, y_off=(n_p - r0) // ns, rt_off=n_p // ns)

    stack = lambda items, j: jnp.stack([it[j] for it in items])
    return (xp.reshape(nb, length, D_MODEL), xs.reshape(ns, 1, D_MODEL),
            stack(new_p, 0), stack(new_p, 1), stack(new_p, 2), stack(new_p, 3),
            new_s[0].reshape(DEPTH, ns, HEADS, DK, DV), stack(new_s5, 0), stack(new_s5, 1),
            new_s[1].reshape(DEPTH, ns, HEADS, DK, DV))
```

```python
import functools

import jax
import jax.numpy as jnp
import numpy as np
from jax import lax
from jax.experimental import pallas as pl
from jax.experimental.compute_on import compute_on
from jax.experimental.pallas import tpu as pltpu

F32, BF16, I32 = jnp.float32, jnp.bfloat16, jnp.int32

D_MODEL = 1024
DEPTH = 2
HEADS, DK, DV = 4, 64, 128
QK, VW = HEADS * DK, HEADS * DV
GLA_RANK, GLA_TAU = 16, 16.0
S5_WIDTH, S5_GROUP, S5_GROUPS, S5_STATE = 512, 16, 32, 64
S5_LANES = S5_GROUPS * S5_STATE
S5_SLABS = 4
MOE_GROUPS, MOE_PER_GROUP, MOE_EXPERTS, MOE_FF = 4, 8, 32, 256
PLE_DIM = 256
EPS = 1e-6

LANES = 128
CHUNK = 64
S5_CHUNK = 64
MOE_TILE = 512
PIPE_CUTS = (0.5,)
SAFE_EXP = 80.0
VMEM_LIMIT = 56 * 1024 * 1024

C_GLA, C_S5, C_HG, C_GLR, C_END = 0, 1536, 2048, 3584, 3712
R_E1, R_E2, R_W1, R_W2, R_RANK1, R_RANK2 = 0, 1, 2, 3, 4, 5


def _cparams(*sem):
    return pltpu.CompilerParams(dimension_semantics=sem, vmem_limit_bytes=VMEM_LIMIT)


def _layer_spec(shape, layer):
    return pl.BlockSpec((None,) + tuple(shape), lambda *_: (layer,) + (0,) * len(shape), pipeline_mode=pl.Buffered(1))


def _dot(a, b):
    return jnp.dot(a, b, preferred_element_type=F32)


def _rms(x, g):
    return x * lax.rsqrt(jnp.mean(x * x, axis=-1, keepdims=True) + EPS) * g


def _log_sigmoid(x):
    return jnp.minimum(x, 0.0) - jnp.log1p(jnp.exp(-jnp.abs(x)))


def _sigmoid(x):
    return 0.5 * jnp.tanh(0.5 * x) + 0.5


def _silu(x):
    return x * _sigmoid(x)


def _split3(x):
    hi = x.astype(BF16)
    r1 = x - hi.astype(F32)
    mid = r1.astype(BF16)
    lo = (r1 - mid.astype(F32)).astype(BF16)
    return hi, mid, lo


def _dot01(m, parts):
    out = _dot(m, parts[0])
    for p in parts[1:]:
        out = out + _dot(m, p)
    return out


W_GLA, W_GLR, W_S5HG, W_END = 1536, 1552, 3600, 6672


def _w_in_layout_kernel(wt_ref, wa_ref, wgm_ref):
    cols = lambda lo, hi: jnp.transpose(wt_ref[lo:hi, :])
    wa_ref[:, C_GLA:C_S5] = cols(0, W_GLA).astype(BF16)
    wa_ref[:, C_S5:C_GLR] = cols(W_GLR, W_S5HG).astype(BF16)
    lane = lax.broadcasted_iota(I32, (LANES, LANES), 1)
    wa_ref[:, C_GLR:C_END] = jnp.where(lane < GLA_RANK, cols(W_GLA, W_GLA + LANES), 0.0).astype(BF16)
    wgm_ref[...] = cols(W_S5HG, W_END).astype(BF16)


def _w_in_layout(w_in):
    wt = jnp.swapaxes(w_in, 1, 2)
    return pl.pallas_call(
        _w_in_layout_kernel,
        grid=(DEPTH, D_MODEL // LANES),
        in_specs=[pl.BlockSpec((None, W_END, LANES), lambda d, i: (d, 0, i))],
        out_specs=[pl.BlockSpec((None, LANES, C_END), lambda d, i: (d, i, 0)),
                   pl.BlockSpec((None, LANES, 3 * D_MODEL), lambda d, i: (d, i, 0))],
        out_shape=[jax.ShapeDtypeStruct((DEPTH, D_MODEL, C_END), BF16),
                   jax.ShapeDtypeStruct((DEPTH, D_MODEL, 3 * D_MODEL), BF16)],
        compiler_params=_cparams("parallel", "parallel"),
        name="w_in_layout",
    )(wt)
def _inproj_kernel(x_ref, nm_ref, w_ref, wg2_ref, bg_ref, lbl_ref, qk_ref, v_ref, gate_ref, g_ref, su_ref, *, layer, sub):
    lg = lbl_ref[...]
    mx = jnp.max(lg, axis=0, keepdims=True)
    ex = jnp.exp(lg - mx)
    sm = ex / jnp.sum(ex, axis=0, keepdims=True)
    cs = sm[0:1]
    for j in range(1, layer + 1):
        cs = cs + sm[j:j + 1]
    lb = cs - sm[0:1]
    log_lb, log_1mlb = jnp.log(lb), jnp.log1p(-lb)

    for r0 in range(0, x_ref.shape[0], sub):
        rs = slice(r0, r0 + sub)
        hb = _rms(x_ref[rs, :], nm_ref[...]).astype(BF16)

        def proj(lo, hi):
            return _dot(hb, w_ref[:, lo:hi])

        qk_ref[0, rs, 0:QK] = (proj(0, 256) * (DK ** -0.5)).astype(BF16)
        qk_ref[0, rs, QK:2 * QK] = proj(256, 512).astype(BF16)
        v_ref[0, rs, :] = proj(512, 1024).astype(BF16)
        gate_ref[0, rs, :] = _silu(proj(1024, 1536)).astype(BF16)
        glr = proj(C_GLR, C_END).astype(BF16)
        g_ref[0, rs, :] = _log_sigmoid(_dot(glr, wg2_ref[...]) + bg_ref[...]) * (1.0 / GLA_TAU)

        su_ref[rs, :] = proj(C_S5, C_HG)

        z = proj(C_HG + 256, C_HG + 512)
        a, c = log_lb, log_1mlb + _log_sigmoid(z)
        g_ref[1, rs, :] = jnp.maximum(a, c) + jnp.log1p(jnp.exp(-jnp.abs(a - c)))
        qk_ref[1, rs, 0:QK] = _silu(proj(C_HG, C_HG + 256)).astype(BF16)
        qk_ref[1, rs, QK:2 * QK] = ((1.0 - lb) * _sigmoid(-z)).astype(BF16)
        v_ref[1, rs, :] = proj(C_HG + 512, C_HG + 1024).astype(BF16)
        gate_ref[1, rs, :] = _silu(proj(C_HG + 1024, C_HG + 1536)).astype(BF16)


def _inproj(x, nm, w, wg2, bg, lbl, *, layer, tm):
    rows = x.shape[0]
    return pl.pallas_call(
        functools.partial(_inproj_kernel, layer=layer, sub=min(tm, 256)),
        grid=(rows // tm,),
        in_specs=[pl.BlockSpec((tm, D_MODEL), lambda i: (i, 0)),
                  _layer_spec((1, D_MODEL), layer),
                  _layer_spec((D_MODEL, C_END), layer),
                  _layer_spec((LANES, QK), layer),
                  _layer_spec((1, QK), layer),
                  pl.BlockSpec((DEPTH, QK), lambda i: (0, 0))],
        out_specs=[pl.BlockSpec((2, tm, 2 * QK), lambda i: (0, i, 0)),
                   pl.BlockSpec((2, tm, VW), lambda i: (0, i, 0)),
                   pl.BlockSpec((2, tm, VW), lambda i: (0, i, 0)),
                   pl.BlockSpec((2, tm, QK), lambda i: (0, i, 0)),
                   pl.BlockSpec((tm, S5_WIDTH), lambda i: (i, 0))],
        out_shape=[jax.ShapeDtypeStruct((2, rows, 2 * QK), BF16),
                   jax.ShapeDtypeStruct((2, rows, VW), BF16),
                   jax.ShapeDtypeStruct((2, rows, VW), BF16),
                   jax.ShapeDtypeStruct((2, rows, QK), F32),
                   jax.ShapeDtypeStruct((rows, S5_WIDTH), F32)],
        compiler_params=_cparams("parallel"),
        name="inproj",
    )(x, nm, w, wg2, bg, lbl)


def _head_stack(x):
    head = lax.broadcasted_iota(I32, x.shape, 1) // DK
    return jnp.concatenate([jnp.where(head == h, x, 0.0) for h in range(HEADS)], axis=0).astype(BF16)


def _stack_scores(qt, kt):
    kt = kt.astype(BF16)
    return lax.dot_general(_head_stack(qt), jnp.concatenate([kt, kt], axis=0), (((1,), (1,)), ((), ())),
                           preferred_element_type=F32)


def _chunk_kernel(qk_ref, v_ref, g_ref, o_ref, st_ref, bc_ref, sc_ref, *, nb, c):
    assert c == DK and 2 * c == LANES

    @pl.when(pl.program_id(1) == 0)
    def _():
        st_ref[...] = jnp.zeros_like(st_ref)

    row = lax.broadcasted_iota(I32, (c, c), 0)
    col = lax.broadcasted_iota(I32, (c, c), 1)
    tri = jnp.where(col <= row, 1.0, 0.0).astype(BF16)
    srow = lax.broadcasted_iota(I32, (HEADS * c, 2 * c), 0) & (c - 1)
    scol = lax.broadcasted_iota(I32, (HEADS * c, 2 * c), 1) & (c - 1)
    mid = c // 2 - 1

    spread = None
    for b in range(nb):
        bc = _dot01(tri, _split3(g_ref[0, b]))
        bc_ref[b] = bc
        ref, last = bc[mid:mid + 1, :], bc[c - 1:c, :]
        s = jnp.maximum(jnp.max(-ref), jnp.max(ref - last))
        spread = s if spread is None else jnp.maximum(spread, s)

    def qk_of(b):
        return qk_ref[0, b, :, 0:QK].astype(F32), qk_ref[0, b, :, QK:2 * QK].astype(F32)

    def scores_one_reference():
        for b in range(nb):
            q, k = qk_of(b)
            bc = bc_ref[b]
            ref = bc[mid:mid + 1, :]
            s = _stack_scores(q * jnp.exp(bc - ref), k * jnp.exp(ref - bc))
            sc_ref[b] = jnp.where(scol <= srow, s, 0.0).astype(BF16)

    def scores_by_levels():
        qrow = lax.broadcasted_iota(I32, (c, QK), 0)
        for b in range(nb):
            q, k = qk_of(b)
            bc = bc_ref[b]
            parts = _split3(g_ref[0, b])
            acc = jnp.where(scol == srow, _stack_scores(q, k), 0.0)
            half = c // 2
            while half >= 1:
                blk = 2 * half
                last_low = (row & ~(blk - 1)) + (half - 1)
                ref = _dot01(jnp.where(col <= last_low, 1.0, 0.0).astype(BF16), parts)
                upper = (qrow & (blk - 1)) >= half
                dq = jnp.minimum(jnp.where(upper, bc - ref, 0.0), 0.0)
                dk = jnp.minimum(jnp.where(upper, 0.0, ref - bc), 0.0)
                s = _stack_scores(q * jnp.exp(dq), k * jnp.exp(dk))
                pair = ((srow & ~(blk - 1)) == (scol & ~(blk - 1))) & ((srow & (blk - 1)) >= half) & ((scol & (blk - 1)) < half)
                acc = acc + jnp.where(pair, s, 0.0)
                half //= 2
            sc_ref[b] = acc.astype(BF16)

    lax.cond(spread <= SAFE_EXP, scores_one_reference, scores_by_levels)

    for b in range(nb):
        q, k = qk_of(b)
        bc = bc_ref[b]
        last = bc[c - 1:c, :]
        v = v_ref[0, b]
        state = st_ref[0, b]
        state_bf = state.astype(BF16)
        q_in = (q * jnp.exp(bc)).astype(BF16)
        k_out = jnp.transpose(k * jnp.exp(last - bc)).astype(BF16)
        decay = jnp.transpose(jnp.broadcast_to(jnp.exp(last), (DV, QK)))
        lane = lax.broadcasted_iota(I32, (c, LANES), 1)
        for h in range(HEADS):
            rs, ls, ks = slice(h * c, (h + 1) * c), slice(h * DV, (h + 1) * DV), slice(h * DK, (h + 1) * DK)
            pair = q_in[:, (h // 2) * LANES:(h // 2 + 1) * LANES]
            if h % 2 == 0:
                lhs = jnp.where(lane < DK, pair, sc_ref[b, rs, :])
                rhs = jnp.concatenate([state_bf[ks, :], v[:, ls]], axis=0)
            else:
                lhs = jnp.where(lane >= DK, pair, sc_ref[b, rs, :])
                rhs = jnp.concatenate([v[:, ls], state_bf[ks, :]], axis=0)
            o_ref[0, b, :, ls] = _dot(lhs, rhs).astype(BF16)
            st_ref[0, b, ks, :] = decay[ks, :] * state[ks, :] + _dot(k_out[ks, :], v[:, ls])


def _chunk_scan(qk, v, g, *, nb, length):
    c = CHUNK
    blk = lambda w: pl.BlockSpec((1, nb, c, w), lambda br, i: (br, 0, i, 0))
    return pl.pallas_call(
        functools.partial(_chunk_kernel, nb=nb, c=c),
        grid=(2, length // c),
        in_specs=[blk(2 * QK), blk(VW), blk(QK)],
        out_specs=[blk(VW), pl.BlockSpec((1, nb, QK, DV), lambda br, i: (br, 0, 0, 0))],
        out_shape=[jax.ShapeDtypeStruct((2, nb, length, VW), BF16),
                   jax.ShapeDtypeStruct((2, nb, QK, DV), F32)],
        scratch_shapes=[pltpu.VMEM((nb, c, QK), F32), pltpu.VMEM((nb, HEADS * c, 2 * c), BF16)],
        compiler_params=_cparams("arbitrary", "arbitrary"),
        name="chunk_scan",
    )(qk, v, g)


def _decode_kernel(qk_ref, v_ref, g_ref, s0_ref, s1_ref, *rest, nt):
    o_ref, n0_ref, n1_ref = rest[-3:]
    for br, (s_ref, n_ref) in enumerate(((s0_ref, n0_ref), (s1_ref, n1_ref))):
        for j in range(nt):
            d = jnp.exp(g_ref[br, j:j + 1, :])
            q = qk_ref[br, j:j + 1, 0:QK].astype(F32)
            k = qk_ref[br, j:j + 1, QK:2 * QK].astype(F32)
            cols = jnp.transpose(jnp.concatenate([d, k, q, jnp.zeros((5, QK), F32)], axis=0))
            vrow = v_ref[br, j:j + 1, :].astype(F32)
            vfull = jnp.concatenate([jnp.broadcast_to(vrow[:, h * DV:(h + 1) * DV], (DK, DV)) for h in range(HEADS)], axis=0)
            new = cols[:, 0:1] * s_ref[j] + cols[:, 1:2] * vfull
            n_ref[j] = new
            t = cols[:, 2:3] * new
            for h in range(HEADS):
                o_ref[br, j:j + 1, h * DV:(h + 1) * DV] = jnp.sum(t[h * DK:(h + 1) * DK, :], axis=0, keepdims=True).astype(BF16)


def _decode(qk, v, g, s_gla, s_hg, prev, *, layer):
    n = qk.shape[1]
    nt = 8
    row = lambda w: pl.BlockSpec((2, nt, w), lambda i: (0, i, 0))
    st = pl.BlockSpec((None, nt, QK, DV), lambda i: (layer, i, 0, 0))
    carried = [] if prev is None else list(prev)
    first = 5
    return pl.pallas_call(
        functools.partial(_decode_kernel, nt=nt),
        grid=(n // nt,),
        in_specs=[row(2 * QK), row(VW), row(QK), st, st] + [pl.BlockSpec(memory_space=pl.ANY)] * len(carried),
        out_specs=[row(VW), st, st],
        out_shape=[jax.ShapeDtypeStruct((2, n, VW), BF16),
                   jax.ShapeDtypeStruct((DEPTH, n, QK, DV), F32),
                   jax.ShapeDtypeStruct((DEPTH, n, QK, DV), F32)],
        input_output_aliases={first + j: 1 + j for j in range(len(carried))},
        compiler_params=_cparams("parallel"),
        name="decode_step",
    )(qk, v, g, s_gla, s_hg, *carried)


def _s5_disc_kernel(lr_ref, li_ref, ldt_ref, br_ref, bi_ref, abr_ref, abi_ref, bbr_ref, bbi_ref):
    lr, li, dt = lr_ref[...], li_ref[...], jnp.exp(ldt_ref[...])
    mag = jnp.exp(lr * dt)
    ab_re, ab_im = mag * jnp.cos(li * dt), mag * jnp.sin(li * dt)
    den = lr * lr + li * li
    num_re = ab_re - 1.0
    coef_re = (num_re * lr + ab_im * li) / den
    coef_im = (ab_im * lr - num_re * li) / den
    br, bi = br_ref[...], bi_ref[...]
    abr_ref[...] = ab_re
    abi_ref[...] = ab_im
    bbr_ref[...] = coef_re * br - coef_im * bi
    bbi_ref[...] = coef_re * bi + coef_im * br


def _s5_discretise(a_re, a_im, log_dt, b_re, b_im):
    n = DEPTH * S5_GROUPS
    rep = lambda t: jnp.repeat(t.reshape(n, S5_STATE), S5_GROUP, axis=1)
    ldt = jnp.broadcast_to(log_dt.reshape(n, 1), (n, S5_STATE * S5_GROUP))
    shape = jax.ShapeDtypeStruct((n, S5_STATE * S5_GROUP), F32)
    ab_re, ab_im, bb_re, bb_im = pl.pallas_call(_s5_disc_kernel, out_shape=[shape] * 4, name="s5_discretise")(
        rep(a_re), rep(a_im), ldt, b_re.reshape(n, -1), b_im.reshape(n, -1))
    pole = lambda t: t[:, ::S5_GROUP].reshape(DEPTH, 1, S5_LANES)
    eye = jnp.eye(8, dtype=F32)

    def blockdiag_in(bb):
        t = bb.reshape(DEPTH, S5_SLABS, 8, S5_STATE, S5_GROUP).transpose(0, 1, 2, 4, 3)
        return jnp.einsum("dcgmp,gh->dcgmhp", t, eye).reshape(DEPTH, S5_SLABS, LANES, 8 * S5_STATE)

    w_in = jnp.concatenate([blockdiag_in(bb_re), blockdiag_in(bb_im)], axis=-1).astype(BF16)
    return w_in, pole(ab_re), pole(ab_im)


def _s5_blockdiag_out(c):
    t = c.reshape(DEPTH, S5_SLABS, 8, S5_GROUP, S5_STATE).transpose(0, 1, 2, 4, 3)
    return jnp.einsum("dcgpm,gh->dcgphm", t, jnp.eye(8, dtype=F32)).reshape(DEPTH, S5_SLABS, 8 * S5_STATE, LANES).astype(BF16)


def _time_major_perm(nb, ct):
    r = np.arange(nb * ct)
    p = np.zeros((nb * ct, nb * ct), np.float32)
    p[r, (r % nb) * ct + r // nb] = 1.0
    return p


def _gelu_tanh(y):
    return 0.5 * y * (1.0 + jnp.tanh(0.7978845608028654 * (y + 0.044715 * (y * y * y))))


def _s5_input(ub, win_ref, xre_ref, xim_ref):
    half = 8 * S5_STATE
    for s in range(S5_SLABS):
        r = _dot(ub[:, s * LANES:(s + 1) * LANES], win_ref[s])
        xre_ref[:, s * half:(s + 1) * half] = r[:, :half]
        xim_ref[:, s * half:(s + 1) * half] = r[:, half:]


def _s5_readout(xre_ref, xim_ref, cre_ref, cim_ref):
    half = 8 * S5_STATE
    ys = []
    for s in range(S5_SLABS):
        ls = slice(s * half, (s + 1) * half)
        ys.append(_dot(xre_ref[:, ls].astype(BF16), cre_ref[s]) - _dot(xim_ref[:, ls].astype(BF16), cim_ref[s]))
    return jnp.concatenate(ys, axis=-1)


def _s5_glu(y, u, d_ref, wglu_ref, bglu_ref):
    z = _gelu_tanh(y + d_ref[...] * u)
    return z * _sigmoid(_dot(z.astype(BF16), wglu_ref[...]) + bglu_ref[...])


def _s5_scan_kernel(su_ref, perm_ref, permt_ref, win_ref, are_ref, aim_ref, cre_ref, cim_ref, d_ref, wglu_ref, bglu_ref,
                    o_ref, hre_ref, him_ref, xre_ref, xim_ref, *, nb, ct):
    @pl.when(pl.program_id(0) == 0)
    def _():
        hre_ref[...] = jnp.zeros_like(hre_ref)
        him_ref[...] = jnp.zeros_like(him_ref)

    u = su_ref[...].reshape(nb * ct, S5_WIDTH)
    ub = _dot(perm_ref[...], u.astype(BF16)).astype(BF16)

    half = 8 * S5_STATE
    ys = []
    for s in range(S5_SLABS):
        ls = slice(s * half, (s + 1) * half)
        r = _dot(ub[:, s * LANES:(s + 1) * LANES], win_ref[s])
        xre_ref[:, ls] = r[:, :half]
        xim_ref[:, ls] = r[:, half:]
        ar = jnp.broadcast_to(are_ref[:, ls], (nb, half))
        ai = jnp.broadcast_to(aim_ref[:, ls], (nb, half))
        hr, hi = hre_ref[:, ls], him_ref[:, ls]
        for t in range(ct):
            rows = slice(t * nb, (t + 1) * nb)
            hr, hi = ar * hr - ai * hi + xre_ref[rows, ls], ar * hi + ai * hr + xim_ref[rows, ls]
            xre_ref[rows, ls] = hr
            xim_ref[rows, ls] = hi
        hre_ref[:, ls] = hr
        him_ref[:, ls] = hi
        ys.append(_dot(xre_ref[:, ls].astype(BF16), cre_ref[s]) - _dot(xim_ref[:, ls].astype(BF16), cim_ref[s]))

    y = jnp.concatenate(ys, axis=-1)
    y_hi = y.astype(BF16)
    y_lo = (y - y_hi.astype(F32)).astype(BF16)
    y = _dot01(permt_ref[...], (y_hi, y_lo))
    o_ref[...] = _s5_glu(y, u, d_ref, wglu_ref, bglu_ref).reshape(nb, ct, S5_WIDTH).astype(BF16)


def _s5_weight_specs(layer):
    return [_layer_spec((S5_SLABS, LANES, 2 * 8 * S5_STATE), layer),
            _layer_spec((1, S5_LANES), layer), _layer_spec((1, S5_LANES), layer),
            _layer_spec((S5_SLABS, 8 * S5_STATE, LANES), layer), _layer_spec((S5_SLABS, 8 * S5_STATE, LANES), layer),
            _layer_spec((1, S5_WIDTH), layer), _layer_spec((S5_WIDTH, S5_WIDTH), layer),
            _layer_spec((1, S5_WIDTH), layer)]


def _s5_scan(su, weights, *, nb, length, layer):
    ct = S5_CHUNK
    perm = _time_major_perm(nb, ct)
    const = pl.BlockSpec((nb * ct, nb * ct), lambda i: (0, 0))
    return pl.pallas_call(
        functools.partial(_s5_scan_kernel, nb=nb, ct=ct),
        grid=(length // ct,),
        in_specs=[pl.BlockSpec((nb, ct, S5_WIDTH), lambda i: (0, i, 0)), const, const] + _s5_weight_specs(layer),
        out_specs=[pl.BlockSpec((nb, ct, S5_WIDTH), lambda i: (0, i, 0)),
                   pl.BlockSpec((nb, S5_LANES), lambda i: (0, 0)), pl.BlockSpec((nb, S5_LANES), lambda i: (0, 0))],
        out_shape=[jax.ShapeDtypeStruct((nb, length, S5_WIDTH), BF16),
                   jax.ShapeDtypeStruct((nb, S5_LANES), F32), jax.ShapeDtypeStruct((nb, S5_LANES), F32)],
        scratch_shapes=[pltpu.VMEM((nb * ct, S5_LANES), F32), pltpu.VMEM((nb * ct, S5_LANES), F32)],
        compiler_params=_cparams("arbitrary"),
        name="s5_scan",
    )(su, jnp.asarray(perm, BF16), jnp.asarray(perm.T, BF16), *weights)


def _s5_step_kernel(su_ref, h0r_ref, h0i_ref, win_ref, are_ref, aim_ref, cre_ref, cim_ref, d_ref, wglu_ref, bglu_ref,
                    o_ref, hre_ref, him_ref):
    u = su_ref[...]
    _s5_input(u.astype(BF16), win_ref, hre_ref, him_ref)
    ar, ai = are_ref[...], aim_ref[...]
    h0r, h0i = h0r_ref[...], h0i_ref[...]
    nr = ar * h0r - ai * h0i + hre_ref[...]
    ni = ar * h0i + ai * h0r + him_ref[...]
    hre_ref[...] = nr
    him_ref[...] = ni
    o_ref[...] = _s5_glu(_s5_readout(hre_ref, him_ref, cre_ref, cim_ref), u, d_ref, wglu_ref, bglu_ref).astype(BF16)


def _s5_step(su, h0r, h0i, weights, *, layer):
    n = su.shape[0]
    full = lambda w: pl.BlockSpec((n, w), lambda i: (0, 0))
    state = pl.BlockSpec((None, n, S5_LANES), lambda i: (layer, 0, 0))
    return pl.pallas_call(
        _s5_step_kernel,
        grid=(1,),
        in_specs=[full(S5_WIDTH), state, state] + _s5_weight_specs(layer),
        out_specs=[full(S5_WIDTH), full(S5_LANES), full(S5_LANES)],
        out_shape=[jax.ShapeDtypeStruct((n, S5_WIDTH), BF16),
                   jax.ShapeDtypeStruct((n, S5_LANES), F32), jax.ShapeDtypeStruct((n, S5_LANES), F32)],
        compiler_params=_cparams("arbitrary"),
        name="s5_step",
    )(su, h0r, h0i, *weights)


def _head_norm(o, g):
    parts = []
    for h in range(HEADS):
        seg = o[:, h * DV:(h + 1) * DV]
        parts.append(seg * lax.rsqrt(jnp.mean(seg * seg, axis=-1, keepdims=True) + EPS))
    return jnp.concatenate(parts, axis=-1) * g


def _route(logits, count):
    rows = logits.shape[0]
    lane = lax.broadcasted_iota(I32, logits.shape, 1)
    neg = -jnp.inf
    first = lambda hit: jnp.min(jnp.where(hit, lane, LANES), axis=-1, keepdims=True)
    glog = jnp.where(lane < MOE_GROUPS, logits, neg)
    gmax = jnp.max(glog, axis=-1, keepdims=True)
    gidx = first(glog == gmax)
    gw = 1.0 / jnp.sum(jnp.where(lane < MOE_GROUPS, jnp.exp(logits - gmax), 0.0), axis=-1, keepdims=True)
    inside = (lane >= MOE_GROUPS) & (lane < MOE_GROUPS + MOE_EXPERTS) & (((lane - MOE_GROUPS) >> 3) == gidx)
    el = jnp.where(inside, logits, neg)
    v1 = jnp.max(el, axis=-1, keepdims=True)
    i1 = first(el == v1)
    el2 = jnp.where(lane == i1, neg, el)
    v2 = jnp.max(el2, axis=-1, keepdims=True)
    i2 = first(el2 == v2)
    p2 = jnp.exp(v2 - v1)
    w1 = gw / (1.0 + p2)
    w2 = gw * p2 / (1.0 + p2)
    hit1, hit2 = lane == i1, lane == i2
    onehot = jnp.where(hit1 | hit2, 1.0, 0.0)
    blk = min(rows, 256)
    tr = lax.broadcasted_iota(I32, (blk, blk), 0)
    tc = lax.broadcasted_iota(I32, (blk, blk), 1)
    lower = jnp.where(tc < tr, 1.0, 0.0).astype(BF16)
    before = []
    for r0 in range(0, rows, blk):
        part = onehot[r0:r0 + blk, :]
        before.append(_dot(lower, part.astype(BF16)) + count)
        count = count + jnp.sum(part, axis=0, keepdims=True)
    before = jnp.concatenate(before, axis=0)
    rank1 = jnp.sum(jnp.where(hit1, before, 0.0), axis=-1, keepdims=True)
    rank2 = jnp.sum(jnp.where(hit2, before, 0.0), axis=-1, keepdims=True)
    slab = jnp.zeros(logits.shape, F32)
    for ln, val in ((R_E1, (i1 - MOE_GROUPS).astype(F32)), (R_E2, (i2 - MOE_GROUPS).astype(F32)), (R_W1, w1), (R_W2, w2),
                    (R_RANK1, rank1), (R_RANK2, rank2)):
        slab = jnp.where(lane == ln, val, slab)
    return slab, count


def _merge_kernel(x_ref, o_ref, gate_ref, os5_ref, cnt0_ref, nm_ref, wgm_ref, gn_ref, wbr_ref, wout_ref, nf_ref,
                  wr_ref, br_ref, *rest, sub):
    xm_ref, h2_ref, rt_ref, rtt_ref, cnt_ref = rest[-5:]

    @pl.when(pl.program_id(0) == 0)
    def _():
        cnt_ref[...] = cnt0_ref[...]

    logits = []
    for r0 in range(0, x_ref.shape[0], sub):
        rs = slice(r0, r0 + sub)
        x = x_ref[rs, :]
        hb = _rms(x, nm_ref[...]).astype(BF16)
        mixed = None
        for i, src in enumerate((0, None, 1)):
            if src is None:
                branch = os5_ref[rs, :]
            else:
                branch = (_head_norm(o_ref[src, rs, :].astype(F32), gn_ref[src]) * gate_ref[src, rs, :].astype(F32)).astype(BF16)
            gate = _sigmoid(_dot(hb, wgm_ref[:, i * D_MODEL:(i + 1) * D_MODEL]))
            term = gate * _dot(branch, wbr_ref[i])
            mixed = term if mixed is None else mixed + term
        xm = x + _dot(mixed.astype(BF16), wout_ref[...])
        xm_ref[rs, :] = xm
        h2 = _rms(xm, nf_ref[...])
        h2_hi = h2.astype(BF16)
        h2_lo = (h2 - h2_hi.astype(F32)).astype(BF16)
        h2_ref[rs, :] = h2_hi
        both = _dot(h2_hi, wr_ref[...])
        logits.append(both[:, :LANES] + (both[:, LANES:] + _dot(h2_lo, wr_ref[:, :LANES])) + br_ref[...])
    slab, cnt_ref[...] = _route(jnp.concatenate(logits, axis=0), cnt_ref[...])
    rt_ref[...] = slab
    rtt_ref[...] = jnp.transpose(slab)[0:8, :]


def _merge(x, o, gate, os5, cnt0, weights, carried, *, layer, tm, total_rows, block_offset):
    rows = x.shape[0]
    tile = lambda w: pl.BlockSpec((tm, w), lambda i: (i, 0))
    shared = lambda w: pl.BlockSpec((tm, w), lambda i: (i + block_offset, 0))
    carried = [] if carried is None else list(carried)
    w_specs = [_layer_spec((1, D_MODEL), layer), _layer_spec((D_MODEL, 3 * D_MODEL), layer),
               _layer_spec((2, 1, VW), layer), _layer_spec((3, VW, D_MODEL), layer),
               _layer_spec((D_MODEL, D_MODEL), layer), _layer_spec((1, D_MODEL), layer),
               _layer_spec((D_MODEL, 2 * LANES), layer), _layer_spec((1, LANES), layer)]
    first = 5 + len(w_specs)
    return pl.pallas_call(
        functools.partial(_merge_kernel, sub=min(tm, 256)),
        grid=(rows // tm,),
        in_specs=[tile(D_MODEL),
                  pl.BlockSpec((2, tm, VW), lambda i: (0, i, 0)),
                  pl.BlockSpec((2, tm, VW), lambda i: (0, i, 0)),
                  tile(S5_WIDTH),
                  pl.BlockSpec((1, LANES), lambda i: (0, 0))] + w_specs + [pl.BlockSpec(memory_space=pl.ANY)] * len(carried),
        out_specs=[tile(D_MODEL), shared(D_MODEL), shared(LANES),
                   pl.BlockSpec((8, tm), lambda i: (0, i + block_offset)), pl.BlockSpec((1, LANES), lambda i: (0, 0))],
        out_shape=[jax.ShapeDtypeStruct((rows, D_MODEL), F32),
                   jax.ShapeDtypeStruct((total_rows, D_MODEL), BF16),
                   jax.ShapeDtypeStruct((total_rows, LANES), F32),
                   jax.ShapeDtypeStruct((8, total_rows), F32),
                   jax.ShapeDtypeStruct((1, LANES), F32)],
        input_output_aliases={first + j: 1 + j for j in range(len(carried))},
        compiler_params=_cparams("arbitrary"),
        name="merge_route",
    )(x, o, gate, os5, cnt0, *weights, *carried)


def _ffn_kernel(te_ref, nu_ref, ea_ref, eb_ref, par_ref, x_ref, wga_ref, wua_ref, wda_ref, wgb_ref, wub_ref, wdb_ref,
                *rest, tile0):
    y_ref, wg_bf, wu_bf, wd_bf = rest[-4:]
    i = pl.program_id(0)
    tile = i + tile0

    @pl.when(tile < nu_ref[0])
    def _():
        new_run = (i == 0) | (te_ref[tile] != te_ref[jnp.maximum(tile - 1, 0)])
        for parity, (wg_ref, wu_ref, wd_ref) in enumerate(((wga_ref, wua_ref, wda_ref), (wgb_ref, wub_ref, wdb_ref))):
            @pl.when(new_run & (par_ref[tile] == parity))
            def _():
                wg_bf[...] = wg_ref[...].astype(BF16)
                wu_bf[...] = wu_ref[...].astype(BF16)
                wd_bf[...] = wd_ref[...].astype(BF16)

        x = x_ref[...]
        act = _silu(_dot(x, wg_bf[...])) * _dot(x, wu_bf[...])
        y_ref[...] = _dot(act.astype(BF16), wd_bf[...]).astype(BF16)


def _grouped_ffn(plan, xg, y_prev, wg, wu, wd, *, layer, tile0, total_rows):
    used = lambda i, nu: jnp.maximum(jnp.minimum(i + tile0, nu[0] - 1), tile0)
    win_a = lambda a, b: pl.BlockSpec((None, None, a, b), lambda i, te, nu, ea, eb, par: (layer, ea[used(i, nu)], 0, 0))
    win_b = lambda a, b: pl.BlockSpec((None, None, a, b), lambda i, te, nu, ea, eb, par: (layer, eb[used(i, nu)], 0, 0))
    carried = [] if y_prev is None else [y_prev]
    shapes = ((D_MODEL, MOE_FF), (D_MODEL, MOE_FF), (MOE_FF, D_MODEL))
    grid_spec = pltpu.PrefetchScalarGridSpec(
        num_scalar_prefetch=5,
        grid=(xg.shape[0] // MOE_TILE,),
        in_specs=[pl.BlockSpec((MOE_TILE, D_MODEL), lambda i, te, nu, *_: (used(i, nu) - tile0, 0))]
        + [win_a(*s) for s in shapes] + [win_b(*s) for s in shapes]
        + [pl.BlockSpec(memory_space=pl.ANY)] * len(carried),
        out_specs=pl.BlockSpec((MOE_TILE, D_MODEL), lambda i, te, nu, *_: (used(i, nu), 0)),
        scratch_shapes=[pltpu.VMEM(s, BF16) for s in shapes],
    )
    return pl.pallas_call(
        functools.partial(_ffn_kernel, tile0=tile0),
        grid_spec=grid_spec,
        out_shape=jax.ShapeDtypeStruct((total_rows, D_MODEL), BF16),
        input_output_aliases={12: 0} if carried else {},
        compiler_params=_cparams("arbitrary"),
        name="expert_ffn",
    )(*plan, xg, wg, wu, wd, wg, wu, wd, *carried)


@compute_on("tpu_sparsecore")
@jax.jit
def _take_rows(x, idx):
    return jnp.take(x, idx, axis=0, mode="clip")


def _piece_cuts(n):
    inner = sorted({min(max(round(f * n), 1), n - 1) for f in PIPE_CUTS}) if n > 1 else []
    return [0] + inner + [n]


def _moe_plan(route_t, count):
    tokens = route_t.shape[1]
    n_tiles = -(-(2 * tokens + MOE_EXPERTS * (MOE_TILE - 1)) // MOE_TILE)
    n_rows = n_tiles * MOE_TILE
    counts = count[0, MOE_GROUPS:MOE_GROUPS + MOE_EXPERTS].astype(I32)
    padded = ((counts + MOE_TILE - 1) // MOE_TILE) * MOE_TILE
    gend = jnp.cumsum(padded)
    gstart = gend - padded
    experts = jnp.arange(MOE_EXPERTS, dtype=I32)[:, None]

    def rows_of(e_lane, rank_lane):
        e = route_t[e_lane].astype(I32)
        return jnp.sum(jnp.where(e[None, :] == experts, gstart[:, None], 0), axis=0) + route_t[rank_lane].astype(I32)

    pos1, pos2 = rows_of(R_E1, R_RANK1), rows_of(R_E2, R_RANK2)
    token = jnp.arange(tokens, dtype=I32) + 1
    marked = jnp.zeros((n_rows,), I32).at[jnp.concatenate([pos1, pos2])].add(jnp.concatenate([token, token]),
                                                                             unique_indices=True)
    filler = jnp.arange(n_rows, dtype=I32) % tokens
    src = jnp.where(marked > 0, marked - 1, filler)
    tile_start = jnp.arange(n_tiles, dtype=I32) * MOE_TILE
    tile_expert = jnp.minimum(jnp.sum((tile_start[:, None] >= gend[None, :]).astype(I32), axis=1), MOE_EXPERTS - 1)
    n_used = (gend[-1] // MOE_TILE).astype(I32).reshape(1)
    ids = jnp.arange(MOE_EXPERTS, dtype=I32)
    present = padded > 0
    parity_e = (jnp.cumsum(present.astype(I32)) - 1) & 1
    later = jnp.min(jnp.where(present[None, :] & (ids[None, :] > ids[:, None]), ids[None, :], MOE_EXPERTS), axis=1)
    next_e = jnp.where(later == MOE_EXPERTS, ids, later)
    per_tile = lambda v: jnp.sum(jnp.where(tile_expert[:, None] == ids[None, :], v[None, :], 0), axis=1)
    window_a = per_tile(jnp.where(parity_e == 0, ids, next_e))
    window_b = per_tile(jnp.where(parity_e == 1, ids, next_e))
    return src, pos1, pos2, (tile_expert, n_used, window_a, window_b, per_tile(parity_e))


def _ple_kernel(xm_ref, y1_ref, y2_ref, rt_ref, p_ref, np_ref, wg_ref, wp_ref, nfin_ref, *rest, final):
    out_ref = rest[-1]
    rt = rt_ref[...]
    x1 = xm_ref[...] + rt[:, R_W1:R_W1 + 1] * y1_ref[...].astype(F32) + rt[:, R_W2:R_W2 + 1] * y2_ref[...].astype(F32)
    gate = _sigmoid(_dot(_rms(x1, np_ref[...]).astype(BF16), wg_ref[...]))
    x2 = x1 + gate * _dot(p_ref[...].astype(BF16), wp_ref[...])
    out_ref[...] = _rms(x2, nfin_ref[...]) if final else x2


def _ple(xm, y1, y2, route, p, x_prev, npl, wg, wp, nfin, *, layer, final, tm, n_blocks, x_off, y_off, rt_off):
    at = lambda w, off: pl.BlockSpec((tm, w), lambda i: (i + off, 0))
    carried = [] if x_prev is None else [x_prev]
    return pl.pallas_call(
        functools.partial(_ple_kernel, final=final),
        grid=(n_blocks,),
        in_specs=[at(D_MODEL, x_off), at(D_MODEL, y_off), at(D_MODEL, y_off), at(LANES, rt_off),
                  pl.BlockSpec((None, tm, PLE_DIM), lambda i: (layer, i + x_off, 0)),
                  _layer_spec((1, D_MODEL), layer), _layer_spec((D_MODEL, D_MODEL), layer),
                  _layer_spec((PLE_DIM, D_MODEL), layer), pl.BlockSpec((1, D_MODEL), lambda i: (0, 0))]
        + [pl.BlockSpec(memory_space=pl.ANY)] * len(carried),
        out_specs=at(D_MODEL, x_off),
        out_shape=jax.ShapeDtypeStruct(xm.shape, F32),
        input_output_aliases={9: 0} if carried else {},
        compiler_params=_cparams("parallel"),
        name="combine_ple",
    )(xm, y1, y2, route, p, npl, wg, wp, nfin, *carried)


def kernel(x_prompt, x_sample, state_gla, state_s5_re, state_s5_im, state_hgrn, p_prompt, p_sample, norm_mix, w_in, gla_w_gate2, gla_b_gate, gla_norm, s5_a_re, s5_a_im, s5_log_dt, s5_b_re, s5_b_im, s5_c_re, s5_c_im, s5_d, s5_w_glu, s5_b_glu, hgrn_lb_logits, hgrn_norm, w_br_gla, w_br_s5, w_br_hgrn, w_out, norm_ffn, moe_w_group, moe_b_group, moe_w_expert, moe_b_expert, moe_w_gate, moe_w_up, moe_w_down, norm_ple, w_ple_gate, w_ple_proj, norm_final):
    nb, length, _ = x_prompt.shape
    ns = x_sample.shape[0]
    n_p = nb * length
    n_all = n_p + ns
    row = lambda t: t.reshape(DEPTH, 1, -1)

    w_a, w_gm = _w_in_layout(w_in)
    wg2 =jnp.concatenate([gla_w_gate2, jnp.zeros((DEPTH, LANES - GLA_RANK, QK), F32)], axis=1).astype(BF16)
    s5_win, s5_are, s5_aim = _s5_discretise(s5_a_re, s5_a_im, s5_log_dt, s5_b_re, s5_b_im)
    s5_w = (s5_win, s5_are, s5_aim, _s5_blockdiag_out(s5_c_re), _s5_blockdiag_out(s5_c_im), row(s5_d),
            s5_w_glu.astype(BF16), row(s5_b_glu))
    head_gain = jnp.stack([jnp.tile(gla_norm, (1, HEADS)), jnp.tile(hgrn_norm, (1, HEADS))], axis=1).reshape(DEPTH, 2, 1, VW)
    w_router = jnp.concatenate([moe_w_group, moe_w_expert,
                                jnp.zeros((DEPTH, D_MODEL, LANES - MOE_GROUPS - MOE_EXPERTS), F32)], axis=-1)
    wr_hi = w_router.astype(BF16)
    wr_lo = (w_router - wr_hi.astype(F32)).astype(BF16)
    wr_hilo = jnp.concatenate([wr_hi, wr_lo], axis=-1)
    b_router = jnp.concatenate([moe_b_group, moe_b_expert,
                                jnp.zeros((DEPTH, LANES - MOE_GROUPS - MOE_EXPERTS), F32)], axis=-1).reshape(DEPTH, 1, LANES)
    merge_w = (row(norm_mix), w_gm, head_gain,
               jnp.stack([w_br_gla, w_br_s5, w_br_hgrn], axis=1).astype(BF16), w_out.astype(BF16), row(norm_ffn),
               wr_hilo, b_router)
    ple_w = (row(norm_ple), w_ple_gate.astype(BF16), w_ple_proj.astype(BF16), norm_final.reshape(1, D_MODEL))
    nm, bg = row(norm_mix), row(gla_b_gate)
    p_p, p_s = p_prompt.reshape(DEPTH, n_p, PLE_DIM), p_sample.reshape(DEPTH, ns, PLE_DIM)
    sg_in, sh_in = state_gla.reshape(DEPTH, ns, QK, DV), state_hgrn.reshape(DEPTH, ns, QK, DV)
    s5r_in, s5i_in = state_s5_re.reshape(DEPTH, ns, S5_LANES), state_s5_im.reshape(DEPTH, ns, S5_LANES)

    xp = x_prompt.reshape(n_p, D_MODEL)
    xs = x_sample.reshape(ns, D_MODEL)
    new_p, new_s5 = [], []
    new_s = None
    for i in range(DEPTH):
        qk, v, gate, g, su = _inproj(xp, nm, w_a, wg2, bg, hgrn_lb_logits, layer=i, tm=1024)
        o_p, st_p = _chunk_scan(qk.reshape(2, nb, length, -1), v.reshape(2, nb, length, -1),
                                g.reshape(2, nb, length, -1), nb=nb, length=length)
        os5_p, hre_p, him_p = _s5_scan(su.reshape(nb, length, -1), s5_w, nb=nb, length=length, layer=i)
        xm_p, h2, rt, rtt, cnt = _merge(xp, o_p.reshape(2, n_p, VW), gate, os5_p.reshape(n_p, -1),
                                        jnp.zeros((1, LANES), F32), merge_w, None, layer=i, tm=1024, total_rows=n_all,
                                        block_offset=0)
        new_p.append((st_p[0].reshape(nb, HEADS, DK, DV), hre_p.reshape(nb, S5_GROUPS, S5_STATE),
                      him_p.reshape(nb, S5_GROUPS, S5_STATE), st_p[1].reshape(nb, HEADS, DK, DV)))

        qk, v, gate, g, su = _inproj(xs, nm, w_a, wg2, bg, hgrn_lb_logits, layer=i, tm=ns)
        o_s, *new_s = _decode(qk, v, g, sg_in, sh_in, new_s, layer=i)
        os5_s, hre_s, him_s = _s5_step(su, s5r_in, s5i_in, s5_w, layer=i)
        xm_s, h2, rt, rtt, cnt = _merge(xs, o_s, gate, os5_s, cnt, merge_w, (h2, rt, rtt), layer=i, tm=ns,
                                        total_rows=n_all, block_offset=n_p // ns)
        new_s5.append((hre_s.reshape(ns, S5_GROUPS, S5_STATE), him_s.reshape(ns, S5_GROUPS, S5_STATE)))

        src, pos1, pos2, plan = _moe_plan(rtt, cnt)
        n_tiles = plan[0].shape[0]
        cuts = _piece_cuts(n_tiles)
        y = None
        for t0, t1 in zip(cuts[:-1], cuts[1:]):
            y = _grouped_ffn(plan, _take_rows(h2, src[t0 * MOE_TILE:t1 * MOE_TILE]), y,
                             moe_w_gate, moe_w_up, moe_w_down, layer=i, tile0=t0, total_rows=n_tiles * MOE_TILE)

        final = i == DEPTH - 1
        tm = 512
        cuts = _piece_cuts(n_p // tm)
        x_new = None
        for b0, b1 in zip(cuts[:-1], cuts[1:]):
            r0, r1 = b0 * tm, (n_all if b1 == cuts[-1] else b1 * tm)
            y1, y2 = _take_rows(y, pos1[r0:r1]), _take_rows(y, pos2[r0:r1])
            x_new = _ple(xm_p, y1, y2, rt, p_p, x_new, *ple_w, layer=i, final=final, tm=tm, n_blocks=b1 - b0,
                         x_off=b0, y_off=0, rt_off=b0)
        xp = x_new
        xs = _ple(xm_s, y1, y2, rt, p_s, None, *ple_w, layer=i, final=final, tm=ns, n_blocks=1,
                  x_off=0, y_off=(n_p - r0) // ns, rt_off=n_p // ns)

    stack = lambda items, j: jnp.stack([it[j] for it in items])
    return (xp.reshape(nb, length, D_MODEL), xs.reshape(ns, 1, D_MODEL),
            stack(new_p, 0), stack(new_p, 1), stack(new_p, 2), stack(new_p, 3),
            new_s[0].reshape(DEPTH, ns, HEADS, DK, DV), stack(new_s5, 0), stack(new_s5, 1),
            new_s[1].reshape(DEPTH, ns, HEADS, DK, DV))
```

```python
import functools

import jax
import jax.numpy as jnp
import numpy as np
from jax import lax
from jax.experimental import pallas as pl
from jax.experimental.compute_on import compute_on
from jax.experimental.pallas import tpu as pltpu

F32, BF16, I32 = jnp.float32, jnp.bfloat16, jnp.int32

D_MODEL = 1024
DEPTH = 2
HEADS, DK, DV = 4, 64, 128
QK, VW = HEADS * DK, HEADS * DV
GLA_RANK, GLA_TAU = 16, 16.0
S5_WIDTH, S5_GROUP, S5_GROUPS, S5_STATE = 512, 16, 32, 64
S5_LANES = S5_GROUPS * S5_STATE
S5_SLABS = 4
MOE_GROUPS, MOE_PER_GROUP, MOE_EXPERTS, MOE_FF = 4, 8, 32, 256
PLE_DIM = 256
EPS = 1e-6

LANES = 128
CHUNK = 64
S5_CHUNK = 64
MOE_TILE = 384
PIPE_CUTS = (0.5,)
SAFE_EXP = 80.0
VMEM_LIMIT = 56 * 1024 * 1024

C_GLA, C_S5, C_HG, C_GLR, C_END = 0, 1536, 2048, 3584, 3712
R_E1, R_E2, R_W1, R_W2, R_RANK1, R_RANK2 = 0, 1, 2, 3, 4, 5


def _cparams(*sem):
    return pltpu.CompilerParams(dimension_semantics=sem, vmem_limit_bytes=VMEM_LIMIT)


def _layer_spec(shape, layer):
    return pl.BlockSpec((None,) + tuple(shape), lambda *_: (layer,) + (0,) * len(shape), pipeline_mode=pl.Buffered(1))


def _dot(a, b):
    return jnp.dot(a, b, preferred_element_type=F32)


def _rms(x, g):
    return x * lax.rsqrt(jnp.mean(x * x, axis=-1, keepdims=True) + EPS) * g


def _log_sigmoid(x):
    return jnp.minimum(x, 0.0) - jnp.log1p(jnp.exp(-jnp.abs(x)))


def _sigmoid(x):
    return 0.5 * jnp.tanh(0.5 * x) + 0.5


def _silu(x):
    return x * _sigmoid(x)


def _split3(x):
    hi = x.astype(BF16)
    r1 = x - hi.astype(F32)
    mid = r1.astype(BF16)
    lo = (r1 - mid.astype(F32)).astype(BF16)
    return hi, mid, lo


def _dot01(m, parts):
    out = _dot(m, parts[0])
    for p in parts[1:]:
        out = out + _dot(m, p)
    return out


W_GLA, W_GLR, W_S5HG, W_END = 1536, 1552, 3600, 6672


def _w_in_layout_kernel(wt_ref, wa_ref, wgm_ref):
    cols = lambda lo, hi: jnp.transpose(wt_ref[lo:hi, :])
    wa_ref[:, C_GLA:C_S5] = cols(0, W_GLA).astype(BF16)
    wa_ref[:, C_S5:C_GLR] = cols(W_GLR, W_S5HG).astype(BF16)
    lane = lax.broadcasted_iota(I32, (LANES, LANES), 1)
    wa_ref[:, C_GLR:C_END] = jnp.where(lane < GLA_RANK, cols(W_GLA, W_GLA + LANES), 0.0).astype(BF16)
    wgm_ref[...] = cols(W_S5HG, W_END).astype(BF16)


def _w_in_layout(w_in):
    wt = jnp.swapaxes(w_in, 1, 2)
    return pl.pallas_call(
        _w_in_layout_kernel,
        grid=(DEPTH, D_MODEL // LANES),
        in_specs=[pl.BlockSpec((None, W_END, LANES), lambda d, i: (d, 0, i))],
        out_specs=[pl.BlockSpec((None, LANES, C_END), lambda d, i: (d, i, 0)),
                   pl.BlockSpec((None, LANES, 3 * D_MODEL), lambda d, i: (d, i, 0))],
        out_shape=[jax.ShapeDtypeStruct((DEPTH, D_MODEL, C_END), BF16),
                   jax.ShapeDtypeStruct((DEPTH, D_MODEL, 3 * D_MODEL), BF16)],
        compiler_params=_cparams("parallel", "parallel"),
        name="w_in_layout",
    )(wt)
def _inproj_kernel(x_ref, nm_ref, w_ref, wg2_ref, bg_ref, lbl_ref, qk_ref, v_ref, gate_ref, g_ref, su_ref, *, layer, sub):
    lg = lbl_ref[...]
    mx = jnp.max(lg, axis=0, keepdims=True)
    ex = jnp.exp(lg - mx)
    sm = ex / jnp.sum(ex, axis=0, keepdims=True)
    cs = sm[0:1]
    for j in range(1, layer + 1):
        cs = cs + sm[j:j + 1]
    lb = cs - sm[0:1]
    log_lb, log_1mlb = jnp.log(lb), jnp.log1p(-lb)

    for r0 in range(0, x_ref.shape[0], sub):
        rs = slice(r0, r0 + sub)
        hb = _rms(x_ref[rs, :], nm_ref[...]).astype(BF16)

        def proj(lo, hi):
            return _dot(hb, w_ref[:, lo:hi])

        qk_ref[0, rs, 0:QK] = (proj(0, 256) * (DK ** -0.5)).astype(BF16)
        qk_ref[0, rs, QK:2 * QK] = proj(256, 512).astype(BF16)
        v_ref[0, rs, :] = proj(512, 1024).astype(BF16)
        gate_ref[0, rs, :] = _silu(proj(1024, 1536)).astype(BF16)
        glr = proj(C_GLR, C_END).astype(BF16)
        g_ref[0, rs, :] = _log_sigmoid(_dot(glr, wg2_ref[...]) + bg_ref[...]) * (1.0 / GLA_TAU)

        su_ref[rs, :] = proj(C_S5, C_HG)

        z = proj(C_HG + 256, C_HG + 512)
        a, c = log_lb, log_1mlb + _log_sigmoid(z)
        g_ref[1, rs, :] = jnp.maximum(a, c) + jnp.log1p(jnp.exp(-jnp.abs(a - c)))
        qk_ref[1, rs, 0:QK] = _silu(proj(C_HG, C_HG + 256)).astype(BF16)
        qk_ref[1, rs, QK:2 * QK] = ((1.0 - lb) * _sigmoid(-z)).astype(BF16)
        v_ref[1, rs, :] = proj(C_HG + 512, C_HG + 1024).astype(BF16)
        gate_ref[1, rs, :] = _silu(proj(C_HG + 1024, C_HG + 1536)).astype(BF16)


def _inproj(x, nm, w, wg2, bg, lbl, *, layer, tm):
    rows = x.shape[0]
    return pl.pallas_call(
        functools.partial(_inproj_kernel, layer=layer, sub=min(tm, 256)),
        grid=(rows // tm,),
        in_specs=[pl.BlockSpec((tm, D_MODEL), lambda i: (i, 0)),
                  _layer_spec((1, D_MODEL), layer),
                  _layer_spec((D_MODEL, C_END), layer),
                  _layer_spec((LANES, QK), layer),
                  _layer_spec((1, QK), layer),
                  pl.BlockSpec((DEPTH, QK), lambda i: (0, 0))],
        out_specs=[pl.BlockSpec((2, tm, 2 * QK), lambda i: (0, i, 0)),
                   pl.BlockSpec((2, tm, VW), lambda i: (0, i, 0)),
                   pl.BlockSpec((2, tm, VW), lambda i: (0, i, 0)),
                   pl.BlockSpec((2, tm, QK), lambda i: (0, i, 0)),
                   pl.BlockSpec((tm, S5_WIDTH), lambda i: (i, 0))],
        out_shape=[jax.ShapeDtypeStruct((2, rows, 2 * QK), BF16),
                   jax.ShapeDtypeStruct((2, rows, VW), BF16),
                   jax.ShapeDtypeStruct((2, rows, VW), BF16),
                   jax.ShapeDtypeStruct((2, rows, QK), F32),
                   jax.ShapeDtypeStruct((rows, S5_WIDTH), F32)],
        compiler_params=_cparams("parallel"),
        name="inproj",
    )(x, nm, w, wg2, bg, lbl)


def _head_stack(x):
    head = lax.broadcasted_iota(I32, x.shape, 1) // DK
    return jnp.concatenate([jnp.where(head == h, x, 0.0) for h in range(HEADS)], axis=0).astype(BF16)


def _stack_scores(qt, kt):
    kt = kt.astype(BF16)
    return lax.dot_general(_head_stack(qt), jnp.concatenate([kt, kt], axis=0), (((1,), (1,)), ((), ())),
                           preferred_element_type=F32)


def _chunk_kernel(qk_ref, v_ref, g_ref, o_ref, st_ref, bc_ref, sc_ref, *, nb, c):
    assert c == DK and 2 * c == LANES

    @pl.when(pl.program_id(1) == 0)
    def _():
        st_ref[...] = jnp.zeros_like(st_ref)

    row = lax.broadcasted_iota(I32, (c, c), 0)
    col = lax.broadcasted_iota(I32, (c, c), 1)
    tri = jnp.where(col <= row, 1.0, 0.0).astype(BF16)
    srow = lax.broadcasted_iota(I32, (HEADS * c, 2 * c), 0) & (c - 1)
    scol = lax.broadcasted_iota(I32, (HEADS * c, 2 * c), 1) & (c - 1)
    mid = c // 2 - 1

    spread = None
    for b in range(nb):
        bc = _dot01(tri, _split3(g_ref[0, b]))
        bc_ref[b] = bc
        ref, last = bc[mid:mid + 1, :], bc[c - 1:c, :]
        s = jnp.maximum(jnp.max(-ref), jnp.max(ref - last))
        spread = s if spread is None else jnp.maximum(spread, s)

    def qk_of(b):
        return qk_ref[0, b, :, 0:QK].astype(F32), qk_ref[0, b, :, QK:2 * QK].astype(F32)

    def scores_one_reference():
        for b in range(nb):
            q, k = qk_of(b)
            bc = bc_ref[b]
            ref = bc[mid:mid + 1, :]
            s = _stack_scores(q * jnp.exp(bc - ref), k * jnp.exp(ref - bc))
            sc_ref[b] = jnp.where(scol <= srow, s, 0.0).astype(BF16)

    def scores_by_levels():
        qrow = lax.broadcasted_iota(I32, (c, QK), 0)
        for b in range(nb):
            q, k = qk_of(b)
            bc = bc_ref[b]
            parts = _split3(g_ref[0, b])
            acc = jnp.where(scol == srow, _stack_scores(q, k), 0.0)
            half = c // 2
            while half >= 1:
                blk = 2 * half
                last_low = (row & ~(blk - 1)) + (half - 1)
                ref = _dot01(jnp.where(col <= last_low, 1.0, 0.0).astype(BF16), parts)
                upper = (qrow & (blk - 1)) >= half
                dq = jnp.minimum(jnp.where(upper, bc - ref, 0.0), 0.0)
                dk = jnp.minimum(jnp.where(upper, 0.0, ref - bc), 0.0)
                s = _stack_scores(q * jnp.exp(dq), k * jnp.exp(dk))
                pair = ((srow & ~(blk - 1)) == (scol & ~(blk - 1))) & ((srow & (blk - 1)) >= half) & ((scol & (blk - 1)) < half)
                acc = acc + jnp.where(pair, s, 0.0)
                half //= 2
            sc_ref[b] = acc.astype(BF16)

    lax.cond(spread <= SAFE_EXP, scores_one_reference, scores_by_levels)

    for b in range(nb):
        q, k = qk_of(b)
        bc = bc_ref[b]
        last = bc[c - 1:c, :]
        v = v_ref[0, b]
        state = st_ref[0, b]
        state_bf = state.astype(BF16)
        q_in = (q * jnp.exp(bc)).astype(BF16)
        k_out = jnp.transpose(k * jnp.exp(last - bc)).astype(BF16)
        decay = jnp.transpose(jnp.broadcast_to(jnp.exp(last), (DV, QK)))
        lane = lax.broadcasted_iota(I32, (c, LANES), 1)
        for h in range(HEADS):
            rs, ls, ks = slice(h * c, (h + 1) * c), slice(h * DV, (h + 1) * DV), slice(h * DK, (h + 1) * DK)
            pair = q_in[:, (h // 2) * LANES:(h // 2 + 1) * LANES]
            if h % 2 == 0:
                lhs = jnp.where(lane < DK, pair, sc_ref[b, rs, :])
                rhs = jnp.concatenate([state_bf[ks, :], v[:, ls]], axis=0)
            else:
                lhs = jnp.where(lane >= DK, pair, sc_ref[b, rs, :])
                rhs = jnp.concatenate([v[:, ls], state_bf[ks, :]], axis=0)
            o_ref[0, b, :, ls] = _dot(lhs, rhs).astype(BF16)
            st_ref[0, b, ks, :] = decay[ks, :] * state[ks, :] + _dot(k_out[ks, :], v[:, ls])


def _chunk_scan(qk, v, g, *, nb, length):
    c = CHUNK
    blk = lambda w: pl.BlockSpec((1, nb, c, w), lambda br, i: (br, 0, i, 0))
    return pl.pallas_call(
        functools.partial(_chunk_kernel, nb=nb, c=c),
        grid=(2, length // c),
        in_specs=[blk(2 * QK), blk(VW), blk(QK)],
        out_specs=[blk(VW), pl.BlockSpec((1, nb, QK, DV), lambda br, i: (br, 0, 0, 0))],
        out_shape=[jax.ShapeDtypeStruct((2, nb, length, VW), BF16),
                   jax.ShapeDtypeStruct((2, nb, QK, DV), F32)],
        scratch_shapes=[pltpu.VMEM((nb, c, QK), F32), pltpu.VMEM((nb, HEADS * c, 2 * c), BF16)],
        compiler_params=_cparams("arbitrary", "arbitrary"),
        name="chunk_scan",
    )(qk, v, g)


def _decode_kernel(qk_ref, v_ref, g_ref, s0_ref, s1_ref, *rest, nt):
    o_ref, n0_ref, n1_ref = rest[-3:]
    for br, (s_ref, n_ref) in enumerate(((s0_ref, n0_ref), (s1_ref, n1_ref))):
        for j in range(nt):
            d = jnp.exp(g_ref[br, j:j + 1, :])
            q = qk_ref[br, j:j + 1, 0:QK].astype(F32)
            k = qk_ref[br, j:j + 1, QK:2 * QK].astype(F32)
            cols = jnp.transpose(jnp.concatenate([d, k, q, jnp.zeros((5, QK), F32)], axis=0))
            vrow = v_ref[br, j:j + 1, :].astype(F32)
            vfull = jnp.concatenate([jnp.broadcast_to(vrow[:, h * DV:(h + 1) * DV], (DK, DV)) for h in range(HEADS)], axis=0)
            new = cols[:, 0:1] * s_ref[j] + cols[:, 1:2] * vfull
            n_ref[j] = new
            t = cols[:, 2:3] * new
            for h in range(HEADS):
                o_ref[br, j:j + 1, h * DV:(h + 1) * DV] = jnp.sum(t[h * DK:(h + 1) * DK, :], axis=0, keepdims=True).astype(BF16)


def _decode(qk, v, g, s_gla, s_hg, prev, *, layer):
    n = qk.shape[1]
    nt = 8
    row = lambda w: pl.BlockSpec((2, nt, w), lambda i: (0, i, 0))
    st = pl.BlockSpec((None, nt, QK, DV), lambda i: (layer, i, 0, 0))
    carried = [] if prev is None else list(prev)
    first = 5
    return pl.pallas_call(
        functools.partial(_decode_kernel, nt=nt),
        grid=(n // nt,),
        in_specs=[row(2 * QK), row(VW), row(QK), st, st] + [pl.BlockSpec(memory_space=pl.ANY)] * len(carried),
        out_specs=[row(VW), st, st],
        out_shape=[jax.ShapeDtypeStruct((2, n, VW), BF16),
                   jax.ShapeDtypeStruct((DEPTH, n, QK, DV), F32),
                   jax.ShapeDtypeStruct((DEPTH, n, QK, DV), F32)],
        input_output_aliases={first + j: 1 + j for j in range(len(carried))},
        compiler_params=_cparams("parallel"),
        name="decode_step",
    )(qk, v, g, s_gla, s_hg, *carried)


def _s5_disc_kernel(lr_ref, li_ref, ldt_ref, br_ref, bi_ref, abr_ref, abi_ref, bbr_ref, bbi_ref):
    lr, li, dt = lr_ref[...], li_ref[...], jnp.exp(ldt_ref[...])
    mag = jnp.exp(lr * dt)
    ab_re, ab_im = mag * jnp.cos(li * dt), mag * jnp.sin(li * dt)
    den = lr * lr + li * li
    num_re = ab_re - 1.0
    coef_re = (num_re * lr + ab_im * li) / den
    coef_im = (ab_im * lr - num_re * li) / den
    br, bi = br_ref[...], bi_ref[...]
    abr_ref[...] = ab_re
    abi_ref[...] = ab_im
    bbr_ref[...] = coef_re * br - coef_im * bi
    bbi_ref[...] = coef_re * bi + coef_im * br


def _s5_discretise(a_re, a_im, log_dt, b_re, b_im):
    n = DEPTH * S5_GROUPS
    rep = lambda t: jnp.repeat(t.reshape(n, S5_STATE), S5_GROUP, axis=1)
    ldt = jnp.broadcast_to(log_dt.reshape(n, 1), (n, S5_STATE * S5_GROUP))
    shape = jax.ShapeDtypeStruct((n, S5_STATE * S5_GROUP), F32)
    ab_re, ab_im, bb_re, bb_im = pl.pallas_call(_s5_disc_kernel, out_shape=[shape] * 4, name="s5_discretise")(
        rep(a_re), rep(a_im), ldt, b_re.reshape(n, -1), b_im.reshape(n, -1))
    pole = lambda t: t[:, ::S5_GROUP].reshape(DEPTH, 1, S5_LANES)
    eye = jnp.eye(8, dtype=F32)

    def blockdiag_in(bb):
        t = bb.reshape(DEPTH, S5_SLABS, 8, S5_STATE, S5_GROUP).transpose(0, 1, 2, 4, 3)
        return jnp.einsum("dcgmp,gh->dcgmhp", t, eye).reshape(DEPTH, S5_SLABS, LANES, 8 * S5_STATE)

    w_in = jnp.concatenate([blockdiag_in(bb_re), blockdiag_in(bb_im)], axis=-1).astype(BF16)
    return w_in, pole(ab_re), pole(ab_im)


def _s5_blockdiag_out(c):
    t = c.reshape(DEPTH, S5_SLABS, 8, S5_GROUP, S5_STATE).transpose(0, 1, 2, 4, 3)
    return jnp.einsum("dcgpm,gh->dcgphm", t, jnp.eye(8, dtype=F32)).reshape(DEPTH, S5_SLABS, 8 * S5_STATE, LANES).astype(BF16)


def _time_major_perm(nb, ct):
    r = np.arange(nb * ct)
    p = np.zeros((nb * ct, nb * ct), np.float32)
    p[r, (r % nb) * ct + r // nb] = 1.0
    return p


def _gelu_tanh(y):
    return 0.5 * y * (1.0 + jnp.tanh(0.7978845608028654 * (y + 0.044715 * (y * y * y))))


def _s5_input(ub, win_ref, xre_ref, xim_ref):
    half = 8 * S5_STATE
    for s in range(S5_SLABS):
        r = _dot(ub[:, s * LANES:(s + 1) * LANES], win_ref[s])
        xre_ref[:, s * half:(s + 1) * half] = r[:, :half]
        xim_ref[:, s * half:(s + 1) * half] = r[:, half:]


def _s5_readout(xre_ref, xim_ref, cre_ref, cim_ref):
    half = 8 * S5_STATE
    ys = []
    for s in range(S5_SLABS):
        ls = slice(s * half, (s + 1) * half)
        ys.append(_dot(xre_ref[:, ls].astype(BF16), cre_ref[s]) - _dot(xim_ref[:, ls].astype(BF16), cim_ref[s]))
    return jnp.concatenate(ys, axis=-1)


def _s5_glu(y, u, d_ref, wglu_ref, bglu_ref):
    z = _gelu_tanh(y + d_ref[...] * u)
    return z * _sigmoid(_dot(z.astype(BF16), wglu_ref[...]) + bglu_ref[...])


def _s5_scan_kernel(su_ref, perm_ref, permt_ref, win_ref, are_ref, aim_ref, cre_ref, cim_ref, d_ref, wglu_ref, bglu_ref,
                    o_ref, hre_ref, him_ref, xre_ref, xim_ref, *, nb, ct):
    @pl.when(pl.program_id(0) == 0)
    def _():
        hre_ref[...] = jnp.zeros_like(hre_ref)
        him_ref[...] = jnp.zeros_like(him_ref)

    u = su_ref[...].reshape(nb * ct, S5_WIDTH)
    ub = _dot(perm_ref[...], u.astype(BF16)).astype(BF16)

    half = 8 * S5_STATE
    ys = []
    for s in range(S5_SLABS):
        ls = slice(s * half, (s + 1) * half)
        r = _dot(ub[:, s * LANES:(s + 1) * LANES], win_ref[s])
        xre_ref[:, ls] = r[:, :half]
        xim_ref[:, ls] = r[:, half:]
        ar = jnp.broadcast_to(are_ref[:, ls], (nb, half))
        ai = jnp.broadcast_to(aim_ref[:, ls], (nb, half))
        hr, hi = hre_ref[:, ls], him_ref[:, ls]
        for t in range(ct):
            rows = slice(t * nb, (t + 1) * nb)
            hr, hi = ar * hr - ai * hi + xre_ref[rows, ls], ar * hi + ai * hr + xim_ref[rows, ls]
            xre_ref[rows, ls] = hr
            xim_ref[rows, ls] = hi
        hre_ref[:, ls] = hr
        him_ref[:, ls] = hi
        ys.append(_dot(xre_ref[:, ls].astype(BF16), cre_ref[s]) - _dot(xim_ref[:, ls].astype(BF16), cim_ref[s]))

    y = jnp.concatenate(ys, axis=-1)
    y_hi = y.astype(BF16)
    y_lo = (y - y_hi.astype(F32)).astype(BF16)
    y = _dot01(permt_ref[...], (y_hi, y_lo))
    o_ref[...] = _s5_glu(y, u, d_ref, wglu_ref, bglu_ref).reshape(nb, ct, S5_WIDTH).astype(BF16)


def _s5_weight_specs(layer):
    return [_layer_spec((S5_SLABS, LANES, 2 * 8 * S5_STATE), layer),
            _layer_spec((1, S5_LANES), layer), _layer_spec((1, S5_LANES), layer),
            _layer_spec((S5_SLABS, 8 * S5_STATE, LANES), layer), _layer_spec((S5_SLABS, 8 * S5_STATE, LANES), layer),
            _layer_spec((1, S5_WIDTH), layer), _layer_spec((S5_WIDTH, S5_WIDTH), layer),
            _layer_spec((1, S5_WIDTH), layer)]


def _s5_scan(su, weights, *, nb, length, layer):
    ct = S5_CHUNK
    perm = _time_major_perm(nb, ct)
    const = pl.BlockSpec((nb * ct, nb * ct), lambda i: (0, 0))
    return pl.pallas_call(
        functools.partial(_s5_scan_kernel, nb=nb, ct=ct),
        grid=(length // ct,),
        in_specs=[pl.BlockSpec((nb, ct, S5_WIDTH), lambda i: (0, i, 0)), const, const] + _s5_weight_specs(layer),
        out_specs=[pl.BlockSpec((nb, ct, S5_WIDTH), lambda i: (0, i, 0)),
                   pl.BlockSpec((nb, S5_LANES), lambda i: (0, 0)), pl.BlockSpec((nb, S5_LANES), lambda i: (0, 0))],
        out_shape=[jax.ShapeDtypeStruct((nb, length, S5_WIDTH), BF16),
                   jax.ShapeDtypeStruct((nb, S5_LANES), F32), jax.ShapeDtypeStruct((nb, S5_LANES), F32)],
        scratch_shapes=[pltpu.VMEM((nb * ct, S5_LANES), F32), pltpu.VMEM((nb * ct, S5_LANES), F32)],
        compiler_params=_cparams("arbitrary"),
        name="s5_scan",
    )(su, jnp.asarray(perm, BF16), jnp.asarray(perm.T, BF16), *weights)


def _s5_step_kernel(su_ref, h0r_ref, h0i_ref, win_ref, are_ref, aim_ref, cre_ref, cim_ref, d_ref, wglu_ref, bglu_ref,
                    o_ref, hre_ref, him_ref):
    u = su_ref[...]
    _s5_input(u.astype(BF16), win_ref, hre_ref, him_ref)
    ar, ai = are_ref[...], aim_ref[...]
    h0r, h0i = h0r_ref[...], h0i_ref[...]
    nr = ar * h0r - ai * h0i + hre_ref[...]
    ni = ar * h0i + ai * h0r + him_ref[...]
    hre_ref[...] = nr
    him_ref[...] = ni
    o_ref[...] = _s5_glu(_s5_readout(hre_ref, him_ref, cre_ref, cim_ref), u, d_ref, wglu_ref, bglu_ref).astype(BF16)


def _s5_step(su, h0r, h0i, weights, *, layer):
    n = su.shape[0]
    full = lambda w: pl.BlockSpec((n, w), lambda i: (0, 0))
    state = pl.BlockSpec((None, n, S5_LANES), lambda i: (layer, 0, 0))
    return pl.pallas_call(
        _s5_step_kernel,
        grid=(1,),
        in_specs=[full(S5_WIDTH), state, state] + _s5_weight_specs(layer),
        out_specs=[full(S5_WIDTH), full(S5_LANES), full(S5_LANES)],
        out_shape=[jax.ShapeDtypeStruct((n, S5_WIDTH), BF16),
                   jax.ShapeDtypeStruct((n, S5_LANES), F32), jax.ShapeDtypeStruct((n, S5_LANES), F32)],
        compiler_params=_cparams("arbitrary"),
        name="s5_step",
    )(su, h0r, h0i, *weights)


def _head_norm(o, g):
    parts = []
    for h in range(HEADS):
        seg = o[:, h * DV:(h + 1) * DV]
        parts.append(seg * lax.rsqrt(jnp.mean(seg * seg, axis=-1, keepdims=True) + EPS))
    return jnp.concatenate(parts, axis=-1) * g


def _route(logits, count):
    rows = logits.shape[0]
    lane = lax.broadcasted_iota(I32, logits.shape, 1)
    neg = -jnp.inf
    first = lambda hit: jnp.min(jnp.where(hit, lane, LANES), axis=-1, keepdims=True)
    glog = jnp.where(lane < MOE_GROUPS, logits, neg)
    gmax = jnp.max(glog, axis=-1, keepdims=True)
    gidx = first(glog == gmax)
    gw = 1.0 / jnp.sum(jnp.where(lane < MOE_GROUPS, jnp.exp(logits - gmax), 0.0), axis=-1, keepdims=True)
    inside = (lane >= MOE_GROUPS) & (lane < MOE_GROUPS + MOE_EXPERTS) & (((lane - MOE_GROUPS) >> 3) == gidx)
    el = jnp.where(inside, logits, neg)
    v1 = jnp.max(el, axis=-1, keepdims=True)
    i1 = first(el == v1)
    el2 = jnp.where(lane == i1, neg, el)
    v2 = jnp.max(el2, axis=-1, keepdims=True)
    i2 = first(el2 == v2)
    p2 = jnp.exp(v2 - v1)
    w1 = gw / (1.0 + p2)
    w2 = gw * p2 / (1.0 + p2)
    hit1, hit2 = lane == i1, lane == i2
    onehot = jnp.where(hit1 | hit2, 1.0, 0.0)
    blk = min(rows, 256)
    tr = lax.broadcasted_iota(I32, (blk, blk), 0)
    tc = lax.broadcasted_iota(I32, (blk, blk), 1)
    lower = jnp.where(tc < tr, 1.0, 0.0).astype(BF16)
    before = []
    for r0 in range(0, rows, blk):
        part = onehot[r0:r0 + blk, :]
        before.append(_dot(lower, part.astype(BF16)) + count)
        count = count + jnp.sum(part, axis=0, keepdims=True)
    before = jnp.concatenate(before, axis=0)
    rank1 = jnp.sum(jnp.where(hit1, before, 0.0), axis=-1, keepdims=True)
    rank2 = jnp.sum(jnp.where(hit2, before, 0.0), axis=-1, keepdims=True)
    slab = jnp.zeros(logits.shape, F32)
    for ln, val in ((R_E1, (i1 - MOE_GROUPS).astype(F32)), (R_E2, (i2 - MOE_GROUPS).astype(F32)), (R_W1, w1), (R_W2, w2),
                    (R_RANK1, rank1), (R_RANK2, rank2)):
        slab = jnp.where(lane == ln, val, slab)
    return slab, count


def _merge_kernel(x_ref, o_ref, gate_ref, os5_ref, cnt0_ref, nm_ref, wgm_ref, gn_ref, wbr_ref, wout_ref, nf_ref,
                  wr_ref, br_ref, *rest, sub):
    xm_ref, h2_ref, rt_ref, rtt_ref, cnt_ref = rest[-5:]

    @pl.when(pl.program_id(0) == 0)
    def _():
        cnt_ref[...] = cnt0_ref[...]

    logits = []
    for r0 in range(0, x_ref.shape[0], sub):
        rs = slice(r0, r0 + sub)
        x = x_ref[rs, :]
        hb = _rms(x, nm_ref[...]).astype(BF16)
        mixed = None
        for i, src in enumerate((0, None, 1)):
            if src is None:
                branch = os5_ref[rs, :]
            else:
                branch = (_head_norm(o_ref[src, rs, :].astype(F32), gn_ref[src]) * gate_ref[src, rs, :].astype(F32)).astype(BF16)
            gate = _sigmoid(_dot(hb, wgm_ref[:, i * D_MODEL:(i + 1) * D_MODEL]))
            term = gate * _dot(branch, wbr_ref[i])
            mixed = term if mixed is None else mixed + term
        xm = x + _dot(mixed.astype(BF16), wout_ref[...])
        xm_ref[rs, :] = xm
        h2 = _rms(xm, nf_ref[...])
        h2_hi = h2.astype(BF16)
        h2_lo = (h2 - h2_hi.astype(F32)).astype(BF16)
        h2_ref[rs, :] = h2_hi
        both = _dot(h2_hi, wr_ref[...])
        logits.append(both[:, :LANES] + (both[:, LANES:] + _dot(h2_lo, wr_ref[:, :LANES])) + br_ref[...])
    slab, cnt_ref[...] = _route(jnp.concatenate(logits, axis=0), cnt_ref[...])
    rt_ref[...] = slab
    rtt_ref[...] = jnp.transpose(slab)[0:8, :]


def _merge(x, o, gate, os5, cnt0, weights, carried, *, layer, tm, total_rows, block_offset):
    rows = x.shape[0]
    tile = lambda w: pl.BlockSpec((tm, w), lambda i: (i, 0))
    shared = lambda w: pl.BlockSpec((tm, w), lambda i: (i + block_offset, 0))
    carried = [] if carried is None else list(carried)
    w_specs = [_layer_spec((1, D_MODEL), layer), _layer_spec((D_MODEL, 3 * D_MODEL), layer),
               _layer_spec((2, 1, VW), layer), _layer_spec((3, VW, D_MODEL), layer),
               _layer_spec((D_MODEL, D_MODEL), layer), _layer_spec((1, D_MODEL), layer),
               _layer_spec((D_MODEL, 2 * LANES), layer), _layer_spec((1, LANES), layer)]
    first = 5 + len(w_specs)
    return pl.pallas_call(
        functools.partial(_merge_kernel, sub=min(tm, 256)),
        grid=(rows // tm,),
        in_specs=[tile(D_MODEL),
                  pl.BlockSpec((2, tm, VW), lambda i: (0, i, 0)),
                  pl.BlockSpec((2, tm, VW), lambda i: (0, i, 0)),
                  tile(S5_WIDTH),
                  pl.BlockSpec((1, LANES), lambda i: (0, 0))] + w_specs + [pl.BlockSpec(memory_space=pl.ANY)] * len(carried),
        out_specs=[tile(D_MODEL), shared(D_MODEL), shared(LANES),
                   pl.BlockSpec((8, tm), lambda i: (0, i + block_offset)), pl.BlockSpec((1, LANES), lambda i: (0, 0))],
        out_shape=[jax.ShapeDtypeStruct((rows, D_MODEL), F32),
                   jax.ShapeDtypeStruct((total_rows, D_MODEL), BF16),
                   jax.ShapeDtypeStruct((total_rows, LANES), F32),
                   jax.ShapeDtypeStruct((8, total_rows), F32),
                   jax.ShapeDtypeStruct((1, LANES), F32)],
        input_output_aliases={first + j: 1 + j for j in range(len(carried))},
        compiler_params=_cparams("arbitrary"),
        name="merge_route",
    )(x, o, gate, os5, cnt0, *weights, *carried)


def _ffn_kernel(te_ref, nu_ref, ea_ref, eb_ref, par_ref, x_ref, wga_ref, wua_ref, wda_ref, wgb_ref, wub_ref, wdb_ref,
                *rest, tile0):
    y_ref, wg_bf, wu_bf, wd_bf = rest[-4:]
    i = pl.program_id(0)
    tile = i + tile0

    @pl.when(tile < nu_ref[0])
    def _():
        new_run = (i == 0) | (te_ref[tile] != te_ref[jnp.maximum(tile - 1, 0)])
        for parity, (wg_ref, wu_ref, wd_ref) in enumerate(((wga_ref, wua_ref, wda_ref), (wgb_ref, wub_ref, wdb_ref))):
            @pl.when(new_run & (par_ref[tile] == parity))
            def _():
                wg_bf[...] = wg_ref[...].astype(BF16)
                wu_bf[...] = wu_ref[...].astype(BF16)
                wd_bf[...] = wd_ref[...].astype(BF16)

        x = x_ref[...]
        act = _silu(_dot(x, wg_bf[...])) * _dot(x, wu_bf[...])
        y_ref[...] = _dot(act.astype(BF16), wd_bf[...]).astype(BF16)


def _grouped_ffn(plan, xg, y_prev, wg, wu, wd, *, layer, tile0, total_rows):
    used = lambda i, nu: jnp.maximum(jnp.minimum(i + tile0, nu[0] - 1), tile0)
    win_a = lambda a, b: pl.BlockSpec((None, None, a, b), lambda i, te, nu, ea, eb, par: (layer, ea[used(i, nu)], 0, 0))
    win_b = lambda a, b: pl.BlockSpec((None, None, a, b), lambda i, te, nu, ea, eb, par: (layer, eb[used(i, nu)], 0, 0))
    carried = [] if y_prev is None else [y_prev]
    shapes = ((D_MODEL, MOE_FF), (D_MODEL, MOE_FF), (MOE_FF, D_MODEL))
    grid_spec = pltpu.PrefetchScalarGridSpec(
        num_scalar_prefetch=5,
        grid=(xg.shape[0] // MOE_TILE,),
        in_specs=[pl.BlockSpec((MOE_TILE, D_MODEL), lambda i, te, nu, *_: (used(i, nu) - tile0, 0))]
        + [win_a(*s) for s in shapes] + [win_b(*s) for s in shapes]
        + [pl.BlockSpec(memory_space=pl.ANY)] * len(carried),
        out_specs=pl.BlockSpec((MOE_TILE, D_MODEL), lambda i, te, nu, *_: (used(i, nu), 0)),
        scratch_shapes=[pltpu.VMEM(s, BF16) for s in shapes],
    )
    return pl.pallas_call(
        functools.partial(_ffn_kernel, tile0=tile0),
        grid_spec=grid_spec,
        out_shape=jax.ShapeDtypeStruct((total_rows, D_MODEL), BF16),
        input_output_aliases={12: 0} if carried else {},
        compiler_params=_cparams("arbitrary"),
        name="expert_ffn",
    )(*plan, xg, wg, wu, wd, wg, wu, wd, *carried)


@compute_on("tpu_sparsecore")
@jax.jit
def _take_rows(x, idx):
    return jnp.take(x, idx, axis=0, mode="clip")


def _piece_cuts(n):
    inner = sorted({min(max(round(f * n), 1), n - 1) for f in PIPE_CUTS}) if n > 1 else []
    return [0] + inner + [n]


def _moe_plan(route_t, count):
    tokens = route_t.shape[1]
    n_tiles = -(-(2 * tokens + MOE_EXPERTS * (MOE_TILE - 1)) // MOE_TILE)
    n_rows = n_tiles * MOE_TILE
    counts = count[0, MOE_GROUPS:MOE_GROUPS + MOE_EXPERTS].astype(I32)
    padded = ((counts + MOE_TILE - 1) // MOE_TILE) * MOE_TILE
    gend = jnp.cumsum(padded)
    gstart = gend - padded
    experts = jnp.arange(MOE_EXPERTS, dtype=I32)[:, None]

    def rows_of(e_lane, rank_lane):
        e = route_t[e_lane].astype(I32)
        return jnp.sum(jnp.where(e[None, :] == experts, gstart[:, None], 0), axis=0) + route_t[rank_lane].astype(I32)

    pos1, pos2 = rows_of(R_E1, R_RANK1), rows_of(R_E2, R_RANK2)
    token = jnp.arange(tokens, dtype=I32) + 1
    marked = jnp.zeros((n_rows,), I32).at[jnp.concatenate([pos1, pos2])].add(jnp.concatenate([token, token]),
                                                                             unique_indices=True)
    filler = jnp.arange(n_rows, dtype=I32) % tokens
    src = jnp.where(marked > 0, marked - 1, filler)
    tile_start = jnp.arange(n_tiles, dtype=I32) * MOE_TILE
    tile_expert = jnp.minimum(jnp.sum((tile_start[:, None] >= gend[None, :]).astype(I32), axis=1), MOE_EXPERTS - 1)
    n_used = (gend[-1] // MOE_TILE).astype(I32).reshape(1)
    ids = jnp.arange(MOE_EXPERTS, dtype=I32)
    present = padded > 0
    parity_e = (jnp.cumsum(present.astype(I32)) - 1) & 1
    later = jnp.min(jnp.where(present[None, :] & (ids[None, :] > ids[:, None]), ids[None, :], MOE_EXPERTS), axis=1)
    next_e = jnp.where(later == MOE_EXPERTS, ids, later)
    per_tile = lambda v: jnp.sum(jnp.where(tile_expert[:, None] == ids[None, :], v[None, :], 0), axis=1)
    window_a = per_tile(jnp.where(parity_e == 0, ids, next_e))
    window_b = per_tile(jnp.where(parity_e == 1, ids, next_e))
    return src, pos1, pos2, (tile_expert, n_used, window_a, window_b, per_tile(parity_e))


def _ple_kernel(xm_ref, y1_ref, y2_ref, rt_ref, p_ref, np_ref, wg_ref, wp_ref, nfin_ref, *rest, final):
    out_ref = rest[-1]
    rt = rt_ref[...]
    x1 = xm_ref[...] + rt[:, R_W1:R_W1 + 1] * y1_ref[...].astype(F32) + rt[:, R_W2:R_W2 + 1] * y2_ref[...].astype(F32)
    gate = _sigmoid(_dot(_rms(x1, np_ref[...]).astype(BF16), wg_ref[...]))
    x2 = x1 + gate * _dot(p_ref[...].astype(BF16), wp_ref[...])
    out_ref[...] = _rms(x2, nfin_ref[...]) if final else x2


def _ple(xm, y1, y2, route, p, x_prev, npl, wg, wp, nfin, *, layer, final, tm, n_blocks, x_off, y_off, rt_off):
    at = lambda w, off: pl.BlockSpec((tm, w), lambda i: (i + off, 0))
    carried = [] if x_prev is None else [x_prev]
    return pl.pallas_call(
        functools.partial(_ple_kernel, final=final),
        grid=(n_blocks,),
        in_specs=[at(D_MODEL, x_off), at(D_MODEL, y_off), at(D_MODEL, y_off), at(LANES, rt_off),
                  pl.BlockSpec((None, tm, PLE_DIM), lambda i: (layer, i + x_off, 0)),
                  _layer_spec((1, D_MODEL), layer), _layer_spec((D_MODEL, D_MODEL), layer),
                  _layer_spec((PLE_DIM, D_MODEL), layer), pl.BlockSpec((1, D_MODEL), lambda i: (0, 0))]
        + [pl.BlockSpec(memory_space=pl.ANY)] * len(carried),
        out_specs=at(D_MODEL, x_off),
        out_shape=jax.ShapeDtypeStruct(xm.shape, F32),
        input_output_aliases={9: 0} if carried else {},
        compiler_params=_cparams("parallel"),
        name="combine_ple",
    )(xm, y1, y2, route, p, npl, wg, wp, nfin, *carried)


def kernel(x_prompt, x_sample, state_gla, state_s5_re, state_s5_im, state_hgrn, p_prompt, p_sample, norm_mix, w_in, gla_w_gate2, gla_b_gate, gla_norm, s5_a_re, s5_a_im, s5_log_dt, s5_b_re, s5_b_im, s5_c_re, s5_c_im, s5_d, s5_w_glu, s5_b_glu, hgrn_lb_logits, hgrn_norm, w_br_gla, w_br_s5, w_br_hgrn, w_out, norm_ffn, moe_w_group, moe_b_group, moe_w_expert, moe_b_expert, moe_w_gate, moe_w_up, moe_w_down, norm_ple, w_ple_gate, w_ple_proj, norm_final):
    nb, length, _ = x_prompt.shape
    ns = x_sample.shape[0]
    n_p = nb * length
    n_all = n_p + ns
    row = lambda t: t.reshape(DEPTH, 1, -1)

    w_a, w_gm = _w_in_layout(w_in)
    wg2 =jnp.concatenate([gla_w_gate2, jnp.zeros((DEPTH, LANES - GLA_RANK, QK), F32)], axis=1).astype(BF16)
    s5_win, s5_are, s5_aim = _s5_discretise(s5_a_re, s5_a_im, s5_log_dt, s5_b_re, s5_b_im)
    s5_w = (s5_win, s5_are, s5_aim, _s5_blockdiag_out(s5_c_re), _s5_blockdiag_out(s5_c_im), row(s5_d),
            s5_w_glu.astype(BF16), row(s5_b_glu))
    head_gain = jnp.stack([jnp.tile(gla_norm, (1, HEADS)), jnp.tile(hgrn_norm, (1, HEADS))], axis=1).reshape(DEPTH, 2, 1, VW)
    w_router = jnp.concatenate([moe_w_group, moe_w_expert,
                                jnp.zeros((DEPTH, D_MODEL, LANES - MOE_GROUPS - MOE_EXPERTS), F32)], axis=-1)
    wr_hi = w_router.astype(BF16)
    wr_lo = (w_router - wr_hi.astype(F32)).astype(BF16)
    wr_hilo = jnp.concatenate([wr_hi, wr_lo], axis=-1)
    b_router = jnp.concatenate([moe_b_group, moe_b_expert,
                                jnp.zeros((DEPTH, LANES - MOE_GROUPS - MOE_EXPERTS), F32)], axis=-1).reshape(DEPTH, 1, LANES)
    merge_w = (row(norm_mix), w_gm, head_gain,
               jnp.stack([w_br_gla, w_br_s5, w_br_hgrn], axis=1).astype(BF16), w_out.astype(BF16), row(norm_ffn),
               wr_hilo, b_router)
    ple_w = (row(norm_ple), w_ple_gate.astype(BF16), w_ple_proj.astype(BF16), norm_final.reshape(1, D_MODEL))
    nm, bg = row(norm_mix), row(gla_b_gate)
    p_p, p_s = p_prompt.reshape(DEPTH, n_p, PLE_DIM), p_sample.reshape(DEPTH, ns, PLE_DIM)
    sg_in, sh_in = state_gla.reshape(DEPTH, ns, QK, DV), state_hgrn.reshape(DEPTH, ns, QK, DV)
    s5r_in, s5i_in = state_s5_re.reshape(DEPTH, ns, S5_LANES), state_s5_im.reshape(DEPTH, ns, S5_LANES)

    xp = x_prompt.reshape(n_p, D_MODEL)
    xs = x_sample.reshape(ns, D_MODEL)
    new_p, new_s5 = [], []
    new_s = None
    for i in range(DEPTH):
        qk, v, gate, g, su = _inproj(xp, nm, w_a, wg2, bg, hgrn_lb_logits, layer=i, tm=1024)
        o_p, st_p = _chunk_scan(qk.reshape(2, nb, length, -1), v.reshape(2, nb, length, -1),
                                g.reshape(2, nb, length, -1), nb=nb, length=length)
        os5_p, hre_p, him_p = _s5_scan(su.reshape(nb, length, -1), s5_w, nb=nb, length=length, layer=i)
        xm_p, h2, rt, rtt, cnt = _merge(xp, o_p.reshape(2, n_p, VW), gate, os5_p.reshape(n_p, -1),
                                        jnp.zeros((1, LANES), F32), merge_w, None, layer=i, tm=1024, total_rows=n_all,
                                        block_offset=0)
        new_p.append((st_p[0].reshape(nb, HEADS, DK, DV), hre_p.reshape(nb, S5_GROUPS, S5_STATE),
                      him_p.reshape(nb, S5_GROUPS, S5_STATE), st_p[1].reshape(nb, HEADS, DK, DV)))

        qk, v, gate, g, su = _inproj(xs, nm, w_a, wg2, bg, hgrn_lb_logits, layer=i, tm=ns)
        o_s, *new_s = _decode(qk, v, g, sg_in, sh_in, new_s, layer=i)
        os5_s, hre_s, him_s = _s5_step(su, s5r_in, s5i_in, s5_w, layer=i)
        xm_s, h2, rt, rtt, cnt = _merge(xs, o_s, gate, os5_s, cnt, merge_w, (h2, rt, rtt), layer=i, tm=ns,
                                        total_rows=n_all, block_offset=n_p // ns)
        new_s5.append((hre_s.reshape(ns, S5_GROUPS, S5_STATE), him_s.reshape(ns, S5_GROUPS, S5_STATE)))

        src, pos1, pos2, plan = _moe_plan(rtt, cnt)
        n_tiles = plan[0].shape[0]
        cuts = _piece_cuts(n_tiles)
        y = None
        for t0, t1 in zip(cuts[:-1], cuts[1:]):
            y = _grouped_ffn(plan, _take_rows(h2, src[t0 * MOE_TILE:t1 * MOE_TILE]), y,
                             moe_w_gate, moe_w_up, moe_w_down, layer=i, tile0=t0, total_rows=n_tiles * MOE_TILE)

        final = i == DEPTH - 1
        tm = 512
        cuts = _piece_cuts(n_p // tm)
        x_new = None
        for b0, b1 in zip(cuts[:-1], cuts[1:]):
            r0, r1 = b0 * tm, (n_all if b1 == cuts[-1] else b1 * tm)
            y1, y2 = _take_rows(y, pos1[r0:r1]), _take_rows(y, pos2[r0:r1])
            x_new = _ple(xm_p, y1, y2, rt, p_p, x_new, *ple_w, layer=i, final=final, tm=tm, n_blocks=b1 - b0,
                         x_off=b0, y_off=0, rt_off=b0)
        xp = x_new
        xs = _ple(xm_s, y1, y2, rt, p_s, None, *ple_w, layer=i, final=final, tm=ns, n_blocks=1,
                  x_off=0, y_off=(n_p - r0) // ns, rt_off=n_p // ns)

    stack = lambda items, j: jnp.stack([it[j] for it in items])
    return (xp.reshape(nb, length, D_MODEL), xs.reshape(ns, 1, D_MODEL),
            stack(new_p, 0), stack(new_p, 1), stack(new_p, 2), stack(new_p, 3),
            new_s[0].reshape(DEPTH, ns, HEADS, DK, DV), stack(new_s5, 0), stack(new_s5, 1),
            new_s[1].reshape(DEPTH, ns, HEADS, DK, DV))
```

```python
import functools

import jax
import jax.numpy as jnp
import numpy as np
from jax import lax
from jax.experimental import pallas as pl
from jax.experimental.compute_on import compute_on
from jax.experimental.pallas import tpu as pltpu

F32, BF16, I32 = jnp.float32, jnp.bfloat16, jnp.int32

D_MODEL = 1024
DEPTH = 2
HEADS, DK, DV = 4, 64, 128
QK, VW = HEADS * DK, HEADS * DV
GLA_RANK, GLA_TAU = 16, 16.0
S5_WIDTH, S5_GROUP, S5_GROUPS, S5_STATE = 512, 16, 32, 64
S5_LANES = S5_GROUPS * S5_STATE
S5_SLABS = 4
MOE_GROUPS, MOE_PER_GROUP, MOE_EXPERTS, MOE_FF = 4, 8, 32, 256
PLE_DIM = 256
EPS = 1e-6

LANES = 128
CHUNK = 64
CHUNKS_PER_STEP = 4
S5_CHUNK = 64
MOE_TILE = 512
PIPE_CUTS = (0.5,)
SAFE_EXP = 80.0
VMEM_LIMIT = 56 * 1024 * 1024

C_GLA, C_S5, C_HG, C_GLR, C_END = 0, 1536, 2048, 3584, 3712
R_E1, R_E2, R_W1, R_W2, R_RANK1, R_RANK2 = 0, 1, 2, 3, 4, 5


def _cparams(*sem):
    return pltpu.CompilerParams(dimension_semantics=sem, vmem_limit_bytes=VMEM_LIMIT)


def _layer_spec(shape, layer):
    return pl.BlockSpec((None,) + tuple(shape), lambda *_: (layer,) + (0,) * len(shape), pipeline_mode=pl.Buffered(1))


def _dot(a, b):
    return jnp.dot(a, b, preferred_element_type=F32)


def _rms(x, g):
    return x * lax.rsqrt(jnp.mean(x * x, axis=-1, keepdims=True) + EPS) * g


def _log_sigmoid(x):
    return jnp.minimum(x, 0.0) - jnp.log1p(jnp.exp(-jnp.abs(x)))


def _sigmoid(x):
    return 0.5 * jnp.tanh(0.5 * x) + 0.5


def _silu(x):
    return x * _sigmoid(x)


def _split3(x):
    hi = x.astype(BF16)
    r1 = x - hi.astype(F32)
    mid = r1.astype(BF16)
    lo = (r1 - mid.astype(F32)).astype(BF16)
    return hi, mid, lo


def _dot01(m, parts):
    out = _dot(m, parts[0])
    for p in parts[1:]:
        out = out + _dot(m, p)
    return out


W_GLA, W_GLR, W_S5HG, W_END = 1536, 1552, 3600, 6672


def _w_in_layout_kernel(wt_ref, wa_ref, wgm_ref):
    cols = lambda lo, hi: jnp.transpose(wt_ref[lo:hi, :])
    wa_ref[:, C_GLA:C_S5] = cols(0, W_GLA).astype(BF16)
    wa_ref[:, C_S5:C_GLR] = cols(W_GLR, W_S5HG).astype(BF16)
    lane = lax.broadcasted_iota(I32, (LANES, LANES), 1)
    wa_ref[:, C_GLR:C_END] = jnp.where(lane < GLA_RANK, cols(W_GLA, W_GLA + LANES), 0.0).astype(BF16)
    wgm_ref[...] = cols(W_S5HG, W_END).astype(BF16)


def _w_in_layout(w_in):
    wt = jnp.swapaxes(w_in, 1, 2)
    return pl.pallas_call(
        _w_in_layout_kernel,
        grid=(DEPTH, D_MODEL // LANES),
        in_specs=[pl.BlockSpec((None, W_END, LANES), lambda d, i: (d, 0, i))],
        out_specs=[pl.BlockSpec((None, LANES, C_END), lambda d, i: (d, i, 0)),
                   pl.BlockSpec((None, LANES, 3 * D_MODEL), lambda d, i: (d, i, 0))],
        out_shape=[jax.ShapeDtypeStruct((DEPTH, D_MODEL, C_END), BF16),
                   jax.ShapeDtypeStruct((DEPTH, D_MODEL, 3 * D_MODEL), BF16)],
        compiler_params=_cparams("parallel", "parallel"),
        name="w_in_layout",
    )(wt)
def _inproj_kernel(x_ref, nm_ref, w_ref, wg2_ref, bg_ref, lbl_ref, qk_ref, v_ref, gate_ref, g_ref, su_ref, *, layer, sub):
    lg = lbl_ref[...]
    mx = jnp.max(lg, axis=0, keepdims=True)
    ex = jnp.exp(lg - mx)
    sm = ex / jnp.sum(ex, axis=0, keepdims=True)
    cs = sm[0:1]
    for j in range(1, layer + 1):
        cs = cs + sm[j:j + 1]
    lb = cs - sm[0:1]
    log_lb, log_1mlb = jnp.log(lb), jnp.log1p(-lb)

    for r0 in range(0, x_ref.shape[0], sub):
        rs = slice(r0, r0 + sub)
        hb = _rms(x_ref[rs, :], nm_ref[...]).astype(BF16)

        def proj(lo, hi):
            return _dot(hb, w_ref[:, lo:hi])

        qk_ref[0, rs, 0:QK] = (proj(0, 256) * (DK ** -0.5)).astype(BF16)
        qk_ref[0, rs, QK:2 * QK] = proj(256, 512).astype(BF16)
        v_ref[0, rs, :] = proj(512, 1024).astype(BF16)
        gate_ref[0, rs, :] = _silu(proj(1024, 1536)).astype(BF16)
        glr = proj(C_GLR, C_END).astype(BF16)
        g_ref[0, rs, :] = _log_sigmoid(_dot(glr, wg2_ref[...]) + bg_ref[...]) * (1.0 / GLA_TAU)

        su_ref[rs, :] = proj(C_S5, C_HG)

        z = proj(C_HG + 256, C_HG + 512)
        a, c = log_lb, log_1mlb + _log_sigmoid(z)
        g_ref[1, rs, :] = jnp.maximum(a, c) + jnp.log1p(jnp.exp(-jnp.abs(a - c)))
        qk_ref[1, rs, 0:QK] = _silu(proj(C_HG, C_HG + 256)).astype(BF16)
        qk_ref[1, rs, QK:2 * QK] = ((1.0 - lb) * _sigmoid(-z)).astype(BF16)
        v_ref[1, rs, :] = proj(C_HG + 512, C_HG + 1024).astype(BF16)
        gate_ref[1, rs, :] = _silu(proj(C_HG + 1024, C_HG + 1536)).astype(BF16)


def _inproj(x, nm, w, wg2, bg, lbl, *, layer, tm):
    rows = x.shape[0]
    return pl.pallas_call(
        functools.partial(_inproj_kernel, layer=layer, sub=min(tm, 256)),
        grid=(rows // tm,),
        in_specs=[pl.BlockSpec((tm, D_MODEL), lambda i: (i, 0)),
                  _layer_spec((1, D_MODEL), layer),
                  _layer_spec((D_MODEL, C_END), layer),
                  _layer_spec((LANES, QK), layer),
                  _layer_spec((1, QK), layer),
                  pl.BlockSpec((DEPTH, QK), lambda i: (0, 0))],
        out_specs=[pl.BlockSpec((2, tm, 2 * QK), lambda i: (0, i, 0)),
                   pl.BlockSpec((2, tm, VW), lambda i: (0, i, 0)),
                   pl.BlockSpec((2, tm, VW), lambda i: (0, i, 0)),
                   pl.BlockSpec((2, tm, QK), lambda i: (0, i, 0)),
                   pl.BlockSpec((tm, S5_WIDTH), lambda i: (i, 0))],
        out_shape=[jax.ShapeDtypeStruct((2, rows, 2 * QK), BF16),
                   jax.ShapeDtypeStruct((2, rows, VW), BF16),
                   jax.ShapeDtypeStruct((2, rows, VW), BF16),
                   jax.ShapeDtypeStruct((2, rows, QK), F32),
                   jax.ShapeDtypeStruct((rows, S5_WIDTH), F32)],
        compiler_params=_cparams("parallel"),
        name="inproj",
    )(x, nm, w, wg2, bg, lbl)


def _head_stack(x):
    head = lax.broadcasted_iota(I32, x.shape, 1) // DK
    return jnp.concatenate([jnp.where(head == h, x, 0.0) for h in range(HEADS)], axis=0).astype(BF16)


def _stack_scores(qt, kt):
    kt = kt.astype(BF16)
    return lax.dot_general(_head_stack(qt), jnp.concatenate([kt, kt], axis=0), (((1,), (1,)), ((), ())),
                           preferred_element_type=F32)


def _chunk_kernel(qk_ref, v_ref, g_ref, o_ref, st_ref, bc_ref, sc_ref, *, nb, c):
    assert c == DK and 2 * c == LANES

    @pl.when(pl.program_id(1) == 0)
    def _():
        st_ref[...] = jnp.zeros_like(st_ref)

    row = lax.broadcasted_iota(I32, (c, c), 0)
    col = lax.broadcasted_iota(I32, (c, c), 1)
    tri = jnp.where(col <= row, 1.0, 0.0).astype(BF16)
    srow = lax.broadcasted_iota(I32, (HEADS * c, 2 * c), 0) & (c - 1)
    scol = lax.broadcasted_iota(I32, (HEADS * c, 2 * c), 1) & (c - 1)
    mid = c // 2 - 1

    def one_chunk(idx, carry):
        tok = pl.ds(pl.multiple_of(idx * c, c), c)
        spread = None
        for b in range(nb):
            bc = _dot01(tri, _split3(g_ref[0, b, tok, :]))
            bc_ref[b] = bc
            ref, last = bc[mid:mid + 1, :], bc[c - 1:c, :]
            s = jnp.maximum(jnp.max(-ref), jnp.max(ref - last))
            spread = s if spread is None else jnp.maximum(spread, s)

        def qk_of(b):
            return qk_ref[0, b, tok, 0:QK].astype(F32), qk_ref[0, b, tok, QK:2 * QK].astype(F32)

        def scores_one_reference():
            for b in range(nb):
                q, k = qk_of(b)
                bc = bc_ref[b]
                ref = bc[mid:mid + 1, :]
                s = _stack_scores(q * jnp.exp(bc - ref), k * jnp.exp(ref - bc))
                sc_ref[b] = jnp.where(scol <= srow, s, 0.0).astype(BF16)

        def scores_by_levels():
            qrow = lax.broadcasted_iota(I32, (c, QK), 0)
            for b in range(nb):
                q, k = qk_of(b)
                bc = bc_ref[b]
                parts = _split3(g_ref[0, b, tok, :])
                acc = jnp.where(scol == srow, _stack_scores(q, k), 0.0)
                half = c // 2
                while half >= 1:
                    blk = 2 * half
                    last_low = (row & ~(blk - 1)) + (half - 1)
                    ref = _dot01(jnp.where(col <= last_low, 1.0, 0.0).astype(BF16), parts)
                    upper = (qrow & (blk - 1)) >= half
                    dq = jnp.minimum(jnp.where(upper, bc - ref, 0.0), 0.0)
                    dk = jnp.minimum(jnp.where(upper, 0.0, ref - bc), 0.0)
                    s = _stack_scores(q * jnp.exp(dq), k * jnp.exp(dk))
                    pair = ((srow & ~(blk - 1)) == (scol & ~(blk - 1))) & ((srow & (blk - 1)) >= half) & ((scol & (blk - 1)) < half)
                    acc = acc + jnp.where(pair, s, 0.0)
                    half //= 2
                sc_ref[b] = acc.astype(BF16)

        lax.cond(spread <= SAFE_EXP, scores_one_reference, scores_by_levels)

        for b in range(nb):
            q, k = qk_of(b)
            bc = bc_ref[b]
            last = bc[c - 1:c, :]
            v = v_ref[0, b, tok, :]
            state = st_ref[0, b]
            state_bf = state.astype(BF16)
            q_in = (q * jnp.exp(bc)).astype(BF16)
            k_out = jnp.transpose(k * jnp.exp(last - bc)).astype(BF16)
            decay = jnp.transpose(jnp.broadcast_to(jnp.exp(last), (DV, QK)))
            lane = lax.broadcasted_iota(I32, (c, LANES), 1)
            for h in range(HEADS):
                rs, ls, ks = slice(h * c, (h + 1) * c), slice(h * DV, (h + 1) * DV), slice(h * DK, (h + 1) * DK)
                pair = q_in[:, (h // 2) * LANES:(h // 2 + 1) * LANES]
                if h % 2 == 0:
                    lhs = jnp.where(lane < DK, pair, sc_ref[b, rs, :])
                    rhs = jnp.concatenate([state_bf[ks, :], v[:, ls]], axis=0)
                else:
                    lhs = jnp.where(lane >= DK, pair, sc_ref[b, rs, :])
                    rhs = jnp.concatenate([v[:, ls], state_bf[ks, :]], axis=0)
                o_ref[0, b, tok, ls] = _dot(lhs, rhs).astype(BF16)
                st_ref[0, b, ks, :] = decay[ks, :] * state[ks, :] + _dot(k_out[ks, :], v[:, ls])
        return carry

    lax.fori_loop(0, qk_ref.shape[2] // c, one_chunk, 0)


def _chunk_scan(qk, v, g, *, nb, length):
    c = CHUNK
    blk = lambda w: pl.BlockSpec((1, nb, CHUNKS_PER_STEP * c, w), lambda br, i: (br, 0, i, 0))
    return pl.pallas_call(
        functools.partial(_chunk_kernel, nb=nb, c=c),
        grid=(2, length // (CHUNKS_PER_STEP * c)),
        in_specs=[blk(2 * QK), blk(VW), blk(QK)],
        out_specs=[blk(VW), pl.BlockSpec((1, nb, QK, DV), lambda br, i: (br, 0, 0, 0))],
        out_shape=[jax.ShapeDtypeStruct((2, nb, length, VW), BF16),
                   jax.ShapeDtypeStruct((2, nb, QK, DV), F32)],
        scratch_shapes=[pltpu.VMEM((nb, c, QK), F32), pltpu.VMEM((nb, HEADS * c, 2 * c), BF16)],
        compiler_params=_cparams("arbitrary", "arbitrary"),
        name="chunk_scan",
    )(qk, v, g)


def _decode_kernel(qk_ref, v_ref, g_ref, s0_ref, s1_ref, *rest, nt):
    o_ref, n0_ref, n1_ref = rest[-3:]
    for br, (s_ref, n_ref) in enumerate(((s0_ref, n0_ref), (s1_ref, n1_ref))):
        for j in range(nt):
            d = jnp.exp(g_ref[br, j:j + 1, :])
            q = qk_ref[br, j:j + 1, 0:QK].astype(F32)
            k = qk_ref[br, j:j + 1, QK:2 * QK].astype(F32)
            cols = jnp.transpose(jnp.concatenate([d, k, q, jnp.zeros((5, QK), F32)], axis=0))
            vrow = v_ref[br, j:j + 1, :].astype(F32)
            vfull = jnp.concatenate([jnp.broadcast_to(vrow[:, h * DV:(h + 1) * DV], (DK, DV)) for h in range(HEADS)], axis=0)
            new = cols[:, 0:1] * s_ref[j] + cols[:, 1:2] * vfull
            n_ref[j] = new
            t = cols[:, 2:3] * new
            for h in range(HEADS):
                o_ref[br, j:j + 1, h * DV:(h + 1) * DV] = jnp.sum(t[h * DK:(h + 1) * DK, :], axis=0, keepdims=True).astype(BF16)


def _decode(qk, v, g, s_gla, s_hg, prev, *, layer):
    n = qk.shape[1]
    nt = 8
    row = lambda w: pl.BlockSpec((2, nt, w), lambda i: (0, i, 0))
    st = pl.BlockSpec((None, nt, QK, DV), lambda i: (layer, i, 0, 0))
    carried = [] if prev is None else list(prev)
    first = 5
    return pl.pallas_call(
        functools.partial(_decode_kernel, nt=nt),
        grid=(n // nt,),
        in_specs=[row(2 * QK), row(VW), row(QK), st, st] + [pl.BlockSpec(memory_space=pl.ANY)] * len(carried),
        out_specs=[row(VW), st, st],
        out_shape=[jax.ShapeDtypeStruct((2, n, VW), BF16),
                   jax.ShapeDtypeStruct((DEPTH, n, QK, DV), F32),
                   jax.ShapeDtypeStruct((DEPTH, n, QK, DV), F32)],
        input_output_aliases={first + j: 1 + j for j in range(len(carried))},
        compiler_params=_cparams("parallel"),
        name="decode_step",
    )(qk, v, g, s_gla, s_hg, *carried)


def _s5_disc_kernel(lr_ref, li_ref, ldt_ref, br_ref, bi_ref, abr_ref, abi_ref, bbr_ref, bbi_ref):
    lr, li, dt = lr_ref[...], li_ref[...], jnp.exp(ldt_ref[...])
    mag = jnp.exp(lr * dt)
    ab_re, ab_im = mag * jnp.cos(li * dt), mag * jnp.sin(li * dt)
    den = lr * lr + li * li
    num_re = ab_re - 1.0
    coef_re = (num_re * lr + ab_im * li) / den
    coef_im = (ab_im * lr - num_re * li) / den
    br, bi = br_ref[...], bi_ref[...]
    abr_ref[...] = ab_re
    abi_ref[...] = ab_im
    bbr_ref[...] = coef_re * br - coef_im * bi
    bbi_ref[...] = coef_re * bi + coef_im * br


def _s5_discretise(a_re, a_im, log_dt, b_re, b_im):
    n = DEPTH * S5_GROUPS
    rep = lambda t: jnp.repeat(t.reshape(n, S5_STATE), S5_GROUP, axis=1)
    ldt = jnp.broadcast_to(log_dt.reshape(n, 1), (n, S5_STATE * S5_GROUP))
    shape = jax.ShapeDtypeStruct((n, S5_STATE * S5_GROUP), F32)
    ab_re, ab_im, bb_re, bb_im = pl.pallas_call(_s5_disc_kernel, out_shape=[shape] * 4, name="s5_discretise")(
        rep(a_re), rep(a_im), ldt, b_re.reshape(n, -1), b_im.reshape(n, -1))
    pole = lambda t: t[:, ::S5_GROUP].reshape(DEPTH, 1, S5_LANES)
    eye = jnp.eye(8, dtype=F32)

    def blockdiag_in(bb):
        t = bb.reshape(DEPTH, S5_SLABS, 8, S5_STATE, S5_GROUP).transpose(0, 1, 2, 4, 3)
        return jnp.einsum("dcgmp,gh->dcgmhp", t, eye).reshape(DEPTH, S5_SLABS, LANES, 8 * S5_STATE)

    w_in = jnp.concatenate([blockdiag_in(bb_re), blockdiag_in(bb_im)], axis=-1).astype(BF16)
    return w_in, pole(ab_re), pole(ab_im)


def _s5_blockdiag_out(c):
    t = c.reshape(DEPTH, S5_SLABS, 8, S5_GROUP, S5_STATE).transpose(0, 1, 2, 4, 3)
    return jnp.einsum("dcgpm,gh->dcgphm", t, jnp.eye(8, dtype=F32)).reshape(DEPTH, S5_SLABS, 8 * S5_STATE, LANES).astype(BF16)


def _time_major_perm(nb, ct):
    r = np.arange(nb * ct)
    p = np.zeros((nb * ct, nb * ct), np.float32)
    p[r, (r % nb) * ct + r // nb] = 1.0
    return p


def _gelu_tanh(y):
    return 0.5 * y * (1.0 + jnp.tanh(0.7978845608028654 * (y + 0.044715 * (y * y * y))))


def _s5_input(ub, win_ref, xre_ref, xim_ref):
    half = 8 * S5_STATE
    for s in range(S5_SLABS):
        r = _dot(ub[:, s * LANES:(s + 1) * LANES], win_ref[s])
        xre_ref[:, s * half:(s + 1) * half] = r[:, :half]
        xim_ref[:, s * half:(s + 1) * half] = r[:, half:]


def _s5_readout(xre_ref, xim_ref, cre_ref, cim_ref):
    half = 8 * S5_STATE
    ys = []
    for s in range(S5_SLABS):
        ls = slice(s * half, (s + 1) * half)
        ys.append(_dot(xre_ref[:, ls].astype(BF16), cre_ref[s]) - _dot(xim_ref[:, ls].astype(BF16), cim_ref[s]))
    return jnp.concatenate(ys, axis=-1)


def _s5_glu(y, u, d_ref, wglu_ref, bglu_ref):
    z = _gelu_tanh(y + d_ref[...] * u)
    return z * _sigmoid(_dot(z.astype(BF16), wglu_ref[...]) + bglu_ref[...])


def _s5_scan_kernel(su_ref, perm_ref, permt_ref, win_ref, are_ref, aim_ref, cre_ref, cim_ref, d_ref, wglu_ref, bglu_ref,
                    o_ref, hre_ref, him_ref, xre_ref, xim_ref, *, nb, ct):
    @pl.when(pl.program_id(0) == 0)
    def _():
        hre_ref[...] = jnp.zeros_like(hre_ref)
        him_ref[...] = jnp.zeros_like(him_ref)

    u = su_ref[...].reshape(nb * ct, S5_WIDTH)
    ub = _dot(perm_ref[...], u.astype(BF16)).astype(BF16)

    half = 8 * S5_STATE
    ys = []
    for s in range(S5_SLABS):
        ls = slice(s * half, (s + 1) * half)
        r = _dot(ub[:, s * LANES:(s + 1) * LANES], win_ref[s])
        xre_ref[:, ls] = r[:, :half]
        xim_ref[:, ls] = r[:, half:]
        ar = jnp.broadcast_to(are_ref[:, ls], (nb, half))
        ai = jnp.broadcast_to(aim_ref[:, ls], (nb, half))
        hr, hi = hre_ref[:, ls], him_ref[:, ls]
        for t in range(ct):
            rows = slice(t * nb, (t + 1) * nb)
            hr, hi = ar * hr - ai * hi + xre_ref[rows, ls], ar * hi + ai * hr + xim_ref[rows, ls]
            xre_ref[rows, ls] = hr
            xim_ref[rows, ls] = hi
        hre_ref[:, ls] = hr
        him_ref[:, ls] = hi
        ys.append(_dot(xre_ref[:, ls].astype(BF16), cre_ref[s]) - _dot(xim_ref[:, ls].astype(BF16), cim_ref[s]))

    y = jnp.concatenate(ys, axis=-1)
    y_hi = y.astype(BF16)
    y_lo = (y - y_hi.astype(F32)).astype(BF16)
    y = _dot01(permt_ref[...], (y_hi, y_lo))
    o_ref[...] = _s5_glu(y, u, d_ref, wglu_ref, bglu_ref).reshape(nb, ct, S5_WIDTH).astype(BF16)


def _s5_weight_specs(layer):
    return [_layer_spec((S5_SLABS, LANES, 2 * 8 * S5_STATE), layer),
            _layer_spec((1, S5_LANES), layer), _layer_spec((1, S5_LANES), layer),
            _layer_spec((S5_SLABS, 8 * S5_STATE, LANES), layer), _layer_spec((S5_SLABS, 8 * S5_STATE, LANES), layer),
            _layer_spec((1, S5_WIDTH), layer), _layer_spec((S5_WIDTH, S5_WIDTH), layer),
            _layer_spec((1, S5_WIDTH), layer)]


def _s5_scan(su, weights, *, nb, length, layer):
    ct = S5_CHUNK
    perm = _time_major_perm(nb, ct)
    const = pl.BlockSpec((nb * ct, nb * ct), lambda i: (0, 0))
    return pl.pallas_call(
        functools.partial(_s5_scan_kernel, nb=nb, ct=ct),
        grid=(length // ct,),
        in_specs=[pl.BlockSpec((nb, ct, S5_WIDTH), lambda i: (0, i, 0)), const, const] + _s5_weight_specs(layer),
        out_specs=[pl.BlockSpec((nb, ct, S5_WIDTH), lambda i: (0, i, 0)),
                   pl.BlockSpec((nb, S5_LANES), lambda i: (0, 0)), pl.BlockSpec((nb, S5_LANES), lambda i: (0, 0))],
        out_shape=[jax.ShapeDtypeStruct((nb, length, S5_WIDTH), BF16),
                   jax.ShapeDtypeStruct((nb, S5_LANES), F32), jax.ShapeDtypeStruct((nb, S5_LANES), F32)],
        scratch_shapes=[pltpu.VMEM((nb * ct, S5_LANES), F32), pltpu.VMEM((nb * ct, S5_LANES), F32)],
        compiler_params=_cparams("arbitrary"),
        name="s5_scan",
    )(su, jnp.asarray(perm, BF16), jnp.asarray(perm.T, BF16), *weights)


def _s5_step_kernel(su_ref, h0r_ref, h0i_ref, win_ref, are_ref, aim_ref, cre_ref, cim_ref, d_ref, wglu_ref, bglu_ref,
                    o_ref, hre_ref, him_ref):
    u = su_ref[...]
    _s5_input(u.astype(BF16), win_ref, hre_ref, him_ref)
    ar, ai = are_ref[...], aim_ref[...]
    h0r, h0i = h0r_ref[...], h0i_ref[...]
    nr = ar * h0r - ai * h0i + hre_ref[...]
    ni = ar * h0i + ai * h0r + him_ref[...]
    hre_ref[...] = nr
    him_ref[...] = ni
    o_ref[...] = _s5_glu(_s5_readout(hre_ref, him_ref, cre_ref, cim_ref), u, d_ref, wglu_ref, bglu_ref).astype(BF16)


def _s5_step(su, h0r, h0i, weights, *, layer):
    n = su.shape[0]
    full = lambda w: pl.BlockSpec((n, w), lambda i: (0, 0))
    state = pl.BlockSpec((None, n, S5_LANES), lambda i: (layer, 0, 0))
    return pl.pallas_call(
        _s5_step_kernel,
        grid=(1,),
        in_specs=[full(S5_WIDTH), state, state] + _s5_weight_specs(layer),
        out_specs=[full(S5_WIDTH), full(S5_LANES), full(S5_LANES)],
        out_shape=[jax.ShapeDtypeStruct((n, S5_WIDTH), BF16),
                   jax.ShapeDtypeStruct((n, S5_LANES), F32), jax.ShapeDtypeStruct((n, S5_LANES), F32)],
        compiler_params=_cparams("arbitrary"),
        name="s5_step",
    )(su, h0r, h0i, *weights)


def _head_norm(o, g):
    parts = []
    for h in range(HEADS):
        seg = o[:, h * DV:(h + 1) * DV]
        parts.append(seg * lax.rsqrt(jnp.mean(seg * seg, axis=-1, keepdims=True) + EPS))
    return jnp.concatenate(parts, axis=-1) * g


def _route(logits, count):
    rows = logits.shape[0]
    lane = lax.broadcasted_iota(I32, logits.shape, 1)
    neg = -jnp.inf
    first = lambda hit: jnp.min(jnp.where(hit, lane, LANES), axis=-1, keepdims=True)
    glog = jnp.where(lane < MOE_GROUPS, logits, neg)
    gmax = jnp.max(glog, axis=-1, keepdims=True)
    gidx = first(glog == gmax)
    gw = 1.0 / jnp.sum(jnp.where(lane < MOE_GROUPS, jnp.exp(logits - gmax), 0.0), axis=-1, keepdims=True)
    inside = (lane >= MOE_GROUPS) & (lane < MOE_GROUPS + MOE_EXPERTS) & (((lane - MOE_GROUPS) >> 3) == gidx)
    el = jnp.where(inside, logits, neg)
    v1 = jnp.max(el, axis=-1, keepdims=True)
    i1 = first(el == v1)
    el2 = jnp.where(lane == i1, neg, el)
    v2 = jnp.max(el2, axis=-1, keepdims=True)
    i2 = first(el2 == v2)
    p2 = jnp.exp(v2 - v1)
    w1 = gw / (1.0 + p2)
    w2 = gw * p2 / (1.0 + p2)
    hit1, hit2 = lane == i1, lane == i2
    onehot = jnp.where(hit1 | hit2, 1.0, 0.0)
    blk = min(rows, 256)
    tr = lax.broadcasted_iota(I32, (blk, blk), 0)
    tc = lax.broadcasted_iota(I32, (blk, blk), 1)
    lower = jnp.where(tc < tr, 1.0, 0.0).astype(BF16)
    before = []
    for r0 in range(0, rows, blk):
        part = onehot[r0:r0 + blk, :]
        before.append(_dot(lower, part.astype(BF16)) + count)
        count = count + jnp.sum(part, axis=0, keepdims=True)
    before = jnp.concatenate(before, axis=0)
    rank1 = jnp.sum(jnp.where(hit1, before, 0.0), axis=-1, keepdims=True)
    rank2 = jnp.sum(jnp.where(hit2, before, 0.0), axis=-1, keepdims=True)
    slab = jnp.zeros(logits.shape, F32)
    for ln, val in ((R_E1, (i1 - MOE_GROUPS).astype(F32)), (R_E2, (i2 - MOE_GROUPS).astype(F32)), (R_W1, w1), (R_W2, w2),
                    (R_RANK1, rank1), (R_RANK2, rank2)):
        slab = jnp.where(lane == ln, val, slab)
    return slab, count


def _merge_kernel(x_ref, o_ref, gate_ref, os5_ref, cnt0_ref, nm_ref, wgm_ref, gn_ref, wbr_ref, wout_ref, nf_ref,
                  wr_ref, br_ref, *rest, sub):
    xm_ref, h2_ref, rt_ref, rtt_ref, cnt_ref = rest[-5:]

    @pl.when(pl.program_id(0) == 0)
    def _():
        cnt_ref[...] = cnt0_ref[...]

    logits = []
    for r0 in range(0, x_ref.shape[0], sub):
        rs = slice(r0, r0 + sub)
        x = x_ref[rs, :]
        hb = _rms(x, nm_ref[...]).astype(BF16)
        mixed = None
        for i, src in enumerate((0, None, 1)):
            if src is None:
                branch = os5_ref[rs, :]
            else:
                branch = (_head_norm(o_ref[src, rs, :].astype(F32), gn_ref[src]) * gate_ref[src, rs, :].astype(F32)).astype(BF16)
            gate = _sigmoid(_dot(hb, wgm_ref[:, i * D_MODEL:(i + 1) * D_MODEL]))
            term = gate * _dot(branch, wbr_ref[i])
            mixed = term if mixed is None else mixed + term
        xm = x + _dot(mixed.astype(BF16), wout_ref[...])
        xm_ref[rs, :] = xm
        h2 = _rms(xm, nf_ref[...])
        h2_hi = h2.astype(BF16)
        h2_lo = (h2 - h2_hi.astype(F32)).astype(BF16)
        h2_ref[rs, :] = h2_hi
        both = _dot(h2_hi, wr_ref[...])
        logits.append(both[:, :LANES] + (both[:, LANES:] + _dot(h2_lo, wr_ref[:, :LANES])) + br_ref[...])
    slab, cnt_ref[...] = _route(jnp.concatenate(logits, axis=0), cnt_ref[...])
    rt_ref[...] = slab
    rtt_ref[...] = jnp.transpose(slab)[0:8, :]


def _merge(x, o, gate, os5, cnt0, weights, carried, *, layer, tm, total_rows, block_offset):
    rows = x.shape[0]
    tile = lambda w: pl.BlockSpec((tm, w), lambda i: (i, 0))
    shared = lambda w: pl.BlockSpec((tm, w), lambda i: (i + block_offset, 0))
    carried = [] if carried is None else list(carried)
    w_specs = [_layer_spec((1, D_MODEL), layer), _layer_spec((D_MODEL, 3 * D_MODEL), layer),
               _layer_spec((2, 1, VW), layer), _layer_spec((3, VW, D_MODEL), layer),
               _layer_spec((D_MODEL, D_MODEL), layer), _layer_spec((1, D_MODEL), layer),
               _layer_spec((D_MODEL, 2 * LANES), layer), _layer_spec((1, LANES), layer)]
    first = 5 + len(w_specs)
    return pl.pallas_call(
        functools.partial(_merge_kernel, sub=min(tm, 256)),
        grid=(rows // tm,),
        in_specs=[tile(D_MODEL),
                  pl.BlockSpec((2, tm, VW), lambda i: (0, i, 0)),
                  pl.BlockSpec((2, tm, VW), lambda i: (0, i, 0)),
                  tile(S5_WIDTH),
                  pl.BlockSpec((1, LANES), lambda i: (0, 0))] + w_specs + [pl.BlockSpec(memory_space=pl.ANY)] * len(carried),
        out_specs=[tile(D_MODEL), shared(D_MODEL), shared(LANES),
                   pl.BlockSpec((8, tm), lambda i: (0, i + block_offset)), pl.BlockSpec((1, LANES), lambda i: (0, 0))],
        out_shape=[jax.ShapeDtypeStruct((rows, D_MODEL), F32),
                   jax.ShapeDtypeStruct((total_rows, D_MODEL), BF16),
                   jax.ShapeDtypeStruct((total_rows, LANES), F32),
                   jax.ShapeDtypeStruct((8, total_rows), F32),
                   jax.ShapeDtypeStruct((1, LANES), F32)],
        input_output_aliases={first + j: 1 + j for j in range(len(carried))},
        compiler_params=_cparams("arbitrary"),
        name="merge_route",
    )(x, o, gate, os5, cnt0, *weights, *carried)


def _ffn_kernel(te_ref, nu_ref, ea_ref, eb_ref, par_ref, x_ref, wga_ref, wua_ref, wda_ref, wgb_ref, wub_ref, wdb_ref,
                *rest, tile0):
    y_ref, wg_bf, wu_bf, wd_bf = rest[-4:]
    i = pl.program_id(0)
    tile = i + tile0

    @pl.when(tile < nu_ref[0])
    def _():
        new_run = (i == 0) | (te_ref[tile] != te_ref[jnp.maximum(tile - 1, 0)])
        for parity, (wg_ref, wu_ref, wd_ref) in enumerate(((wga_ref, wua_ref, wda_ref), (wgb_ref, wub_ref, wdb_ref))):
            @pl.when(new_run & (par_ref[tile] == parity))
            def _():
                wg_bf[...] = wg_ref[...].astype(BF16)
                wu_bf[...] = wu_ref[...].astype(BF16)
                wd_bf[...] = wd_ref[...].astype(BF16)

        x = x_ref[...]
        act = _silu(_dot(x, wg_bf[...])) * _dot(x, wu_bf[...])
        y_ref[...] = _dot(act.astype(BF16), wd_bf[...]).astype(BF16)


def _grouped_ffn(plan, xg, y_prev, wg, wu, wd, *, layer, tile0, total_rows):
    used = lambda i, nu: jnp.maximum(jnp.minimum(i + tile0, nu[0] - 1), tile0)
    win_a = lambda a, b: pl.BlockSpec((None, None, a, b), lambda i, te, nu, ea, eb, par: (layer, ea[used(i, nu)], 0, 0))
    win_b = lambda a, b: pl.BlockSpec((None, None, a, b), lambda i, te, nu, ea, eb, par: (layer, eb[used(i, nu)], 0, 0))
    carried = [] if y_prev is None else [y_prev]
    shapes = ((D_MODEL, MOE_FF), (D_MODEL, MOE_FF), (MOE_FF, D_MODEL))
    grid_spec = pltpu.PrefetchScalarGridSpec(
        num_scalar_prefetch=5,
        grid=(xg.shape[0] // MOE_TILE,),
        in_specs=[pl.BlockSpec((MOE_TILE, D_MODEL), lambda i, te, nu, *_: (used(i, nu) - tile0, 0))]
        + [win_a(*s) for s in shapes] + [win_b(*s) for s in shapes]
        + [pl.BlockSpec(memory_space=pl.ANY)] * len(carried),
        out_specs=pl.BlockSpec((MOE_TILE, D_MODEL), lambda i, te, nu, *_: (used(i, nu), 0)),
        scratch_shapes=[pltpu.VMEM(s, BF16) for s in shapes],
    )
    return pl.pallas_call(
        functools.partial(_ffn_kernel, tile0=tile0),
        grid_spec=grid_spec,
        out_shape=jax.ShapeDtypeStruct((total_rows, D_MODEL), BF16),
        input_output_aliases={12: 0} if carried else {},
        compiler_params=_cparams("arbitrary"),
        name="expert_ffn",
    )(*plan, xg, wg, wu, wd, wg, wu, wd, *carried)


@compute_on("tpu_sparsecore")
@jax.jit
def _take_rows(x, idx):
    return jnp.take(x, idx, axis=0, mode="clip")


def _piece_cuts(n):
    inner = sorted({min(max(round(f * n), 1), n - 1) for f in PIPE_CUTS}) if n > 1 else []
    return [0] + inner + [n]


def _moe_plan(route_t, count):
    tokens = route_t.shape[1]
    n_tiles = -(-(2 * tokens + MOE_EXPERTS * (MOE_TILE - 1)) // MOE_TILE)
    n_rows = n_tiles * MOE_TILE
    counts = count[0, MOE_GROUPS:MOE_GROUPS + MOE_EXPERTS].astype(I32)
    padded = ((counts + MOE_TILE - 1) // MOE_TILE) * MOE_TILE
    gend = jnp.cumsum(padded)
    gstart = gend - padded
    experts = jnp.arange(MOE_EXPERTS, dtype=I32)[:, None]

    def rows_of(e_lane, rank_lane):
        e = route_t[e_lane].astype(I32)
        return jnp.sum(jnp.where(e[None, :] == experts, gstart[:, None], 0), axis=0) + route_t[rank_lane].astype(I32)

    pos1, pos2 = rows_of(R_E1, R_RANK1), rows_of(R_E2, R_RANK2)
    token = jnp.arange(tokens, dtype=I32) + 1
    marked = jnp.zeros((n_rows,), I32).at[jnp.concatenate([pos1, pos2])].add(jnp.concatenate([token, token]),
                                                                             unique_indices=True)
    filler = jnp.arange(n_rows, dtype=I32) % tokens
    src = jnp.where(marked > 0, marked - 1, filler)
    tile_start = jnp.arange(n_tiles, dtype=I32) * MOE_TILE
    tile_expert = jnp.minimum(jnp.sum((tile_start[:, None] >= gend[None, :]).astype(I32), axis=1), MOE_EXPERTS - 1)
    n_used = (gend[-1] // MOE_TILE).astype(I32).reshape(1)
    ids = jnp.arange(MOE_EXPERTS, dtype=I32)
    present = padded > 0
    parity_e = (jnp.cumsum(present.astype(I32)) - 1) & 1
    later = jnp.min(jnp.where(present[None, :] & (ids[None, :] > ids[:, None]), ids[None, :], MOE_EXPERTS), axis=1)
    next_e = jnp.where(later == MOE_EXPERTS, ids, later)
    per_tile = lambda v: jnp.sum(jnp.where(tile_expert[:, None] == ids[None, :], v[None, :], 0), axis=1)
    window_a = per_tile(jnp.where(parity_e == 0, ids, next_e))
    window_b = per_tile(jnp.where(parity_e == 1, ids, next_e))
    return src, pos1, pos2, (tile_expert, n_used, window_a, window_b, per_tile(parity_e))


def _ple_kernel(xm_ref, y1_ref, y2_ref, rt_ref, p_ref, np_ref, wg_ref, wp_ref, nfin_ref, *rest, final):
    out_ref = rest[-1]
    rt = rt_ref[...]
    x1 = xm_ref[...] + rt[:, R_W1:R_W1 + 1] * y1_ref[...].astype(F32) + rt[:, R_W2:R_W2 + 1] * y2_ref[...].astype(F32)
    gate = _sigmoid(_dot(_rms(x1, np_ref[...]).astype(BF16), wg_ref[...]))
    x2 = x1 + gate * _dot(p_ref[...].astype(BF16), wp_ref[...])
    out_ref[...] = _rms(x2, nfin_ref[...]) if final else x2


def _ple(xm, y1, y2, route, p, x_prev, npl, wg, wp, nfin, *, layer, final, tm, n_blocks, x_off, y_off, rt_off):
    at = lambda w, off: pl.BlockSpec((tm, w), lambda i: (i + off, 0))
    carried = [] if x_prev is None else [x_prev]
    return pl.pallas_call(
        functools.partial(_ple_kernel, final=final),
        grid=(n_blocks,),
        in_specs=[at(D_MODEL, x_off), at(D_MODEL, y_off), at(D_MODEL, y_off), at(LANES, rt_off),
                  pl.BlockSpec((None, tm, PLE_DIM), lambda i: (layer, i + x_off, 0)),
                  _layer_spec((1, D_MODEL), layer), _layer_spec((D_MODEL, D_MODEL), layer),
                  _layer_spec((PLE_DIM, D_MODEL), layer), pl.BlockSpec((1, D_MODEL), lambda i: (0, 0))]
        + [pl.BlockSpec(memory_space=pl.ANY)] * len(carried),
        out_specs=at(D_MODEL, x_off),
        out_shape=jax.ShapeDtypeStruct(xm.shape, F32),
        input_output_aliases={9: 0} if carried else {},
        compiler_params=_cparams("parallel"),
        name="combine_ple",
    )(xm, y1, y2, route, p, npl, wg, wp, nfin, *carried)


def kernel(x_prompt, x_sample, state_gla, state_s5_re, state_s5_im, state_hgrn, p_prompt, p_sample, norm_mix, w_in, gla_w_gate2, gla_b_gate, gla_norm, s5_a_re, s5_a_im, s5_log_dt, s5_b_re, s5_b_im, s5_c_re, s5_c_im, s5_d, s5_w_glu, s5_b_glu, hgrn_lb_logits, hgrn_norm, w_br_gla, w_br_s5, w_br_hgrn, w_out, norm_ffn, moe_w_group, moe_b_group, moe_w_expert, moe_b_expert, moe_w_gate, moe_w_up, moe_w_down, norm_ple, w_ple_gate, w_ple_proj, norm_final):
    nb, length, _ = x_prompt.shape
    ns = x_sample.shape[0]
    n_p = nb * length
    n_all = n_p + ns
    row = lambda t: t.reshape(DEPTH, 1, -1)

    w_a, w_gm = _w_in_layout(w_in)
    wg2 =jnp.concatenate([gla_w_gate2, jnp.zeros((DEPTH, LANES - GLA_RANK, QK), F32)], axis=1).astype(BF16)
    s5_win, s5_are, s5_aim = _s5_discretise(s5_a_re, s5_a_im, s5_log_dt, s5_b_re, s5_b_im)
    s5_w = (s5_win, s5_are, s5_aim, _s5_blockdiag_out(s5_c_re), _s5_blockdiag_out(s5_c_im), row(s5_d),
            s5_w_glu.astype(BF16), row(s5_b_glu))
    head_gain = jnp.stack([jnp.tile(gla_norm, (1, HEADS)), jnp.tile(hgrn_norm, (1, HEADS))], axis=1).reshape(DEPTH, 2, 1, VW)
    w_router = jnp.concatenate([moe_w_group, moe_w_expert,
                                jnp.zeros((DEPTH, D_MODEL, LANES - MOE_GROUPS - MOE_EXPERTS), F32)], axis=-1)
    wr_hi = w_router.astype(BF16)
    wr_lo = (w_router - wr_hi.astype(F32)).astype(BF16)
    wr_hilo = jnp.concatenate([wr_hi, wr_lo], axis=-1)
    b_router = jnp.concatenate([moe_b_group, moe_b_expert,
                                jnp.zeros((DEPTH, LANES - MOE_GROUPS - MOE_EXPERTS), F32)], axis=-1).reshape(DEPTH, 1, LANES)
    merge_w = (row(norm_mix), w_gm, head_gain,
               jnp.stack([w_br_gla, w_br_s5, w_br_hgrn], axis=1).astype(BF16), w_out.astype(BF16), row(norm_ffn),
               wr_hilo, b_router)
    ple_w = (row(norm_ple), w_ple_gate.astype(BF16), w_ple_proj.astype(BF16), norm_final.reshape(1, D_MODEL))
    nm, bg = row(norm_mix), row(gla_b_gate)
    p_p, p_s = p_prompt.reshape(DEPTH, n_p, PLE_DIM), p_sample.reshape(DEPTH, ns, PLE_DIM)
    sg_in, sh_in = state_gla.reshape(DEPTH, ns, QK, DV), state_hgrn.reshape(DEPTH, ns, QK, DV)
    s5r_in, s5i_in = state_s5_re.reshape(DEPTH, ns, S5_LANES), state_s5_im.reshape(DEPTH, ns, S5_LANES)

    xp = x_prompt.reshape(n_p, D_MODEL)
    xs = x_sample.reshape(ns, D_MODEL)
    new_p, new_s5 = [], []
    new_s = None
    for i in range(DEPTH):
        qk, v, gate, g, su = _inproj(xp, nm, w_a, wg2, bg, hgrn_lb_logits, layer=i, tm=1024)
        o_p, st_p = _chunk_scan(qk.reshape(2, nb, length, -1), v.reshape(2, nb, length, -1),
                                g.reshape(2, nb, length, -1), nb=nb, length=length)
        os5_p, hre_p, him_p = _s5_scan(su.reshape(nb, length, -1), s5_w, nb=nb, length=length, layer=i)
        xm_p, h2, rt, rtt, cnt = _merge(xp, o_p.reshape(2, n_p, VW), gate, os5_p.reshape(n_p, -1),
                                        jnp.zeros((1, LANES), F32), merge_w, None, layer=i, tm=1024, total_rows=n_all,
                                        block_offset=0)
        new_p.append((st_p[0].reshape(nb, HEADS, DK, DV), hre_p.reshape(nb, S5_GROUPS, S5_STATE),
                      him_p.reshape(nb, S5_GROUPS, S5_STATE), st_p[1].reshape(nb, HEADS, DK, DV)))

        qk, v, gate, g, su = _inproj(xs, nm, w_a, wg2, bg, hgrn_lb_logits, layer=i, tm=ns)
        o_s, *new_s = _decode(qk, v, g, sg_in, sh_in, new_s, layer=i)
        os5_s, hre_s, him_s = _s5_step(su, s5r_in, s5i_in, s5_w, layer=i)
        xm_s, h2, rt, rtt, cnt = _merge(xs, o_s, gate, os5_s, cnt, merge_w, (h2, rt, rtt), layer=i, tm=ns,
                                        total_rows=n_all, block_offset=n_p // ns)
        new_s5.append((hre_s.reshape(ns, S5_GROUPS, S5_STATE), him_s.reshape(ns, S5_GROUPS, S5_STATE)))

        src, pos1, pos2, plan = _moe_plan(rtt, cnt)
        n_tiles = plan[0].shape[0]
        cuts = _piece_cuts(n_tiles)
        y = None
        for t0, t1 in zip(cuts[:-1], cuts[1:]):
            y = _grouped_ffn(plan, _take_rows(h2, src[t0 * MOE_TILE:t1 * MOE_TILE]), y,
                             moe_w_gate, moe_w_up, moe_w_down, layer=i, tile0=t0, total_rows=n_tiles * MOE_TILE)

        final = i == DEPTH - 1
        tm = 1024
        cuts = _piece_cuts(n_p // tm)
        x_new = None
        for b0, b1 in zip(cuts[:-1], cuts[1:]):
            r0, r1 = b0 * tm, (n_all if b1 == cuts[-1] else b1 * tm)
            y1, y2 = _take_rows(y, pos1[r0:r1]), _take_rows(y, pos2[r0:r1])
            x_new = _ple(xm_p, y1, y2, rt, p_p, x_new, *ple_w, layer=i, final=final, tm=tm, n_blocks=b1 - b0,
                         x_off=b0, y_off=0, rt_off=b0)
        xp = x_new
        xs = _ple(xm_s, y1, y2, rt, p_s, None, *ple_w, layer=i, final=final, tm=ns, n_blocks=1,
                  x_off=0, y_off=(n_p - r0) // ns, rt_off=n_p // ns)

    stack = lambda items, j: jnp.stack([it[j] for it in items])
    return (xp.reshape(nb, length, D_MODEL), xs.reshape(ns, 1, D_MODEL),
            stack(new_p, 0), stack(new_p, 1), stack(new_p, 2), stack(new_p, 3),
            new_s[0].reshape(DEPTH, ns, HEADS, DK, DV), stack(new_s5, 0), stack(new_s5, 1),
            new_s[1].reshape(DEPTH, ns, HEADS, DK, DV))
```

```python
import functools

import jax
import jax.numpy as jnp
import numpy as np
from jax import lax
from jax.experimental import pallas as pl
from jax.experimental.compute_on import compute_on
from jax.experimental.pallas import tpu as pltpu

F32, BF16, I32 = jnp.float32, jnp.bfloat16, jnp.int32

D_MODEL = 1024
DEPTH = 2
HEADS, DK, DV = 4, 64, 128
QK, VW = HEADS * DK, HEADS * DV
GLA_RANK, GLA_TAU = 16, 16.0
S5_WIDTH, S5_GROUP, S5_GROUPS, S5_STATE = 512, 16, 32, 64
S5_LANES = S5_GROUPS * S5_STATE
S5_SLABS = 4
MOE_GROUPS, MOE_PER_GROUP, MOE_EXPERTS, MOE_FF = 4, 8, 32, 256
PLE_DIM = 256
EPS = 1e-6

LANES = 128
CHUNK = 64
S5_CHUNK = 64
MOE_TILE = 512
PIPE_CUTS = (0.5,)
COMBINE_CUTS = (0.25, 0.5, 0.75)
SAFE_EXP = 80.0
VMEM_LIMIT = 56 * 1024 * 1024

C_GLA, C_S5, C_HG, C_GLR, C_END = 0, 1536, 2048, 3584, 3712
R_E1, R_E2, R_W1, R_W2, R_RANK1, R_RANK2 = 0, 1, 2, 3, 4, 5


def _cparams(*sem):
    return pltpu.CompilerParams(dimension_semantics=sem, vmem_limit_bytes=VMEM_LIMIT)


def _layer_spec(shape, layer):
    return pl.BlockSpec((None,) + tuple(shape), lambda *_: (layer,) + (0,) * len(shape), pipeline_mode=pl.Buffered(1))


def _dot(a, b):
    return jnp.dot(a, b, preferred_element_type=F32)


def _rms(x, g):
    return x * lax.rsqrt(jnp.mean(x * x, axis=-1, keepdims=True) + EPS) * g


def _log_sigmoid(x):
    return jnp.minimum(x, 0.0) - jnp.log1p(jnp.exp(-jnp.abs(x)))


def _sigmoid(x):
    return 0.5 * jnp.tanh(0.5 * x) + 0.5


def _silu(x):
    return x * _sigmoid(x)


def _split3(x):
    hi = x.astype(BF16)
    r1 = x - hi.astype(F32)
    mid = r1.astype(BF16)
    lo = (r1 - mid.astype(F32)).astype(BF16)
    return hi, mid, lo


def _dot01(m, parts):
    out = _dot(m, parts[0])
    for p in parts[1:]:
        out = out + _dot(m, p)
    return out


W_GLA, W_GLR, W_S5HG, W_END = 1536, 1552, 3600, 6672


def _w_in_layout_kernel(wt_ref, wa_ref, wgm_ref):
    cols = lambda lo, hi: jnp.transpose(wt_ref[lo:hi, :])
    wa_ref[:, C_GLA:C_S5] = cols(0, W_GLA).astype(BF16)
    wa_ref[:, C_S5:C_GLR] = cols(W_GLR, W_S5HG).astype(BF16)
    lane = lax.broadcasted_iota(I32, (LANES, LANES), 1)
    wa_ref[:, C_GLR:C_END] = jnp.where(lane < GLA_RANK, cols(W_GLA, W_GLA + LANES), 0.0).astype(BF16)
    wgm_ref[...] = cols(W_S5HG, W_END).astype(BF16)


def _w_in_layout(w_in):
    wt = jnp.swapaxes(w_in, 1, 2)
    return pl.pallas_call(
        _w_in_layout_kernel,
        grid=(DEPTH, D_MODEL // LANES),
        in_specs=[pl.BlockSpec((None, W_END, LANES), lambda d, i: (d, 0, i))],
        out_specs=[pl.BlockSpec((None, LANES, C_END), lambda d, i: (d, i, 0)),
                   pl.BlockSpec((None, LANES, 3 * D_MODEL), lambda d, i: (d, i, 0))],
        out_shape=[jax.ShapeDtypeStruct((DEPTH, D_MODEL, C_END), BF16),
                   jax.ShapeDtypeStruct((DEPTH, D_MODEL, 3 * D_MODEL), BF16)],
        compiler_params=_cparams("parallel", "parallel"),
        name="w_in_layout",
    )(wt)
def _inproj_kernel(x_ref, nm_ref, w_ref, wg2_ref, bg_ref, lbl_ref, qk_ref, v_ref, gate_ref, g_ref, su_ref, *, layer, sub):
    lg = lbl_ref[...]
    mx = jnp.max(lg, axis=0, keepdims=True)
    ex = jnp.exp(lg - mx)
    sm = ex / jnp.sum(ex, axis=0, keepdims=True)
    cs = sm[0:1]
    for j in range(1, layer + 1):
        cs = cs + sm[j:j + 1]
    lb = cs - sm[0:1]
    log_lb, log_1mlb = jnp.log(lb), jnp.log1p(-lb)

    for r0 in range(0, x_ref.shape[0], sub):
        rs = slice(r0, r0 + sub)
        hb = _rms(x_ref[rs, :], nm_ref[...]).astype(BF16)

        def proj(lo, hi):
            return _dot(hb, w_ref[:, lo:hi])

        qk_ref[0, rs, 0:QK] = (proj(0, 256) * (DK ** -0.5)).astype(BF16)
        qk_ref[0, rs, QK:2 * QK] = proj(256, 512).astype(BF16)
        v_ref[0, rs, :] = proj(512, 1024).astype(BF16)
        gate_ref[0, rs, :] = _silu(proj(1024, 1536)).astype(BF16)
        glr = proj(C_GLR, C_END).astype(BF16)
        g_ref[0, rs, :] = _log_sigmoid(_dot(glr, wg2_ref[...]) + bg_ref[...]) * (1.0 / GLA_TAU)

        su_ref[rs, :] = proj(C_S5, C_HG)

        z = proj(C_HG + 256, C_HG + 512)
        a, c = log_lb, log_1mlb + _log_sigmoid(z)
        g_ref[1, rs, :] = jnp.maximum(a, c) + jnp.log1p(jnp.exp(-jnp.abs(a - c)))
        qk_ref[1, rs, 0:QK] = _silu(proj(C_HG, C_HG + 256)).astype(BF16)
        qk_ref[1, rs, QK:2 * QK] = ((1.0 - lb) * _sigmoid(-z)).astype(BF16)
        v_ref[1, rs, :] = proj(C_HG + 512, C_HG + 1024).astype(BF16)
        gate_ref[1, rs, :] = _silu(proj(C_HG + 1024, C_HG + 1536)).astype(BF16)


def _inproj(x, nm, w, wg2, bg, lbl, *, layer, tm):
    rows = x.shape[0]
    return pl.pallas_call(
        functools.partial(_inproj_kernel, layer=layer, sub=min(tm, 256)),
        grid=(rows // tm,),
        in_specs=[pl.BlockSpec((tm, D_MODEL), lambda i: (i, 0)),
                  _layer_spec((1, D_MODEL), layer),
                  _layer_spec((D_MODEL, C_END), layer),
                  _layer_spec((LANES, QK), layer),
                  _layer_spec((1, QK), layer),
                  pl.BlockSpec((DEPTH, QK), lambda i: (0, 0))],
        out_specs=[pl.BlockSpec((2, tm, 2 * QK), lambda i: (0, i, 0)),
                   pl.BlockSpec((2, tm, VW), lambda i: (0, i, 0)),
                   pl.BlockSpec((2, tm, VW), lambda i: (0, i, 0)),
                   pl.BlockSpec((2, tm, QK), lambda i: (0, i, 0)),
                   pl.BlockSpec((tm, S5_WIDTH), lambda i: (i, 0))],
        out_shape=[jax.ShapeDtypeStruct((2, rows, 2 * QK), BF16),
                   jax.ShapeDtypeStruct((2, rows, VW), BF16),
                   jax.ShapeDtypeStruct((2, rows, VW), BF16),
                   jax.ShapeDtypeStruct((2, rows, QK), F32),
                   jax.ShapeDtypeStruct((rows, S5_WIDTH), F32)],
        compiler_params=_cparams("parallel"),
        name="inproj",
    )(x, nm, w, wg2, bg, lbl)


def _head_stack(x):
    head = lax.broadcasted_iota(I32, x.shape, 1) // DK
    return jnp.concatenate([jnp.where(head == h, x, 0.0) for h in range(HEADS)], axis=0).astype(BF16)


def _stack_scores(qt, kt):
    kt = kt.astype(BF16)
    return lax.dot_general(_head_stack(qt), jnp.concatenate([kt, kt], axis=0), (((1,), (1,)), ((), ())),
                           preferred_element_type=F32)


def _chunk_kernel(qk_ref, v_ref, g_ref, o_ref, st_ref, bc_ref, sc_ref, *, nb, c):
    assert c == DK and 2 * c == LANES

    @pl.when(pl.program_id(1) == 0)
    def _():
        st_ref[...] = jnp.zeros_like(st_ref)

    row = lax.broadcasted_iota(I32, (c, c), 0)
    col = lax.broadcasted_iota(I32, (c, c), 1)
    tri = jnp.where(col <= row, 1.0, 0.0).astype(BF16)
    srow = lax.broadcasted_iota(I32, (HEADS * c, 2 * c), 0) & (c - 1)
    scol = lax.broadcasted_iota(I32, (HEADS * c, 2 * c), 1) & (c - 1)
    mid = c // 2 - 1

    spread = None
    for b in range(nb):
        bc = _dot01(tri, _split3(g_ref[0, b]))
        bc_ref[b] = bc
        ref, last = bc[mid:mid + 1, :], bc[c - 1:c, :]
        s = jnp.maximum(jnp.max(-ref), jnp.max(ref - last))
        spread = s if spread is None else jnp.maximum(spread, s)

    def qk_of(b):
        return qk_ref[0, b, :, 0:QK].astype(F32), qk_ref[0, b, :, QK:2 * QK].astype(F32)

    def scores_one_reference():
        for b in range(nb):
            q, k = qk_of(b)
            bc = bc_ref[b]
            ref = bc[mid:mid + 1, :]
            s = _stack_scores(q * jnp.exp(bc - ref), k * jnp.exp(ref - bc))
            sc_ref[b] = jnp.where(scol <= srow, s, 0.0).astype(BF16)

    def scores_by_levels():
        qrow = lax.broadcasted_iota(I32, (c, QK), 0)
        for b in range(nb):
            q, k = qk_of(b)
            bc = bc_ref[b]
            parts = _split3(g_ref[0, b])
            acc = jnp.where(scol == srow, _stack_scores(q, k), 0.0)
            half = c // 2
            while half >= 1:
                blk = 2 * half
                last_low = (row & ~(blk - 1)) + (half - 1)
                ref = _dot01(jnp.where(col <= last_low, 1.0, 0.0).astype(BF16), parts)
                upper = (qrow & (blk - 1)) >= half
                dq = jnp.minimum(jnp.where(upper, bc - ref, 0.0), 0.0)
                dk = jnp.minimum(jnp.where(upper, 0.0, ref - bc), 0.0)
                s = _stack_scores(q * jnp.exp(dq), k * jnp.exp(dk))
                pair = ((srow & ~(blk - 1)) == (scol & ~(blk - 1))) & ((srow & (blk - 1)) >= half) & ((scol & (blk - 1)) < half)
                acc = acc + jnp.where(pair, s, 0.0)
                half //= 2
            sc_ref[b] = acc.astype(BF16)

    lax.cond(spread <= SAFE_EXP, scores_one_reference, scores_by_levels)

    for b in range(nb):
        q, k = qk_of(b)
        bc = bc_ref[b]
        last = bc[c - 1:c, :]
        v = v_ref[0, b]
        state = st_ref[0, b]
        state_bf = state.astype(BF16)
        q_in = (q * jnp.exp(bc)).astype(BF16)
        k_out = jnp.transpose(k * jnp.exp(last - bc)).astype(BF16)
        decay = jnp.transpose(jnp.broadcast_to(jnp.exp(last), (DV, QK)))
        lane = lax.broadcasted_iota(I32, (c, LANES), 1)
        for h in range(HEADS):
            rs, ls, ks = slice(h * c, (h + 1) * c), slice(h * DV, (h + 1) * DV), slice(h * DK, (h + 1) * DK)
            pair = q_in[:, (h // 2) * LANES:(h // 2 + 1) * LANES]
            if h % 2 == 0:
                lhs = jnp.where(lane < DK, pair, sc_ref[b, rs, :])
                rhs = jnp.concatenate([state_bf[ks, :], v[:, ls]], axis=0)
            else:
                lhs = jnp.where(lane >= DK, pair, sc_ref[b, rs, :])
                rhs = jnp.concatenate([v[:, ls], state_bf[ks, :]], axis=0)
            o_ref[0, b, :, ls] = _dot(lhs, rhs).astype(BF16)
            st_ref[0, b, ks, :] = decay[ks, :] * state[ks, :] + _dot(k_out[ks, :], v[:, ls])


def _chunk_scan(qk, v, g, *, nb, length):
    c = CHUNK
    blk = lambda w: pl.BlockSpec((1, nb, c, w), lambda br, i: (br, 0, i, 0))
    return pl.pallas_call(
        functools.partial(_chunk_kernel, nb=nb, c=c),
        grid=(2, length // c),
        in_specs=[blk(2 * QK), blk(VW), blk(QK)],
        out_specs=[blk(VW), pl.BlockSpec((1, nb, QK, DV), lambda br, i: (br, 0, 0, 0))],
        out_shape=[jax.ShapeDtypeStruct((2, nb, length, VW), BF16),
                   jax.ShapeDtypeStruct((2, nb, QK, DV), F32)],
        scratch_shapes=[pltpu.VMEM((nb, c, QK), F32), pltpu.VMEM((nb, HEADS * c, 2 * c), BF16)],
        compiler_params=_cparams("arbitrary", "arbitrary"),
        name="chunk_scan",
    )(qk, v, g)


def _decode_kernel(qk_ref, v_ref, g_ref, s0_ref, s1_ref, *rest, nt):
    o_ref, n0_ref, n1_ref = rest[-3:]
    for br, (s_ref, n_ref) in enumerate(((s0_ref, n0_ref), (s1_ref, n1_ref))):
        for j in range(nt):
            d = jnp.exp(g_ref[br, j:j + 1, :])
            q = qk_ref[br, j:j + 1, 0:QK].astype(F32)
            k = qk_ref[br, j:j + 1, QK:2 * QK].astype(F32)
            cols = jnp.transpose(jnp.concatenate([d, k, q, jnp.zeros((5, QK), F32)], axis=0))
            vrow = v_ref[br, j:j + 1, :].astype(F32)
            vfull = jnp.concatenate([jnp.broadcast_to(vrow[:, h * DV:(h + 1) * DV], (DK, DV)) for h in range(HEADS)], axis=0)
            new = cols[:, 0:1] * s_ref[j] + cols[:, 1:2] * vfull
            n_ref[j] = new
            t = cols[:, 2:3] * new
            for h in range(HEADS):
                o_ref[br, j:j + 1, h * DV:(h + 1) * DV] = jnp.sum(t[h * DK:(h + 1) * DK, :], axis=0, keepdims=True).astype(BF16)


def _decode(qk, v, g, s_gla, s_hg, prev, *, layer):
    n = qk.shape[1]
    nt = 8
    row = lambda w: pl.BlockSpec((2, nt, w), lambda i: (0, i, 0))
    st = pl.BlockSpec((None, nt, QK, DV), lambda i: (layer, i, 0, 0))
    carried = [] if prev is None else list(prev)
    first = 5
    return pl.pallas_call(
        functools.partial(_decode_kernel, nt=nt),
        grid=(n // nt,),
        in_specs=[row(2 * QK), row(VW), row(QK), st, st] + [pl.BlockSpec(memory_space=pl.ANY)] * len(carried),
        out_specs=[row(VW), st, st],
        out_shape=[jax.ShapeDtypeStruct((2, n, VW), BF16),
                   jax.ShapeDtypeStruct((DEPTH, n, QK, DV), F32),
                   jax.ShapeDtypeStruct((DEPTH, n, QK, DV), F32)],
        input_output_aliases={first + j: 1 + j for j in range(len(carried))},
        compiler_params=_cparams("parallel"),
        name="decode_step",
    )(qk, v, g, s_gla, s_hg, *carried)


def _s5_disc_kernel(lr_ref, li_ref, ldt_ref, br_ref, bi_ref, abr_ref, abi_ref, bbr_ref, bbi_ref):
    lr, li, dt = lr_ref[...], li_ref[...], jnp.exp(ldt_ref[...])
    mag = jnp.exp(lr * dt)
    ab_re, ab_im = mag * jnp.cos(li * dt), mag * jnp.sin(li * dt)
    den = lr * lr + li * li
    num_re = ab_re - 1.0
    coef_re = (num_re * lr + ab_im * li) / den
    coef_im = (ab_im * lr - num_re * li) / den
    br, bi = br_ref[...], bi_ref[...]
    abr_ref[...] = ab_re
    abi_ref[...] = ab_im
    bbr_ref[...] = coef_re * br - coef_im * bi
    bbi_ref[...] = coef_re * bi + coef_im * br


def _s5_discretise(a_re, a_im, log_dt, b_re, b_im):
    n = DEPTH * S5_GROUPS
    rep = lambda t: jnp.repeat(t.reshape(n, S5_STATE), S5_GROUP, axis=1)
    ldt = jnp.broadcast_to(log_dt.reshape(n, 1), (n, S5_STATE * S5_GROUP))
    shape = jax.ShapeDtypeStruct((n, S5_STATE * S5_GROUP), F32)
    ab_re, ab_im, bb_re, bb_im = pl.pallas_call(_s5_disc_kernel, out_shape=[shape] * 4, name="s5_discretise")(
        rep(a_re), rep(a_im), ldt, b_re.reshape(n, -1), b_im.reshape(n, -1))
    pole = lambda t: t[:, ::S5_GROUP].reshape(DEPTH, 1, S5_LANES)
    eye = jnp.eye(8, dtype=F32)

    def blockdiag_in(bb):
        t = bb.reshape(DEPTH, S5_SLABS, 8, S5_STATE, S5_GROUP).transpose(0, 1, 2, 4, 3)
        return jnp.einsum("dcgmp,gh->dcgmhp", t, eye).reshape(DEPTH, S5_SLABS, LANES, 8 * S5_STATE)

    w_in = jnp.concatenate([blockdiag_in(bb_re), blockdiag_in(bb_im)], axis=-1).astype(BF16)
    return w_in, pole(ab_re), pole(ab_im)


def _s5_blockdiag_out(c):
    t = c.reshape(DEPTH, S5_SLABS, 8, S5_GROUP, S5_STATE).transpose(0, 1, 2, 4, 3)
    return jnp.einsum("dcgpm,gh->dcgphm", t, jnp.eye(8, dtype=F32)).reshape(DEPTH, S5_SLABS, 8 * S5_STATE, LANES).astype(BF16)


def _time_major_perm(nb, ct):
    r = np.arange(nb * ct)
    p = np.zeros((nb * ct, nb * ct), np.float32)
    p[r, (r % nb) * ct + r // nb] = 1.0
    return p


def _gelu_tanh(y):
    return 0.5 * y * (1.0 + jnp.tanh(0.7978845608028654 * (y + 0.044715 * (y * y * y))))


def _s5_input(ub, win_ref, xre_ref, xim_ref):
    half = 8 * S5_STATE
    for s in range(S5_SLABS):
        r = _dot(ub[:, s * LANES:(s + 1) * LANES], win_ref[s])
        xre_ref[:, s * half:(s + 1) * half] = r[:, :half]
        xim_ref[:, s * half:(s + 1) * half] = r[:, half:]


def _s5_readout(xre_ref, xim_ref, cre_ref, cim_ref):
    half = 8 * S5_STATE
    ys = []
    for s in range(S5_SLABS):
        ls = slice(s * half, (s + 1) * half)
        ys.append(_dot(xre_ref[:, ls].astype(BF16), cre_ref[s]) - _dot(xim_ref[:, ls].astype(BF16), cim_ref[s]))
    return jnp.concatenate(ys, axis=-1)


def _s5_glu(y, u, d_ref, wglu_ref, bglu_ref):
    z = _gelu_tanh(y + d_ref[...] * u)
    return z * _sigmoid(_dot(z.astype(BF16), wglu_ref[...]) + bglu_ref[...])


def _s5_scan_kernel(su_ref, perm_ref, permt_ref, win_ref, are_ref, aim_ref, cre_ref, cim_ref, d_ref, wglu_ref, bglu_ref,
                    o_ref, hre_ref, him_ref, xre_ref, xim_ref, *, nb, ct):
    @pl.when(pl.program_id(0) == 0)
    def _():
        hre_ref[...] = jnp.zeros_like(hre_ref)
        him_ref[...] = jnp.zeros_like(him_ref)

    u = su_ref[...].reshape(nb * ct, S5_WIDTH)
    ub = _dot(perm_ref[...], u.astype(BF16)).astype(BF16)

    half = 8 * S5_STATE
    ys = []
    for s in range(S5_SLABS):
        ls = slice(s * half, (s + 1) * half)
        r = _dot(ub[:, s * LANES:(s + 1) * LANES], win_ref[s])
        xre_ref[:, ls] = r[:, :half]
        xim_ref[:, ls] = r[:, half:]
        ar = jnp.broadcast_to(are_ref[:, ls], (nb, half))
        ai = jnp.broadcast_to(aim_ref[:, ls], (nb, half))
        hr, hi = hre_ref[:, ls], him_ref[:, ls]
        for t in range(ct):
            rows = slice(t * nb, (t + 1) * nb)
            hr, hi = ar * hr - ai * hi + xre_ref[rows, ls], ar * hi + ai * hr + xim_ref[rows, ls]
            xre_ref[rows, ls] = hr
            xim_ref[rows, ls] = hi
        hre_ref[:, ls] = hr
        him_ref[:, ls] = hi
        ys.append(_dot(xre_ref[:, ls].astype(BF16), cre_ref[s]) - _dot(xim_ref[:, ls].astype(BF16), cim_ref[s]))

    y = jnp.concatenate(ys, axis=-1)
    y_hi = y.astype(BF16)
    y_lo = (y - y_hi.astype(F32)).astype(BF16)
    y = _dot01(permt_ref[...], (y_hi, y_lo))
    o_ref[...] = _s5_glu(y, u, d_ref, wglu_ref, bglu_ref).reshape(nb, ct, S5_WIDTH).astype(BF16)


def _s5_weight_specs(layer):
    return [_layer_spec((S5_SLABS, LANES, 2 * 8 * S5_STATE), layer),
            _layer_spec((1, S5_LANES), layer), _layer_spec((1, S5_LANES), layer),
            _layer_spec((S5_SLABS, 8 * S5_STATE, LANES), layer), _layer_spec((S5_SLABS, 8 * S5_STATE, LANES), layer),
            _layer_spec((1, S5_WIDTH), layer), _layer_spec((S5_WIDTH, S5_WIDTH), layer),
            _layer_spec((1, S5_WIDTH), layer)]


def _s5_scan(su, weights, *, nb, length, layer):
    ct = S5_CHUNK
    perm = _time_major_perm(nb, ct)
    const = pl.BlockSpec((nb * ct, nb * ct), lambda i: (0, 0))
    return pl.pallas_call(
        functools.partial(_s5_scan_kernel, nb=nb, ct=ct),
        grid=(length // ct,),
        in_specs=[pl.BlockSpec((nb, ct, S5_WIDTH), lambda i: (0, i, 0)), const, const] + _s5_weight_specs(layer),
        out_specs=[pl.BlockSpec((nb, ct, S5_WIDTH), lambda i: (0, i, 0)),
                   pl.BlockSpec((nb, S5_LANES), lambda i: (0, 0)), pl.BlockSpec((nb, S5_LANES), lambda i: (0, 0))],
        out_shape=[jax.ShapeDtypeStruct((nb, length, S5_WIDTH), BF16),
                   jax.ShapeDtypeStruct((nb, S5_LANES), F32), jax.ShapeDtypeStruct((nb, S5_LANES), F32)],
        scratch_shapes=[pltpu.VMEM((nb * ct, S5_LANES), F32), pltpu.VMEM((nb * ct, S5_LANES), F32)],
        compiler_params=_cparams("arbitrary"),
        name="s5_scan",
    )(su, jnp.asarray(perm, BF16), jnp.asarray(perm.T, BF16), *weights)


def _s5_step_kernel(su_ref, h0r_ref, h0i_ref, win_ref, are_ref, aim_ref, cre_ref, cim_ref, d_ref, wglu_ref, bglu_ref,
                    o_ref, hre_ref, him_ref):
    u = su_ref[...]
    _s5_input(u.astype(BF16), win_ref, hre_ref, him_ref)
    ar, ai = are_ref[...], aim_ref[...]
    h0r, h0i = h0r_ref[...], h0i_ref[...]
    nr = ar * h0r - ai * h0i + hre_ref[...]
    ni = ar * h0i + ai * h0r + him_ref[...]
    hre_ref[...] = nr
    him_ref[...] = ni
    o_ref[...] = _s5_glu(_s5_readout(hre_ref, him_ref, cre_ref, cim_ref), u, d_ref, wglu_ref, bglu_ref).astype(BF16)


def _s5_step(su, h0r, h0i, weights, *, layer):
    n = su.shape[0]
    full = lambda w: pl.BlockSpec((n, w), lambda i: (0, 0))
    state = pl.BlockSpec((None, n, S5_LANES), lambda i: (layer, 0, 0))
    return pl.pallas_call(
        _s5_step_kernel,
        grid=(1,),
        in_specs=[full(S5_WIDTH), state, state] + _s5_weight_specs(layer),
        out_specs=[full(S5_WIDTH), full(S5_LANES), full(S5_LANES)],
        out_shape=[jax.ShapeDtypeStruct((n, S5_WIDTH), BF16),
                   jax.ShapeDtypeStruct((n, S5_LANES), F32), jax.ShapeDtypeStruct((n, S5_LANES), F32)],
        compiler_params=_cparams("arbitrary"),
        name="s5_step",
    )(su, h0r, h0i, *weights)


def _head_norm(o, g):
    parts = []
    for h in range(HEADS):
        seg = o[:, h * DV:(h + 1) * DV]
        parts.append(seg * lax.rsqrt(jnp.mean(seg * seg, axis=-1, keepdims=True) + EPS))
    return jnp.concatenate(parts, axis=-1) * g


def _route(logits, count):
    rows = logits.shape[0]
    lane = lax.broadcasted_iota(I32, logits.shape, 1)
    neg = -jnp.inf
    first = lambda hit: jnp.min(jnp.where(hit, lane, LANES), axis=-1, keepdims=True)
    glog = jnp.where(lane < MOE_GROUPS, logits, neg)
    gmax = jnp.max(glog, axis=-1, keepdims=True)
    gidx = first(glog == gmax)
    gw = 1.0 / jnp.sum(jnp.where(lane < MOE_GROUPS, jnp.exp(logits - gmax), 0.0), axis=-1, keepdims=True)
    inside = (lane >= MOE_GROUPS) & (lane < MOE_GROUPS + MOE_EXPERTS) & (((lane - MOE_GROUPS) >> 3) == gidx)
    el = jnp.where(inside, logits, neg)
    v1 = jnp.max(el, axis=-1, keepdims=True)
    i1 = first(el == v1)
    el2 = jnp.where(lane == i1, neg, el)
    v2 = jnp.max(el2, axis=-1, keepdims=True)
    i2 = first(el2 == v2)
    p2 = jnp.exp(v2 - v1)
    w1 = gw / (1.0 + p2)
    w2 = gw * p2 / (1.0 + p2)
    hit1, hit2 = lane == i1, lane == i2
    onehot = jnp.where(hit1 | hit2, 1.0, 0.0)
    blk = min(rows, 256)
    tr = lax.broadcasted_iota(I32, (blk, blk), 0)
    tc = lax.broadcasted_iota(I32, (blk, blk), 1)
    lower = jnp.where(tc < tr, 1.0, 0.0).astype(BF16)
    before = []
    for r0 in range(0, rows, blk):
        part = onehot[r0:r0 + blk, :]
        before.append(_dot(lower, part.astype(BF16)) + count)
        count = count + jnp.sum(part, axis=0, keepdims=True)
    before = jnp.concatenate(before, axis=0)
    rank1 = jnp.sum(jnp.where(hit1, before, 0.0), axis=-1, keepdims=True)
    rank2 = jnp.sum(jnp.where(hit2, before, 0.0), axis=-1, keepdims=True)
    slab = jnp.zeros(logits.shape, F32)
    for ln, val in ((R_E1, (i1 - MOE_GROUPS).astype(F32)), (R_E2, (i2 - MOE_GROUPS).astype(F32)), (R_W1, w1), (R_W2, w2),
                    (R_RANK1, rank1), (R_RANK2, rank2)):
        slab = jnp.where(lane == ln, val, slab)
    return slab, count


def _merge_kernel(x_ref, o_ref, gate_ref, os5_ref, cnt0_ref, nm_ref, wgm_ref, gn_ref, wbr_ref, wout_ref, nf_ref,
                  wr_ref, br_ref, *rest, sub):
    xm_ref, h2_ref, rt_ref, rtt_ref, cnt_ref = rest[-5:]

    @pl.when(pl.program_id(0) == 0)
    def _():
        cnt_ref[...] = cnt0_ref[...]

    logits = []
    for r0 in range(0, x_ref.shape[0], sub):
        rs = slice(r0, r0 + sub)
        x = x_ref[rs, :]
        hb = _rms(x, nm_ref[...]).astype(BF16)
        mixed = None
        for i, src in enumerate((0, None, 1)):
            if src is None:
                branch = os5_ref[rs, :]
            else:
                branch = (_head_norm(o_ref[src, rs, :].astype(F32), gn_ref[src]) * gate_ref[src, rs, :].astype(F32)).astype(BF16)
            gate = _sigmoid(_dot(hb, wgm_ref[:, i * D_MODEL:(i + 1) * D_MODEL]))
            term = gate * _dot(branch, wbr_ref[i])
            mixed = term if mixed is None else mixed + term
        xm = x + _dot(mixed.astype(BF16), wout_ref[...])
        xm_ref[rs, :] = xm
        h2 = _rms(xm, nf_ref[...])
        h2_hi = h2.astype(BF16)
        h2_lo = (h2 - h2_hi.astype(F32)).astype(BF16)
        h2_ref[rs, :] = h2_hi
        both = _dot(h2_hi, wr_ref[...])
        logits.append(both[:, :LANES] + (both[:, LANES:] + _dot(h2_lo, wr_ref[:, :LANES])) + br_ref[...])
    slab, cnt_ref[...] = _route(jnp.concatenate(logits, axis=0), cnt_ref[...])
    rt_ref[...] = slab
    rtt_ref[...] = jnp.transpose(slab)[0:8, :]


def _merge(x, o, gate, os5, cnt0, weights, carried, *, layer, tm, total_rows, block_offset):
    rows = x.shape[0]
    tile = lambda w: pl.BlockSpec((tm, w), lambda i: (i, 0))
    shared = lambda w: pl.BlockSpec((tm, w), lambda i: (i + block_offset, 0))
    carried = [] if carried is None else list(carried)
    w_specs = [_layer_spec((1, D_MODEL), layer), _layer_spec((D_MODEL, 3 * D_MODEL), layer),
               _layer_spec((2, 1, VW), layer), _layer_spec((3, VW, D_MODEL), layer),
               _layer_spec((D_MODEL, D_MODEL), layer), _layer_spec((1, D_MODEL), layer),
               _layer_spec((D_MODEL, 2 * LANES), layer), _layer_spec((1, LANES), layer)]
    first = 5 + len(w_specs)
    return pl.pallas_call(
        functools.partial(_merge_kernel, sub=min(tm, 256)),
        grid=(rows // tm,),
        in_specs=[tile(D_MODEL),
                  pl.BlockSpec((2, tm, VW), lambda i: (0, i, 0)),
                  pl.BlockSpec((2, tm, VW), lambda i: (0, i, 0)),
                  tile(S5_WIDTH),
                  pl.BlockSpec((1, LANES), lambda i: (0, 0))] + w_specs + [pl.BlockSpec(memory_space=pl.ANY)] * len(carried),
        out_specs=[tile(D_MODEL), shared(D_MODEL), shared(LANES),
                   pl.BlockSpec((8, tm), lambda i: (0, i + block_offset)), pl.BlockSpec((1, LANES), lambda i: (0, 0))],
        out_shape=[jax.ShapeDtypeStruct((rows, D_MODEL), F32),
                   jax.ShapeDtypeStruct((total_rows, D_MODEL), BF16),
                   jax.ShapeDtypeStruct((total_rows, LANES), F32),
                   jax.ShapeDtypeStruct((8, total_rows), F32),
                   jax.ShapeDtypeStruct((1, LANES), F32)],
        input_output_aliases={first + j: 1 + j for j in range(len(carried))},
        compiler_params=_cparams("arbitrary"),
        name="merge_route",
    )(x, o, gate, os5, cnt0, *weights, *carried)


def _ffn_kernel(te_ref, nu_ref, ea_ref, eb_ref, par_ref, x_ref, wga_ref, wua_ref, wda_ref, wgb_ref, wub_ref, wdb_ref,
                *rest, tile0):
    y_ref, wg_bf, wu_bf, wd_bf = rest[-4:]
    i = pl.program_id(0)
    tile = i + tile0

    @pl.when(tile < nu_ref[0])
    def _():
        new_run = (i == 0) | (te_ref[tile] != te_ref[jnp.maximum(tile - 1, 0)])
        for parity, (wg_ref, wu_ref, wd_ref) in enumerate(((wga_ref, wua_ref, wda_ref), (wgb_ref, wub_ref, wdb_ref))):
            @pl.when(new_run & (par_ref[tile] == parity))
            def _():
                wg_bf[...] = wg_ref[...].astype(BF16)
                wu_bf[...] = wu_ref[...].astype(BF16)
                wd_bf[...] = wd_ref[...].astype(BF16)

        x = x_ref[...]
        act = _silu(_dot(x, wg_bf[...])) * _dot(x, wu_bf[...])
        y_ref[...] = _dot(act.astype(BF16), wd_bf[...]).astype(BF16)


def _grouped_ffn(plan, xg, y_prev, wg, wu, wd, *, layer, tile0, total_rows):
    used = lambda i, nu: jnp.maximum(jnp.minimum(i + tile0, nu[0] - 1), tile0)
    win_a = lambda a, b: pl.BlockSpec((None, None, a, b), lambda i, te, nu, ea, eb, par: (layer, ea[used(i, nu)], 0, 0))
    win_b = lambda a, b: pl.BlockSpec((None, None, a, b), lambda i, te, nu, ea, eb, par: (layer, eb[used(i, nu)], 0, 0))
    carried = [] if y_prev is None else [y_prev]
    shapes = ((D_MODEL, MOE_FF), (D_MODEL, MOE_FF), (MOE_FF, D_MODEL))
    grid_spec = pltpu.PrefetchScalarGridSpec(
        num_scalar_prefetch=5,
        grid=(xg.shape[0] // MOE_TILE,),
        in_specs=[pl.BlockSpec((MOE_TILE, D_MODEL), lambda i, te, nu, *_: (used(i, nu) - tile0, 0))]
        + [win_a(*s) for s in shapes] + [win_b(*s) for s in shapes]
        + [pl.BlockSpec(memory_space=pl.ANY)] * len(carried),
        out_specs=pl.BlockSpec((MOE_TILE, D_MODEL), lambda i, te, nu, *_: (used(i, nu), 0)),
        scratch_shapes=[pltpu.VMEM(s, BF16) for s in shapes],
    )
    return pl.pallas_call(
        functools.partial(_ffn_kernel, tile0=tile0),
        grid_spec=grid_spec,
        out_shape=jax.ShapeDtypeStruct((total_rows, D_MODEL), BF16),
        input_output_aliases={12: 0} if carried else {},
        compiler_params=_cparams("arbitrary"),
        name="expert_ffn",
    )(*plan, xg, wg, wu, wd, wg, wu, wd, *carried)


@compute_on("tpu_sparsecore")
@jax.jit
def _take_rows(x, idx):
    return jnp.take(x, idx, axis=0, mode="clip")


def _piece_cuts(n, fractions=PIPE_CUTS):
    inner = sorted({min(max(round(f * n), 1), n - 1) for f in fractions}) if n > 1 else []
    return [0] + inner + [n]


def _moe_plan(route_t, count):
    tokens = route_t.shape[1]
    n_tiles = -(-(2 * tokens + MOE_EXPERTS * (MOE_TILE - 1)) // MOE_TILE)
    n_rows = n_tiles * MOE_TILE
    counts = count[0, MOE_GROUPS:MOE_GROUPS + MOE_EXPERTS].astype(I32)
    padded = ((counts + MOE_TILE - 1) // MOE_TILE) * MOE_TILE
    gend = jnp.cumsum(padded)
    gstart = gend - padded
    experts = jnp.arange(MOE_EXPERTS, dtype=I32)[:, None]

    def rows_of(e_lane, rank_lane):
        e = route_t[e_lane].astype(I32)
        return jnp.sum(jnp.where(e[None, :] == experts, gstart[:, None], 0), axis=0) + route_t[rank_lane].astype(I32)

    pos1, pos2 = rows_of(R_E1, R_RANK1), rows_of(R_E2, R_RANK2)
    token = jnp.arange(tokens, dtype=I32) + 1
    marked = jnp.zeros((n_rows,), I32).at[jnp.concatenate([pos1, pos2])].add(jnp.concatenate([token, token]),
                                                                             unique_indices=True)
    filler = jnp.arange(n_rows, dtype=I32) % tokens
    src = jnp.where(marked > 0, marked - 1, filler)
    tile_start = jnp.arange(n_tiles, dtype=I32) * MOE_TILE
    tile_expert = jnp.minimum(jnp.sum((tile_start[:, None] >= gend[None, :]).astype(I32), axis=1), MOE_EXPERTS - 1)
    n_used = (gend[-1] // MOE_TILE).astype(I32).reshape(1)
    ids = jnp.arange(MOE_EXPERTS, dtype=I32)
    present = padded > 0
    parity_e = (jnp.cumsum(present.astype(I32)) - 1) & 1
    later = jnp.min(jnp.where(present[None, :] & (ids[None, :] > ids[:, None]), ids[None, :], MOE_EXPERTS), axis=1)
    next_e = jnp.where(later == MOE_EXPERTS, ids, later)
    per_tile = lambda v: jnp.sum(jnp.where(tile_expert[:, None] == ids[None, :], v[None, :], 0), axis=1)
    window_a = per_tile(jnp.where(parity_e == 0, ids, next_e))
    window_b = per_tile(jnp.where(parity_e == 1, ids, next_e))
    return src, pos1, pos2, (tile_expert, n_used, window_a, window_b, per_tile(parity_e))


def _ple_kernel(xm_ref, y1_ref, y2_ref, rt_ref, p_ref, np_ref, wg_ref, wp_ref, nfin_ref, *rest, final):
    out_ref = rest[-1]
    rt = rt_ref[...]
    x1 = xm_ref[...] + rt[:, R_W1:R_W1 + 1] * y1_ref[...].astype(F32) + rt[:, R_W2:R_W2 + 1] * y2_ref[...].astype(F32)
    gate = _sigmoid(_dot(_rms(x1, np_ref[...]).astype(BF16), wg_ref[...]))
    x2 = x1 + gate * _dot(p_ref[...].astype(BF16), wp_ref[...])
    out_ref[...] = _rms(x2, nfin_ref[...]) if final else x2


def _ple(xm, yy, route, p, x_prev, npl, wg, wp, nfin, *, layer, final, tm, n_blocks, x_off, y1_off, y2_off, rt_off):
    at = lambda w, off: pl.BlockSpec((tm, w), lambda i: (i + off, 0))
    carried = [] if x_prev is None else [x_prev]
    return pl.pallas_call(
        functools.partial(_ple_kernel, final=final),
        grid=(n_blocks,),
        in_specs=[at(D_MODEL, x_off), at(D_MODEL, y1_off), at(D_MODEL, y2_off), at(LANES, rt_off),
                  pl.BlockSpec((None, tm, PLE_DIM), lambda i: (layer, i + x_off, 0)),
                  _layer_spec((1, D_MODEL), layer), _layer_spec((D_MODEL, D_MODEL), layer),
                  _layer_spec((PLE_DIM, D_MODEL), layer), pl.BlockSpec((1, D_MODEL), lambda i: (0, 0))]
        + [pl.BlockSpec(memory_space=pl.ANY)] * len(carried),
        out_specs=at(D_MODEL, x_off),
        out_shape=jax.ShapeDtypeStruct(xm.shape, F32),
        input_output_aliases={9: 0} if carried else {},
        compiler_params=_cparams("parallel"),
        name="combine_ple",
    )(xm, yy, yy, route, p, npl, wg, wp, nfin, *carried)


def kernel(x_prompt, x_sample, state_gla, state_s5_re, state_s5_im, state_hgrn, p_prompt, p_sample, norm_mix, w_in, gla_w_gate2, gla_b_gate, gla_norm, s5_a_re, s5_a_im, s5_log_dt, s5_b_re, s5_b_im, s5_c_re, s5_c_im, s5_d, s5_w_glu, s5_b_glu, hgrn_lb_logits, hgrn_norm, w_br_gla, w_br_s5, w_br_hgrn, w_out, norm_ffn, moe_w_group, moe_b_group, moe_w_expert, moe_b_expert, moe_w_gate, moe_w_up, moe_w_down, norm_ple, w_ple_gate, w_ple_proj, norm_final):
    nb, length, _ = x_prompt.shape
    ns = x_sample.shape[0]
    n_p = nb * length
    n_all = n_p + ns
    row = lambda t: t.reshape(DEPTH, 1, -1)

    w_a, w_gm = _w_in_layout(w_in)
    wg2 =jnp.concatenate([gla_w_gate2, jnp.zeros((DEPTH, LANES - GLA_RANK, QK), F32)], axis=1).astype(BF16)
    s5_win, s5_are, s5_aim = _s5_discretise(s5_a_re, s5_a_im, s5_log_dt, s5_b_re, s5_b_im)
    s5_w = (s5_win, s5_are, s5_aim, _s5_blockdiag_out(s5_c_re), _s5_blockdiag_out(s5_c_im), row(s5_d),
            s5_w_glu.astype(BF16), row(s5_b_glu))
    head_gain = jnp.stack([jnp.tile(gla_norm, (1, HEADS)), jnp.tile(hgrn_norm, (1, HEADS))], axis=1).reshape(DEPTH, 2, 1, VW)
    w_router = jnp.concatenate([moe_w_group, moe_w_expert,
                                jnp.zeros((DEPTH, D_MODEL, LANES - MOE_GROUPS - MOE_EXPERTS), F32)], axis=-1)
    wr_hi = w_router.astype(BF16)
    wr_lo = (w_router - wr_hi.astype(F32)).astype(BF16)
    wr_hilo = jnp.concatenate([wr_hi, wr_lo], axis=-1)
    b_router = jnp.concatenate([moe_b_group, moe_b_expert,
                                jnp.zeros((DEPTH, LANES - MOE_GROUPS - MOE_EXPERTS), F32)], axis=-1).reshape(DEPTH, 1, LANES)
    merge_w = (row(norm_mix), w_gm, head_gain,
               jnp.stack([w_br_gla, w_br_s5, w_br_hgrn], axis=1).astype(BF16), w_out.astype(BF16), row(norm_ffn),
               wr_hilo, b_router)
    ple_w = (row(norm_ple), w_ple_gate.astype(BF16), w_ple_proj.astype(BF16), norm_final.reshape(1, D_MODEL))
    nm, bg = row(norm_mix), row(gla_b_gate)
    p_p, p_s = p_prompt.reshape(DEPTH, n_p, PLE_DIM), p_sample.reshape(DEPTH, ns, PLE_DIM)
    sg_in, sh_in = state_gla.reshape(DEPTH, ns, QK, DV), state_hgrn.reshape(DEPTH, ns, QK, DV)
    s5r_in, s5i_in = state_s5_re.reshape(DEPTH, ns, S5_LANES), state_s5_im.reshape(DEPTH, ns, S5_LANES)

    xp = x_prompt.reshape(n_p, D_MODEL)
    xs = x_sample.reshape(ns, D_MODEL)
    new_p, new_s5 = [], []
    new_s = None
    for i in range(DEPTH):
        qk, v, gate, g, su = _inproj(xp, nm, w_a, wg2, bg, hgrn_lb_logits, layer=i, tm=1024)
        o_p, st_p = _chunk_scan(qk.reshape(2, nb, length, -1), v.reshape(2, nb, length, -1),
                                g.reshape(2, nb, length, -1), nb=nb, length=length)
        os5_p, hre_p, him_p = _s5_scan(su.reshape(nb, length, -1), s5_w, nb=nb, length=length, layer=i)
        xm_p, h2, rt, rtt, cnt = _merge(xp, o_p.reshape(2, n_p, VW), gate, os5_p.reshape(n_p, -1),
                                        jnp.zeros((1, LANES), F32), merge_w, None, layer=i, tm=1024, total_rows=n_all,
                                        block_offset=0)
        new_p.append((st_p[0].reshape(nb, HEADS, DK, DV), hre_p.reshape(nb, S5_GROUPS, S5_STATE),
                      him_p.reshape(nb, S5_GROUPS, S5_STATE), st_p[1].reshape(nb, HEADS, DK, DV)))

        qk, v, gate, g, su = _inproj(xs, nm, w_a, wg2, bg, hgrn_lb_logits, layer=i, tm=ns)
        o_s, *new_s = _decode(qk, v, g, sg_in, sh_in, new_s, layer=i)
        os5_s, hre_s, him_s = _s5_step(su, s5r_in, s5i_in, s5_w, layer=i)
        xm_s, h2, rt, rtt, cnt = _merge(xs, o_s, gate, os5_s, cnt, merge_w, (h2, rt, rtt), layer=i, tm=ns,
                                        total_rows=n_all, block_offset=n_p // ns)
        new_s5.append((hre_s.reshape(ns, S5_GROUPS, S5_STATE), him_s.reshape(ns, S5_GROUPS, S5_STATE)))

        src, pos1, pos2, plan = _moe_plan(rtt, cnt)
        n_tiles = plan[0].shape[0]
        cuts = _piece_cuts(n_tiles)
        y = None
        for t0, t1 in zip(cuts[:-1], cuts[1:]):
            y = _grouped_ffn(plan, _take_rows(h2, src[t0 * MOE_TILE:t1 * MOE_TILE]), y,
                             moe_w_gate, moe_w_up, moe_w_down, layer=i, tile0=t0, total_rows=n_tiles * MOE_TILE)

        final = i == DEPTH - 1
        tm = 512
        cuts = _piece_cuts(n_p // tm, COMBINE_CUTS)
        x_new = None
        for b0, b1 in zip(cuts[:-1], cuts[1:]):
            r0, r1 = b0 * tm, (n_all if b1 == cuts[-1] else b1 * tm)
            seg = -(-(r1 - r0) // tm) * tm
            pad = [jnp.arange(seg - (r1 - r0), dtype=I32)] if seg > r1 - r0 else []
            yy = _take_rows(y, jnp.concatenate([pos1[r0:r1], *pad, pos2[r0:r1], *pad]))
            x_new = _ple(xm_p, yy, rt, p_p, x_new, *ple_w, layer=i, final=final, tm=tm, n_blocks=b1 - b0,
                         x_off=b0, y1_off=0, y2_off=seg // tm, rt_off=b0)
        xp = x_new
        xs = _ple(xm_s, yy, rt, p_s, None, *ple_w, layer=i, final=final, tm=ns, n_blocks=1,
                  x_off=0, y1_off=(n_p - r0) // ns, y2_off=(seg + n_p - r0) // ns, rt_off=n_p // ns)

    stack = lambda items, j: jnp.stack([it[j] for it in items])
    return (xp.reshape(nb, length, D_MODEL), xs.reshape(ns, 1, D_MODEL),
            stack(new_p, 0), stack(new_p, 1), stack(new_p, 2), stack(new_p, 3),
            new_s[0].reshape(DEPTH, ns, HEADS, DK, DV), stack(new_s5, 0), stack(new_s5, 1),
            new_s[1].reshape(DEPTH, ns, HEADS, DK, DV))
```
